```python
import math
import jax, jax.numpy as jnp
from jax import lax
import numpy as np

D_MODEL = 1024
BATCH = 8
SEQ = 4096
DEPTH = 1

D_MIX = D_MODEL
CONV_WIDTH = D_MIX // 2
ATTN_WIDTH = D_MIX - CONV_WIDTH
HEAD_DIM = 64
N_ATTN_HEADS = ATTN_WIDTH // HEAD_DIM
CONV_K = 3
Q_BLOCK = 128
LN_EPS = 1e-5
DEEPNORM_ALPHA = (2.0 * DEPTH) ** 0.25
DEEPNORM_BETA = (8.0 * DEPTH) ** -0.25
PROJ_WIDTH = 4 * CONV_WIDTH + 4 * ATTN_WIDTH

kernel_name = "hymba_shortconv_stickbreaking_deepnorm"


def _layer_norm(y, gain, bias):
    y32 = y.astype(jnp.float32)
    mu = jnp.mean(y32, axis=-1, keepdims=True)
    var = jnp.mean(jnp.square(y32 - mu), axis=-1, keepdims=True)
    out = (y32 - mu) * lax.rsqrt(var + LN_EPS) * gain.astype(jnp.float32) + bias.astype(jnp.float32)
    return out.astype(y.dtype)


def _short_conv(b_gate, c_gate, h, conv_w):
    u = c_gate * h
    seq = u.shape[1]
    u_pad = jnp.pad(u, ((0, 0), (CONV_K - 1, 0), (0, 0)))
    y = conv_w[CONV_K - 1] * u_pad[:, CONV_K - 1:CONV_K - 1 + seq]
    for tap in range(CONV_K - 1):
        y = y + conv_w[tap] * u_pad[:, tap:tap + seq]
    return b_gate * y


def _stick_breaking_attention(q, k, v):
    bsz, seq, _ = q.shape
    out_dtype = v.dtype
    def heads(t):
        return t.reshape(bsz, seq, N_ATTN_HEADS, HEAD_DIM).transpose(0, 2, 1, 3).astype(jnp.float32)
    qh = heads(q) * (HEAD_DIM ** -0.5)
    kh, vh = heads(k), heads(v)
    n_blocks = seq // Q_BLOCK
    outs = []
    for blk in range(n_blocks):
        q0 = blk * Q_BLOCK
        kv_len = q0 + Q_BLOCK
        qb = qh[:, :, q0:kv_len]
        kb = kh[:, :, :kv_len]
        vb = vh[:, :, :kv_len]
        z = jnp.einsum('bhqd,bhkd->bhqk', qb, kb)
        t_pos = q0 + jnp.arange(Q_BLOCK)[:, None]
        s_pos = jnp.arange(kv_len)[None, :]
        strict = s_pos < t_pos
        neg_log_keep = jnp.where(strict, jax.nn.softplus(z), 0.0)
        after = lax.cumsum(neg_log_keep, axis=3, reverse=True) - neg_log_keep
        log_w = jax.nn.log_sigmoid(z) - after
        w = jnp.where(strict, jnp.exp(log_w), 0.0)
        outs.append(jnp.einsum('bhqk,bhkd->bhqd', w, vb))
    o = jnp.concatenate(outs, axis=2)
    return o.transpose(0, 2, 1, 3).reshape(bsz, seq, ATTN_WIDTH).astype(out_dtype)


def _fwd_setup_inputs(seed: int = 0) -> dict:
    key = jax.random.key(seed)
    k_x, k_in, k_conv, k_out, k_g, k_b = jax.random.split(key, 6)
    x = jax.random.normal(k_x, (BATCH, SEQ, D_MODEL), jnp.float32)
    w_in = jax.random.normal(k_in, (DEPTH, D_MODEL, PROJ_WIDTH), jnp.float32) * (D_MODEL ** -0.5)
    conv_w = jax.random.normal(k_conv, (DEPTH, CONV_K, CONV_WIDTH), jnp.float32) * (CONV_K ** -0.5)
    w_out = jax.random.normal(k_out, (DEPTH, D_MIX, D_MODEL), jnp.float32) * (D_MIX ** -0.5) * DEEPNORM_BETA
    ln_gain = 1.0 + 0.05 * jax.random.normal(k_g, (DEPTH, D_MODEL), jnp.float32)
    ln_bias = 0.02 * jax.random.normal(k_b, (DEPTH, D_MODEL), jnp.float32)
    return {"x": x, "w_in": w_in, "conv_w": conv_w, "w_out": w_out,
            "ln_gain": ln_gain, "ln_bias": ln_bias}


def _fwd_reference(x, w_in, conv_w, w_out, ln_gain, ln_bias):
    h_res = x
    for layer in range(DEPTH):
        proj = jnp.einsum('bsd,de->bse', h_res, w_in[layer])
        b_c, c_c, h_c, z_c, q, k, v, z_a = jnp.split(proj, 8, axis=-1)
        conv_out = jax.nn.silu(z_c) * _short_conv(b_c, c_c, h_c, conv_w[layer])
        attn_out = jax.nn.silu(z_a) * _stick_breaking_attention(q, k, v)
        mix = jnp.concatenate([conv_out, attn_out], axis=-1)
        sub = jnp.einsum('bse,ed->bsd', mix, w_out[layer])
        h_res = _layer_norm(DEEPNORM_ALPHA * h_res + sub, ln_gain[layer], ln_bias[layer])
    return h_res


import jax as _jax
import jax.numpy as _jnp

TWIN_FORMAT = 'train_step'
FWD_PARAMS = ['x', 'w_in', 'conv_w', 'w_out', 'ln_gain', 'ln_bias']
TWIN_WEIGHTS = ['w_in', 'conv_w', 'w_out', 'ln_gain', 'ln_bias']
TWIN_DIFF_INPUT = 'x'
TWIN_INPUTS = ['x', 'w_in', 'conv_w', 'w_out', 'ln_gain', 'ln_bias', 'loss_target', 'm_w_in', 'm_conv_w', 'm_w_out', 'm_ln_gain', 'm_ln_bias', 'v_w_in', 'v_conv_w', 'v_w_out', 'v_ln_gain', 'v_ln_bias']
TWIN_OUTPUTS = ['loss', 'grad_x', 'grad_w_in', 'grad_conv_w', 'grad_w_out', 'grad_ln_gain', 'grad_ln_bias', 'delta_w_in', 'delta_conv_w', 'delta_w_out', 'delta_ln_gain', 'delta_ln_bias', 'new_m_w_in', 'new_m_conv_w', 'new_m_w_out', 'new_m_ln_gain', 'new_m_ln_bias', 'new_v_w_in', 'new_v_conv_w', 'new_v_w_out', 'new_v_ln_gain', 'new_v_ln_bias']
TWIN_LEAF_KINDS = {'loss': 'loss', 'grad_x': 'grad_x', 'grad_w_in': 'grad_w', 'grad_conv_w': 'grad_w', 'grad_w_out': 'grad_w', 'grad_ln_gain': 'grad_w', 'grad_ln_bias': 'grad_w', 'delta_w_in': 'delta_w', 'delta_conv_w': 'delta_w', 'delta_w_out': 'delta_w', 'delta_ln_gain': 'delta_w', 'delta_ln_bias': 'delta_w', 'new_m_w_in': 'new_m', 'new_m_conv_w': 'new_m', 'new_m_w_out': 'new_m', 'new_m_ln_gain': 'new_m', 'new_m_ln_bias': 'new_m', 'new_v_w_in': 'new_v', 'new_v_conv_w': 'new_v', 'new_v_w_out': 'new_v', 'new_v_ln_gain': 'new_v', 'new_v_ln_bias': 'new_v'}


def _forward(args):
    return _fwd_reference(*[args[k] for k in FWD_PARAMS])


def _output_shape():
    out = _jax.eval_shape(lambda: _forward(_fwd_setup_inputs(0)))
    return out.shape, out.dtype

N_MICROBATCH = 1
ADAM_LR = 0.001
ADAM_B1 = 0.9
ADAM_B2 = 0.999
ADAM_EPS = 1e-08
ADAM_WD = 0.01
ADAM_STEP = 10
PER_EXAMPLE_BATCH_AXIS = {'x': 0, 'loss_target': 0}
SHARED_INPUTS = []
_WEIGHT_DTYPES = {'w_in': _jnp.float32, 'conv_w': _jnp.float32, 'w_out': _jnp.float32, 'ln_gain': _jnp.float32, 'ln_bias': _jnp.float32}
MOMENT_SCALE = {'w_in': 4.108937e-02, 'conv_w': 5.186907e-02, 'w_out': 7.344354e-02, 'ln_gain': 3.194146e+01, 'ln_bias': 4.855635e-01}


def _to_microbatches(a, axis):
    t = _jnp.moveaxis(a, axis, 0)
    t = t.reshape((N_MICROBATCH, t.shape[0] // N_MICROBATCH) + t.shape[1:])
    return _jnp.moveaxis(t, 1, axis + 1)


def setup_inputs(seed: int = 0) -> dict:
    inp = _fwd_setup_inputs(seed)
    key = _jax.random.fold_in(_jax.random.key(seed), 7919)
    shape, _ = _output_shape()
    out = dict(inp)
    out["loss_target"] = _jax.random.normal(_jax.random.fold_in(key, 0), shape, _jnp.float32)
    for i, name in enumerate(TWIN_WEIGHTS):
        w = inp[name].astype(_jnp.float32)
        if MOMENT_SCALE is None:
            s = _jnp.sqrt(_jnp.mean(_jnp.square(w)) + 1e-30)
        else:
            s = MOMENT_SCALE[name]
        km, kv = _jax.random.split(_jax.random.fold_in(key, i + 1))
        out[name] = w
        out["m_" + name] = s * _jax.random.normal(km, w.shape, _jnp.float32)
        out["v_" + name] = (s * s) * _jax.random.uniform(kv, w.shape, _jnp.float32, 0.5, 1.5)
    if N_MICROBATCH > 1:
        for name, axis in PER_EXAMPLE_BATCH_AXIS.items():
            out[name] = _to_microbatches(out[name], axis)
    return {'x': out['x'], 'w_in': out['w_in'], 'conv_w': out['conv_w'], 'w_out': out['w_out'], 'ln_gain': out['ln_gain'], 'ln_bias': out['ln_bias'], 'loss_target': out['loss_target'], 'm_w_in': out['m_w_in'], 'm_conv_w': out['m_conv_w'], 'm_w_out': out['m_w_out'], 'm_ln_gain': out['m_ln_gain'], 'm_ln_bias': out['m_ln_bias'], 'v_w_in': out['v_w_in'], 'v_conv_w': out['v_conv_w'], 'v_w_out': out['v_w_out'], 'v_ln_gain': out['v_ln_gain'], 'v_ln_bias': out['v_ln_bias']}


def _loss(weights, diff, rest, loss_target):
    with _jax.named_scope("forward"):
        args = {**rest, TWIN_DIFF_INPUT: diff, **{k: w.astype(_WEIGHT_DTYPES[k]) for k, w in weights.items()}}
        y = _forward(args)
    with _jax.named_scope("loss_head"):
        err = _jnp.square(y.astype(_jnp.float32) - loss_target)
        return 0.5 * _jnp.sum(_jnp.mean(err, axis=-1)) if err.ndim else 0.5 * err


def _adamw(w, g, m, v):
    m = ADAM_B1 * m + (1.0 - ADAM_B1) * g
    v = ADAM_B2 * v + (1.0 - ADAM_B2) * _jnp.square(g)
    m_hat = m / (1.0 - ADAM_B1 ** ADAM_STEP)
    v_hat = v / (1.0 - ADAM_B2 ** ADAM_STEP)
    delta = -ADAM_LR * (m_hat / (_jnp.sqrt(v_hat) + ADAM_EPS) + ADAM_WD * w)
    return delta, m, v


def reference(x, w_in, conv_w, w_out, ln_gain, ln_bias, loss_target, m_w_in, m_conv_w, m_w_out, m_ln_gain, m_ln_bias, v_w_in, v_conv_w, v_w_out, v_ln_gain, v_ln_bias):
    given = dict(x=x, w_in=w_in, conv_w=conv_w, w_out=w_out, ln_gain=ln_gain, ln_bias=ln_bias, loss_target=loss_target, m_w_in=m_w_in, m_conv_w=m_conv_w, m_w_out=m_w_out, m_ln_gain=m_ln_gain, m_ln_bias=m_ln_bias, v_w_in=v_w_in, v_conv_w=v_conv_w, v_w_out=v_w_out, v_ln_gain=v_ln_gain, v_ln_bias=v_ln_bias)
    weights = {n: given[n] for n in TWIN_WEIGHTS}
    shared = {n: given[n] for n in SHARED_INPUTS}
    per_example = {n: given[n] for n in ['x']}
    grad_fn = _jax.value_and_grad(_loss, argnums=(0, 1))

    def one_microbatch(ex, loss_target):
        ex = dict(ex)
        diff = ex.pop(TWIN_DIFF_INPUT)
        return grad_fn(weights, diff, {**shared, **ex}, loss_target)

    if N_MICROBATCH == 1:
        loss, (grad_w, grad_x) = one_microbatch(per_example, given["loss_target"])
    else:
        def body(carry, xs):
            loss_sum, grad_sum = carry
            l_k, (gw_k, gx_k) = one_microbatch(xs[0], xs[1])
            with _jax.named_scope("update"):
                return (loss_sum + l_k, _jax.tree.map(_jnp.add, grad_sum, gw_k)), gx_k

        init = (_jnp.zeros((), _jnp.float32), _jax.tree.map(_jnp.zeros_like, weights))
        (loss, grad_w), grad_x = _jax.lax.scan(body, init, (per_example, given["loss_target"]))
    with _jax.named_scope("update"):
        delta_w, new_m, new_v = {}, {}, {}
        for n in TWIN_WEIGHTS:
            delta_w[n], new_m[n], new_v[n] = _adamw(weights[n], grad_w[n], given["m_" + n], given["v_" + n])
    return (loss, grad_x, *[grad_w[n] for n in TWIN_WEIGHTS], *[delta_w[n] for n in TWIN_WEIGHTS],
            *[new_m[n] for n in TWIN_WEIGHTS], *[new_v[n] for n in TWIN_WEIGHTS])
```

```python
import functools

import jax
import jax.numpy as jnp
from jax import lax
from jax.experimental import pallas as pl
from jax.experimental.pallas import tpu as pltpu

F32 = jnp.float32
BF16 = jnp.bfloat16
MESH = pl.DeviceIdType.MESH

D_MODEL = 1024
CONV_WIDTH = 512
ATTN_WIDTH = 512
HEAD_DIM = 64
N_GROUPS = 8
N_CHIPS = 4
N_DEVICES = 8
LN_EPS = 1e-5
DEEPNORM_ALPHA = 2.0 ** 0.25
Q_SCALE = HEAD_DIM ** -0.5
ADAM_LR = 0.001
ADAM_B1 = 0.9
ADAM_B2 = 0.999
ADAM_EPS = 1e-08
ADAM_WD = 0.01
ADAM_STEP = 10

LANES = 128
SUBLANES = 8
V7X_VMEM_BYTES = 64 * 1024 * 1024
MIB = 1024 * 1024

TQ = 256
TK = 256
HEADS_PER_STEP = LANES // HEAD_DIM
CONV_ROWS = 512


def _compiler_params(vmem_mib, semantics=None):
    assert vmem_mib * MIB < V7X_VMEM_BYTES
    return pltpu.CompilerParams(dimension_semantics=semantics, vmem_limit_bytes=vmem_mib * MIB)


def _sigmoid(z):
    return 1.0 / (1.0 + jnp.exp(-z))


def _dot(a, b):
    return jnp.dot(a, b, preferred_element_type=F32)


def _dot_nt(a, b):
    return lax.dot_general(a, b, (((1,), (1,)), ((), ())), preferred_element_type=F32)


def _dot_tn(a, b):
    return lax.dot_general(a, b, (((0,), (0,)), ((), ())), preferred_element_type=F32)


def _split_bf16(a):
    hi = a.astype(BF16)
    lo = (a - hi.astype(F32)).astype(BF16)
    return hi, lo


def _softplus_parts(z):
    e = jnp.exp(-jnp.abs(z))
    return jnp.maximum(z, 0.0) + jnp.log(1.0 + e)


def _gather_weights(w_in, w_out, conv_w):
    d_rows, d_cols = w_in.shape
    o_rows, o_cols = w_out.shape
    row_chunk = 128

    def body(win_ref, wout_ref, cw_ref, wing_ref, woutg_ref, cwg_ref, send_sems, recv_sems):
        x, y, c = lax.axis_index("x"), lax.axis_index("y"), lax.axis_index("c")
        me = 2 * x + y

        def cast_in(r, carry):
            rows = pl.ds(pl.multiple_of(r * row_chunk, row_chunk), row_chunk)
            wing_ref[me, rows, :] = win_ref[rows, :].astype(BF16)
            return carry

        lax.fori_loop(0, d_rows // row_chunk, cast_in, 0)

        def cast_out(r, carry):
            rows = pl.ds(pl.multiple_of(r * row_chunk, row_chunk), row_chunk)
            woutg_ref[me, rows, :] = wout_ref[rows, :].astype(BF16)
            return carry

        lax.fori_loop(0, o_rows // row_chunk, cast_out, 0)
        cwg_ref[me] = cw_ref[...]

        chips = [(1 - x, y), (x, 1 - y), (1 - x, 1 - y)]
        bufs = (wing_ref, woutg_ref, cwg_ref)

        def copy(k, a, shard, to):
            return pltpu.make_async_remote_copy(
                src_ref=bufs[a].at[shard], dst_ref=bufs[a].at[shard],
                send_sem=send_sems.at[3 * k + a], recv_sem=recv_sems.at[3 * k + a],
                device_id=to, device_id_type=MESH)

        sends = []
        for k, (px, py) in enumerate(chips):
            for a in range(3):
                cp = copy(k, a, me, (px, py, c))
                cp.start()
                sends.append(cp)
        for k, (px, py) in enumerate(chips):
            for a in range(3):
                copy(k, a, 2 * px + py, (px, py, c)).wait_recv()
        for cp in sends:
            cp.wait_send()

    vmem = pl.BlockSpec(memory_space=pltpu.VMEM)
    return pl.pallas_call(
        body, name="gather_weights",
        out_shape=(jax.ShapeDtypeStruct((N_CHIPS, d_rows, d_cols), BF16),
                   jax.ShapeDtypeStruct((N_CHIPS, o_rows, o_cols), BF16),
                   jax.ShapeDtypeStruct((N_CHIPS,) + conv_w.shape, F32)),
        in_specs=[vmem, vmem, vmem], out_specs=(vmem, vmem, vmem),
        scratch_shapes=[pltpu.SemaphoreType.DMA((9,)), pltpu.SemaphoreType.DMA((9,))],
        compiler_params=_compiler_params(32),
    )(w_in, w_out, conv_w)


def _proj(x, wing):
    seq = x.shape[0]
    tm = 512

    def body(x_ref, w_ref, p_ref, xt_ref):
        xv = x_ref[...]
        p_ref[...] = _dot(xv.astype(BF16), w_ref[...])

        @pl.when(pl.program_id(1) == 0)
        def _():
            xt_ref[...] = xv.T.astype(BF16)

    return pl.pallas_call(
        body, name="proj",
        grid=(seq // tm, N_CHIPS),
        in_specs=[pl.BlockSpec((tm, D_MODEL), lambda i, j: (i, 0)),
                  pl.BlockSpec((None, D_MODEL, D_MODEL), lambda i, j: (j, 0, 0))],
        out_specs=(pl.BlockSpec((tm, D_MODEL), lambda i, j: (i, j)),
                   pl.BlockSpec((D_MODEL, tm), lambda i, j: (0, i))),
        out_shape=(jax.ShapeDtypeStruct((seq, N_CHIPS * D_MODEL), F32),
                   jax.ShapeDtypeStruct((D_MODEL, seq), BF16)),
        compiler_params=_compiler_params(40, ("arbitrary", "arbitrary")),
    )(x, wing)


def _col_block(group, n_sub=CONV_WIDTH // LANES):
    return lambda j: (0, group * n_sub + j)


def _shift_down(ext, k, rows):
    return pltpu.roll(ext, k, 0)[SUBLANES:, :]


def _shift_up(ext, k, rows):
    return pltpu.roll(ext, rows + SUBLANES - k, 0)[:rows, :]


def _conv_fwd(proj, conv_w):
    seq = proj.shape[0]
    rows = CONV_ROWS
    n_chunks = seq // rows

    def body(b_ref, c_ref, h_ref, z_ref, w_ref, out_ref, u_s):
        u_s[0:SUBLANES, :] = jnp.zeros((SUBLANES, LANES), F32)

        def fill(r, carry):
            rs = pl.ds(pl.multiple_of(r * rows, rows), rows)
            u_s[pl.ds(pl.multiple_of(r * rows + SUBLANES, SUBLANES), rows), :] = c_ref[rs, :] * h_ref[rs, :]
            return carry

        lax.fori_loop(0, n_chunks, fill, 0)
        w = w_ref[...]

        def chunk(r, carry):
            r0 = pl.multiple_of(r * rows, rows)
            rs = pl.ds(r0, rows)
            ext = u_s[pl.ds(r0, rows + SUBLANES), :]
            u = ext[SUBLANES:, :]
            y = w[2:3, :] * u
            y = y + w[0:1, :] * _shift_down(ext, 2, rows)
            y = y + w[1:2, :] * _shift_down(ext, 1, rows)
            z = z_ref[rs, :]
            out_ref[rs, :] = ((z * _sigmoid(z)) * (b_ref[rs, :] * y)).astype(BF16)
            return carry

        lax.fori_loop(0, n_chunks, chunk, 0)

    col = lambda g: pl.BlockSpec((seq, LANES), _col_block(g))
    return pl.pallas_call(
        body, name="conv_fwd",
        grid=(CONV_WIDTH // LANES,),
        in_specs=[col(0), col(1), col(2), col(3), pl.BlockSpec((3, LANES), lambda j: (0, j))],
        out_specs=pl.BlockSpec((seq, LANES), lambda j: (0, j)),
        out_shape=jax.ShapeDtypeStruct((seq, CONV_WIDTH), BF16),
        scratch_shapes=[pltpu.VMEM((seq + SUBLANES, LANES), F32)],
        compiler_params=_compiler_params(40, ("arbitrary",)),
    )(proj, proj, proj, proj, conv_w)


def _conv_bwd(proj, dmix, conv_w):
    seq = proj.shape[0]
    rows = CONV_ROWS
    n_chunks = seq // rows

    def body(b_ref, c_ref, h_ref, z_ref, d_ref, w_ref, dp_ref, dw_ref, u_s, dy_s):
        u_s[0:SUBLANES, :] = jnp.zeros((SUBLANES, LANES), F32)
        dy_s[seq:seq + SUBLANES, :] = jnp.zeros((SUBLANES, LANES), F32)

        def fill(r, carry):
            r0 = pl.multiple_of(r * rows, rows)
            rs = pl.ds(r0, rows)
            u_s[pl.ds(pl.multiple_of(r0 + SUBLANES, SUBLANES), rows), :] = c_ref[rs, :] * h_ref[rs, :]
            z = z_ref[rs, :]
            dy_s[rs, :] = d_ref[rs, :] * (z * _sigmoid(z)) * b_ref[rs, :]
            return carry

        lax.fori_loop(0, n_chunks, fill, 0)
        w = w_ref[...]

        def chunk(r, acc):
            r0 = pl.multiple_of(r * rows, rows)
            rs = pl.ds(r0, rows)
            ext = u_s[pl.ds(r0, rows + SUBLANES), :]
            u = ext[SUBLANES:, :]
            um1 = _shift_down(ext, 1, rows)
            um2 = _shift_down(ext, 2, rows)
            y = w[2:3, :] * u
            y = y + w[0:1, :] * um2
            y = y + w[1:2, :] * um1
            z = z_ref[rs, :]
            b = b_ref[rs, :]
            dco = d_ref[rs, :]
            sg = _sigmoid(z)
            g = z * sg
            dp_ref[0, rs, :] = (dco * g * y).astype(BF16)
            dp_ref[3, rs, :] = (dco * b * y * (sg * (1.0 + z * (1.0 - sg)))).astype(BF16)
            ext_dy = dy_s[pl.ds(r0, rows + SUBLANES), :]
            dy = ext_dy[:rows, :]
            du = w[2:3, :] * dy + w[1:2, :] * _shift_up(ext_dy, 1, rows) + w[0:1, :] * _shift_up(ext_dy, 2, rows)
            dp_ref[1, rs, :] = (du * h_ref[rs, :]).astype(BF16)
            dp_ref[2, rs, :] = (du * c_ref[rs, :]).astype(BF16)
            a0, a1, a2 = acc
            return (a0 + jnp.sum(dy * um2, axis=0, keepdims=True),
                    a1 + jnp.sum(dy * um1, axis=0, keepdims=True),
                    a2 + jnp.sum(dy * u, axis=0, keepdims=True))

        zero = jnp.zeros((1, LANES), F32)
        a0, a1, a2 = lax.fori_loop(0, n_chunks, chunk, (zero, zero, zero))
        dw_ref[...] = jnp.concatenate([a0, a1, a2, jnp.zeros((SUBLANES - 3, LANES), F32)], axis=0)

    col = lambda g: pl.BlockSpec((seq, LANES), _col_block(g))
    return pl.pallas_call(
        body, name="conv_bwd",
        grid=(CONV_WIDTH // LANES,),
        in_specs=[col(0), col(1), col(2), col(3), col(0), pl.BlockSpec((3, LANES), lambda j: (0, j))],
        out_specs=(pl.BlockSpec((4, seq, LANES), lambda j: (0, 0, j)),
                   pl.BlockSpec((SUBLANES, LANES), lambda j: (0, j))),
        out_shape=(jax.ShapeDtypeStruct((4, seq, CONV_WIDTH), BF16),
                   jax.ShapeDtypeStruct((SUBLANES, CONV_WIDTH), F32)),
        scratch_shapes=[pltpu.VMEM((seq + SUBLANES, LANES), F32), pltpu.VMEM((seq + SUBLANES, LANES), F32)],
        compiler_params=_compiler_params(48, ("arbitrary",)),
    )(proj, proj, proj, proj, dmix, conv_w)


def _tri_masks():
    r = lax.broadcasted_iota(jnp.int32, (TK, TK), 0)
    s = lax.broadcasted_iota(jnp.int32, (TK, TK), 1)
    return r, s


def _head_mask():
    lane = lax.broadcasted_iota(jnp.int32, (1, LANES), 1)
    return lane < HEAD_DIM


def _tile_lanes(a):
    return jnp.tile(a, (1, TK // LANES))


def _scores(qh, kt, suffix_m, strict):
    z = _dot_nt(qh, kt)
    sp = _softplus_parts(z)
    if strict is not None:
        sp = jnp.where(strict, sp, 0.0)
    hi, lo = _split_bf16(sp)
    suf = _dot(hi, suffix_m) + _dot(lo, suffix_m)
    return z, sp, suf


def _attn_fwd(proj):
    seq = proj.shape[0]
    n_q = seq // TQ
    assert seq // TK <= LANES

    def body(q_ref, k_ref, v_ref, za_ref, o_ref, mix_ref, car_ref,
             qa_s, qb_s, k_s, v_s, suffix_s, oacc_s, carry_s, cw_s):
        head_a = _head_mask()
        r_i, s_i = _tri_masks()
        strict = r_i > s_i
        suffix_s[...] = jnp.where(strict, 1.0, 0.0).astype(BF16)
        lane = lax.broadcasted_iota(jnp.int32, (1, LANES), 1)

        def prep(r, carry):
            rs = pl.ds(pl.multiple_of(r * TQ, TQ), TQ)
            q = q_ref[rs, :] * Q_SCALE
            qa_s[rs, :] = jnp.where(head_a, q, 0.0).astype(BF16)
            qb_s[rs, :] = jnp.where(head_a, 0.0, q).astype(BF16)
            k_s[rs, :] = k_ref[rs, :].astype(BF16)
            v_s[rs, :] = v_ref[rs, :].astype(BF16)
            return carry

        lax.fori_loop(0, n_q, prep, 0)

        def q_block(qb, carry):
            rs = pl.ds(pl.multiple_of(qb * TQ, TQ), TQ)
            q_heads = (qa_s[rs, :], qb_s[rs, :])
            oacc_s[...] = jnp.zeros((TQ, LANES), F32)
            carry_s[...] = jnp.zeros((HEADS_PER_STEP, TQ, LANES), F32)
            cw_s[...] = jnp.zeros((HEADS_PER_STEP, TQ, LANES), F32)

            def tile(kb, mask):
                cs = pl.ds(pl.multiple_of(kb * TK, TK), TK)
                kt = k_s[cs, :]
                vt = v_s[cs, :]
                pv = []
                for h in range(HEADS_PER_STEP):
                    z, sp, suf = _scores(q_heads[h], kt, suffix_s[...], mask)
                    car = carry_s[h]
                    cw_s[h] = jnp.where(lane == kb, car, cw_s[h])
                    w = jnp.exp(z - sp - (suf + _tile_lanes(car)))
                    if mask is not None:
                        w = jnp.where(mask, w, 0.0)
                    pv.append(_dot(w.astype(BF16), vt))
                    carry_s[h] = car + jnp.sum(sp, axis=1, keepdims=True)
                oacc_s[...] += jnp.where(head_a, pv[0], pv[1])

            tile(qb, strict)

            def left(i, c2):
                tile(qb - 1 - i, None)
                return c2

            lax.fori_loop(0, qb, left, 0)
            o = oacc_s[...]
            o_ref[rs, :] = o
            za = za_ref[rs, :]
            mix_ref[rs, :] = ((za * _sigmoid(za)) * o).astype(BF16)
            for h in range(HEADS_PER_STEP):
                car_ref[h, rs, :] = cw_s[h]
            return carry

        lax.fori_loop(0, n_q, q_block, 0)

    col = lambda g: pl.BlockSpec((seq, LANES), _col_block(g), pipeline_mode=pl.Buffered(1))
    n_pairs = ATTN_WIDTH // LANES
    return pl.pallas_call(
        body, name="attn_fwd",
        grid=(n_pairs,),
        in_specs=[col(4), col(5), col(6), col(7)],
        out_specs=(pl.BlockSpec((seq, LANES), lambda p: (0, p)),
                   pl.BlockSpec((seq, LANES), lambda p: (0, p)),
                   pl.BlockSpec((HEADS_PER_STEP, seq, LANES), lambda p: (p, 0, 0))),
        out_shape=(jax.ShapeDtypeStruct((seq, ATTN_WIDTH), F32),
                   jax.ShapeDtypeStruct((seq, ATTN_WIDTH), BF16),
                   jax.ShapeDtypeStruct((n_pairs * HEADS_PER_STEP, seq, LANES), F32)),
        scratch_shapes=[pltpu.VMEM((seq, LANES), BF16), pltpu.VMEM((seq, LANES), BF16),
                        pltpu.VMEM((seq, LANES), BF16), pltpu.VMEM((seq, LANES), BF16),
                        pltpu.VMEM((TK, TK), BF16),
                        pltpu.VMEM((TQ, LANES), F32),
                        pltpu.VMEM((HEADS_PER_STEP, TQ, LANES), F32),
                        pltpu.VMEM((HEADS_PER_STEP, TQ, LANES), F32)],
        compiler_params=_compiler_params(48, ("arbitrary",)),
    )(proj, proj, proj, proj)


def _attn_bwd(proj, o, dmix, carries):
    seq = proj.shape[0]
    n_q = seq // TQ

    def body(q_ref, k_ref, v_ref, za_ref, o_ref, dm_ref, car_ref, dp_ref,
             qa_s, qb_s, k_s, v_s, doa_s, dob_s, suffix_s, prefix_s, dq_s, ccar_s, dk_s, dv_s):
        head_a = _head_mask()
        r_i, s_i = _tri_masks()
        strict = r_i > s_i
        suffix_s[...] = jnp.where(strict, 1.0, 0.0).astype(BF16)
        prefix_s[...] = jnp.where(r_i < s_i, 1.0, 0.0).astype(BF16)
        lane = lax.broadcasted_iota(jnp.int32, (1, LANES), 1)

        def prep(r, carry):
            rs = pl.ds(pl.multiple_of(r * TQ, TQ), TQ)
            q = q_ref[rs, :] * Q_SCALE
            qa_s[rs, :] = jnp.where(head_a, q, 0.0).astype(BF16)
            qb_s[rs, :] = jnp.where(head_a, 0.0, q).astype(BF16)
            k_s[rs, :] = k_ref[rs, :].astype(BF16)
            v_s[rs, :] = v_ref[rs, :].astype(BF16)
            za = za_ref[rs, :]
            sg = _sigmoid(za)
            dm = dm_ref[rs, :]
            do = dm * (za * sg)
            doa_s[rs, :] = jnp.where(head_a, do, 0.0).astype(BF16)
            dob_s[rs, :] = jnp.where(head_a, 0.0, do).astype(BF16)
            dp_ref[3, rs, :] = (dm * o_ref[rs, :] * (sg * (1.0 + za * (1.0 - sg)))).astype(BF16)
            dk_s[rs, :] = jnp.zeros((TQ, LANES), F32)
            dv_s[rs, :] = jnp.zeros((TQ, LANES), F32)
            return carry

        lax.fori_loop(0, n_q, prep, 0)

        def q_block(qb, carry):
            rs = pl.ds(pl.multiple_of(qb * TQ, TQ), TQ)
            q_heads = (qa_s[rs, :], qb_s[rs, :])
            do_heads = (doa_s[rs, :], dob_s[rs, :])
            dq_s[...] = jnp.zeros((TQ, LANES), F32)
            ccar_s[...] = jnp.zeros((HEADS_PER_STEP, TQ, LANES), F32)

            def tile(kb, mask):
                cs = pl.ds(pl.multiple_of(kb * TK, TK), TK)
                kt = k_s[cs, :]
                vt = v_s[cs, :]
                dq_parts = []
                dk_part = None
                dv_part = None
                for h in range(HEADS_PER_STEP):
                    z, sp, suf = _scores(q_heads[h], kt, suffix_s[...], mask)
                    cin = jnp.sum(jnp.where(lane == kb, car_ref[h, rs, :], 0.0), axis=1, keepdims=True)
                    w = jnp.exp(z - sp - (suf + cin))
                    if mask is not None:
                        w = jnp.where(mask, w, 0.0)
                    dlw = _dot_nt(do_heads[h], vt) * w
                    hi, lo = _split_bf16(dlw)
                    ccar = ccar_s[h]
                    c_before = _dot(hi, prefix_s[...]) + _dot(lo, prefix_s[...]) + _tile_lanes(ccar)
                    sg = jnp.exp(z - sp)
                    dz = dlw - sg * (dlw + c_before)
                    if mask is not None:
                        dz = jnp.where(mask, dz, 0.0)
                    dzb = dz.astype(BF16)
                    wb = w.astype(BF16)
                    dq_parts.append(_dot(dzb, kt))
                    dk_h = _dot_tn(dzb, q_heads[h])
                    dv_h = _dot_tn(wb, do_heads[h])
                    dk_part = dk_h if dk_part is None else dk_part + dk_h
                    dv_part = dv_h if dv_part is None else dv_part + dv_h
                    ccar_s[h] = ccar + jnp.sum(dlw, axis=1, keepdims=True)
                dq_s[...] += jnp.where(head_a, dq_parts[0], dq_parts[1])
                dk_s[cs, :] += dk_part
                dv_s[cs, :] += dv_part

            def left(kb, c2):
                tile(kb, None)
                return c2

            lax.fori_loop(0, qb, left, 0)
            tile(qb, strict)
            dp_ref[0, rs, :] = (dq_s[...] * Q_SCALE).astype(BF16)
            return carry

        lax.fori_loop(0, n_q, q_block, 0)

        def finish(r, carry):
            rs = pl.ds(pl.multiple_of(r * TQ, TQ), TQ)
            dp_ref[1, rs, :] = dk_s[rs, :].astype(BF16)
            dp_ref[2, rs, :] = dv_s[rs, :].astype(BF16)
            return carry

        lax.fori_loop(0, n_q, finish, 0)

    def col(g, n_sub=CONV_WIDTH // LANES):
        return pl.BlockSpec((seq, LANES), _col_block(g, n_sub), pipeline_mode=pl.Buffered(1))

    n_pairs = ATTN_WIDTH // LANES
    bf = lambda: pltpu.VMEM((seq, LANES), BF16)
    return pl.pallas_call(
        body, name="attn_bwd",
        grid=(n_pairs,),
        in_specs=[col(4), col(5), col(6), col(7), col(0), col(1),
                  pl.BlockSpec((HEADS_PER_STEP, seq, LANES), lambda p: (p, 0, 0), pipeline_mode=pl.Buffered(1))],
        out_specs=pl.BlockSpec((4, seq, LANES), lambda p: (0, 0, p)),
        out_shape=jax.ShapeDtypeStruct((4, seq, ATTN_WIDTH), BF16),
        scratch_shapes=[bf(), bf(), bf(), bf(), bf(), bf(),
                        pltpu.VMEM((TK, TK), BF16), pltpu.VMEM((TK, TK), BF16),
                        pltpu.VMEM((TQ, LANES), F32),
                        pltpu.VMEM((HEADS_PER_STEP, TQ, LANES), F32),
                        pltpu.VMEM((seq, LANES), F32), pltpu.VMEM((seq, LANES), F32)],
        compiler_params=_compiler_params(56, ("arbitrary",)),
    )(proj, proj, proj, proj, o, dmix, carries)


def _out_ln(mix_c, mix_a, woutg, x, target, gain, bias):
    seq = x.shape[0]
    tm = 256
    inv_d = 1.0 / D_MODEL

    def body(mc_ref, ma_ref, wo_ref, x_ref, t_ref, g_ref, b_ref, dxa_ref, dmix_ref, gw_ref, st_ref):
        @pl.when(pl.program_id(0) == 0)
        def _():
            gw_ref[...] = jnp.zeros(gw_ref.shape, F32)
            st_ref[...] = jnp.zeros(st_ref.shape, F32)

        mc = mc_ref[...]
        ma = ma_ref[...]
        sub = _dot(mc, wo_ref[0:CONV_WIDTH, :]) + _dot(ma, wo_ref[CONV_WIDTH:, :])
        r = DEEPNORM_ALPHA * x_ref[...] + sub
        mu = jnp.mean(r, axis=-1, keepdims=True)
        rc = r - mu
        var = jnp.mean(rc * rc, axis=-1, keepdims=True)
        rstd = lax.rsqrt(var + LN_EPS)
        xhat = rc * rstd
        gain_v = g_ref[...]
        diff = (xhat * gain_v + b_ref[...]) - t_ref[...]
        dy = diff * inv_d
        st_ref[0:1, :] += jnp.sum(dy * xhat, axis=0, keepdims=True)
        st_ref[1:2, :] += jnp.sum(dy, axis=0, keepdims=True)
        st_ref[2:3, :] += jnp.sum(diff * diff, axis=0, keepdims=True)
        dxh = dy * gain_v
        m1 = jnp.mean(dxh, axis=-1, keepdims=True)
        m2 = jnp.mean(dxh * xhat, axis=-1, keepdims=True)
        dr = rstd * (dxh - m1 - xhat * m2)
        dxa_ref[...] = DEEPNORM_ALPHA * dr
        drb = dr.astype(BF16)
        dmix_ref[...] = _dot_nt(drb, wo_ref[...])
        gw_ref[0:CONV_WIDTH, :] += _dot_tn(mc, drb)
        gw_ref[CONV_WIDTH:, :] += _dot_tn(ma, drb)

    rows_d = lambda: pl.BlockSpec((tm, D_MODEL), lambda i: (i, 0))
    rows_h = lambda: pl.BlockSpec((tm, CONV_WIDTH), lambda i: (i, 0))
    whole = lambda shape: pl.BlockSpec(shape, lambda i: (0, 0))
    return pl.pallas_call(
        body, name="out_ln",
        grid=(seq // tm,),
        in_specs=[rows_h(), rows_h(), whole((D_MODEL, D_MODEL)), rows_d(), rows_d(),
                  whole((1, D_MODEL)), whole((1, D_MODEL))],
        out_specs=(rows_d(), rows_d(), whole((D_MODEL, D_MODEL)), whole((SUBLANES, D_MODEL))),
        out_shape=(jax.ShapeDtypeStruct((seq, D_MODEL), F32), jax.ShapeDtypeStruct((seq, D_MODEL), F32),
                   jax.ShapeDtypeStruct((D_MODEL, D_MODEL), F32), jax.ShapeDtypeStruct((SUBLANES, D_MODEL), F32)),
        compiler_params=_compiler_params(48, ("arbitrary",)),
    )(mix_c, mix_a, woutg, x, target, gain, bias)


def _group_maps():
    half = N_GROUPS // 2
    return (lambda g: jnp.minimum(g, half - 1)), (lambda g: jnp.maximum(g - half, 0))


def _grad_w_in(xt, dpc, dpa):
    seq = xt.shape[1]
    tk = 1024
    half = N_GROUPS // 2
    in_conv, in_attn = _group_maps()

    def body(xt_ref, dc_ref, da_ref, out_ref):
        g = pl.program_id(0)

        @pl.when(pl.program_id(1) == 0)
        def _():
            out_ref[...] = jnp.zeros(out_ref.shape, F32)

        @pl.when(g < half)
        def _():
            out_ref[...] += _dot(xt_ref[...], dc_ref[...])

        @pl.when(g >= half)
        def _():
            out_ref[...] += _dot(xt_ref[...], da_ref[...])

    return pl.pallas_call(
        body, name="grad_w_in",
        grid=(N_GROUPS, seq // tk),
        in_specs=[pl.BlockSpec((D_MODEL, tk), lambda g, s: (0, s)),
                  pl.BlockSpec((None, tk, CONV_WIDTH), lambda g, s: (in_conv(g), s, 0)),
                  pl.BlockSpec((None, tk, CONV_WIDTH), lambda g, s: (in_attn(g), s, 0))],
        out_specs=pl.BlockSpec((None, D_MODEL, CONV_WIDTH), lambda g, s: (g // 2, 0, g % 2)),
        out_shape=jax.ShapeDtypeStruct((N_CHIPS, D_MODEL, D_MODEL), F32),
        compiler_params=_compiler_params(40, ("arbitrary", "arbitrary")),
    )(xt, dpc, dpa)


def _grad_x(dxa, dpc, dpa, wing):
    seq = dxa.shape[0]
    tm = 512
    half = N_GROUPS // 2
    in_conv, in_attn = _group_maps()

    def body(dxa_ref, dc_ref, da_ref, w_ref, out_ref):
        g = pl.program_id(1)

        @pl.when(g == 0)
        def _():
            out_ref[...] = dxa_ref[...]

        @pl.when(g < half)
        def _():
            out_ref[...] += _dot_nt(dc_ref[...], w_ref[...])

        @pl.when(g >= half)
        def _():
            out_ref[...] += _dot_nt(da_ref[...], w_ref[...])

    return pl.pallas_call(
        body, name="grad_x",
        grid=(seq // tm, N_GROUPS),
        in_specs=[pl.BlockSpec((tm, D_MODEL), lambda i, g: (i, 0)),
                  pl.BlockSpec((None, tm, CONV_WIDTH), lambda i, g: (in_conv(g), i, 0)),
                  pl.BlockSpec((None, tm, CONV_WIDTH), lambda i, g: (in_attn(g), i, 0)),
                  pl.BlockSpec((None, D_MODEL, CONV_WIDTH), lambda i, g: (g // 2, 0, g % 2))],
        out_specs=pl.BlockSpec((tm, D_MODEL), lambda i, g: (i, 0)),
        out_shape=jax.ShapeDtypeStruct((seq, D_MODEL), F32),
        compiler_params=_compiler_params(40, ("arbitrary", "arbitrary")),
    )(dxa, dpc, dpa, wing)


def _reduce_grads(gwin, gwout, pack):
    n_rows_in, n_cols = gwin.shape[1], gwin.shape[2]
    n_rows_out = gwout.shape[1]
    row_chunk = 128

    def body(gwin_ref, gwout_ref, pack_ref, oin_ref, oout_ref, opack_ref,
             rin, rout, sib_in, sib_out, own_in, own_out, rpack,
             local_sems, send_sems, recv_sems, pack_send, pack_recv, sib_send, sib_recv):
        x, y, c = lax.axis_index("x"), lax.axis_index("y"), lax.axis_index("c")
        me = 2 * x + y
        my_id = 4 * x + 2 * y + c
        chips = [(1 - x, y), (x, 1 - y), (1 - x, 1 - y)]

        load_in = pltpu.make_async_copy(gwin_ref.at[me], own_in, local_sems.at[0])
        load_out = pltpu.make_async_copy(gwout_ref.at[me], own_out, local_sems.at[1])
        load_in.start()
        load_out.start()

        def shard_copy(k, a, shard, to):
            src = (gwin_ref, gwout_ref)[a].at[shard]
            dst = (rin, rout)[a].at[k]
            return pltpu.make_async_remote_copy(
                src_ref=src, dst_ref=dst, send_sem=send_sems.at[2 * k + a], recv_sem=recv_sems.at[2 * k + a],
                device_id=to, device_id_type=MESH)

        sends = []
        for k, (px, py) in enumerate(chips):
            for a in range(2):
                cp = shard_copy(k, a, 2 * px + py, (px, py, c))
                cp.start()
                sends.append(cp)

        rpack[my_id] = pack_ref[...]

        def pack_copy(rel, slot, to):
            return pltpu.make_async_remote_copy(
                src_ref=rpack.at[slot], dst_ref=rpack.at[slot],
                send_sem=pack_send.at[rel - 1], recv_sem=pack_recv.at[rel - 1],
                device_id=to, device_id_type=MESH)

        def related(rel):
            px = (1 - x) if rel & 4 else x
            py = (1 - y) if rel & 2 else y
            pc = (1 - c) if rel & 1 else c
            return px, py, pc

        for rel in range(1, N_DEVICES):
            cp = pack_copy(rel, my_id, related(rel))
            cp.start()
            sends.append(cp)

        load_in.wait()
        load_out.wait()
        for k, (px, py) in enumerate(chips):
            for a in range(2):
                shard_copy(k, a, me, (px, py, c)).wait_recv()

        def add_in(r, carry):
            rs = pl.ds(pl.multiple_of(r * row_chunk, row_chunk), row_chunk)
            own_in[rs, :] = ((own_in[rs, :] + rin[0, rs, :]) + rin[1, rs, :]) + rin[2, rs, :]
            return carry

        def add_out(r, carry):
            rs = pl.ds(pl.multiple_of(r * row_chunk, row_chunk), row_chunk)
            own_out[rs, :] = ((own_out[rs, :] + rout[0, rs, :]) + rout[1, rs, :]) + rout[2, rs, :]
            return carry

        lax.fori_loop(0, n_rows_in // row_chunk, add_in, 0)
        lax.fori_loop(0, n_rows_out // row_chunk, add_out, 0)

        sibling = (x, y, 1 - c)
        swap_in = pltpu.make_async_remote_copy(
            src_ref=own_in, dst_ref=sib_in, send_sem=sib_send.at[0], recv_sem=sib_recv.at[0],
            device_id=sibling, device_id_type=MESH)
        swap_out = pltpu.make_async_remote_copy(
            src_ref=own_out, dst_ref=sib_out, send_sem=sib_send.at[1], recv_sem=sib_recv.at[1],
            device_id=sibling, device_id_type=MESH)
        swap_in.start()
        swap_out.start()

        for rel in range(1, N_DEVICES):
            px, py, pc = related(rel)
            pack_copy(rel, 4 * px + 2 * py + pc, (px, py, pc)).wait_recv()
        total = rpack[0]
        for i in range(1, N_DEVICES):
            total = total + rpack[i]
        opack_ref[...] = total

        swap_in.wait()
        swap_out.wait()

        def sum_in(r, carry):
            rs = pl.ds(pl.multiple_of(r * row_chunk, row_chunk), row_chunk)
            oin_ref[rs, :] = own_in[rs, :] + sib_in[rs, :]
            return carry

        def sum_out(r, carry):
            rs = pl.ds(pl.multiple_of(r * row_chunk, row_chunk), row_chunk)
            oout_ref[rs, :] = own_out[rs, :] + sib_out[rs, :]
            return carry

        lax.fori_loop(0, n_rows_in // row_chunk, sum_in, 0)
        lax.fori_loop(0, n_rows_out // row_chunk, sum_out, 0)
        for cp in sends:
            cp.wait_send()

    vmem = pl.BlockSpec(memory_space=pltpu.VMEM)
    hbm = pl.BlockSpec(memory_space=pl.ANY)
    in_shape = (n_rows_in, n_cols)
    out_shape = (n_rows_out, n_cols)
    return pl.pallas_call(
        body, name="reduce_grads",
        out_shape=(jax.ShapeDtypeStruct(in_shape, F32), jax.ShapeDtypeStruct(out_shape, F32),
                   jax.ShapeDtypeStruct(pack.shape, F32)),
        in_specs=[hbm, hbm, vmem], out_specs=(vmem, vmem, vmem),
        scratch_shapes=[pltpu.VMEM((N_CHIPS - 1,) + in_shape, F32), pltpu.VMEM((N_CHIPS - 1,) + out_shape, F32),
                        pltpu.VMEM(in_shape, F32), pltpu.VMEM(out_shape, F32),
                        pltpu.VMEM(in_shape, F32), pltpu.VMEM(out_shape, F32),
                        pltpu.VMEM((N_DEVICES,) + pack.shape, F32),
                        pltpu.SemaphoreType.DMA((2,)),
                        pltpu.SemaphoreType.DMA((6,)), pltpu.SemaphoreType.DMA((6,)),
                        pltpu.SemaphoreType.DMA((N_DEVICES - 1,)), pltpu.SemaphoreType.DMA((N_DEVICES - 1,)),
                        pltpu.SemaphoreType.DMA((2,)), pltpu.SemaphoreType.DMA((2,))],
        compiler_params=_compiler_params(48),
    )(gwin, gwout, pack)


def _adamw(name, w, g, m, v):
    n_rows, n_cols = w.shape
    tr = 256 if n_rows % 256 == 0 else n_rows
    m_corr = 1.0 - ADAM_B1 ** ADAM_STEP
    v_corr = 1.0 - ADAM_B2 ** ADAM_STEP

    def body(w_ref, g_ref, m_ref, v_ref, d_ref, nm_ref, nv_ref):
        gv = g_ref[...]
        nm = ADAM_B1 * m_ref[...] + (1.0 - ADAM_B1) * gv
        nv = ADAM_B2 * v_ref[...] + (1.0 - ADAM_B2) * (gv * gv)
        nm_ref[...] = nm
        nv_ref[...] = nv
        d_ref[...] = -ADAM_LR * ((nm / m_corr) / (jnp.sqrt(nv / v_corr) + ADAM_EPS) + ADAM_WD * w_ref[...])

    blk = lambda: pl.BlockSpec((tr, n_cols), lambda i: (i, 0))
    shape = jax.ShapeDtypeStruct(w.shape, F32)
    return pl.pallas_call(
        body, name=name,
        grid=(n_rows // tr,),
        in_specs=[blk(), blk(), blk(), blk()], out_specs=(blk(), blk(), blk()),
        out_shape=(shape, shape, shape),
        compiler_params=_compiler_params(32, ("arbitrary",)),
    )(w, g, m, v)


def kernel(x, w_in, conv_w, w_out, ln_gain, ln_bias, loss_target, m_w_in, m_conv_w, m_w_out, m_ln_gain, m_ln_bias, v_w_in, v_conv_w, v_w_out, v_ln_gain, v_ln_bias):
    xs = x[0]
    target = loss_target[0]

    wing, woutg, cwg = _gather_weights(w_in[0], w_out[0], conv_w[0])
    conv_full = jnp.transpose(cwg, (1, 0, 2)).reshape(3, CONV_WIDTH)
    wout_full = woutg.reshape(D_MODEL, D_MODEL)

    proj, xt = _proj(xs, wing)
    mix_c = _conv_fwd(proj, conv_full)
    o, mix_a, carries = _attn_fwd(proj)
    dxa, dmix, gwout, stats = _out_ln(mix_c, mix_a, wout_full, xs, target, ln_gain, ln_bias)
    dpc, dconv = _conv_bwd(proj, dmix, conv_full)
    dpa = _attn_bwd(proj, o, dmix, carries)
    gwin = _grad_w_in(xt, dpc, dpa)
    grad_x = _grad_x(dxa, dpc, dpa, wing)

    pack = jnp.concatenate(
        [stats[0:2], jnp.pad(dconv[0:3], ((0, 0), (0, D_MODEL - CONV_WIDTH))), jnp.zeros((3, D_MODEL), F32)], axis=0)
    g_w_in, g_w_out, tot = _reduce_grads(gwin, gwout.reshape(N_CHIPS, D_MODEL // N_CHIPS, D_MODEL), pack)

    chip = 2 * lax.axis_index("x") + lax.axis_index("y")
    g_gain = tot[0:1]
    g_bias = tot[1:2]
    g_conv = lax.dynamic_slice(tot, (2, chip * LANES), (3, LANES))

    loss_local = (0.5 / D_MODEL) * jnp.sum(stats[2])
    loss = lax.psum(loss_local, ("x", "y", "c"))

    d_w_in, nm_w_in, nv_w_in = _adamw("adamw_w_in", w_in[0], g_w_in, m_w_in[0], v_w_in[0])
    d_w_out, nm_w_out, nv_w_out = _adamw("adamw_w_out", w_out[0], g_w_out, m_w_out[0], v_w_out[0])
    d_conv, nm_conv, nv_conv = _adamw("adamw_conv_w", conv_w[0], g_conv, m_conv_w[0], v_conv_w[0])
    d_gain, nm_gain, nv_gain = _adamw("adamw_ln_gain", ln_gain, g_gain, m_ln_gain, v_ln_gain)
    d_bias, nm_bias, nv_bias = _adamw("adamw_ln_bias", ln_bias, g_bias, m_ln_bias, v_ln_bias)

    lead = lambda a: a[None]
    return (loss, lead(grad_x),
            lead(g_w_in), lead(g_conv), lead(g_w_out), g_gain, g_bias,
            lead(d_w_in), lead(d_conv), lead(d_w_out), d_gain, d_bias,
            lead(nm_w_in), lead(nm_conv), lead(nm_w_out), nm_gain, nm_bias,
            lead(nv_w_in), lead(nv_conv), lead(nv_w_out), nv_gain, nv_bias)
```

```python
import functools

import jax
import jax.numpy as jnp
from jax import lax
from jax.experimental import pallas as pl
from jax.experimental.pallas import tpu as pltpu

F32 = jnp.float32
BF16 = jnp.bfloat16
MESH = pl.DeviceIdType.MESH

D_MODEL = 1024
CONV_WIDTH = 512
ATTN_WIDTH = 512
HEAD_DIM = 64
N_GROUPS = 8
N_CHIPS = 4
N_DEVICES = 8
LN_EPS = 1e-5
DEEPNORM_ALPHA = 2.0 ** 0.25
Q_SCALE = HEAD_DIM ** -0.5
ADAM_LR = 0.001
ADAM_B1 = 0.9
ADAM_B2 = 0.999
ADAM_EPS = 1e-08
ADAM_WD = 0.01
ADAM_STEP = 10

LANES = 128
SUBLANES = 8
V7X_VMEM_BYTES = 64 * 1024 * 1024
MIB = 1024 * 1024

TQ = 256
TK = 256
HEADS_PER_STEP = LANES // HEAD_DIM
CONV_ROWS = 512


def _compiler_params(vmem_mib, semantics=None):
    assert vmem_mib * MIB < V7X_VMEM_BYTES
    return pltpu.CompilerParams(dimension_semantics=semantics, vmem_limit_bytes=vmem_mib * MIB)


def _sigmoid(z):
    return 1.0 / (1.0 + jnp.exp(-z))


def _dot(a, b):
    return jnp.dot(a, b, preferred_element_type=F32)


def _dot_nt(a, b):
    return lax.dot_general(a, b, (((1,), (1,)), ((), ())), preferred_element_type=F32)


def _dot_tn(a, b):
    return lax.dot_general(a, b, (((0,), (0,)), ((), ())), preferred_element_type=F32)


def _split_bf16(a):
    hi = a.astype(BF16)
    lo = (a - hi.astype(F32)).astype(BF16)
    return hi, lo


def _truncate_to_bf16(a):
    bits = lax.bitcast_convert_type(a, jnp.uint32) & jnp.uint32(0xFFFF0000)
    return lax.bitcast_convert_type(bits, F32)


MASKED_LOGIT = -1e30


def _softplus_parts(z):
    e = jnp.exp(-jnp.abs(z))
    return jnp.maximum(z, 0.0) + jnp.log(1.0 + e)


def _gather_weights(w_in, w_out, conv_w):
    d_rows, d_cols = w_in.shape
    o_rows, o_cols = w_out.shape
    row_chunk = 128

    def body(win_ref, wout_ref, cw_ref, wing_ref, woutg_ref, cwg_ref, send_sems, recv_sems):
        x, y, c = lax.axis_index("x"), lax.axis_index("y"), lax.axis_index("c")
        me = 2 * x + y

        def cast_in(r, carry):
            rows = pl.ds(pl.multiple_of(r * row_chunk, row_chunk), row_chunk)
            wing_ref[me, rows, :] = win_ref[rows, :].astype(BF16)
            return carry

        lax.fori_loop(0, d_rows // row_chunk, cast_in, 0)

        def cast_out(r, carry):
            rows = pl.ds(pl.multiple_of(r * row_chunk, row_chunk), row_chunk)
            woutg_ref[me, rows, :] = wout_ref[rows, :].astype(BF16)
            return carry

        lax.fori_loop(0, o_rows // row_chunk, cast_out, 0)
        cwg_ref[me] = cw_ref[...]

        chips = [(1 - x, y), (x, 1 - y), (1 - x, 1 - y)]
        bufs = (wing_ref, woutg_ref, cwg_ref)

        def copy(k, a, shard, to):
            return pltpu.make_async_remote_copy(
                src_ref=bufs[a].at[shard], dst_ref=bufs[a].at[shard],
                send_sem=send_sems.at[3 * k + a], recv_sem=recv_sems.at[3 * k + a],
                device_id=to, device_id_type=MESH)

        sends = []
        for k, (px, py) in enumerate(chips):
            for a in range(3):
                cp = copy(k, a, me, (px, py, c))
                cp.start()
                sends.append(cp)
        for k, (px, py) in enumerate(chips):
            for a in range(3):
                copy(k, a, 2 * px + py, (px, py, c)).wait_recv()
        for cp in sends:
            cp.wait_send()

    vmem = pl.BlockSpec(memory_space=pltpu.VMEM)
    return pl.pallas_call(
        body, name="gather_weights",
        out_shape=(jax.ShapeDtypeStruct((N_CHIPS, d_rows, d_cols), BF16),
                   jax.ShapeDtypeStruct((N_CHIPS, o_rows, o_cols), BF16),
                   jax.ShapeDtypeStruct((N_CHIPS,) + conv_w.shape, F32)),
        in_specs=[vmem, vmem, vmem], out_specs=(vmem, vmem, vmem),
        scratch_shapes=[pltpu.SemaphoreType.DMA((9,)), pltpu.SemaphoreType.DMA((9,))],
        compiler_params=_compiler_params(32),
    )(w_in, w_out, conv_w)


def _proj(x, wing):
    seq = x.shape[0]
    tm = 512

    def body(x_ref, w_ref, p_ref, xt_ref):
        xv = x_ref[...]
        p_ref[...] = _dot(xv.astype(BF16), w_ref[...])

        @pl.when(pl.program_id(1) == 0)
        def _():
            xt_ref[...] = xv.T.astype(BF16)

    return pl.pallas_call(
        body, name="proj",
        grid=(seq // tm, N_CHIPS),
        in_specs=[pl.BlockSpec((tm, D_MODEL), lambda i, j: (i, 0)),
                  pl.BlockSpec((None, D_MODEL, D_MODEL), lambda i, j: (j, 0, 0))],
        out_specs=(pl.BlockSpec((tm, D_MODEL), lambda i, j: (i, j)),
                   pl.BlockSpec((D_MODEL, tm), lambda i, j: (0, i))),
        out_shape=(jax.ShapeDtypeStruct((seq, N_CHIPS * D_MODEL), F32),
                   jax.ShapeDtypeStruct((D_MODEL, seq), BF16)),
        compiler_params=_compiler_params(40, ("arbitrary", "arbitrary")),
    )(x, wing)


def _col_block(group, n_sub=CONV_WIDTH // LANES):
    return lambda j: (0, group * n_sub + j)


def _shift_down(ext, k, rows):
    return pltpu.roll(ext, k, 0)[SUBLANES:, :]


def _shift_up(ext, k, rows):
    return pltpu.roll(ext, rows + SUBLANES - k, 0)[:rows, :]


def _conv_fwd(proj, conv_w):
    seq = proj.shape[0]
    rows = CONV_ROWS
    n_chunks = seq // rows

    def body(b_ref, c_ref, h_ref, z_ref, w_ref, out_ref, u_s):
        u_s[0:SUBLANES, :] = jnp.zeros((SUBLANES, LANES), F32)

        def fill(r, carry):
            rs = pl.ds(pl.multiple_of(r * rows, rows), rows)
            u_s[pl.ds(pl.multiple_of(r * rows + SUBLANES, SUBLANES), rows), :] = c_ref[rs, :] * h_ref[rs, :]
            return carry

        lax.fori_loop(0, n_chunks, fill, 0)
        w = w_ref[...]

        def chunk(r, carry):
            r0 = pl.multiple_of(r * rows, rows)
            rs = pl.ds(r0, rows)
            ext = u_s[pl.ds(r0, rows + SUBLANES), :]
            u = ext[SUBLANES:, :]
            y = w[2:3, :] * u
            y = y + w[0:1, :] * _shift_down(ext, 2, rows)
            y = y + w[1:2, :] * _shift_down(ext, 1, rows)
            z = z_ref[rs, :]
            out_ref[rs, :] = ((z * _sigmoid(z)) * (b_ref[rs, :] * y)).astype(BF16)
            return carry

        lax.fori_loop(0, n_chunks, chunk, 0)

    col = lambda g: pl.BlockSpec((seq, LANES), _col_block(g))
    return pl.pallas_call(
        body, name="conv_fwd",
        grid=(CONV_WIDTH // LANES,),
        in_specs=[col(0), col(1), col(2), col(3), pl.BlockSpec((3, LANES), lambda j: (0, j))],
        out_specs=pl.BlockSpec((seq, LANES), lambda j: (0, j)),
        out_shape=jax.ShapeDtypeStruct((seq, CONV_WIDTH), BF16),
        scratch_shapes=[pltpu.VMEM((seq + SUBLANES, LANES), F32)],
        compiler_params=_compiler_params(40, ("arbitrary",)),
    )(proj, proj, proj, proj, conv_w)


def _conv_bwd(proj, dmix, conv_w):
    seq = proj.shape[0]
    rows = CONV_ROWS
    n_chunks = seq // rows

    def body(b_ref, c_ref, h_ref, z_ref, d_ref, w_ref, dp_ref, dw_ref, u_s, dy_s):
        u_s[0:SUBLANES, :] = jnp.zeros((SUBLANES, LANES), F32)
        dy_s[seq:seq + SUBLANES, :] = jnp.zeros((SUBLANES, LANES), F32)

        def fill(r, carry):
            r0 = pl.multiple_of(r * rows, rows)
            rs = pl.ds(r0, rows)
            u_s[pl.ds(pl.multiple_of(r0 + SUBLANES, SUBLANES), rows), :] = c_ref[rs, :] * h_ref[rs, :]
            z = z_ref[rs, :]
            dy_s[rs, :] = d_ref[rs, :] * (z * _sigmoid(z)) * b_ref[rs, :]
            return carry

        lax.fori_loop(0, n_chunks, fill, 0)
        w = w_ref[...]

        def chunk(r, acc):
            r0 = pl.multiple_of(r * rows, rows)
            rs = pl.ds(r0, rows)
            ext = u_s[pl.ds(r0, rows + SUBLANES), :]
            u = ext[SUBLANES:, :]
            um1 = _shift_down(ext, 1, rows)
            um2 = _shift_down(ext, 2, rows)
            y = w[2:3, :] * u
            y = y + w[0:1, :] * um2
            y = y + w[1:2, :] * um1
            z = z_ref[rs, :]
            b = b_ref[rs, :]
            dco = d_ref[rs, :]
            sg = _sigmoid(z)
            g = z * sg
            dp_ref[0, rs, :] = (dco * g * y).astype(BF16)
            dp_ref[3, rs, :] = (dco * b * y * (sg * (1.0 + z * (1.0 - sg)))).astype(BF16)
            ext_dy = dy_s[pl.ds(r0, rows + SUBLANES), :]
            dy = ext_dy[:rows, :]
            du = w[2:3, :] * dy + w[1:2, :] * _shift_up(ext_dy, 1, rows) + w[0:1, :] * _shift_up(ext_dy, 2, rows)
            dp_ref[1, rs, :] = (du * h_ref[rs, :]).astype(BF16)
            dp_ref[2, rs, :] = (du * c_ref[rs, :]).astype(BF16)
            a0, a1, a2 = acc
            return (a0 + jnp.sum(dy * um2, axis=0, keepdims=True),
                    a1 + jnp.sum(dy * um1, axis=0, keepdims=True),
                    a2 + jnp.sum(dy * u, axis=0, keepdims=True))

        zero = jnp.zeros((1, LANES), F32)
        a0, a1, a2 = lax.fori_loop(0, n_chunks, chunk, (zero, zero, zero))
        dw_ref[...] = jnp.concatenate([a0, a1, a2, jnp.zeros((SUBLANES - 3, LANES), F32)], axis=0)

    col = lambda g: pl.BlockSpec((seq, LANES), _col_block(g))
    return pl.pallas_call(
        body, name="conv_bwd",
        grid=(CONV_WIDTH // LANES,),
        in_specs=[col(0), col(1), col(2), col(3), col(0), pl.BlockSpec((3, LANES), lambda j: (0, j))],
        out_specs=(pl.BlockSpec((4, seq, LANES), lambda j: (0, 0, j)),
                   pl.BlockSpec((SUBLANES, LANES), lambda j: (0, j))),
        out_shape=(jax.ShapeDtypeStruct((4, seq, CONV_WIDTH), BF16),
                   jax.ShapeDtypeStruct((SUBLANES, CONV_WIDTH), F32)),
        scratch_shapes=[pltpu.VMEM((seq + SUBLANES, LANES), F32), pltpu.VMEM((seq + SUBLANES, LANES), F32)],
        compiler_params=_compiler_params(48, ("arbitrary",)),
    )(proj, proj, proj, proj, dmix, conv_w)


def _tri_masks():
    r = lax.broadcasted_iota(jnp.int32, (TK, TK), 0)
    s = lax.broadcasted_iota(jnp.int32, (TK, TK), 1)
    return r, s


def _head_mask():
    lane = lax.broadcasted_iota(jnp.int32, (1, LANES), 1)
    return lane < HEAD_DIM


def _tile_lanes(a):
    return jnp.tile(a, (1, TK // LANES))


def _scores(qh, kt, suffix_m, strict):
    z = _dot_nt(qh, kt)
    sp = _softplus_parts(z)
    if strict is not None:
        sp = jnp.where(strict, sp, 0.0)
    hi, lo = _split_bf16(sp)
    suf = _dot(hi, suffix_m) + _dot(lo, suffix_m)
    return z, sp, suf


def _attn_fwd(proj):
    seq = proj.shape[0]
    n_q = seq // TQ
    assert seq // TK <= LANES

    n_kb = seq // TK
    heads = range(HEADS_PER_STEP)

    def key_rows(kb):
        return pl.ds(pl.multiple_of(jnp.clip(kb, 0, n_kb - 1) * TK, TK), TK)

    def body(q_ref, k_ref, v_ref, za_ref, o_ref, mix_ref, car_ref,
             qa_s, qb_s, k_s, v_s, suffix_s, oacc_s, carry_s, cw_s, z_s, zm_s, hi_s, lo_s, w_s, rsum_s):
        head_a = _head_mask()
        r_i, s_i = _tri_masks()
        strict = r_i > s_i
        suffix_s[...] = jnp.where(strict, 1.0, 0.0).astype(BF16)
        lane = lax.broadcasted_iota(jnp.int32, (1, LANES), 1)

        def prep(r, carry):
            rs = pl.ds(pl.multiple_of(r * TQ, TQ), TQ)
            q = q_ref[rs, :] * Q_SCALE
            qa_s[rs, :] = jnp.where(head_a, q, 0.0).astype(BF16)
            qb_s[rs, :] = jnp.where(head_a, 0.0, q).astype(BF16)
            k_s[rs, :] = k_ref[rs, :].astype(BF16)
            v_s[rs, :] = v_ref[rs, :].astype(BF16)
            return carry

        lax.fori_loop(0, n_q, prep, 0)

        def q_block(qb, carry):
            rs = pl.ds(pl.multiple_of(qb * TQ, TQ), TQ)
            q_heads = (qa_s[rs, :], qb_s[rs, :])
            oacc_s[...] = jnp.zeros((TQ, LANES), F32)
            carry_s[...] = jnp.zeros((HEADS_PER_STEP, TQ, LANES), F32)
            cw_s[...] = jnp.zeros((HEADS_PER_STEP, TQ, LANES), F32)

            w_s[1] = jnp.zeros((HEADS_PER_STEP, TQ, TK), BF16)

            def stage_a(kb, slot):
                kt = k_s[key_rows(kb), :]
                for h in heads:
                    z_s[slot, h] = _dot_nt(q_heads[h], kt)

            def stage_b(slot, mask):
                for h in heads:
                    z = z_s[slot, h]
                    sp = _softplus_parts(z)
                    if mask is None:
                        zm_s[slot, h] = z - sp
                    else:
                        sp = jnp.where(mask, sp, 0.0)
                        zm_s[slot, h] = jnp.where(mask, z - sp, MASKED_LOGIT)
                    hi = _truncate_to_bf16(sp)
                    hi_s[slot, h] = hi.astype(BF16)
                    lo_s[slot, h] = (sp - hi).astype(BF16)
                    rsum_s[slot, h] = jnp.broadcast_to(jnp.sum(sp, axis=1, keepdims=True), (TQ, LANES))

            def stage_c(slot):
                return [_dot(hi_s[slot, h], suffix_s[...]) + _dot(lo_s[slot, h], suffix_s[...]) for h in heads]

            def stage_d(slot, sufs, kb):
                for h in heads:
                    car = carry_s[h]
                    cw_s[h] = jnp.where(lane == kb, car, cw_s[h])
                    w_s[slot, h] = jnp.exp(zm_s[slot, h] - (sufs[h] + _tile_lanes(car))).astype(BF16)
                    carry_s[h] = car + rsum_s[slot, h]

            def stage_e(kb, slot):
                vt = v_s[key_rows(kb), :]
                pv = [_dot(w_s[slot, h], vt) for h in heads]
                return jnp.where(head_a, pv[0], pv[1])

            stage_a(qb, 0)
            stage_a(qb - 1, 1)
            stage_b(0, strict)

            def trip(i, slot):
                sufs = stage_c(slot)
                stage_b(1 - slot, None)
                stage_a(qb - i - 2, slot)
                pv = stage_e(qb - i + 1, 1 - slot)
                stage_d(slot, sufs, qb - i)
                oacc_s[...] += pv

            def two_trips(j, c2):
                trip(2 * j, 0)
                trip(2 * j + 1, 1)
                return c2

            lax.fori_loop(0, qb // 2, two_trips, 0)

            @pl.when(qb % 2 == 1)
            def _():
                trip(qb - 1, 0)

            for last in range(2):
                @pl.when(qb % 2 == last)
                def _():
                    sufs = stage_c(last)
                    pv = stage_e(1, 1 - last)
                    stage_d(last, sufs, 0)
                    oacc_s[...] += pv + stage_e(0, last)

            o = oacc_s[...]
            o_ref[rs, :] = o
            za = za_ref[rs, :]
            mix_ref[rs, :] = ((za * _sigmoid(za)) * o).astype(BF16)
            for h in range(HEADS_PER_STEP):
                car_ref[h, rs, :] = cw_s[h]
            return carry

        lax.fori_loop(0, n_q, q_block, 0)

    col = lambda g: pl.BlockSpec((seq, LANES), _col_block(g), pipeline_mode=pl.Buffered(1))
    n_pairs = ATTN_WIDTH // LANES
    return pl.pallas_call(
        body, name="attn_fwd",
        grid=(n_pairs,),
        in_specs=[col(4), col(5), col(6), col(7)],
        out_specs=(pl.BlockSpec((seq, LANES), lambda p: (0, p)),
                   pl.BlockSpec((seq, LANES), lambda p: (0, p)),
                   pl.BlockSpec((HEADS_PER_STEP, seq, LANES), lambda p: (p, 0, 0))),
        out_shape=(jax.ShapeDtypeStruct((seq, ATTN_WIDTH), F32),
                   jax.ShapeDtypeStruct((seq, ATTN_WIDTH), BF16),
                   jax.ShapeDtypeStruct((n_pairs * HEADS_PER_STEP, seq, LANES), F32)),
        scratch_shapes=[pltpu.VMEM((seq, LANES), BF16), pltpu.VMEM((seq, LANES), BF16),
                        pltpu.VMEM((seq, LANES), BF16), pltpu.VMEM((seq, LANES), BF16),
                        pltpu.VMEM((TK, TK), BF16),
                        pltpu.VMEM((TQ, LANES), F32),
                        pltpu.VMEM((HEADS_PER_STEP, TQ, LANES), F32),
                        pltpu.VMEM((HEADS_PER_STEP, TQ, LANES), F32),
                        pltpu.VMEM((2, HEADS_PER_STEP, TQ, TK), F32),
                        pltpu.VMEM((2, HEADS_PER_STEP, TQ, TK), F32),
                        pltpu.VMEM((2, HEADS_PER_STEP, TQ, TK), BF16),
                        pltpu.VMEM((2, HEADS_PER_STEP, TQ, TK), BF16),
                        pltpu.VMEM((2, HEADS_PER_STEP, TQ, TK), BF16),
                        pltpu.VMEM((2, HEADS_PER_STEP, TQ, LANES), F32)],
        compiler_params=_compiler_params(48, ("arbitrary",)),
    )(proj, proj, proj, proj)


def _attn_bwd(proj, o, dmix, carries):
    seq = proj.shape[0]
    n_q = seq // TQ
    n_kb = seq // TK
    heads = range(HEADS_PER_STEP)

    def key_rows(kb):
        return pl.ds(pl.multiple_of(jnp.clip(kb, 0, n_kb - 1) * TK, TK), TK)

    def body(q_ref, k_ref, v_ref, za_ref, o_ref, dm_ref, car_ref, dp_ref,
             qa_s, qb_s, k_s, v_s, doa_s, dob_s, suffix_s, prefix_s, dq_s, ccar_s, dk_s, dv_s,
             z_s, zm_s, dlw_s, sg_s, hi_s, lo_s, hi2_s, lo2_s, w_s, rsum_s, diff_s):
        head_a = _head_mask()
        r_i, s_i = _tri_masks()
        strict = r_i > s_i
        diff_s[...] = r_i - s_i
        suffix_s[...] = jnp.where(strict, 1.0, 0.0).astype(BF16)
        prefix_s[...] = jnp.where(r_i < s_i, 1.0, 0.0).astype(BF16)
        lane = lax.broadcasted_iota(jnp.int32, (1, LANES), 1)

        def prep(r, carry):
            rs = pl.ds(pl.multiple_of(r * TQ, TQ), TQ)
            q = q_ref[rs, :] * Q_SCALE
            qa_s[rs, :] = jnp.where(head_a, q, 0.0).astype(BF16)
            qb_s[rs, :] = jnp.where(head_a, 0.0, q).astype(BF16)
            k_s[rs, :] = k_ref[rs, :].astype(BF16)
            v_s[rs, :] = v_ref[rs, :].astype(BF16)
            za = za_ref[rs, :]
            sg = _sigmoid(za)
            dm = dm_ref[rs, :]
            do = dm * (za * sg)
            doa_s[rs, :] = jnp.where(head_a, do, 0.0).astype(BF16)
            dob_s[rs, :] = jnp.where(head_a, 0.0, do).astype(BF16)
            dp_ref[3, rs, :] = (dm * o_ref[rs, :] * (sg * (1.0 + za * (1.0 - sg)))).astype(BF16)
            dk_s[rs, :] = jnp.zeros((TQ, LANES), F32)
            dv_s[rs, :] = jnp.zeros((TQ, LANES), F32)
            return carry

        lax.fori_loop(0, n_q, prep, 0)

        def q_block(qb, carry):
            rs = pl.ds(pl.multiple_of(qb * TQ, TQ), TQ)
            q_heads = (qa_s[rs, :], qb_s[rs, :])
            do_heads = (doa_s[rs, :], dob_s[rs, :])
            dq_s[...] = jnp.zeros((TQ, LANES), F32)
            ccar_s[...] = jnp.zeros((HEADS_PER_STEP, TQ, LANES), F32)
            hi2_s[1] = jnp.zeros((HEADS_PER_STEP, TQ, TK), BF16)
            lo2_s[1] = jnp.zeros((HEADS_PER_STEP, TQ, TK), BF16)
            w_s[1] = jnp.zeros((HEADS_PER_STEP, TQ, TK), BF16)
            dlw_s[1] = jnp.zeros((HEADS_PER_STEP, TQ, TK), F32)
            sg_s[1] = jnp.zeros((HEADS_PER_STEP, TQ, TK), F32)
            rsum_s[1] = jnp.zeros((HEADS_PER_STEP, TQ, LANES), F32)

            def stage_a(kb, slot):
                kt = k_s[key_rows(kb), :]
                for h in heads:
                    z_s[slot, h] = _dot_nt(q_heads[h], kt)

            def stage_b(slot, kb):
                bound = jnp.where(kb == qb, 0, -TK)
                mask = diff_s[...] > bound
                for h in heads:
                    z = z_s[slot, h]
                    sp = jnp.where(mask, _softplus_parts(z), 0.0)
                    zm_s[slot, h] = jnp.where(mask, z - sp, MASKED_LOGIT)
                    hi = _truncate_to_bf16(sp)
                    hi_s[slot, h] = hi.astype(BF16)
                    lo_s[slot, h] = (sp - hi).astype(BF16)

            def stage_c(slot, kb):
                vt = v_s[key_rows(kb), :]
                sufs = [_dot(hi_s[slot, h], suffix_s[...]) + _dot(lo_s[slot, h], suffix_s[...]) for h in heads]
                dws = [_dot_nt(do_heads[h], vt) for h in heads]
                return sufs, dws

            def stage_d(slot, sufs, dws, kb):
                for h in heads:
                    cin = jnp.sum(jnp.where(lane == kb, car_ref[h, rs, :], 0.0), axis=1, keepdims=True)
                    zm = zm_s[slot, h]
                    w = jnp.exp(zm - (sufs[h] + cin))
                    w_s[slot, h] = w.astype(BF16)
                    sg_s[slot, h] = jnp.exp(zm)
                    dlw = dws[h] * w
                    dlw_s[slot, h] = dlw
                    hi = _truncate_to_bf16(dlw)
                    hi2_s[slot, h] = hi.astype(BF16)
                    lo2_s[slot, h] = (dlw - hi).astype(BF16)
                    rsum_s[slot, h] = jnp.broadcast_to(jnp.sum(dlw, axis=1, keepdims=True), (TQ, LANES))

            def stage_e(slot):
                return [_dot(hi2_s[slot, h], prefix_s[...]) + _dot(lo2_s[slot, h], prefix_s[...]) for h in heads]

            def stage_f(slot, pres):
                dzs = []
                for h in heads:
                    ccar = ccar_s[h]
                    dlw = dlw_s[slot, h]
                    dz = dlw - sg_s[slot, h] * (dlw + (pres[h] + _tile_lanes(ccar)))
                    dzs.append(dz.astype(BF16))
                    ccar_s[h] = ccar + rsum_s[slot, h]
                return dzs

            def stage_g(slot, dzs, kb):
                cs = key_rows(kb)
                kt = k_s[cs, :]
                dq = [_dot(dzs[h], kt) for h in heads]
                dk = _dot_tn(dzs[0], q_heads[0]) + _dot_tn(dzs[1], q_heads[1])
                dv = _dot_tn(w_s[slot, 0], do_heads[0]) + _dot_tn(w_s[slot, 1], do_heads[1])
                dq_s[...] += jnp.where(head_a, dq[0], dq[1])
                dk_s[cs, :] += dk
                dv_s[cs, :] += dv

            stage_a(0, 0)
            stage_a(1, 1)
            stage_b(0, 0)

            def trip(t, slot):
                pres = stage_e(1 - slot)
                sufs, dws = stage_c(slot, t)
                stage_a(t + 2, slot)
                stage_b(1 - slot, t + 1)
                dzs = stage_f(1 - slot, pres)
                stage_g(1 - slot, dzs, t - 1)
                stage_d(slot, sufs, dws, t)

            def two_trips(j, c2):
                trip(2 * j, 0)
                trip(2 * j + 1, 1)
                return c2

            lax.fori_loop(0, qb // 2, two_trips, 0)

            @pl.when(qb % 2 == 1)
            def _():
                trip(qb - 1, 0)

            for last in range(2):
                @pl.when(qb % 2 == last)
                def _():
                    pres = stage_e(1 - last)
                    sufs, dws = stage_c(last, qb)
                    dzs = stage_f(1 - last, pres)
                    stage_g(1 - last, dzs, qb - 1)
                    stage_d(last, sufs, dws, qb)
                    stage_g(last, stage_f(last, stage_e(last)), qb)

            dp_ref[0, rs, :] = (dq_s[...] * Q_SCALE).astype(BF16)
            return carry

        lax.fori_loop(0, n_q, q_block, 0)

        def finish(r, carry):
            rs = pl.ds(pl.multiple_of(r * TQ, TQ), TQ)
            dp_ref[1, rs, :] = dk_s[rs, :].astype(BF16)
            dp_ref[2, rs, :] = dv_s[rs, :].astype(BF16)
            return carry

        lax.fori_loop(0, n_q, finish, 0)

    def col(g, n_sub=CONV_WIDTH // LANES):
        return pl.BlockSpec((seq, LANES), _col_block(g, n_sub), pipeline_mode=pl.Buffered(1))

    n_pairs = ATTN_WIDTH // LANES
    bf = lambda: pltpu.VMEM((seq, LANES), BF16)
    stage = lambda dtype: pltpu.VMEM((2, HEADS_PER_STEP, TQ, TK), dtype)
    return pl.pallas_call(
        body, name="attn_bwd",
        grid=(n_pairs,),
        in_specs=[col(4), col(5), col(6), col(7), col(0), col(1),
                  pl.BlockSpec((HEADS_PER_STEP, seq, LANES), lambda p: (p, 0, 0), pipeline_mode=pl.Buffered(1))],
        out_specs=pl.BlockSpec((4, seq, LANES), lambda p: (0, 0, p)),
        out_shape=jax.ShapeDtypeStruct((4, seq, ATTN_WIDTH), BF16),
        scratch_shapes=[bf(), bf(), bf(), bf(), bf(), bf(),
                        pltpu.VMEM((TK, TK), BF16), pltpu.VMEM((TK, TK), BF16),
                        pltpu.VMEM((TQ, LANES), F32),
                        pltpu.VMEM((HEADS_PER_STEP, TQ, LANES), F32),
                        pltpu.VMEM((seq, LANES), F32), pltpu.VMEM((seq, LANES), F32),
                        stage(F32), stage(F32), stage(F32), stage(F32),
                        stage(BF16), stage(BF16), stage(BF16), stage(BF16), stage(BF16),
                        pltpu.VMEM((2, HEADS_PER_STEP, TQ, LANES), F32),
                        pltpu.VMEM((TQ, TK), jnp.int32)],
        compiler_params=_compiler_params(56, ("arbitrary",)),
    )(proj, proj, proj, proj, o, dmix, carries)


def _out_ln(mix_c, mix_a, woutg, x, target, gain, bias):
    seq = x.shape[0]
    tm = 256
    inv_d = 1.0 / D_MODEL

    def body(mc_ref, ma_ref, wo_ref, x_ref, t_ref, g_ref, b_ref, dxa_ref, dmix_ref, gw_ref, st_ref):
        @pl.when(pl.program_id(0) == 0)
        def _():
            gw_ref[...] = jnp.zeros(gw_ref.shape, F32)
            st_ref[...] = jnp.zeros(st_ref.shape, F32)

        mc = mc_ref[...]
        ma = ma_ref[...]
        sub = _dot(mc, wo_ref[0:CONV_WIDTH, :]) + _dot(ma, wo_ref[CONV_WIDTH:, :])
        r = DEEPNORM_ALPHA * x_ref[...] + sub
        mu = jnp.mean(r, axis=-1, keepdims=True)
        rc = r - mu
        var = jnp.mean(rc * rc, axis=-1, keepdims=True)
        rstd = lax.rsqrt(var + LN_EPS)
        xhat = rc * rstd
        gain_v = g_ref[...]
        diff = (xhat * gain_v + b_ref[...]) - t_ref[...]
        dy = diff * inv_d
        st_ref[0:1, :] += jnp.sum(dy * xhat, axis=0, keepdims=True)
        st_ref[1:2, :] += jnp.sum(dy, axis=0, keepdims=True)
        st_ref[2:3, :] += jnp.sum(diff * diff, axis=0, keepdims=True)
        dxh = dy * gain_v
        m1 = jnp.mean(dxh, axis=-1, keepdims=True)
        m2 = jnp.mean(dxh * xhat, axis=-1, keepdims=True)
        dr = rstd * (dxh - m1 - xhat * m2)
        dxa_ref[...] = DEEPNORM_ALPHA * dr
        drb = dr.astype(BF16)
        dmix_ref[...] = _dot_nt(drb, wo_ref[...])
        gw_ref[0:CONV_WIDTH, :] += _dot_tn(mc, drb)
        gw_ref[CONV_WIDTH:, :] += _dot_tn(ma, drb)

    rows_d = lambda: pl.BlockSpec((tm, D_MODEL), lambda i: (i, 0))
    rows_h = lambda: pl.BlockSpec((tm, CONV_WIDTH), lambda i: (i, 0))
    whole = lambda shape: pl.BlockSpec(shape, lambda i: (0, 0))
    return pl.pallas_call(
        body, name="out_ln",
        grid=(seq // tm,),
        in_specs=[rows_h(), rows_h(), whole((D_MODEL, D_MODEL)), rows_d(), rows_d(),
                  whole((1, D_MODEL)), whole((1, D_MODEL))],
        out_specs=(rows_d(), rows_d(), whole((D_MODEL, D_MODEL)), whole((SUBLANES, D_MODEL))),
        out_shape=(jax.ShapeDtypeStruct((seq, D_MODEL), F32), jax.ShapeDtypeStruct((seq, D_MODEL), F32),
                   jax.ShapeDtypeStruct((D_MODEL, D_MODEL), F32), jax.ShapeDtypeStruct((SUBLANES, D_MODEL), F32)),
        compiler_params=_compiler_params(48, ("arbitrary",)),
    )(mix_c, mix_a, woutg, x, target, gain, bias)


def _group_maps():
    half = N_GROUPS // 2
    return (lambda g: jnp.minimum(g, half - 1)), (lambda g: jnp.maximum(g - half, 0))


def _grad_w_in(xt, dpc, dpa):
    seq = xt.shape[1]
    tk = 1024
    half = N_GROUPS // 2
    in_conv, in_attn = _group_maps()

    def body(xt_ref, dc_ref, da_ref, out_ref):
        g = pl.program_id(0)

        @pl.when(pl.program_id(1) == 0)
        def _():
            out_ref[...] = jnp.zeros(out_ref.shape, F32)

        @pl.when(g < half)
        def _():
            out_ref[...] += _dot(xt_ref[...], dc_ref[...])

        @pl.when(g >= half)
        def _():
            out_ref[...] += _dot(xt_ref[...], da_ref[...])

    return pl.pallas_call(
        body, name="grad_w_in",
        grid=(N_GROUPS, seq // tk),
        in_specs=[pl.BlockSpec((D_MODEL, tk), lambda g, s: (0, s)),
                  pl.BlockSpec((None, tk, CONV_WIDTH), lambda g, s: (in_conv(g), s, 0)),
                  pl.BlockSpec((None, tk, CONV_WIDTH), lambda g, s: (in_attn(g), s, 0))],
        out_specs=pl.BlockSpec((None, D_MODEL, CONV_WIDTH), lambda g, s: (g // 2, 0, g % 2)),
        out_shape=jax.ShapeDtypeStruct((N_CHIPS, D_MODEL, D_MODEL), F32),
        compiler_params=_compiler_params(40, ("arbitrary", "arbitrary")),
    )(xt, dpc, dpa)


def _grad_x(dxa, dpc, dpa, wing):
    seq = dxa.shape[0]
    tm = 512
    half = N_GROUPS // 2
    in_conv, in_attn = _group_maps()

    def body(dxa_ref, dc_ref, da_ref, w_ref, out_ref):
        g = pl.program_id(1)

        @pl.when(g == 0)
        def _():
            out_ref[...] = dxa_ref[...]

        @pl.when(g < half)
        def _():
            out_ref[...] += _dot_nt(dc_ref[...], w_ref[...])

        @pl.when(g >= half)
        def _():
            out_ref[...] += _dot_nt(da_ref[...], w_ref[...])

    return pl.pallas_call(
        body, name="grad_x",
        grid=(seq // tm, N_GROUPS),
        in_specs=[pl.BlockSpec((tm, D_MODEL), lambda i, g: (i, 0)),
                  pl.BlockSpec((None, tm, CONV_WIDTH), lambda i, g: (in_conv(g), i, 0)),
                  pl.BlockSpec((None, tm, CONV_WIDTH), lambda i, g: (in_attn(g), i, 0)),
                  pl.BlockSpec((None, D_MODEL, CONV_WIDTH), lambda i, g: (g // 2, 0, g % 2))],
        out_specs=pl.BlockSpec((tm, D_MODEL), lambda i, g: (i, 0)),
        out_shape=jax.ShapeDtypeStruct((seq, D_MODEL), F32),
        compiler_params=_compiler_params(40, ("arbitrary", "arbitrary")),
    )(dxa, dpc, dpa, wing)


def _reduce_grads(gwin, gwout, pack):
    n_rows_in, n_cols = gwin.shape[1], gwin.shape[2]
    n_rows_out = gwout.shape[1]
    row_chunk = 128

    def body(gwin_ref, gwout_ref, pack_ref, oin_ref, oout_ref, opack_ref,
             rin, rout, sib_in, sib_out, own_in, own_out, rpack,
             local_sems, send_sems, recv_sems, pack_send, pack_recv, sib_send, sib_recv):
        x, y, c = lax.axis_index("x"), lax.axis_index("y"), lax.axis_index("c")
        me = 2 * x + y
        my_id = 4 * x + 2 * y + c
        chips = [(1 - x, y), (x, 1 - y), (1 - x, 1 - y)]

        load_in = pltpu.make_async_copy(gwin_ref.at[me], own_in, local_sems.at[0])
        load_out = pltpu.make_async_copy(gwout_ref.at[me], own_out, local_sems.at[1])
        load_in.start()
        load_out.start()

        def shard_copy(k, a, shard, to):
            src = (gwin_ref, gwout_ref)[a].at[shard]
            dst = (rin, rout)[a].at[k]
            return pltpu.make_async_remote_copy(
                src_ref=src, dst_ref=dst, send_sem=send_sems.at[2 * k + a], recv_sem=recv_sems.at[2 * k + a],
                device_id=to, device_id_type=MESH)

        sends = []
        for k, (px, py) in enumerate(chips):
            for a in range(2):
                cp = shard_copy(k, a, 2 * px + py, (px, py, c))
                cp.start()
                sends.append(cp)

        rpack[my_id] = pack_ref[...]

        def pack_copy(rel, slot, to):
            return pltpu.make_async_remote_copy(
                src_ref=rpack.at[slot], dst_ref=rpack.at[slot],
                send_sem=pack_send.at[rel - 1], recv_sem=pack_recv.at[rel - 1],
                device_id=to, device_id_type=MESH)

        def related(rel):
            px = (1 - x) if rel & 4 else x
            py = (1 - y) if rel & 2 else y
            pc = (1 - c) if rel & 1 else c
            return px, py, pc

        for rel in range(1, N_DEVICES):
            cp = pack_copy(rel, my_id, related(rel))
            cp.start()
            sends.append(cp)

        load_in.wait()
        load_out.wait()
        for k, (px, py) in enumerate(chips):
            for a in range(2):
                shard_copy(k, a, me, (px, py, c)).wait_recv()

        def add_in(r, carry):
            rs = pl.ds(pl.multiple_of(r * row_chunk, row_chunk), row_chunk)
            own_in[rs, :] = ((own_in[rs, :] + rin[0, rs, :]) + rin[1, rs, :]) + rin[2, rs, :]
            return carry

        def add_out(r, carry):
            rs = pl.ds(pl.multiple_of(r * row_chunk, row_chunk), row_chunk)
            own_out[rs, :] = ((own_out[rs, :] + rout[0, rs, :]) + rout[1, rs, :]) + rout[2, rs, :]
            return carry

        lax.fori_loop(0, n_rows_in // row_chunk, add_in, 0)
        lax.fori_loop(0, n_rows_out // row_chunk, add_out, 0)

        sibling = (x, y, 1 - c)
        swap_in = pltpu.make_async_remote_copy(
            src_ref=own_in, dst_ref=sib_in, send_sem=sib_send.at[0], recv_sem=sib_recv.at[0],
            device_id=sibling, device_id_type=MESH)
        swap_out = pltpu.make_async_remote_copy(
            src_ref=own_out, dst_ref=sib_out, send_sem=sib_send.at[1], recv_sem=sib_recv.at[1],
            device_id=sibling, device_id_type=MESH)
        swap_in.start()
        swap_out.start()

        for rel in range(1, N_DEVICES):
            px, py, pc = related(rel)
            pack_copy(rel, 4 * px + 2 * py + pc, (px, py, pc)).wait_recv()
        total = rpack[0]
        for i in range(1, N_DEVICES):
            total = total + rpack[i]
        opack_ref[...] = total

        swap_in.wait()
        swap_out.wait()

        def sum_in(r, carry):
            rs = pl.ds(pl.multiple_of(r * row_chunk, row_chunk), row_chunk)
            oin_ref[rs, :] = own_in[rs, :] + sib_in[rs, :]
            return carry

        def sum_out(r, carry):
            rs = pl.ds(pl.multiple_of(r * row_chunk, row_chunk), row_chunk)
            oout_ref[rs, :] = own_out[rs, :] + sib_out[rs, :]
            return carry

        lax.fori_loop(0, n_rows_in // row_chunk, sum_in, 0)
        lax.fori_loop(0, n_rows_out // row_chunk, sum_out, 0)
        for cp in sends:
            cp.wait_send()

    vmem = pl.BlockSpec(memory_space=pltpu.VMEM)
    hbm = pl.BlockSpec(memory_space=pl.ANY)
    in_shape = (n_rows_in, n_cols)
    out_shape = (n_rows_out, n_cols)
    return pl.pallas_call(
        body, name="reduce_grads",
        out_shape=(jax.ShapeDtypeStruct(in_shape, F32), jax.ShapeDtypeStruct(out_shape, F32),
                   jax.ShapeDtypeStruct(pack.shape, F32)),
        in_specs=[hbm, hbm, vmem], out_specs=(vmem, vmem, vmem),
        scratch_shapes=[pltpu.VMEM((N_CHIPS - 1,) + in_shape, F32), pltpu.VMEM((N_CHIPS - 1,) + out_shape, F32),
                        pltpu.VMEM(in_shape, F32), pltpu.VMEM(out_shape, F32),
                        pltpu.VMEM(in_shape, F32), pltpu.VMEM(out_shape, F32),
                        pltpu.VMEM((N_DEVICES,) + pack.shape, F32),
                        pltpu.SemaphoreType.DMA((2,)),
                        pltpu.SemaphoreType.DMA((6,)), pltpu.SemaphoreType.DMA((6,)),
                        pltpu.SemaphoreType.DMA((N_DEVICES - 1,)), pltpu.SemaphoreType.DMA((N_DEVICES - 1,)),
                        pltpu.SemaphoreType.DMA((2,)), pltpu.SemaphoreType.DMA((2,))],
        compiler_params=_compiler_params(48),
    )(gwin, gwout, pack)


def _adamw(name, w, g, m, v):
    n_rows, n_cols = w.shape
    tr = 256 if n_rows % 256 == 0 else n_rows
    m_corr = 1.0 - ADAM_B1 ** ADAM_STEP
    v_corr = 1.0 - ADAM_B2 ** ADAM_STEP

    def body(w_ref, g_ref, m_ref, v_ref, d_ref, nm_ref, nv_ref):
        gv = g_ref[...]
        nm = ADAM_B1 * m_ref[...] + (1.0 - ADAM_B1) * gv
        nv = ADAM_B2 * v_ref[...] + (1.0 - ADAM_B2) * (gv * gv)
        nm_ref[...] = nm
        nv_ref[...] = nv
        d_ref[...] = -ADAM_LR * ((nm / m_corr) / (jnp.sqrt(nv / v_corr) + ADAM_EPS) + ADAM_WD * w_ref[...])

    blk = lambda: pl.BlockSpec((tr, n_cols), lambda i: (i, 0))
    shape = jax.ShapeDtypeStruct(w.shape, F32)
    return pl.pallas_call(
        body, name=name,
        grid=(n_rows // tr,),
        in_specs=[blk(), blk(), blk(), blk()], out_specs=(blk(), blk(), blk()),
        out_shape=(shape, shape, shape),
        compiler_params=_compiler_params(32, ("arbitrary",)),
    )(w, g, m, v)


def kernel(x, w_in, conv_w, w_out, ln_gain, ln_bias, loss_target, m_w_in, m_conv_w, m_w_out, m_ln_gain, m_ln_bias, v_w_in, v_conv_w, v_w_out, v_ln_gain, v_ln_bias):
    xs = x[0]
    target = loss_target[0]

    wing, woutg, cwg = _gather_weights(w_in[0], w_out[0], conv_w[0])
    conv_full = jnp.transpose(cwg, (1, 0, 2)).reshape(3, CONV_WIDTH)
    wout_full = woutg.reshape(D_MODEL, D_MODEL)

    proj, xt = _proj(xs, wing)
    mix_c = _conv_fwd(proj, conv_full)
    o, mix_a, carries = _attn_fwd(proj)
    dxa, dmix, gwout, stats = _out_ln(mix_c, mix_a, wout_full, xs, target, ln_gain, ln_bias)
    dpc, dconv = _conv_bwd(proj, dmix, conv_full)
    dpa = _attn_bwd(proj, o, dmix, carries)
    gwin = _grad_w_in(xt, dpc, dpa)
    grad_x = _grad_x(dxa, dpc, dpa, wing)

    pack = jnp.concatenate(
        [stats[0:2], jnp.pad(dconv[0:3], ((0, 0), (0, D_MODEL - CONV_WIDTH))), jnp.zeros((3, D_MODEL), F32)], axis=0)
    g_w_in, g_w_out, tot = _reduce_grads(gwin, gwout.reshape(N_CHIPS, D_MODEL // N_CHIPS, D_MODEL), pack)

    chip = 2 * lax.axis_index("x") + lax.axis_index("y")
    g_gain = tot[0:1]
    g_bias = tot[1:2]
    g_conv = lax.dynamic_slice(tot, (2, chip * LANES), (3, LANES))

    loss_local = (0.5 / D_MODEL) * jnp.sum(stats[2])
    loss = lax.psum(loss_local, ("x", "y", "c"))

    d_w_in, nm_w_in, nv_w_in = _adamw("adamw_w_in", w_in[0], g_w_in, m_w_in[0], v_w_in[0])
    d_w_out, nm_w_out, nv_w_out = _adamw("adamw_w_out", w_out[0], g_w_out, m_w_out[0], v_w_out[0])
    d_conv, nm_conv, nv_conv = _adamw("adamw_conv_w", conv_w[0], g_conv, m_conv_w[0], v_conv_w[0])
    d_gain, nm_gain, nv_gain = _adamw("adamw_ln_gain", ln_gain, g_gain, m_ln_gain, v_ln_gain)
    d_bias, nm_bias, nv_bias = _adamw("adamw_ln_bias", ln_bias, g_bias, m_ln_bias, v_ln_bias)

    lead = lambda a: a[None]
    return (loss, lead(grad_x),
            lead(g_w_in), lead(g_conv), lead(g_w_out), g_gain, g_bias,
            lead(d_w_in), lead(d_conv), lead(d_w_out), d_gain, d_bias,
            lead(nm_w_in), lead(nm_conv), lead(nm_w_out), nm_gain, nm_bias,
            lead(nv_w_in), lead(nv_conv), lead(nv_w_out), nv_gain, nv_bias)
```

```python
import functools

import jax
import jax.numpy as jnp
from jax import lax
from jax.experimental import pallas as pl
from jax.experimental.pallas import tpu as pltpu

F32 = jnp.float32
BF16 = jnp.bfloat16
MESH = pl.DeviceIdType.MESH

D_MODEL = 1024
CONV_WIDTH = 512
ATTN_WIDTH = 512
HEAD_DIM = 64
N_GROUPS = 8
N_CHIPS = 4
N_DEVICES = 8
LN_EPS = 1e-5
DEEPNORM_ALPHA = 2.0 ** 0.25
Q_SCALE = HEAD_DIM ** -0.5
ADAM_LR = 0.001
ADAM_B1 = 0.9
ADAM_B2 = 0.999
ADAM_EPS = 1e-08
ADAM_WD = 0.01
ADAM_STEP = 10

LANES = 128
SUBLANES = 8
V7X_VMEM_BYTES = 64 * 1024 * 1024
MIB = 1024 * 1024

TQ = 256
TK = 256
HEADS_PER_STEP = LANES // HEAD_DIM
CONV_ROWS = 512


def _compiler_params(vmem_mib, semantics=None):
    assert vmem_mib * MIB < V7X_VMEM_BYTES
    return pltpu.CompilerParams(dimension_semantics=semantics, vmem_limit_bytes=vmem_mib * MIB)


def _sigmoid(z):
    return 1.0 / (1.0 + jnp.exp(-z))


def _dot(a, b):
    return jnp.dot(a, b, preferred_element_type=F32)


def _dot_nt(a, b):
    return lax.dot_general(a, b, (((1,), (1,)), ((), ())), preferred_element_type=F32)


def _dot_tn(a, b):
    return lax.dot_general(a, b, (((0,), (0,)), ((), ())), preferred_element_type=F32)


def _split_bf16(a):
    hi = a.astype(BF16)
    lo = (a - hi.astype(F32)).astype(BF16)
    return hi, lo


def _truncate_to_bf16(a):
    bits = lax.bitcast_convert_type(a, jnp.uint32) & jnp.uint32(0xFFFF0000)
    return lax.bitcast_convert_type(bits, F32)


MASKED_LOGIT = -1e30


def _softplus_parts(z):
    e = jnp.exp(-jnp.abs(z))
    return jnp.maximum(z, 0.0) + jnp.log(1.0 + e)


def _gather_weights(w_in, w_out, conv_w):
    d_rows, d_cols = w_in.shape
    o_rows, o_cols = w_out.shape
    half_rows = (d_rows // 2, o_rows // 2)
    row_chunk = 128

    def body(win_ref, wout_ref, cw_ref, wing_ref, woutg_ref, cwg_ref, send_sems, recv_sems, pass_send, pass_recv):
        x, y, c = lax.axis_index("x"), lax.axis_index("y"), lax.axis_index("c")
        me = 2 * x + y
        srcs = (win_ref, wout_ref)
        bufs = (wing_ref, woutg_ref)

        def half(a, shard, which):
            rows = half_rows[a]
            return bufs[a].at[shard, pl.ds(pl.multiple_of(which * rows, rows), rows), :]

        def cast_half(which):
            for a in range(2):
                def cast(r, carry):
                    rows = pl.ds(pl.multiple_of(which * half_rows[a] + r * row_chunk, row_chunk), row_chunk)
                    bufs[a][me, rows, :] = srcs[a][rows, :].astype(BF16)
                    return carry

                lax.fori_loop(0, half_rows[a] // row_chunk, cast, 0)

        chips = [(1 - x, y), (x, 1 - y), (1 - x, 1 - y)]

        def ici_copy(k, a, shard, to):
            ref = cwg_ref.at[shard] if a == 2 else half(a, shard, c)
            return pltpu.make_async_remote_copy(
                src_ref=ref, dst_ref=ref, send_sem=send_sems.at[3 * k + a], recv_sem=recv_sems.at[3 * k + a],
                device_id=to, device_id_type=MESH)

        def pass_copy(k, a, shard, which):
            ref = half(a, shard, which)
            return pltpu.make_async_remote_copy(
                src_ref=ref, dst_ref=ref, send_sem=pass_send.at[2 * k + a], recv_sem=pass_recv.at[2 * k + a],
                device_id=(x, y, 1 - c), device_id_type=MESH)

        cast_half(c)
        cwg_ref[me] = cw_ref[...]
        sends = []
        for k, (px, py) in enumerate(chips):
            for a in range(3):
                cp = ici_copy(k, a, me, (px, py, c))
                cp.start()
                sends.append(cp)
        cast_half(1 - c)
        for k, (px, py) in enumerate(chips):
            for a in range(2):
                ici_copy(k, a, 2 * px + py, (px, py, c)).wait_recv()
                cp = pass_copy(k, a, 2 * px + py, c)
                cp.start()
                sends.append(cp)
        for k, (px, py) in enumerate(chips):
            ici_copy(k, 2, 2 * px + py, (px, py, c)).wait_recv()
            for a in range(2):
                pass_copy(k, a, 2 * px + py, 1 - c).wait_recv()
        for cp in sends:
            cp.wait_send()

    vmem = pl.BlockSpec(memory_space=pltpu.VMEM)
    return pl.pallas_call(
        body, name="gather_weights",
        out_shape=(jax.ShapeDtypeStruct((N_CHIPS, d_rows, d_cols), BF16),
                   jax.ShapeDtypeStruct((N_CHIPS, o_rows, o_cols), BF16),
                   jax.ShapeDtypeStruct((N_CHIPS,) + conv_w.shape, F32)),
        in_specs=[vmem, vmem, vmem], out_specs=(vmem, vmem, vmem),
        scratch_shapes=[pltpu.SemaphoreType.DMA((9,)), pltpu.SemaphoreType.DMA((9,)),
                        pltpu.SemaphoreType.DMA((6,)), pltpu.SemaphoreType.DMA((6,))],
        compiler_params=_compiler_params(32),
    )(w_in, w_out, conv_w)


def _proj(x, wing):
    seq = x.shape[0]
    tm = 512

    def body(x_ref, w_ref, p_ref, xt_ref):
        xv = x_ref[...]
        p_ref[...] = _dot(xv.astype(BF16), w_ref[...])

        @pl.when(pl.program_id(1) == 0)
        def _():
            xt_ref[...] = xv.T.astype(BF16)

    return pl.pallas_call(
        body, name="proj",
        grid=(seq // tm, N_CHIPS),
        in_specs=[pl.BlockSpec((tm, D_MODEL), lambda i, j: (i, 0)),
                  pl.BlockSpec((None, D_MODEL, D_MODEL), lambda i, j: (j, 0, 0))],
        out_specs=(pl.BlockSpec((tm, D_MODEL), lambda i, j: (i, j)),
                   pl.BlockSpec((D_MODEL, tm), lambda i, j: (0, i))),
        out_shape=(jax.ShapeDtypeStruct((seq, N_CHIPS * D_MODEL), F32),
                   jax.ShapeDtypeStruct((D_MODEL, seq), BF16)),
        compiler_params=_compiler_params(40, ("arbitrary", "arbitrary")),
    )(x, wing)


def _col_block(group, n_sub=CONV_WIDTH // LANES):
    return lambda j: (0, group * n_sub + j)


def _shift_down(ext, k, rows):
    return pltpu.roll(ext, k, 0)[SUBLANES:, :]


def _shift_up(ext, k, rows):
    return pltpu.roll(ext, rows + SUBLANES - k, 0)[:rows, :]


def _conv_fwd(proj, conv_w):
    seq = proj.shape[0]
    rows = CONV_ROWS
    n_chunks = seq // rows

    def body(b_ref, c_ref, h_ref, z_ref, w_ref, out_ref, u_s):
        u_s[0:SUBLANES, :] = jnp.zeros((SUBLANES, LANES), F32)

        def fill(r, carry):
            rs = pl.ds(pl.multiple_of(r * rows, rows), rows)
            u_s[pl.ds(pl.multiple_of(r * rows + SUBLANES, SUBLANES), rows), :] = c_ref[rs, :] * h_ref[rs, :]
            return carry

        lax.fori_loop(0, n_chunks, fill, 0)
        w = w_ref[...]

        def chunk(r, carry):
            r0 = pl.multiple_of(r * rows, rows)
            rs = pl.ds(r0, rows)
            ext = u_s[pl.ds(r0, rows + SUBLANES), :]
            u = ext[SUBLANES:, :]
            y = w[2:3, :] * u
            y = y + w[0:1, :] * _shift_down(ext, 2, rows)
            y = y + w[1:2, :] * _shift_down(ext, 1, rows)
            z = z_ref[rs, :]
            out_ref[rs, :] = ((z * _sigmoid(z)) * (b_ref[rs, :] * y)).astype(BF16)
            return carry

        lax.fori_loop(0, n_chunks, chunk, 0)

    col = lambda g: pl.BlockSpec((seq, LANES), _col_block(g))
    return pl.pallas_call(
        body, name="conv_fwd",
        grid=(CONV_WIDTH // LANES,),
        in_specs=[col(0), col(1), col(2), col(3), pl.BlockSpec((3, LANES), lambda j: (0, j))],
        out_specs=pl.BlockSpec((seq, LANES), lambda j: (0, j)),
        out_shape=jax.ShapeDtypeStruct((seq, CONV_WIDTH), BF16),
        scratch_shapes=[pltpu.VMEM((seq + SUBLANES, LANES), F32)],
        compiler_params=_compiler_params(40, ("arbitrary",)),
    )(proj, proj, proj, proj, conv_w)


def _conv_bwd(proj, dmix, conv_w):
    seq = proj.shape[0]
    rows = CONV_ROWS
    n_chunks = seq // rows

    def body(b_ref, c_ref, h_ref, z_ref, d_ref, w_ref, dp_ref, dw_ref, u_s, dy_s):
        u_s[0:SUBLANES, :] = jnp.zeros((SUBLANES, LANES), F32)
        dy_s[seq:seq + SUBLANES, :] = jnp.zeros((SUBLANES, LANES), F32)

        def fill(r, carry):
            r0 = pl.multiple_of(r * rows, rows)
            rs = pl.ds(r0, rows)
            u_s[pl.ds(pl.multiple_of(r0 + SUBLANES, SUBLANES), rows), :] = c_ref[rs, :] * h_ref[rs, :]
            z = z_ref[rs, :]
            dy_s[rs, :] = d_ref[rs, :] * (z * _sigmoid(z)) * b_ref[rs, :]
            return carry

        lax.fori_loop(0, n_chunks, fill, 0)
        w = w_ref[...]

        def chunk(r, acc):
            r0 = pl.multiple_of(r * rows, rows)
            rs = pl.ds(r0, rows)
            ext = u_s[pl.ds(r0, rows + SUBLANES), :]
            u = ext[SUBLANES:, :]
            um1 = _shift_down(ext, 1, rows)
            um2 = _shift_down(ext, 2, rows)
            y = w[2:3, :] * u
            y = y + w[0:1, :] * um2
            y = y + w[1:2, :] * um1
            z = z_ref[rs, :]
            b = b_ref[rs, :]
            dco = d_ref[rs, :]
            sg = _sigmoid(z)
            g = z * sg
            dp_ref[0, rs, :] = (dco * g * y).astype(BF16)
            dp_ref[3, rs, :] = (dco * b * y * (sg * (1.0 + z * (1.0 - sg)))).astype(BF16)
            ext_dy = dy_s[pl.ds(r0, rows + SUBLANES), :]
            dy = ext_dy[:rows, :]
            du = w[2:3, :] * dy + w[1:2, :] * _shift_up(ext_dy, 1, rows) + w[0:1, :] * _shift_up(ext_dy, 2, rows)
            dp_ref[1, rs, :] = (du * h_ref[rs, :]).astype(BF16)
            dp_ref[2, rs, :] = (du * c_ref[rs, :]).astype(BF16)
            a0, a1, a2 = acc
            return (a0 + jnp.sum(dy * um2, axis=0, keepdims=True),
                    a1 + jnp.sum(dy * um1, axis=0, keepdims=True),
                    a2 + jnp.sum(dy * u, axis=0, keepdims=True))

        zero = jnp.zeros((1, LANES), F32)
        a0, a1, a2 = lax.fori_loop(0, n_chunks, chunk, (zero, zero, zero))
        dw_ref[...] = jnp.concatenate([a0, a1, a2, jnp.zeros((SUBLANES - 3, LANES), F32)], axis=0)

    col = lambda g: pl.BlockSpec((seq, LANES), _col_block(g))
    return pl.pallas_call(
        body, name="conv_bwd",
        grid=(CONV_WIDTH // LANES,),
        in_specs=[col(0), col(1), col(2), col(3), col(0), pl.BlockSpec((3, LANES), lambda j: (0, j))],
        out_specs=(pl.BlockSpec((4, seq, LANES), lambda j: (0, 0, j)),
                   pl.BlockSpec((SUBLANES, LANES), lambda j: (0, j))),
        out_shape=(jax.ShapeDtypeStruct((4, seq, CONV_WIDTH), BF16),
                   jax.ShapeDtypeStruct((SUBLANES, CONV_WIDTH), F32)),
        scratch_shapes=[pltpu.VMEM((seq + SUBLANES, LANES), F32), pltpu.VMEM((seq + SUBLANES, LANES), F32)],
        compiler_params=_compiler_params(48, ("arbitrary",)),
    )(proj, proj, proj, proj, dmix, conv_w)


def _tri_masks():
    r = lax.broadcasted_iota(jnp.int32, (TK, TK), 0)
    s = lax.broadcasted_iota(jnp.int32, (TK, TK), 1)
    return r, s


def _head_mask():
    lane = lax.broadcasted_iota(jnp.int32, (1, LANES), 1)
    return lane < HEAD_DIM


def _tile_lanes(a):
    return jnp.tile(a, (1, TK // LANES))


def _scores(qh, kt, suffix_m, strict):
    z = _dot_nt(qh, kt)
    sp = _softplus_parts(z)
    if strict is not None:
        sp = jnp.where(strict, sp, 0.0)
    hi, lo = _split_bf16(sp)
    suf = _dot(hi, suffix_m) + _dot(lo, suffix_m)
    return z, sp, suf


def _attn_fwd(proj):
    seq = proj.shape[0]
    n_q = seq // TQ
    assert seq // TK <= LANES

    n_kb = seq // TK
    heads = range(HEADS_PER_STEP)

    def key_rows(kb):
        return pl.ds(pl.multiple_of(jnp.clip(kb, 0, n_kb - 1) * TK, TK), TK)

    def body(q_ref, k_ref, v_ref, za_ref, o_ref, mix_ref, car_ref,
             qa_s, qb_s, k_s, v_s, suffix_s, oacc_s, carry_s, cw_s, z_s, zm_s, hi_s, lo_s, w_s, rsum_s):
        head_a = _head_mask()
        r_i, s_i = _tri_masks()
        strict = r_i > s_i
        suffix_s[...] = jnp.where(strict, 1.0, 0.0).astype(BF16)
        lane = lax.broadcasted_iota(jnp.int32, (1, LANES), 1)

        def prep(r, carry):
            rs = pl.ds(pl.multiple_of(r * TQ, TQ), TQ)
            q = q_ref[rs, :] * Q_SCALE
            qa_s[rs, :] = jnp.where(head_a, q, 0.0).astype(BF16)
            qb_s[rs, :] = jnp.where(head_a, 0.0, q).astype(BF16)
            k_s[rs, :] = k_ref[rs, :].astype(BF16)
            v_s[rs, :] = v_ref[rs, :].astype(BF16)
            return carry

        lax.fori_loop(0, n_q, prep, 0)

        def q_block(qb, carry):
            rs = pl.ds(pl.multiple_of(qb * TQ, TQ), TQ)
            q_heads = (qa_s[rs, :], qb_s[rs, :])
            oacc_s[...] = jnp.zeros((TQ, LANES), F32)
            carry_s[...] = jnp.zeros((HEADS_PER_STEP, TQ, LANES), F32)
            cw_s[...] = jnp.zeros((HEADS_PER_STEP, TQ, LANES), F32)

            w_s[1] = jnp.zeros((HEADS_PER_STEP, TQ, TK), BF16)

            def stage_a(kb, slot):
                kt = k_s[key_rows(kb), :]
                for h in heads:
                    z_s[slot, h] = _dot_nt(q_heads[h], kt)

            def stage_b(slot, mask):
                for h in heads:
                    z = z_s[slot, h]
                    sp = _softplus_parts(z)
                    if mask is None:
                        zm_s[slot, h] = z - sp
                    else:
                        sp = jnp.where(mask, sp, 0.0)
                        zm_s[slot, h] = jnp.where(mask, z - sp, MASKED_LOGIT)
                    hi = _truncate_to_bf16(sp)
                    hi_s[slot, h] = hi.astype(BF16)
                    lo_s[slot, h] = (sp - hi).astype(BF16)
                    rsum_s[slot, h] = jnp.broadcast_to(jnp.sum(sp, axis=1, keepdims=True), (TQ, LANES))

            def stage_c(slot):
                return [_dot(hi_s[slot, h], suffix_s[...]) + _dot(lo_s[slot, h], suffix_s[...]) for h in heads]

            def stage_d(slot, sufs, kb):
                for h in heads:
                    car = carry_s[h]
                    cw_s[h] = jnp.where(lane == kb, car, cw_s[h])
                    w_s[slot, h] = jnp.exp(zm_s[slot, h] - (sufs[h] + _tile_lanes(car))).astype(BF16)
                    carry_s[h] = car + rsum_s[slot, h]

            def stage_e(kb, slot):
                vt = v_s[key_rows(kb), :]
                pv = [_dot(w_s[slot, h], vt) for h in heads]
                return jnp.where(head_a, pv[0], pv[1])

            stage_a(qb, 0)
            stage_a(qb - 1, 1)
            stage_b(0, strict)

            def trip(i, slot):
                sufs = stage_c(slot)
                stage_b(1 - slot, None)
                stage_a(qb - i - 2, slot)
                pv = stage_e(qb - i + 1, 1 - slot)
                stage_d(slot, sufs, qb - i)
                oacc_s[...] += pv

            def two_trips(j, c2):
                trip(2 * j, 0)
                trip(2 * j + 1, 1)
                return c2

            lax.fori_loop(0, qb // 2, two_trips, 0)

            @pl.when(qb % 2 == 1)
            def _():
                trip(qb - 1, 0)

            for last in range(2):
                @pl.when(qb % 2 == last)
                def _():
                    sufs = stage_c(last)
                    pv = stage_e(1, 1 - last)
                    stage_d(last, sufs, 0)
                    oacc_s[...] += pv + stage_e(0, last)

            o = oacc_s[...]
            o_ref[rs, :] = o
            za = za_ref[rs, :]
            mix_ref[rs, :] = ((za * _sigmoid(za)) * o).astype(BF16)
            for h in range(HEADS_PER_STEP):
                car_ref[h, rs, :] = cw_s[h]
            return carry

        lax.fori_loop(0, n_q, q_block, 0)

    col = lambda g: pl.BlockSpec((seq, LANES), _col_block(g), pipeline_mode=pl.Buffered(1))
    n_pairs = ATTN_WIDTH // LANES
    return pl.pallas_call(
        body, name="attn_fwd",
        grid=(n_pairs,),
        in_specs=[col(4), col(5), col(6), col(7)],
        out_specs=(pl.BlockSpec((seq, LANES), lambda p: (0, p)),
                   pl.BlockSpec((seq, LANES), lambda p: (0, p)),
                   pl.BlockSpec((HEADS_PER_STEP, seq, LANES), lambda p: (p, 0, 0))),
        out_shape=(jax.ShapeDtypeStruct((seq, ATTN_WIDTH), F32),
                   jax.ShapeDtypeStruct((seq, ATTN_WIDTH), BF16),
                   jax.ShapeDtypeStruct((n_pairs * HEADS_PER_STEP, seq, LANES), F32)),
        scratch_shapes=[pltpu.VMEM((seq, LANES), BF16), pltpu.VMEM((seq, LANES), BF16),
                        pltpu.VMEM((seq, LANES), BF16), pltpu.VMEM((seq, LANES), BF16),
                        pltpu.VMEM((TK, TK), BF16),
                        pltpu.VMEM((TQ, LANES), F32),
                        pltpu.VMEM((HEADS_PER_STEP, TQ, LANES), F32),
                        pltpu.VMEM((HEADS_PER_STEP, TQ, LANES), F32),
                        pltpu.VMEM((2, HEADS_PER_STEP, TQ, TK), F32),
                        pltpu.VMEM((2, HEADS_PER_STEP, TQ, TK), F32),
                        pltpu.VMEM((2, HEADS_PER_STEP, TQ, TK), BF16),
                        pltpu.VMEM((2, HEADS_PER_STEP, TQ, TK), BF16),
                        pltpu.VMEM((2, HEADS_PER_STEP, TQ, TK), BF16),
                        pltpu.VMEM((2, HEADS_PER_STEP, TQ, LANES), F32)],
        compiler_params=_compiler_params(48, ("arbitrary",)),
    )(proj, proj, proj, proj)


def _attn_bwd(proj, o, dmix, carries):
    seq = proj.shape[0]
    n_q = seq // TQ
    n_kb = seq // TK
    heads = range(HEADS_PER_STEP)

    def key_rows(kb):
        return pl.ds(pl.multiple_of(jnp.clip(kb, 0, n_kb - 1) * TK, TK), TK)

    def body(q_ref, k_ref, v_ref, za_ref, o_ref, dm_ref, car_ref, dp_ref,
             qa_s, qb_s, k_s, v_s, doa_s, dob_s, suffix_s, prefix_s, dq_s, ccar_s, dk_s, dv_s,
             z_s, zm_s, dlw_s, sg_s, hi_s, lo_s, hi2_s, lo2_s, w_s, rsum_s, diff_s):
        head_a = _head_mask()
        r_i, s_i = _tri_masks()
        strict = r_i > s_i
        diff_s[...] = r_i - s_i
        suffix_s[...] = jnp.where(strict, 1.0, 0.0).astype(BF16)
        prefix_s[...] = jnp.where(r_i < s_i, 1.0, 0.0).astype(BF16)
        lane = lax.broadcasted_iota(jnp.int32, (1, LANES), 1)

        def prep(r, carry):
            rs = pl.ds(pl.multiple_of(r * TQ, TQ), TQ)
            q = q_ref[rs, :] * Q_SCALE
            qa_s[rs, :] = jnp.where(head_a, q, 0.0).astype(BF16)
            qb_s[rs, :] = jnp.where(head_a, 0.0, q).astype(BF16)
            k_s[rs, :] = k_ref[rs, :].astype(BF16)
            v_s[rs, :] = v_ref[rs, :].astype(BF16)
            za = za_ref[rs, :]
            sg = _sigmoid(za)
            dm = dm_ref[rs, :]
            do = dm * (za * sg)
            doa_s[rs, :] = jnp.where(head_a, do, 0.0).astype(BF16)
            dob_s[rs, :] = jnp.where(head_a, 0.0, do).astype(BF16)
            dp_ref[3, rs, :] = (dm * o_ref[rs, :] * (sg * (1.0 + za * (1.0 - sg)))).astype(BF16)
            dk_s[rs, :] = jnp.zeros((TQ, LANES), F32)
            dv_s[rs, :] = jnp.zeros((TQ, LANES), F32)
            return carry

        lax.fori_loop(0, n_q, prep, 0)

        def q_block(qb, carry):
            rs = pl.ds(pl.multiple_of(qb * TQ, TQ), TQ)
            q_heads = (qa_s[rs, :], qb_s[rs, :])
            do_heads = (doa_s[rs, :], dob_s[rs, :])
            dq_s[...] = jnp.zeros((TQ, LANES), F32)
            ccar_s[...] = jnp.zeros((HEADS_PER_STEP, TQ, LANES), F32)
            hi2_s[1] = jnp.zeros((HEADS_PER_STEP, TQ, TK), BF16)
            lo2_s[1] = jnp.zeros((HEADS_PER_STEP, TQ, TK), BF16)
            w_s[1] = jnp.zeros((HEADS_PER_STEP, TQ, TK), BF16)
            dlw_s[1] = jnp.zeros((HEADS_PER_STEP, TQ, TK), F32)
            sg_s[1] = jnp.zeros((HEADS_PER_STEP, TQ, TK), F32)
            rsum_s[1] = jnp.zeros((HEADS_PER_STEP, TQ, LANES), F32)

            def stage_a(kb, slot):
                kt = k_s[key_rows(kb), :]
                for h in heads:
                    z_s[slot, h] = _dot_nt(q_heads[h], kt)

            def stage_b(slot, kb):
                bound = jnp.where(kb == qb, 0, -TK)
                mask = diff_s[...] > bound
                for h in heads:
                    z = z_s[slot, h]
                    sp = jnp.where(mask, _softplus_parts(z), 0.0)
                    zm_s[slot, h] = jnp.where(mask, z - sp, MASKED_LOGIT)
                    hi = _truncate_to_bf16(sp)
                    hi_s[slot, h] = hi.astype(BF16)
                    lo_s[slot, h] = (sp - hi).astype(BF16)

            def stage_c(slot, kb):
                vt = v_s[key_rows(kb), :]
                sufs = [_dot(hi_s[slot, h], suffix_s[...]) + _dot(lo_s[slot, h], suffix_s[...]) for h in heads]
                dws = [_dot_nt(do_heads[h], vt) for h in heads]
                return sufs, dws

            def stage_d(slot, sufs, dws, kb):
                for h in heads:
                    cin = jnp.sum(jnp.where(lane == kb, car_ref[h, rs, :], 0.0), axis=1, keepdims=True)
                    zm = zm_s[slot, h]
                    w = jnp.exp(zm - (sufs[h] + cin))
                    w_s[slot, h] = w.astype(BF16)
                    sg_s[slot, h] = jnp.exp(zm)
                    dlw = dws[h] * w
                    dlw_s[slot, h] = dlw
                    hi = _truncate_to_bf16(dlw)
                    hi2_s[slot, h] = hi.astype(BF16)
                    lo2_s[slot, h] = (dlw - hi).astype(BF16)
                    rsum_s[slot, h] = jnp.broadcast_to(jnp.sum(dlw, axis=1, keepdims=True), (TQ, LANES))

            def stage_e(slot):
                return [_dot(hi2_s[slot, h], prefix_s[...]) + _dot(lo2_s[slot, h], prefix_s[...]) for h in heads]

            def stage_f(slot, pres):
                dzs = []
                for h in heads:
                    ccar = ccar_s[h]
                    dlw = dlw_s[slot, h]
                    dz = dlw - sg_s[slot, h] * (dlw + (pres[h] + _tile_lanes(ccar)))
                    dzs.append(dz.astype(BF16))
                    ccar_s[h] = ccar + rsum_s[slot, h]
                return dzs

            def stage_g(slot, dzs, kb):
                cs = key_rows(kb)
                kt = k_s[cs, :]
                dq = [_dot(dzs[h], kt) for h in heads]
                dk = _dot_tn(dzs[0], q_heads[0]) + _dot_tn(dzs[1], q_heads[1])
                dv = _dot_tn(w_s[slot, 0], do_heads[0]) + _dot_tn(w_s[slot, 1], do_heads[1])
                dq_s[...] += jnp.where(head_a, dq[0], dq[1])
                dk_s[cs, :] += dk
                dv_s[cs, :] += dv

            stage_a(0, 0)
            stage_a(1, 1)
            stage_b(0, 0)

            def trip(t, slot):
                pres = stage_e(1 - slot)
                sufs, dws = stage_c(slot, t)
                stage_a(t + 2, slot)
                stage_b(1 - slot, t + 1)
                dzs = stage_f(1 - slot, pres)
                stage_g(1 - slot, dzs, t - 1)
                stage_d(slot, sufs, dws, t)

            def two_trips(j, c2):
                trip(2 * j, 0)
                trip(2 * j + 1, 1)
                return c2

            lax.fori_loop(0, qb // 2, two_trips, 0)

            @pl.when(qb % 2 == 1)
            def _():
                trip(qb - 1, 0)

            for last in range(2):
                @pl.when(qb % 2 == last)
                def _():
                    pres = stage_e(1 - last)
                    sufs, dws = stage_c(last, qb)
                    dzs = stage_f(1 - last, pres)
                    stage_g(1 - last, dzs, qb - 1)
                    stage_d(last, sufs, dws, qb)
                    stage_g(last, stage_f(last, stage_e(last)), qb)

            dp_ref[0, rs, :] = (dq_s[...] * Q_SCALE).astype(BF16)
            return carry

        lax.fori_loop(0, n_q, q_block, 0)

        def finish(r, carry):
            rs = pl.ds(pl.multiple_of(r * TQ, TQ), TQ)
            dp_ref[1, rs, :] = dk_s[rs, :].astype(BF16)
            dp_ref[2, rs, :] = dv_s[rs, :].astype(BF16)
            return carry

        lax.fori_loop(0, n_q, finish, 0)

    def col(g, n_sub=CONV_WIDTH // LANES):
        return pl.BlockSpec((seq, LANES), _col_block(g, n_sub), pipeline_mode=pl.Buffered(1))

    n_pairs = ATTN_WIDTH // LANES
    bf = lambda: pltpu.VMEM((seq, LANES), BF16)
    stage = lambda dtype: pltpu.VMEM((2, HEADS_PER_STEP, TQ, TK), dtype)
    return pl.pallas_call(
        body, name="attn_bwd",
        grid=(n_pairs,),
        in_specs=[col(4), col(5), col(6), col(7), col(0), col(1),
                  pl.BlockSpec((HEADS_PER_STEP, seq, LANES), lambda p: (p, 0, 0), pipeline_mode=pl.Buffered(1))],
        out_specs=pl.BlockSpec((4, seq, LANES), lambda p: (0, 0, p)),
        out_shape=jax.ShapeDtypeStruct((4, seq, ATTN_WIDTH), BF16),
        scratch_shapes=[bf(), bf(), bf(), bf(), bf(), bf(),
                        pltpu.VMEM((TK, TK), BF16), pltpu.VMEM((TK, TK), BF16),
                        pltpu.VMEM((TQ, LANES), F32),
                        pltpu.VMEM((HEADS_PER_STEP, TQ, LANES), F32),
                        pltpu.VMEM((seq, LANES), F32), pltpu.VMEM((seq, LANES), F32),
                        stage(F32), stage(F32), stage(F32), stage(F32),
                        stage(BF16), stage(BF16), stage(BF16), stage(BF16), stage(BF16),
                        pltpu.VMEM((2, HEADS_PER_STEP, TQ, LANES), F32),
                        pltpu.VMEM((TQ, TK), jnp.int32)],
        compiler_params=_compiler_params(56, ("arbitrary",)),
    )(proj, proj, proj, proj, o, dmix, carries)


def _out_ln(mix_c, mix_a, woutg, x, target, gain, bias):
    seq = x.shape[0]
    tm = 256
    inv_d = 1.0 / D_MODEL

    def body(mc_ref, ma_ref, wo_ref, x_ref, t_ref, g_ref, b_ref, dxa_ref, dmix_ref, gw_ref, st_ref):
        @pl.when(pl.program_id(0) == 0)
        def _():
            gw_ref[...] = jnp.zeros(gw_ref.shape, F32)
            st_ref[...] = jnp.zeros(st_ref.shape, F32)

        mc = mc_ref[...]
        ma = ma_ref[...]
        sub = _dot(mc, wo_ref[0:CONV_WIDTH, :]) + _dot(ma, wo_ref[CONV_WIDTH:, :])
        r = DEEPNORM_ALPHA * x_ref[...] + sub
        mu = jnp.mean(r, axis=-1, keepdims=True)
        rc = r - mu
        var = jnp.mean(rc * rc, axis=-1, keepdims=True)
        rstd = lax.rsqrt(var + LN_EPS)
        xhat = rc * rstd
        gain_v = g_ref[...]
        diff = (xhat * gain_v + b_ref[...]) - t_ref[...]
        dy = diff * inv_d
        st_ref[0:1, :] += jnp.sum(dy * xhat, axis=0, keepdims=True)
        st_ref[1:2, :] += jnp.sum(dy, axis=0, keepdims=True)
        st_ref[2:3, :] += jnp.sum(diff * diff, axis=0, keepdims=True)
        dxh = dy * gain_v
        m1 = jnp.mean(dxh, axis=-1, keepdims=True)
        m2 = jnp.mean(dxh * xhat, axis=-1, keepdims=True)
        dr = rstd * (dxh - m1 - xhat * m2)
        dxa_ref[...] = DEEPNORM_ALPHA * dr
        drb = dr.astype(BF16)
        dmix_ref[...] = _dot_nt(drb, wo_ref[...])
        gw_ref[0:CONV_WIDTH, :] += _dot_tn(mc, drb)
        gw_ref[CONV_WIDTH:, :] += _dot_tn(ma, drb)

    rows_d = lambda: pl.BlockSpec((tm, D_MODEL), lambda i: (i, 0))
    rows_h = lambda: pl.BlockSpec((tm, CONV_WIDTH), lambda i: (i, 0))
    whole = lambda shape: pl.BlockSpec(shape, lambda i: (0, 0))
    return pl.pallas_call(
        body, name="out_ln",
        grid=(seq // tm,),
        in_specs=[rows_h(), rows_h(), whole((D_MODEL, D_MODEL)), rows_d(), rows_d(),
                  whole((1, D_MODEL)), whole((1, D_MODEL))],
        out_specs=(rows_d(), rows_d(), whole((D_MODEL, D_MODEL)), whole((SUBLANES, D_MODEL))),
        out_shape=(jax.ShapeDtypeStruct((seq, D_MODEL), F32), jax.ShapeDtypeStruct((seq, D_MODEL), F32),
                   jax.ShapeDtypeStruct((D_MODEL, D_MODEL), F32), jax.ShapeDtypeStruct((SUBLANES, D_MODEL), F32)),
        compiler_params=_compiler_params(48, ("arbitrary",)),
    )(mix_c, mix_a, woutg, x, target, gain, bias)


def _group_maps():
    half = N_GROUPS // 2
    return (lambda g: jnp.minimum(g, half - 1)), (lambda g: jnp.maximum(g - half, 0))


def _grad_w_in(xt, dpc, dpa):
    seq = xt.shape[1]
    tk = 1024
    half = N_GROUPS // 2
    in_conv, in_attn = _group_maps()

    def body(xt_ref, dc_ref, da_ref, out_ref):
        g = pl.program_id(0)

        @pl.when(pl.program_id(1) == 0)
        def _():
            out_ref[...] = jnp.zeros(out_ref.shape, F32)

        @pl.when(g < half)
        def _():
            out_ref[...] += _dot(xt_ref[...], dc_ref[...])

        @pl.when(g >= half)
        def _():
            out_ref[...] += _dot(xt_ref[...], da_ref[...])

    return pl.pallas_call(
        body, name="grad_w_in",
        grid=(N_GROUPS, seq // tk),
        in_specs=[pl.BlockSpec((D_MODEL, tk), lambda g, s: (0, s)),
                  pl.BlockSpec((None, tk, CONV_WIDTH), lambda g, s: (in_conv(g), s, 0)),
                  pl.BlockSpec((None, tk, CONV_WIDTH), lambda g, s: (in_attn(g), s, 0))],
        out_specs=pl.BlockSpec((None, D_MODEL, CONV_WIDTH), lambda g, s: (g // 2, 0, g % 2)),
        out_shape=jax.ShapeDtypeStruct((N_CHIPS, D_MODEL, D_MODEL), F32),
        compiler_params=_compiler_params(40, ("arbitrary", "arbitrary")),
    )(xt, dpc, dpa)


def _grad_x(dxa, dpc, dpa, wing):
    seq = dxa.shape[0]
    tm = 512
    half = N_GROUPS // 2
    in_conv, in_attn = _group_maps()

    def body(dxa_ref, dc_ref, da_ref, w_ref, out_ref):
        g = pl.program_id(1)

        @pl.when(g == 0)
        def _():
            out_ref[...] = dxa_ref[...]

        @pl.when(g < half)
        def _():
            out_ref[...] += _dot_nt(dc_ref[...], w_ref[...])

        @pl.when(g >= half)
        def _():
            out_ref[...] += _dot_nt(da_ref[...], w_ref[...])

    return pl.pallas_call(
        body, name="grad_x",
        grid=(seq // tm, N_GROUPS),
        in_specs=[pl.BlockSpec((tm, D_MODEL), lambda i, g: (i, 0)),
                  pl.BlockSpec((None, tm, CONV_WIDTH), lambda i, g: (in_conv(g), i, 0)),
                  pl.BlockSpec((None, tm, CONV_WIDTH), lambda i, g: (in_attn(g), i, 0)),
                  pl.BlockSpec((None, D_MODEL, CONV_WIDTH), lambda i, g: (g // 2, 0, g % 2))],
        out_specs=pl.BlockSpec((tm, D_MODEL), lambda i, g: (i, 0)),
        out_shape=jax.ShapeDtypeStruct((seq, D_MODEL), F32),
        compiler_params=_compiler_params(40, ("arbitrary", "arbitrary")),
    )(dxa, dpc, dpa, wing)


def _reduce_grads(gwin, gwout, pack):
    n_shards, n_rows_in, n_cols = gwin.shape
    n_rows_out = gwout.shape[1]
    half_rows = (n_rows_in // 2, n_rows_out // 2)
    row_chunk = 128

    def body(gwin_ref, gwout_ref, pack_ref, oin_ref, oout_ref, opack_ref,
             mine_in, mine_out, sib_in, sib_out, rin, rout, rpack,
             local_sems, sib_send, sib_recv, send_sems, recv_sems, pack_send, pack_recv, fin_send, fin_recv):
        x, y, c = lax.axis_index("x"), lax.axis_index("y"), lax.axis_index("c")
        me = 2 * x + y
        my_id = 4 * x + 2 * y + c
        chips = [(1 - x, y), (x, 1 - y), (1 - x, 1 - y)]
        sibling = (x, y, 1 - c)
        partial = (gwin_ref, gwout_ref)
        mine = (mine_in, mine_out)
        from_sib = (sib_in, sib_out)
        from_chips = (rin, rout)
        result = (oin_ref, oout_ref)

        def half(a, which):
            rows = half_rows[a]
            return pl.ds(pl.multiple_of(which * rows, rows), rows)

        sends = []
        loads = []
        for a in range(2):
            ld = pltpu.make_async_copy(partial[a].at[:, half(a, c), :], mine[a], local_sems.at[a])
            ld.start()
            loads.append(ld)
            cp = pltpu.make_async_remote_copy(
                src_ref=partial[a].at[:, half(a, 1 - c), :], dst_ref=from_sib[a],
                send_sem=sib_send.at[a], recv_sem=sib_recv.at[a], device_id=sibling, device_id_type=MESH)
            cp.start()
            sends.append(cp)

        rpack[my_id] = pack_ref[...]

        def pack_copy(rel, slot, to):
            return pltpu.make_async_remote_copy(
                src_ref=rpack.at[slot], dst_ref=rpack.at[slot],
                send_sem=pack_send.at[rel - 1], recv_sem=pack_recv.at[rel - 1],
                device_id=to, device_id_type=MESH)

        def related(rel):
            px = (1 - x) if rel & 4 else x
            py = (1 - y) if rel & 2 else y
            pc = (1 - c) if rel & 1 else c
            return px, py, pc

        for rel in range(1, N_DEVICES):
            cp = pack_copy(rel, my_id, related(rel))
            cp.start()
            sends.append(cp)

        for ld in loads:
            ld.wait()
        for a in range(2):
            pltpu.make_async_remote_copy(
                src_ref=partial[a].at[:, half(a, 1 - c), :], dst_ref=from_sib[a],
                send_sem=sib_send.at[a], recv_sem=sib_recv.at[a], device_id=sibling, device_id_type=MESH).wait_recv()

        def chip_sum(a, shard):
            def add(r, carry):
                rs = pl.ds(pl.multiple_of(r * row_chunk, row_chunk), row_chunk)
                mine[a][shard, rs, :] = mine[a][shard, rs, :] + from_sib[a][shard, rs, :]
                return carry

            lax.fori_loop(0, half_rows[a] // row_chunk, add, 0)

        def shard_copy(k, a, shard, to):
            return pltpu.make_async_remote_copy(
                src_ref=mine[a].at[shard], dst_ref=from_chips[a].at[k],
                send_sem=send_sems.at[2 * k + a], recv_sem=recv_sems.at[2 * k + a],
                device_id=to, device_id_type=MESH)

        for k, (px, py) in enumerate(chips):
            for a in range(2):
                chip_sum(a, 2 * px + py)
                cp = shard_copy(k, a, 2 * px + py, (px, py, c))
                cp.start()
                sends.append(cp)
        for a in range(2):
            chip_sum(a, me)

        for rel in range(1, N_DEVICES):
            px, py, pc = related(rel)
            pack_copy(rel, 4 * px + 2 * py + pc, (px, py, pc)).wait_recv()
        total = rpack[0]
        for i in range(1, N_DEVICES):
            total = total + rpack[i]
        opack_ref[...] = total

        for k, (px, py) in enumerate(chips):
            for a in range(2):
                shard_copy(k, a, me, (px, py, c)).wait_recv()

        def finish(a):
            def add(r, carry):
                rs = pl.ds(pl.multiple_of(r * row_chunk, row_chunk), row_chunk)
                dst = pl.ds(pl.multiple_of(c * half_rows[a] + r * row_chunk, row_chunk), row_chunk)
                result[a][dst, :] = ((mine[a][me, rs, :] + from_chips[a][0, rs, :])
                                     + from_chips[a][1, rs, :]) + from_chips[a][2, rs, :]
                return carry

            lax.fori_loop(0, half_rows[a] // row_chunk, add, 0)

        def final_copy(a, which):
            ref = result[a].at[half(a, which), :]
            return pltpu.make_async_remote_copy(
                src_ref=ref, dst_ref=ref, send_sem=fin_send.at[a], recv_sem=fin_recv.at[a],
                device_id=sibling, device_id_type=MESH)

        for a in range(2):
            finish(a)
            cp = final_copy(a, c)
            cp.start()
            sends.append(cp)
        for a in range(2):
            final_copy(a, 1 - c).wait_recv()
        for cp in sends:
            cp.wait_send()

    vmem = pl.BlockSpec(memory_space=pltpu.VMEM)
    hbm = pl.BlockSpec(memory_space=pl.ANY)
    in_shape = (n_rows_in, n_cols)
    out_shape = (n_rows_out, n_cols)
    half_in = (half_rows[0], n_cols)
    half_out = (half_rows[1], n_cols)
    return pl.pallas_call(
        body, name="reduce_grads",
        out_shape=(jax.ShapeDtypeStruct(in_shape, F32), jax.ShapeDtypeStruct(out_shape, F32),
                   jax.ShapeDtypeStruct(pack.shape, F32)),
        in_specs=[hbm, hbm, vmem], out_specs=(vmem, vmem, vmem),
        scratch_shapes=[pltpu.VMEM((n_shards,) + half_in, F32), pltpu.VMEM((n_shards,) + half_out, F32),
                        pltpu.VMEM((n_shards,) + half_in, F32), pltpu.VMEM((n_shards,) + half_out, F32),
                        pltpu.VMEM((N_CHIPS - 1,) + half_in, F32), pltpu.VMEM((N_CHIPS - 1,) + half_out, F32),
                        pltpu.VMEM((N_DEVICES,) + pack.shape, F32),
                        pltpu.SemaphoreType.DMA((2,)),
                        pltpu.SemaphoreType.DMA((2,)), pltpu.SemaphoreType.DMA((2,)),
                        pltpu.SemaphoreType.DMA((6,)), pltpu.SemaphoreType.DMA((6,)),
                        pltpu.SemaphoreType.DMA((N_DEVICES - 1,)), pltpu.SemaphoreType.DMA((N_DEVICES - 1,)),
                        pltpu.SemaphoreType.DMA((2,)), pltpu.SemaphoreType.DMA((2,))],
        compiler_params=_compiler_params(48),
    )(gwin, gwout, pack)


def _adamw(name, w, g, m, v):
    n_rows, n_cols = w.shape
    tr = 256 if n_rows % 256 == 0 else n_rows
    m_corr = 1.0 - ADAM_B1 ** ADAM_STEP
    v_corr = 1.0 - ADAM_B2 ** ADAM_STEP

    def body(w_ref, g_ref, m_ref, v_ref, d_ref, nm_ref, nv_ref):
        gv = g_ref[...]
        nm = ADAM_B1 * m_ref[...] + (1.0 - ADAM_B1) * gv
        nv = ADAM_B2 * v_ref[...] + (1.0 - ADAM_B2) * (gv * gv)
        nm_ref[...] = nm
        nv_ref[...] = nv
        d_ref[...] = -ADAM_LR * ((nm / m_corr) / (jnp.sqrt(nv / v_corr) + ADAM_EPS) + ADAM_WD * w_ref[...])

    blk = lambda: pl.BlockSpec((tr, n_cols), lambda i: (i, 0))
    shape = jax.ShapeDtypeStruct(w.shape, F32)
    return pl.pallas_call(
        body, name=name,
        grid=(n_rows // tr,),
        in_specs=[blk(), blk(), blk(), blk()], out_specs=(blk(), blk(), blk()),
        out_shape=(shape, shape, shape),
        compiler_params=_compiler_params(32, ("arbitrary",)),
    )(w, g, m, v)


def kernel(x, w_in, conv_w, w_out, ln_gain, ln_bias, loss_target, m_w_in, m_conv_w, m_w_out, m_ln_gain, m_ln_bias, v_w_in, v_conv_w, v_w_out, v_ln_gain, v_ln_bias):
    xs = x[0]
    target = loss_target[0]

    wing, woutg, cwg = _gather_weights(w_in[0], w_out[0], conv_w[0])
    conv_full = jnp.transpose(cwg, (1, 0, 2)).reshape(3, CONV_WIDTH)
    wout_full = woutg.reshape(D_MODEL, D_MODEL)

    proj, xt = _proj(xs, wing)
    mix_c = _conv_fwd(proj, conv_full)
    o, mix_a, carries = _attn_fwd(proj)
    dxa, dmix, gwout, stats = _out_ln(mix_c, mix_a, wout_full, xs, target, ln_gain, ln_bias)
    dpc, dconv = _conv_bwd(proj, dmix, conv_full)
    dpa = _attn_bwd(proj, o, dmix, carries)
    gwin = _grad_w_in(xt, dpc, dpa)
    grad_x = _grad_x(dxa, dpc, dpa, wing)

    pack = jnp.concatenate(
        [stats[0:2], jnp.pad(dconv[0:3], ((0, 0), (0, D_MODEL - CONV_WIDTH))), jnp.zeros((3, D_MODEL), F32)], axis=0)
    g_w_in, g_w_out, tot = _reduce_grads(gwin, gwout.reshape(N_CHIPS, D_MODEL // N_CHIPS, D_MODEL), pack)

    chip = 2 * lax.axis_index("x") + lax.axis_index("y")
    g_gain = tot[0:1]
    g_bias = tot[1:2]
    g_conv = lax.dynamic_slice(tot, (2, chip * LANES), (3, LANES))

    loss_local = (0.5 / D_MODEL) * jnp.sum(stats[2])
    loss = lax.psum(loss_local, ("x", "y", "c"))

    d_w_in, nm_w_in, nv_w_in = _adamw("adamw_w_in", w_in[0], g_w_in, m_w_in[0], v_w_in[0])
    d_w_out, nm_w_out, nv_w_out = _adamw("adamw_w_out", w_out[0], g_w_out, m_w_out[0], v_w_out[0])
    d_conv, nm_conv, nv_conv = _adamw("adamw_conv_w", conv_w[0], g_conv, m_conv_w[0], v_conv_w[0])
    d_gain, nm_gain, nv_gain = _adamw("adamw_ln_gain", ln_gain, g_gain, m_ln_gain, v_ln_gain)
    d_bias, nm_bias, nv_bias = _adamw("adamw_ln_bias", ln_bias, g_bias, m_ln_bias, v_ln_bias)

    lead = lambda a: a[None]
    return (loss, lead(grad_x),
            lead(g_w_in), lead(g_conv), lead(g_w_out), g_gain, g_bias,
            lead(d_w_in), lead(d_conv), lead(d_w_out), d_gain, d_bias,
            lead(nm_w_in), lead(nm_conv), lead(nm_w_out), nm_gain, nm_bias,
            lead(nv_w_in), lead(nv_conv), lead(nv_w_out), nv_gain, nv_bias)
```

```python
import functools

import jax
import jax.numpy as jnp
from jax import lax
from jax.experimental import pallas as pl
from jax.experimental.pallas import tpu as pltpu

F32 = jnp.float32
BF16 = jnp.bfloat16
MESH = pl.DeviceIdType.MESH

D_MODEL = 1024
CONV_WIDTH = 512
ATTN_WIDTH = 512
HEAD_DIM = 64
N_GROUPS = 8
N_CHIPS = 4
N_DEVICES = 8
LN_EPS = 1e-5
DEEPNORM_ALPHA = 2.0 ** 0.25
Q_SCALE = HEAD_DIM ** -0.5
ADAM_LR = 0.001
ADAM_B1 = 0.9
ADAM_B2 = 0.999
ADAM_EPS = 1e-08
ADAM_WD = 0.01
ADAM_STEP = 10

LANES = 128
SUBLANES = 8
V7X_VMEM_BYTES = 64 * 1024 * 1024
MIB = 1024 * 1024

TQ = 256
TK = 256
HEADS_PER_STEP = LANES // HEAD_DIM
CONV_ROWS = 512


def _compiler_params(vmem_mib, semantics=None):
    assert vmem_mib * MIB < V7X_VMEM_BYTES
    return pltpu.CompilerParams(dimension_semantics=semantics, vmem_limit_bytes=vmem_mib * MIB)


def _sigmoid(z):
    return 1.0 / (1.0 + jnp.exp(-z))


def _dot(a, b):
    return jnp.dot(a, b, preferred_element_type=F32)


def _dot_nt(a, b):
    return lax.dot_general(a, b, (((1,), (1,)), ((), ())), preferred_element_type=F32)


def _dot_tn(a, b):
    return lax.dot_general(a, b, (((0,), (0,)), ((), ())), preferred_element_type=F32)


def _split_bf16(a):
    hi = a.astype(BF16)
    lo = (a - hi.astype(F32)).astype(BF16)
    return hi, lo


def _truncate_to_bf16(a):
    bits = lax.bitcast_convert_type(a, jnp.uint32) & jnp.uint32(0xFFFF0000)
    return lax.bitcast_convert_type(bits, F32)


MASKED_LOGIT = -1e30


def _softplus_parts(z):
    e = jnp.exp(-jnp.abs(z))
    return jnp.maximum(z, 0.0) + jnp.log(1.0 + e)


def _gather_weights(w_in, w_out, conv_w):
    d_rows, d_cols = w_in.shape
    o_rows, o_cols = w_out.shape
    half_rows = (d_rows // 2, o_rows // 2)
    row_chunk = 128

    def body(win_ref, wout_ref, cw_ref, wing_ref, woutg_ref, cwg_ref, send_sems, recv_sems, pass_send, pass_recv):
        x, y, c = lax.axis_index("x"), lax.axis_index("y"), lax.axis_index("c")
        me = 2 * x + y
        srcs = (win_ref, wout_ref)
        bufs = (wing_ref, woutg_ref)

        def half(a, shard, which):
            rows = half_rows[a]
            return bufs[a].at[shard, pl.ds(pl.multiple_of(which * rows, rows), rows), :]

        def cast_half(which):
            for a in range(2):
                def cast(r, carry):
                    rows = pl.ds(pl.multiple_of(which * half_rows[a] + r * row_chunk, row_chunk), row_chunk)
                    bufs[a][me, rows, :] = srcs[a][rows, :].astype(BF16)
                    return carry

                lax.fori_loop(0, half_rows[a] // row_chunk, cast, 0)

        chips = [(1 - x, y), (x, 1 - y), (1 - x, 1 - y)]

        def ici_copy(k, a, shard, to):
            ref = cwg_ref.at[shard] if a == 2 else half(a, shard, c)
            return pltpu.make_async_remote_copy(
                src_ref=ref, dst_ref=ref, send_sem=send_sems.at[3 * k + a], recv_sem=recv_sems.at[3 * k + a],
                device_id=to, device_id_type=MESH)

        def pass_copy(k, a, shard, which):
            ref = half(a, shard, which)
            return pltpu.make_async_remote_copy(
                src_ref=ref, dst_ref=ref, send_sem=pass_send.at[2 * k + a], recv_sem=pass_recv.at[2 * k + a],
                device_id=(x, y, 1 - c), device_id_type=MESH)

        cast_half(c)
        cwg_ref[me] = cw_ref[...]
        sends = []
        for k, (px, py) in enumerate(chips):
            for a in range(3):
                cp = ici_copy(k, a, me, (px, py, c))
                cp.start()
                sends.append(cp)
        cast_half(1 - c)
        for k, (px, py) in enumerate(chips):
            for a in range(2):
                ici_copy(k, a, 2 * px + py, (px, py, c)).wait_recv()
                cp = pass_copy(k, a, 2 * px + py, c)
                cp.start()
                sends.append(cp)
        for k, (px, py) in enumerate(chips):
            ici_copy(k, 2, 2 * px + py, (px, py, c)).wait_recv()
            for a in range(2):
                pass_copy(k, a, 2 * px + py, 1 - c).wait_recv()
        for cp in sends:
            cp.wait_send()

    vmem = pl.BlockSpec(memory_space=pltpu.VMEM)
    return pl.pallas_call(
        body, name="gather_weights",
        out_shape=(jax.ShapeDtypeStruct((N_CHIPS, d_rows, d_cols), BF16),
                   jax.ShapeDtypeStruct((N_CHIPS, o_rows, o_cols), BF16),
                   jax.ShapeDtypeStruct((N_CHIPS,) + conv_w.shape, F32)),
        in_specs=[vmem, vmem, vmem], out_specs=(vmem, vmem, vmem),
        scratch_shapes=[pltpu.SemaphoreType.DMA((9,)), pltpu.SemaphoreType.DMA((9,)),
                        pltpu.SemaphoreType.DMA((6,)), pltpu.SemaphoreType.DMA((6,))],
        compiler_params=_compiler_params(32),
    )(w_in, w_out, conv_w)


def _proj(x, wing):
    seq = x.shape[0]
    tm = 512

    def body(x_ref, w_ref, p_ref, xt_ref):
        xv = x_ref[...]
        p_ref[...] = _dot(xv.astype(BF16), w_ref[...])

        @pl.when(pl.program_id(1) == 0)
        def _():
            xt_ref[...] = xv.T.astype(BF16)

    return pl.pallas_call(
        body, name="proj",
        grid=(seq // tm, N_CHIPS),
        in_specs=[pl.BlockSpec((tm, D_MODEL), lambda i, j: (i, 0)),
                  pl.BlockSpec((None, D_MODEL, D_MODEL), lambda i, j: (j, 0, 0))],
        out_specs=(pl.BlockSpec((tm, D_MODEL), lambda i, j: (i, j)),
                   pl.BlockSpec((D_MODEL, tm), lambda i, j: (0, i))),
        out_shape=(jax.ShapeDtypeStruct((seq, N_CHIPS * D_MODEL), F32),
                   jax.ShapeDtypeStruct((D_MODEL, seq), BF16)),
        compiler_params=_compiler_params(40, ("arbitrary", "arbitrary")),
    )(x, wing)


def _col_block(group, n_sub=CONV_WIDTH // LANES):
    return lambda j: (0, group * n_sub + j)


def _shift_down(ext, k, rows):
    return pltpu.roll(ext, k, 0)[SUBLANES:, :]


def _shift_up(ext, k, rows):
    return pltpu.roll(ext, rows + SUBLANES - k, 0)[:rows, :]


def _conv_fwd(proj, conv_w):
    seq = proj.shape[0]
    rows = CONV_ROWS
    n_chunks = seq // rows

    def body(b_ref, c_ref, h_ref, z_ref, w_ref, out_ref, u_s):
        u_s[0:SUBLANES, :] = jnp.zeros((SUBLANES, LANES), F32)

        def fill(r, carry):
            rs = pl.ds(pl.multiple_of(r * rows, rows), rows)
            u_s[pl.ds(pl.multiple_of(r * rows + SUBLANES, SUBLANES), rows), :] = c_ref[rs, :] * h_ref[rs, :]
            return carry

        lax.fori_loop(0, n_chunks, fill, 0)
        w = w_ref[...]

        def chunk(r, carry):
            r0 = pl.multiple_of(r * rows, rows)
            rs = pl.ds(r0, rows)
            ext = u_s[pl.ds(r0, rows + SUBLANES), :]
            u = ext[SUBLANES:, :]
            y = w[2:3, :] * u
            y = y + w[0:1, :] * _shift_down(ext, 2, rows)
            y = y + w[1:2, :] * _shift_down(ext, 1, rows)
            z = z_ref[rs, :]
            out_ref[rs, :] = ((z * _sigmoid(z)) * (b_ref[rs, :] * y)).astype(BF16)
            return carry

        lax.fori_loop(0, n_chunks, chunk, 0)

    col = lambda g: pl.BlockSpec((seq, LANES), _col_block(g))
    return pl.pallas_call(
        body, name="conv_fwd",
        grid=(CONV_WIDTH // LANES,),
        in_specs=[col(0), col(1), col(2), col(3), pl.BlockSpec((3, LANES), lambda j: (0, j))],
        out_specs=pl.BlockSpec((seq, LANES), lambda j: (0, j)),
        out_shape=jax.ShapeDtypeStruct((seq, CONV_WIDTH), BF16),
        scratch_shapes=[pltpu.VMEM((seq + SUBLANES, LANES), F32)],
        compiler_params=_compiler_params(40, ("arbitrary",)),
    )(proj, proj, proj, proj, conv_w)


def _conv_bwd(proj, dmix, conv_w):
    seq = proj.shape[0]
    rows = CONV_ROWS
    n_chunks = seq // rows

    def body(b_ref, c_ref, h_ref, z_ref, d_ref, w_ref, dp_ref, dw_ref, u_s, dy_s):
        u_s[0:SUBLANES, :] = jnp.zeros((SUBLANES, LANES), F32)
        dy_s[seq:seq + SUBLANES, :] = jnp.zeros((SUBLANES, LANES), F32)

        def fill(r, carry):
            r0 = pl.multiple_of(r * rows, rows)
            rs = pl.ds(r0, rows)
            u_s[pl.ds(pl.multiple_of(r0 + SUBLANES, SUBLANES), rows), :] = c_ref[rs, :] * h_ref[rs, :]
            z = z_ref[rs, :]
            dy_s[rs, :] = d_ref[rs, :] * (z * _sigmoid(z)) * b_ref[rs, :]
            return carry

        lax.fori_loop(0, n_chunks, fill, 0)
        w = w_ref[...]

        def chunk(r, acc):
            r0 = pl.multiple_of(r * rows, rows)
            rs = pl.ds(r0, rows)
            ext = u_s[pl.ds(r0, rows + SUBLANES), :]
            u = ext[SUBLANES:, :]
            um1 = _shift_down(ext, 1, rows)
            um2 = _shift_down(ext, 2, rows)
            y = w[2:3, :] * u
            y = y + w[0:1, :] * um2
            y = y + w[1:2, :] * um1
            z = z_ref[rs, :]
            b = b_ref[rs, :]
            dco = d_ref[rs, :]
            sg = _sigmoid(z)
            g = z * sg
            dp_ref[0, rs, :] = (dco * g * y).astype(BF16)
            dp_ref[3, rs, :] = (dco * b * y * (sg * (1.0 + z * (1.0 - sg)))).astype(BF16)
            ext_dy = dy_s[pl.ds(r0, rows + SUBLANES), :]
            dy = ext_dy[:rows, :]
            du = w[2:3, :] * dy + w[1:2, :] * _shift_up(ext_dy, 1, rows) + w[0:1, :] * _shift_up(ext_dy, 2, rows)
            dp_ref[1, rs, :] = (du * h_ref[rs, :]).astype(BF16)
            dp_ref[2, rs, :] = (du * c_ref[rs, :]).astype(BF16)
            a0, a1, a2 = acc
            return (a0 + jnp.sum(dy * um2, axis=0, keepdims=True),
                    a1 + jnp.sum(dy * um1, axis=0, keepdims=True),
                    a2 + jnp.sum(dy * u, axis=0, keepdims=True))

        zero = jnp.zeros((1, LANES), F32)
        a0, a1, a2 = lax.fori_loop(0, n_chunks, chunk, (zero, zero, zero))
        dw_ref[...] = jnp.concatenate([a0, a1, a2, jnp.zeros((SUBLANES - 3, LANES), F32)], axis=0)

    col = lambda g: pl.BlockSpec((seq, LANES), _col_block(g))
    return pl.pallas_call(
        body, name="conv_bwd",
        grid=(CONV_WIDTH // LANES,),
        in_specs=[col(0), col(1), col(2), col(3), col(0), pl.BlockSpec((3, LANES), lambda j: (0, j))],
        out_specs=(pl.BlockSpec((4, seq, LANES), lambda j: (0, 0, j)),
                   pl.BlockSpec((SUBLANES, LANES), lambda j: (0, j))),
        out_shape=(jax.ShapeDtypeStruct((4, seq, CONV_WIDTH), BF16),
                   jax.ShapeDtypeStruct((SUBLANES, CONV_WIDTH), F32)),
        scratch_shapes=[pltpu.VMEM((seq + SUBLANES, LANES), F32), pltpu.VMEM((seq + SUBLANES, LANES), F32)],
        compiler_params=_compiler_params(48, ("arbitrary",)),
    )(proj, proj, proj, proj, dmix, conv_w)


def _tri_masks():
    r = lax.broadcasted_iota(jnp.int32, (TK, TK), 0)
    s = lax.broadcasted_iota(jnp.int32, (TK, TK), 1)
    return r, s


def _head_mask():
    lane = lax.broadcasted_iota(jnp.int32, (1, LANES), 1)
    return lane < HEAD_DIM


def _tile_lanes(a):
    return jnp.tile(a, (1, TK // LANES))


def _next_tile(tile):
    qb, n = tile
    more = n < qb
    return jnp.where(more, qb, qb + 1), jnp.where(more, n + 1, 0)


def _prev_tile(tile):
    qb, n = tile
    first = n == 0
    last_of_prev = jnp.maximum(qb - 1, 0)
    return jnp.where(first, last_of_prev, qb), jnp.where(first, last_of_prev, n - 1)


def _attn_fwd(proj):
    seq = proj.shape[0]
    n_q = seq // TQ
    assert seq // TK <= LANES

    n_kb = seq // TK
    heads = range(HEADS_PER_STEP)

    def key_rows(kb):
        return pl.ds(pl.multiple_of(jnp.clip(kb, 0, n_kb - 1) * TK, TK), TK)

    def query_rows(qb):
        return pl.ds(pl.multiple_of(jnp.clip(qb, 0, n_q - 1) * TQ, TQ), TQ)

    def body(q_ref, k_ref, v_ref, za_ref, o_ref, mix_ref, car_ref,
             qa_s, qb_s, k_s, v_s, suffix_s, carry_s, z_s, zm_s, hi_s, lo_s, w_s, rsum_s, diff_s):
        head_a = _head_mask()
        r_i, s_i = _tri_masks()
        diff_s[...] = r_i - s_i
        suffix_s[...] = jnp.where(r_i > s_i, 1.0, 0.0).astype(BF16)
        lane = lax.broadcasted_iota(jnp.int32, (1, LANES), 1)

        def prep(r, carry):
            rs = pl.ds(pl.multiple_of(r * TQ, TQ), TQ)
            q = q_ref[rs, :] * Q_SCALE
            qa_s[rs, :] = jnp.where(head_a, q, 0.0).astype(BF16)
            qb_s[rs, :] = jnp.where(head_a, 0.0, q).astype(BF16)
            k_s[rs, :] = k_ref[rs, :].astype(BF16)
            v_s[rs, :] = v_ref[rs, :].astype(BF16)
            o_ref[rs, :] = jnp.zeros((TQ, LANES), F32)
            for h in heads:
                car_ref[h, rs, :] = jnp.zeros((TQ, LANES), F32)
            return carry

        lax.fori_loop(0, n_q, prep, 0)
        carry_s[...] = jnp.zeros((HEADS_PER_STEP, TQ, LANES), F32)

        w_s[1] = jnp.zeros((HEADS_PER_STEP, TQ, TK), BF16)

        def stage_a(tile, slot):
            qb, n = tile
            kt = k_s[key_rows(qb - n), :]
            rs = query_rows(qb)
            z_s[slot, 0] = _dot_nt(qa_s[rs, :], kt)
            z_s[slot, 1] = _dot_nt(qb_s[rs, :], kt)

        def stage_b(tile, slot):
            mask = diff_s[...] > jnp.where(tile[1] == 0, 0, -TK)
            for h in heads:
                z = z_s[slot, h]
                sp = jnp.where(mask, _softplus_parts(z), 0.0)
                zm_s[slot, h] = jnp.where(mask, z - sp, MASKED_LOGIT)
                hi = _truncate_to_bf16(sp)
                hi_s[slot, h] = hi.astype(BF16)
                lo_s[slot, h] = (sp - hi).astype(BF16)
                rsum_s[slot, h] = jnp.broadcast_to(jnp.sum(sp, axis=1, keepdims=True), (TQ, LANES))

        def stage_c(slot):
            return [_dot(hi_s[slot, h], suffix_s[...]) + _dot(lo_s[slot, h], suffix_s[...]) for h in heads]

        def stage_d(tile, slot, sufs):
            qb, n = tile
            rs = query_rows(qb)
            first = n == 0
            for h in heads:
                car = jnp.where(first, 0.0, carry_s[h])
                car_ref[h, rs, :] = jnp.where(lane == qb - n, car, car_ref[h, rs, :])
                w_s[slot, h] = jnp.exp(zm_s[slot, h] - (sufs[h] + _tile_lanes(car))).astype(BF16)
                carry_s[h] = car + rsum_s[slot, h]

        def stage_e(tile, slot):
            qb, n = tile
            vt = v_s[key_rows(qb - n), :]
            pv = [_dot(w_s[slot, h], vt) for h in heads]
            rs = query_rows(qb)
            o_ref[rs, :] += jnp.where(head_a, pv[0], pv[1])

        zero = jnp.int32(0)
        stage_a((zero, zero), 0)
        stage_a(_next_tile((zero, zero)), 1)
        stage_b((zero, zero), 0)

        def trip(tile, slot):
            after = _next_tile(tile)
            sufs = stage_c(slot)
            stage_b(after, 1 - slot)
            stage_a(_next_tile(after), slot)
            stage_e(_prev_tile(tile), 1 - slot)
            stage_d(tile, slot, sufs)
            return after

        def two_trips(j, tile):
            return trip(trip(tile, 0), 1)

        n_tiles = n_q * (n_q + 1) // 2
        assert n_tiles % 2 == 0
        lax.fori_loop(0, n_tiles // 2, two_trips, (zero, zero))
        stage_e((jnp.int32(n_q - 1), jnp.int32(n_q - 1)), 1)

        def gate(r, carry):
            rs = pl.ds(pl.multiple_of(r * TQ, TQ), TQ)
            za = za_ref[rs, :]
            mix_ref[rs, :] = ((za * _sigmoid(za)) * o_ref[rs, :]).astype(BF16)
            return carry

        lax.fori_loop(0, n_q, gate, 0)

    col = lambda g: pl.BlockSpec((seq, LANES), _col_block(g), pipeline_mode=pl.Buffered(1))
    n_pairs = ATTN_WIDTH // LANES
    return pl.pallas_call(
        body, name="attn_fwd",
        grid=(n_pairs,),
        in_specs=[col(4), col(5), col(6), col(7)],
        out_specs=(pl.BlockSpec((seq, LANES), lambda p: (0, p)),
                   pl.BlockSpec((seq, LANES), lambda p: (0, p)),
                   pl.BlockSpec((HEADS_PER_STEP, seq, LANES), lambda p: (p, 0, 0))),
        out_shape=(jax.ShapeDtypeStruct((seq, ATTN_WIDTH), F32),
                   jax.ShapeDtypeStruct((seq, ATTN_WIDTH), BF16),
                   jax.ShapeDtypeStruct((n_pairs * HEADS_PER_STEP, seq, LANES), F32)),
        scratch_shapes=[pltpu.VMEM((seq, LANES), BF16), pltpu.VMEM((seq, LANES), BF16),
                        pltpu.VMEM((seq, LANES), BF16), pltpu.VMEM((seq, LANES), BF16),
                        pltpu.VMEM((TK, TK), BF16),
                        pltpu.VMEM((HEADS_PER_STEP, TQ, LANES), F32),
                        pltpu.VMEM((2, HEADS_PER_STEP, TQ, TK), F32),
                        pltpu.VMEM((2, HEADS_PER_STEP, TQ, TK), F32),
                        pltpu.VMEM((2, HEADS_PER_STEP, TQ, TK), BF16),
                        pltpu.VMEM((2, HEADS_PER_STEP, TQ, TK), BF16),
                        pltpu.VMEM((2, HEADS_PER_STEP, TQ, TK), BF16),
                        pltpu.VMEM((2, HEADS_PER_STEP, TQ, LANES), F32),
                        pltpu.VMEM((TQ, TK), jnp.int32)],
        compiler_params=_compiler_params(48, ("arbitrary",)),
    )(proj, proj, proj, proj)


def _attn_bwd(proj, o, dmix, carries):
    seq = proj.shape[0]
    n_q = seq // TQ
    n_kb = seq // TK
    heads = range(HEADS_PER_STEP)

    def key_rows(kb):
        return pl.ds(pl.multiple_of(jnp.clip(kb, 0, n_kb - 1) * TK, TK), TK)

    def query_rows(qb):
        return pl.ds(pl.multiple_of(jnp.clip(qb, 0, n_q - 1) * TQ, TQ), TQ)

    def body(q_ref, k_ref, v_ref, za_ref, o_ref, dm_ref, car_ref, dp_ref,
             qa_s, qb_s, k_s, v_s, doa_s, dob_s, suffix_s, prefix_s, dq_s, ccar_s, dk_s, dv_s,
             z_s, zm_s, dlw_s, sg_s, hi_s, lo_s, hi2_s, lo2_s, w_s, rsum_s, diff_s):
        head_a = _head_mask()
        r_i, s_i = _tri_masks()
        strict = r_i > s_i
        diff_s[...] = r_i - s_i
        suffix_s[...] = jnp.where(strict, 1.0, 0.0).astype(BF16)
        prefix_s[...] = jnp.where(r_i < s_i, 1.0, 0.0).astype(BF16)
        lane = lax.broadcasted_iota(jnp.int32, (1, LANES), 1)

        def prep(r, carry):
            rs = pl.ds(pl.multiple_of(r * TQ, TQ), TQ)
            q = q_ref[rs, :] * Q_SCALE
            qa_s[rs, :] = jnp.where(head_a, q, 0.0).astype(BF16)
            qb_s[rs, :] = jnp.where(head_a, 0.0, q).astype(BF16)
            k_s[rs, :] = k_ref[rs, :].astype(BF16)
            v_s[rs, :] = v_ref[rs, :].astype(BF16)
            za = za_ref[rs, :]
            sg = _sigmoid(za)
            dm = dm_ref[rs, :]
            do = dm * (za * sg)
            doa_s[rs, :] = jnp.where(head_a, do, 0.0).astype(BF16)
            dob_s[rs, :] = jnp.where(head_a, 0.0, do).astype(BF16)
            dp_ref[3, rs, :] = (dm * o_ref[rs, :] * (sg * (1.0 + za * (1.0 - sg)))).astype(BF16)
            dq_s[rs, :] = jnp.zeros((TQ, LANES), F32)
            dk_s[rs, :] = jnp.zeros((TQ, LANES), F32)
            dv_s[rs, :] = jnp.zeros((TQ, LANES), F32)
            return carry

        lax.fori_loop(0, n_q, prep, 0)

        ccar_s[...] = jnp.zeros((HEADS_PER_STEP, TQ, LANES), F32)
        hi2_s[1] = jnp.zeros((HEADS_PER_STEP, TQ, TK), BF16)
        lo2_s[1] = jnp.zeros((HEADS_PER_STEP, TQ, TK), BF16)
        w_s[1] = jnp.zeros((HEADS_PER_STEP, TQ, TK), BF16)
        dlw_s[1] = jnp.zeros((HEADS_PER_STEP, TQ, TK), F32)
        sg_s[1] = jnp.zeros((HEADS_PER_STEP, TQ, TK), F32)
        rsum_s[1] = jnp.zeros((HEADS_PER_STEP, TQ, LANES), F32)

        def stage_a(tile, slot):
            qb, n = tile
            kt = k_s[key_rows(n), :]
            rs = query_rows(qb)
            z_s[slot, 0] = _dot_nt(qa_s[rs, :], kt)
            z_s[slot, 1] = _dot_nt(qb_s[rs, :], kt)

        def stage_b(tile, slot):
            mask = diff_s[...] > jnp.where(tile[1] == tile[0], 0, -TK)
            for h in heads:
                z = z_s[slot, h]
                sp = jnp.where(mask, _softplus_parts(z), 0.0)
                zm_s[slot, h] = jnp.where(mask, z - sp, MASKED_LOGIT)
                hi = _truncate_to_bf16(sp)
                hi_s[slot, h] = hi.astype(BF16)
                lo_s[slot, h] = (sp - hi).astype(BF16)

        def stage_c(tile, slot):
            qb, n = tile
            vt = v_s[key_rows(n), :]
            rs = query_rows(qb)
            sufs = [_dot(hi_s[slot, h], suffix_s[...]) + _dot(lo_s[slot, h], suffix_s[...]) for h in heads]
            dws = [_dot_nt(doa_s[rs, :], vt), _dot_nt(dob_s[rs, :], vt)]
            return sufs, dws

        def stage_d(tile, slot, sufs, dws):
            qb, n = tile
            rs = query_rows(qb)
            for h in heads:
                cin = jnp.sum(jnp.where(lane == n, car_ref[h, rs, :], 0.0), axis=1, keepdims=True)
                zm = zm_s[slot, h]
                w = jnp.exp(zm - (sufs[h] + cin))
                w_s[slot, h] = w.astype(BF16)
                sg_s[slot, h] = jnp.exp(zm)
                dlw = dws[h] * w
                dlw_s[slot, h] = dlw
                hi = _truncate_to_bf16(dlw)
                hi2_s[slot, h] = hi.astype(BF16)
                lo2_s[slot, h] = (dlw - hi).astype(BF16)
                rsum_s[slot, h] = jnp.broadcast_to(jnp.sum(dlw, axis=1, keepdims=True), (TQ, LANES))

        def stage_e(slot):
            return [_dot(hi2_s[slot, h], prefix_s[...]) + _dot(lo2_s[slot, h], prefix_s[...]) for h in heads]

        def stage_f(tile, slot, pres):
            first = tile[1] == 0
            dzs = []
            for h in heads:
                ccar = jnp.where(first, 0.0, ccar_s[h])
                dlw = dlw_s[slot, h]
                dz = dlw - sg_s[slot, h] * (dlw + (pres[h] + _tile_lanes(ccar)))
                dzs.append(dz.astype(BF16))
                ccar_s[h] = ccar + rsum_s[slot, h]
            return dzs

        def stage_g(tile, slot, dzs):
            qb, n = tile
            cs = key_rows(n)
            rs = query_rows(qb)
            kt = k_s[cs, :]
            dq = [_dot(dzs[h], kt) for h in heads]
            dk = _dot_tn(dzs[0], qa_s[rs, :]) + _dot_tn(dzs[1], qb_s[rs, :])
            dv = _dot_tn(w_s[slot, 0], doa_s[rs, :]) + _dot_tn(w_s[slot, 1], dob_s[rs, :])
            dq_s[rs, :] += jnp.where(head_a, dq[0], dq[1])
            dk_s[cs, :] += dk
            dv_s[cs, :] += dv

        zero = jnp.int32(0)
        stage_a((zero, zero), 0)
        stage_a(_next_tile((zero, zero)), 1)
        stage_b((zero, zero), 0)

        def trip(tile, slot):
            after = _next_tile(tile)
            before = _prev_tile(tile)
            pres = stage_e(1 - slot)
            sufs, dws = stage_c(tile, slot)
            stage_a(_next_tile(after), slot)
            stage_b(after, 1 - slot)
            dzs = stage_f(before, 1 - slot, pres)
            stage_g(before, 1 - slot, dzs)
            stage_d(tile, slot, sufs, dws)
            return after

        def two_trips(j, tile):
            return trip(trip(tile, 0), 1)

        n_tiles = n_q * (n_q + 1) // 2
        assert n_tiles % 2 == 0
        lax.fori_loop(0, n_tiles // 2, two_trips, (zero, zero))
        last = (jnp.int32(n_q - 1), jnp.int32(n_q - 1))
        stage_g(last, 1, stage_f(last, 1, stage_e(1)))

        def finish(r, carry):
            rs = pl.ds(pl.multiple_of(r * TQ, TQ), TQ)
            dp_ref[0, rs, :] = (dq_s[rs, :] * Q_SCALE).astype(BF16)
            dp_ref[1, rs, :] = dk_s[rs, :].astype(BF16)
            dp_ref[2, rs, :] = dv_s[rs, :].astype(BF16)
            return carry

        lax.fori_loop(0, n_q, finish, 0)

    def col(g, n_sub=CONV_WIDTH // LANES):
        return pl.BlockSpec((seq, LANES), _col_block(g, n_sub), pipeline_mode=pl.Buffered(1))

    n_pairs = ATTN_WIDTH // LANES
    bf = lambda: pltpu.VMEM((seq, LANES), BF16)
    stage = lambda dtype: pltpu.VMEM((2, HEADS_PER_STEP, TQ, TK), dtype)
    return pl.pallas_call(
        body, name="attn_bwd",
        grid=(n_pairs,),
        in_specs=[col(4), col(5), col(6), col(7), col(0), col(1),
                  pl.BlockSpec((HEADS_PER_STEP, seq, LANES), lambda p: (p, 0, 0), pipeline_mode=pl.Buffered(1))],
        out_specs=pl.BlockSpec((4, seq, LANES), lambda p: (0, 0, p)),
        out_shape=jax.ShapeDtypeStruct((4, seq, ATTN_WIDTH), BF16),
        scratch_shapes=[bf(), bf(), bf(), bf(), bf(), bf(),
                        pltpu.VMEM((TK, TK), BF16), pltpu.VMEM((TK, TK), BF16),
                        pltpu.VMEM((seq, LANES), F32),
                        pltpu.VMEM((HEADS_PER_STEP, TQ, LANES), F32),
                        pltpu.VMEM((seq, LANES), F32), pltpu.VMEM((seq, LANES), F32),
                        stage(F32), stage(F32), stage(F32), stage(F32),
                        stage(BF16), stage(BF16), stage(BF16), stage(BF16), stage(BF16),
                        pltpu.VMEM((2, HEADS_PER_STEP, TQ, LANES), F32),
                        pltpu.VMEM((TQ, TK), jnp.int32)],
        compiler_params=_compiler_params(56, ("arbitrary",)),
    )(proj, proj, proj, proj, o, dmix, carries)


def _out_ln(mix_c, mix_a, woutg, x, target, gain, bias):
    seq = x.shape[0]
    tm = 256
    inv_d = 1.0 / D_MODEL

    def body(mc_ref, ma_ref, wo_ref, x_ref, t_ref, g_ref, b_ref, dxa_ref, dmix_ref, gw_ref, st_ref):
        @pl.when(pl.program_id(0) == 0)
        def _():
            gw_ref[...] = jnp.zeros(gw_ref.shape, F32)
            st_ref[...] = jnp.zeros(st_ref.shape, F32)

        mc = mc_ref[...]
        ma = ma_ref[...]
        sub = _dot(mc, wo_ref[0:CONV_WIDTH, :]) + _dot(ma, wo_ref[CONV_WIDTH:, :])
        r = DEEPNORM_ALPHA * x_ref[...] + sub
        mu = jnp.mean(r, axis=-1, keepdims=True)
        rc = r - mu
        var = jnp.mean(rc * rc, axis=-1, keepdims=True)
        rstd = lax.rsqrt(var + LN_EPS)
        xhat = rc * rstd
        gain_v = g_ref[...]
        diff = (xhat * gain_v + b_ref[...]) - t_ref[...]
        dy = diff * inv_d
        st_ref[0:1, :] += jnp.sum(dy * xhat, axis=0, keepdims=True)
        st_ref[1:2, :] += jnp.sum(dy, axis=0, keepdims=True)
        st_ref[2:3, :] += jnp.sum(diff * diff, axis=0, keepdims=True)
        dxh = dy * gain_v
        m1 = jnp.mean(dxh, axis=-1, keepdims=True)
        m2 = jnp.mean(dxh * xhat, axis=-1, keepdims=True)
        dr = rstd * (dxh - m1 - xhat * m2)
        dxa_ref[...] = DEEPNORM_ALPHA * dr
        drb = dr.astype(BF16)
        dmix_ref[...] = _dot_nt(drb, wo_ref[...])
        gw_ref[0:CONV_WIDTH, :] += _dot_tn(mc, drb)
        gw_ref[CONV_WIDTH:, :] += _dot_tn(ma, drb)

    rows_d = lambda: pl.BlockSpec((tm, D_MODEL), lambda i: (i, 0))
    rows_h = lambda: pl.BlockSpec((tm, CONV_WIDTH), lambda i: (i, 0))
    whole = lambda shape: pl.BlockSpec(shape, lambda i: (0, 0))
    return pl.pallas_call(
        body, name="out_ln",
        grid=(seq // tm,),
        in_specs=[rows_h(), rows_h(), whole((D_MODEL, D_MODEL)), rows_d(), rows_d(),
                  whole((1, D_MODEL)), whole((1, D_MODEL))],
        out_specs=(rows_d(), rows_d(), whole((D_MODEL, D_MODEL)), whole((SUBLANES, D_MODEL))),
        out_shape=(jax.ShapeDtypeStruct((seq, D_MODEL), F32), jax.ShapeDtypeStruct((seq, D_MODEL), F32),
                   jax.ShapeDtypeStruct((D_MODEL, D_MODEL), F32), jax.ShapeDtypeStruct((SUBLANES, D_MODEL), F32)),
        compiler_params=_compiler_params(48, ("arbitrary",)),
    )(mix_c, mix_a, woutg, x, target, gain, bias)


def _group_maps():
    half = N_GROUPS // 2
    return (lambda g: jnp.minimum(g, half - 1)), (lambda g: jnp.maximum(g - half, 0))


def _grad_w_in(xt, dpc, dpa):
    seq = xt.shape[1]
    tk = 1024
    half = N_GROUPS // 2
    in_conv, in_attn = _group_maps()

    def body(xt_ref, dc_ref, da_ref, out_ref):
        g = pl.program_id(0)

        @pl.when(pl.program_id(1) == 0)
        def _():
            out_ref[...] = jnp.zeros(out_ref.shape, F32)

        @pl.when(g < half)
        def _():
            out_ref[...] += _dot(xt_ref[...], dc_ref[...])

        @pl.when(g >= half)
        def _():
            out_ref[...] += _dot(xt_ref[...], da_ref[...])

    return pl.pallas_call(
        body, name="grad_w_in",
        grid=(N_GROUPS, seq // tk),
        in_specs=[pl.BlockSpec((D_MODEL, tk), lambda g, s: (0, s)),
                  pl.BlockSpec((None, tk, CONV_WIDTH), lambda g, s: (in_conv(g), s, 0)),
                  pl.BlockSpec((None, tk, CONV_WIDTH), lambda g, s: (in_attn(g), s, 0))],
        out_specs=pl.BlockSpec((None, D_MODEL, CONV_WIDTH), lambda g, s: (g // 2, 0, g % 2)),
        out_shape=jax.ShapeDtypeStruct((N_CHIPS, D_MODEL, D_MODEL), F32),
        compiler_params=_compiler_params(40, ("arbitrary", "arbitrary")),
    )(xt, dpc, dpa)


def _grad_x(dxa, dpc, dpa, wing):
    seq = dxa.shape[0]
    tm = 512
    half = N_GROUPS // 2
    in_conv, in_attn = _group_maps()

    def body(dxa_ref, dc_ref, da_ref, w_ref, out_ref):
        g = pl.program_id(1)

        @pl.when(g == 0)
        def _():
            out_ref[...] = dxa_ref[...]

        @pl.when(g < half)
        def _():
            out_ref[...] += _dot_nt(dc_ref[...], w_ref[...])

        @pl.when(g >= half)
        def _():
            out_ref[...] += _dot_nt(da_ref[...], w_ref[...])

    return pl.pallas_call(
        body, name="grad_x",
        grid=(seq // tm, N_GROUPS),
        in_specs=[pl.BlockSpec((tm, D_MODEL), lambda i, g: (i, 0)),
                  pl.BlockSpec((None, tm, CONV_WIDTH), lambda i, g: (in_conv(g), i, 0)),
                  pl.BlockSpec((None, tm, CONV_WIDTH), lambda i, g: (in_attn(g), i, 0)),
                  pl.BlockSpec((None, D_MODEL, CONV_WIDTH), lambda i, g: (g // 2, 0, g % 2))],
        out_specs=pl.BlockSpec((tm, D_MODEL), lambda i, g: (i, 0)),
        out_shape=jax.ShapeDtypeStruct((seq, D_MODEL), F32),
        compiler_params=_compiler_params(40, ("arbitrary", "arbitrary")),
    )(dxa, dpc, dpa, wing)


def _reduce_grads(gwin, gwout, pack):
    n_shards, n_rows_in, n_cols = gwin.shape
    n_rows_out = gwout.shape[1]
    half_rows = (n_rows_in // 2, n_rows_out // 2)
    row_chunk = 128

    def body(gwin_ref, gwout_ref, pack_ref, oin_ref, oout_ref, opack_ref,
             mine_in, mine_out, sib_in, sib_out, rin, rout, rpack,
             local_sems, sib_send, sib_recv, send_sems, recv_sems, pack_send, pack_recv, fin_send, fin_recv):
        x, y, c = lax.axis_index("x"), lax.axis_index("y"), lax.axis_index("c")
        me = 2 * x + y
        my_id = 4 * x + 2 * y + c
        chips = [(1 - x, y), (x, 1 - y), (1 - x, 1 - y)]
        sibling = (x, y, 1 - c)
        partial = (gwin_ref, gwout_ref)
        mine = (mine_in, mine_out)
        from_sib = (sib_in, sib_out)
        from_chips = (rin, rout)
        result = (oin_ref, oout_ref)

        def half(a, which):
            rows = half_rows[a]
            return pl.ds(pl.multiple_of(which * rows, rows), rows)

        sends = []
        loads = []
        for a in range(2):
            ld = pltpu.make_async_copy(partial[a].at[:, half(a, c), :], mine[a], local_sems.at[a])
            ld.start()
            loads.append(ld)
            cp = pltpu.make_async_remote_copy(
                src_ref=partial[a].at[:, half(a, 1 - c), :], dst_ref=from_sib[a],
                send_sem=sib_send.at[a], recv_sem=sib_recv.at[a], device_id=sibling, device_id_type=MESH)
            cp.start()
            sends.append(cp)

        rpack[my_id] = pack_ref[...]

        def pack_copy(rel, slot, to):
            return pltpu.make_async_remote_copy(
                src_ref=rpack.at[slot], dst_ref=rpack.at[slot],
                send_sem=pack_send.at[rel - 1], recv_sem=pack_recv.at[rel - 1],
                device_id=to, device_id_type=MESH)

        def related(rel):
            px = (1 - x) if rel & 4 else x
            py = (1 - y) if rel & 2 else y
            pc = (1 - c) if rel & 1 else c
            return px, py, pc

        for rel in range(1, N_DEVICES):
            cp = pack_copy(rel, my_id, related(rel))
            cp.start()
            sends.append(cp)

        for ld in loads:
            ld.wait()
        for a in range(2):
            pltpu.make_async_remote_copy(
                src_ref=partial[a].at[:, half(a, 1 - c), :], dst_ref=from_sib[a],
                send_sem=sib_send.at[a], recv_sem=sib_recv.at[a], device_id=sibling, device_id_type=MESH).wait_recv()

        def chip_sum(a, shard):
            def add(r, carry):
                rs = pl.ds(pl.multiple_of(r * row_chunk, row_chunk), row_chunk)
                mine[a][shard, rs, :] = mine[a][shard, rs, :] + from_sib[a][shard, rs, :]
                return carry

            lax.fori_loop(0, half_rows[a] // row_chunk, add, 0)

        def shard_copy(k, a, shard, to):
            return pltpu.make_async_remote_copy(
                src_ref=mine[a].at[shard], dst_ref=from_chips[a].at[k],
                send_sem=send_sems.at[2 * k + a], recv_sem=recv_sems.at[2 * k + a],
                device_id=to, device_id_type=MESH)

        for k, (px, py) in enumerate(chips):
            for a in range(2):
                chip_sum(a, 2 * px + py)
                cp = shard_copy(k, a, 2 * px + py, (px, py, c))
                cp.start()
                sends.append(cp)
        for a in range(2):
            chip_sum(a, me)

        for rel in range(1, N_DEVICES):
            px, py, pc = related(rel)
            pack_copy(rel, 4 * px + 2 * py + pc, (px, py, pc)).wait_recv()
        total = rpack[0]
        for i in range(1, N_DEVICES):
            total = total + rpack[i]
        opack_ref[...] = total

        for k, (px, py) in enumerate(chips):
            for a in range(2):
                shard_copy(k, a, me, (px, py, c)).wait_recv()

        def finish(a):
            def add(r, carry):
                rs = pl.ds(pl.multiple_of(r * row_chunk, row_chunk), row_chunk)
                dst = pl.ds(pl.multiple_of(c * half_rows[a] + r * row_chunk, row_chunk), row_chunk)
                result[a][dst, :] = ((mine[a][me, rs, :] + from_chips[a][0, rs, :])
                                     + from_chips[a][1, rs, :]) + from_chips[a][2, rs, :]
                return carry

            lax.fori_loop(0, half_rows[a] // row_chunk, add, 0)

        def final_copy(a, which):
            ref = result[a].at[half(a, which), :]
            return pltpu.make_async_remote_copy(
                src_ref=ref, dst_ref=ref, send_sem=fin_send.at[a], recv_sem=fin_recv.at[a],
                device_id=sibling, device_id_type=MESH)

        for a in range(2):
            finish(a)
            cp = final_copy(a, c)
            cp.start()
            sends.append(cp)
        for a in range(2):
            final_copy(a, 1 - c).wait_recv()
        for cp in sends:
            cp.wait_send()

    vmem = pl.BlockSpec(memory_space=pltpu.VMEM)
    hbm = pl.BlockSpec(memory_space=pl.ANY)
    in_shape = (n_rows_in, n_cols)
    out_shape = (n_rows_out, n_cols)
    half_in = (half_rows[0], n_cols)
    half_out = (half_rows[1], n_cols)
    return pl.pallas_call(
        body, name="reduce_grads",
        out_shape=(jax.ShapeDtypeStruct(in_shape, F32), jax.ShapeDtypeStruct(out_shape, F32),
                   jax.ShapeDtypeStruct(pack.shape, F32)),
        in_specs=[hbm, hbm, vmem], out_specs=(vmem, vmem, vmem),
        scratch_shapes=[pltpu.VMEM((n_shards,) + half_in, F32), pltpu.VMEM((n_shards,) + half_out, F32),
                        pltpu.VMEM((n_shards,) + half_in, F32), pltpu.VMEM((n_shards,) + half_out, F32),
                        pltpu.VMEM((N_CHIPS - 1,) + half_in, F32), pltpu.VMEM((N_CHIPS - 1,) + half_out, F32),
                        pltpu.VMEM((N_DEVICES,) + pack.shape, F32),
                        pltpu.SemaphoreType.DMA((2,)),
                        pltpu.SemaphoreType.DMA((2,)), pltpu.SemaphoreType.DMA((2,)),
                        pltpu.SemaphoreType.DMA((6,)), pltpu.SemaphoreType.DMA((6,)),
                        pltpu.SemaphoreType.DMA((N_DEVICES - 1,)), pltpu.SemaphoreType.DMA((N_DEVICES - 1,)),
                        pltpu.SemaphoreType.DMA((2,)), pltpu.SemaphoreType.DMA((2,))],
        compiler_params=_compiler_params(48),
    )(gwin, gwout, pack)


def _adamw(name, w, g, m, v):
    n_rows, n_cols = w.shape
    tr = 256 if n_rows % 256 == 0 else n_rows
    m_corr = 1.0 - ADAM_B1 ** ADAM_STEP
    v_corr = 1.0 - ADAM_B2 ** ADAM_STEP

    def body(w_ref, g_ref, m_ref, v_ref, d_ref, nm_ref, nv_ref):
        gv = g_ref[...]
        nm = ADAM_B1 * m_ref[...] + (1.0 - ADAM_B1) * gv
        nv = ADAM_B2 * v_ref[...] + (1.0 - ADAM_B2) * (gv * gv)
        nm_ref[...] = nm
        nv_ref[...] = nv
        d_ref[...] = -ADAM_LR * ((nm / m_corr) / (jnp.sqrt(nv / v_corr) + ADAM_EPS) + ADAM_WD * w_ref[...])

    blk = lambda: pl.BlockSpec((tr, n_cols), lambda i: (i, 0))
    shape = jax.ShapeDtypeStruct(w.shape, F32)
    return pl.pallas_call(
        body, name=name,
        grid=(n_rows // tr,),
        in_specs=[blk(), blk(), blk(), blk()], out_specs=(blk(), blk(), blk()),
        out_shape=(shape, shape, shape),
        compiler_params=_compiler_params(32, ("arbitrary",)),
    )(w, g, m, v)


def kernel(x, w_in, conv_w, w_out, ln_gain, ln_bias, loss_target, m_w_in, m_conv_w, m_w_out, m_ln_gain, m_ln_bias, v_w_in, v_conv_w, v_w_out, v_ln_gain, v_ln_bias):
    xs = x[0]
    target = loss_target[0]

    wing, woutg, cwg = _gather_weights(w_in[0], w_out[0], conv_w[0])
    conv_full = jnp.transpose(cwg, (1, 0, 2)).reshape(3, CONV_WIDTH)
    wout_full = woutg.reshape(D_MODEL, D_MODEL)

    proj, xt = _proj(xs, wing)
    mix_c = _conv_fwd(proj, conv_full)
    o, mix_a, carries = _attn_fwd(proj)
    dxa, dmix, gwout, stats = _out_ln(mix_c, mix_a, wout_full, xs, target, ln_gain, ln_bias)
    dpc, dconv = _conv_bwd(proj, dmix, conv_full)
    dpa = _attn_bwd(proj, o, dmix, carries)
    gwin = _grad_w_in(xt, dpc, dpa)
    grad_x = _grad_x(dxa, dpc, dpa, wing)

    pack = jnp.concatenate(
        [stats[0:2], jnp.pad(dconv[0:3], ((0, 0), (0, D_MODEL - CONV_WIDTH))), jnp.zeros((3, D_MODEL), F32)], axis=0)
    g_w_in, g_w_out, tot = _reduce_grads(gwin, gwout.reshape(N_CHIPS, D_MODEL // N_CHIPS, D_MODEL), pack)

    chip = 2 * lax.axis_index("x") + lax.axis_index("y")
    g_gain = tot[0:1]
    g_bias = tot[1:2]
    g_conv = lax.dynamic_slice(tot, (2, chip * LANES), (3, LANES))

    loss_local = (0.5 / D_MODEL) * jnp.sum(stats[2])
    loss = lax.psum(loss_local, ("x", "y", "c"))

    d_w_in, nm_w_in, nv_w_in = _adamw("adamw_w_in", w_in[0], g_w_in, m_w_in[0], v_w_in[0])
    d_w_out, nm_w_out, nv_w_out = _adamw("adamw_w_out", w_out[0], g_w_out, m_w_out[0], v_w_out[0])
    d_conv, nm_conv, nv_conv = _adamw("adamw_conv_w", conv_w[0], g_conv, m_conv_w[0], v_conv_w[0])
    d_gain, nm_gain, nv_gain = _adamw("adamw_ln_gain", ln_gain, g_gain, m_ln_gain, v_ln_gain)
    d_bias, nm_bias, nv_bias = _adamw("adamw_ln_bias", ln_bias, g_bias, m_ln_bias, v_ln_bias)

    lead = lambda a: a[None]
    return (loss, lead(grad_x),
            lead(g_w_in), lead(g_conv), lead(g_w_out), g_gain, g_bias,
            lead(d_w_in), lead(d_conv), lead(d_w_out), d_gain, d_bias,
            lead(nm_w_in), lead(nm_conv), lead(nm_w_out), nm_gain, nm_bias,
            lead(nv_w_in), lead(nv_conv), lead(nv_w_out), nv_gain, nv_bias)
```

```python
import functools

import jax
import jax.numpy as jnp
from jax import lax
from jax.experimental import pallas as pl
from jax.experimental.pallas import tpu as pltpu

F32 = jnp.float32
BF16 = jnp.bfloat16
MESH = pl.DeviceIdType.MESH

D_MODEL = 1024
CONV_WIDTH = 512
ATTN_WIDTH = 512
HEAD_DIM = 64
N_GROUPS = 8
N_CHIPS = 4
N_DEVICES = 8
LN_EPS = 1e-5
DEEPNORM_ALPHA = 2.0 ** 0.25
Q_SCALE = HEAD_DIM ** -0.5
ADAM_LR = 0.001
ADAM_B1 = 0.9
ADAM_B2 = 0.999
ADAM_EPS = 1e-08
ADAM_WD = 0.01
ADAM_STEP = 10

LANES = 128
SUBLANES = 8
V7X_VMEM_BYTES = 64 * 1024 * 1024
MIB = 1024 * 1024

TQ = 256
TK = 256
HEADS_PER_STEP = LANES // HEAD_DIM
CONV_ROWS = 512


def _compiler_params(vmem_mib, semantics=None):
    assert vmem_mib * MIB < V7X_VMEM_BYTES
    return pltpu.CompilerParams(dimension_semantics=semantics, vmem_limit_bytes=vmem_mib * MIB)


def _sigmoid(z):
    return 1.0 / (1.0 + jnp.exp(-z))


def _dot(a, b):
    return jnp.dot(a, b, preferred_element_type=F32)


def _dot_nt(a, b):
    return lax.dot_general(a, b, (((1,), (1,)), ((), ())), preferred_element_type=F32)


def _dot_tn(a, b):
    return lax.dot_general(a, b, (((0,), (0,)), ((), ())), preferred_element_type=F32)


def _truncate_to_bf16(a):
    bits = lax.bitcast_convert_type(a, jnp.uint32) & jnp.uint32(0xFFFF0000)
    return lax.bitcast_convert_type(bits, F32)


def _split_store(ref, slot, h, a):
    hi = _truncate_to_bf16(a)
    ref[slot, h, :, 0:TK] = hi.astype(BF16)
    ref[slot, h, :, TK:2 * TK] = (a - hi).astype(BF16)


MASKED_LOGIT = -1e30
LOG2_E = 1.4426950408889634


def _softplus2(z2):
    return jnp.maximum(z2, 0.0) + jnp.log2(1.0 + jnp.exp2(-jnp.abs(z2)))


def _gather_weights(w_in, w_out, conv_w):
    d_rows, d_cols = w_in.shape
    o_rows, o_cols = w_out.shape
    half_rows = (d_rows // 2, o_rows // 2)
    row_chunk = 128

    def body(win_ref, wout_ref, cw_ref, wing_ref, woutg_ref, cwg_ref, send_sems, recv_sems, pass_send, pass_recv):
        x, y, c = lax.axis_index("x"), lax.axis_index("y"), lax.axis_index("c")
        me = 2 * x + y
        srcs = (win_ref, wout_ref)
        bufs = (wing_ref, woutg_ref)

        def half(a, shard, which):
            rows = half_rows[a]
            return bufs[a].at[shard, pl.ds(pl.multiple_of(which * rows, rows), rows), :]

        def cast_half(which):
            for a in range(2):
                def cast(r, carry):
                    rows = pl.ds(pl.multiple_of(which * half_rows[a] + r * row_chunk, row_chunk), row_chunk)
                    bufs[a][me, rows, :] = srcs[a][rows, :].astype(BF16)
                    return carry

                lax.fori_loop(0, half_rows[a] // row_chunk, cast, 0)

        chips = [(1 - x, y), (x, 1 - y), (1 - x, 1 - y)]

        def ici_copy(k, a, shard, to):
            ref = cwg_ref.at[shard] if a == 2 else half(a, shard, c)
            return pltpu.make_async_remote_copy(
                src_ref=ref, dst_ref=ref, send_sem=send_sems.at[3 * k + a], recv_sem=recv_sems.at[3 * k + a],
                device_id=to, device_id_type=MESH)

        def pass_copy(k, a, shard, which):
            ref = half(a, shard, which)
            return pltpu.make_async_remote_copy(
                src_ref=ref, dst_ref=ref, send_sem=pass_send.at[2 * k + a], recv_sem=pass_recv.at[2 * k + a],
                device_id=(x, y, 1 - c), device_id_type=MESH)

        cast_half(c)
        cwg_ref[me] = cw_ref[...]
        sends = []
        for k, (px, py) in enumerate(chips):
            for a in range(3):
                cp = ici_copy(k, a, me, (px, py, c))
                cp.start()
                sends.append(cp)
        cast_half(1 - c)
        for k, (px, py) in enumerate(chips):
            for a in range(2):
                ici_copy(k, a, 2 * px + py, (px, py, c)).wait_recv()
                cp = pass_copy(k, a, 2 * px + py, c)
                cp.start()
                sends.append(cp)
        for k, (px, py) in enumerate(chips):
            ici_copy(k, 2, 2 * px + py, (px, py, c)).wait_recv()
            for a in range(2):
                pass_copy(k, a, 2 * px + py, 1 - c).wait_recv()
        for cp in sends:
            cp.wait_send()

    vmem = pl.BlockSpec(memory_space=pltpu.VMEM)
    return pl.pallas_call(
        body, name="gather_weights",
        out_shape=(jax.ShapeDtypeStruct((N_CHIPS, d_rows, d_cols), BF16),
                   jax.ShapeDtypeStruct((N_CHIPS, o_rows, o_cols), BF16),
                   jax.ShapeDtypeStruct((N_CHIPS,) + conv_w.shape, F32)),
        in_specs=[vmem, vmem, vmem], out_specs=(vmem, vmem, vmem),
        scratch_shapes=[pltpu.SemaphoreType.DMA((9,)), pltpu.SemaphoreType.DMA((9,)),
                        pltpu.SemaphoreType.DMA((6,)), pltpu.SemaphoreType.DMA((6,))],
        compiler_params=_compiler_params(32),
    )(w_in, w_out, conv_w)


def _proj(x, wing):
    seq = x.shape[0]
    tm = 256

    def body(x_ref, w_ref, p_ref, xt_ref):
        xv = x_ref[...]
        xb = xv.astype(BF16)
        for j in range(N_CHIPS):
            p_ref[:, j * D_MODEL:(j + 1) * D_MODEL] = _dot(xb, w_ref[j])
        xt_ref[...] = xv.T.astype(BF16)

    return pl.pallas_call(
        body, name="proj",
        grid=(seq // tm,),
        in_specs=[pl.BlockSpec((tm, D_MODEL), lambda i: (i, 0)),
                  pl.BlockSpec((N_CHIPS, D_MODEL, D_MODEL), lambda i: (0, 0, 0), pipeline_mode=pl.Buffered(1))],
        out_specs=(pl.BlockSpec((tm, N_CHIPS * D_MODEL), lambda i: (i, 0)),
                   pl.BlockSpec((D_MODEL, tm), lambda i: (0, i))),
        out_shape=(jax.ShapeDtypeStruct((seq, N_CHIPS * D_MODEL), F32),
                   jax.ShapeDtypeStruct((D_MODEL, seq), BF16)),
        compiler_params=_compiler_params(40, ("arbitrary",)),
    )(x, wing)


def _col_block(group, n_sub=CONV_WIDTH // LANES):
    return lambda j: (0, group * n_sub + j)


def _shift_down(ext, k, rows):
    return pltpu.roll(ext, k, 0)[SUBLANES:, :]


def _shift_up(ext, k, rows):
    return pltpu.roll(ext, rows + SUBLANES - k, 0)[:rows, :]


def _conv_fwd(proj, conv_w):
    seq = proj.shape[0]
    rows = CONV_ROWS
    n_chunks = seq // rows

    def body(b_ref, c_ref, h_ref, z_ref, w_ref, out_ref, u_s):
        u_s[0:SUBLANES, :] = jnp.zeros((SUBLANES, LANES), F32)

        def fill(r, carry):
            rs = pl.ds(pl.multiple_of(r * rows, rows), rows)
            u_s[pl.ds(pl.multiple_of(r * rows + SUBLANES, SUBLANES), rows), :] = c_ref[rs, :] * h_ref[rs, :]
            return carry

        lax.fori_loop(0, n_chunks, fill, 0)
        w = w_ref[...]

        def chunk(r, carry):
            r0 = pl.multiple_of(r * rows, rows)
            rs = pl.ds(r0, rows)
            ext = u_s[pl.ds(r0, rows + SUBLANES), :]
            u = ext[SUBLANES:, :]
            y = w[2:3, :] * u
            y = y + w[0:1, :] * _shift_down(ext, 2, rows)
            y = y + w[1:2, :] * _shift_down(ext, 1, rows)
            z = z_ref[rs, :]
            out_ref[rs, :] = ((z * _sigmoid(z)) * (b_ref[rs, :] * y)).astype(BF16)
            return carry

        lax.fori_loop(0, n_chunks, chunk, 0)

    col = lambda g: pl.BlockSpec((seq, LANES), _col_block(g))
    return pl.pallas_call(
        body, name="conv_fwd",
        grid=(CONV_WIDTH // LANES,),
        in_specs=[col(0), col(1), col(2), col(3), pl.BlockSpec((3, LANES), lambda j: (0, j))],
        out_specs=pl.BlockSpec((seq, LANES), lambda j: (0, j)),
        out_shape=jax.ShapeDtypeStruct((seq, CONV_WIDTH), BF16),
        scratch_shapes=[pltpu.VMEM((seq + SUBLANES, LANES), F32)],
        compiler_params=_compiler_params(40, ("arbitrary",)),
    )(proj, proj, proj, proj, conv_w)


def _conv_bwd(proj, dmix, conv_w):
    seq = proj.shape[0]
    rows = CONV_ROWS
    n_chunks = seq // rows

    def body(b_ref, c_ref, h_ref, z_ref, d_ref, w_ref, dp_ref, dw_ref, u_s, dy_s):
        u_s[0:SUBLANES, :] = jnp.zeros((SUBLANES, LANES), F32)
        dy_s[seq:seq + SUBLANES, :] = jnp.zeros((SUBLANES, LANES), F32)

        def fill(r, carry):
            r0 = pl.multiple_of(r * rows, rows)
            rs = pl.ds(r0, rows)
            u_s[pl.ds(pl.multiple_of(r0 + SUBLANES, SUBLANES), rows), :] = c_ref[rs, :] * h_ref[rs, :]
            z = z_ref[rs, :]
            dy_s[rs, :] = d_ref[rs, :] * (z * _sigmoid(z)) * b_ref[rs, :]
            return carry

        lax.fori_loop(0, n_chunks, fill, 0)
        w = w_ref[...]

        def chunk(r, acc):
            r0 = pl.multiple_of(r * rows, rows)
            rs = pl.ds(r0, rows)
            ext = u_s[pl.ds(r0, rows + SUBLANES), :]
            u = ext[SUBLANES:, :]
            um1 = _shift_down(ext, 1, rows)
            um2 = _shift_down(ext, 2, rows)
            y = w[2:3, :] * u
            y = y + w[0:1, :] * um2
            y = y + w[1:2, :] * um1
            z = z_ref[rs, :]
            b = b_ref[rs, :]
            dco = d_ref[rs, :]
            sg = _sigmoid(z)
            g = z * sg
            dp_ref[0, rs, :] = (dco * g * y).astype(BF16)
            dp_ref[3, rs, :] = (dco * b * y * (sg * (1.0 + z * (1.0 - sg)))).astype(BF16)
            ext_dy = dy_s[pl.ds(r0, rows + SUBLANES), :]
            dy = ext_dy[:rows, :]
            du = w[2:3, :] * dy + w[1:2, :] * _shift_up(ext_dy, 1, rows) + w[0:1, :] * _shift_up(ext_dy, 2, rows)
            dp_ref[1, rs, :] = (du * h_ref[rs, :]).astype(BF16)
            dp_ref[2, rs, :] = (du * c_ref[rs, :]).astype(BF16)
            a0, a1, a2 = acc
            return (a0 + jnp.sum(dy * um2, axis=0, keepdims=True),
                    a1 + jnp.sum(dy * um1, axis=0, keepdims=True),
                    a2 + jnp.sum(dy * u, axis=0, keepdims=True))

        zero = jnp.zeros((1, LANES), F32)
        a0, a1, a2 = lax.fori_loop(0, n_chunks, chunk, (zero, zero, zero))
        dw_ref[...] = jnp.concatenate([a0, a1, a2, jnp.zeros((SUBLANES - 3, LANES), F32)], axis=0)

    col = lambda g: pl.BlockSpec((seq, LANES), _col_block(g))
    return pl.pallas_call(
        body, name="conv_bwd",
        grid=(CONV_WIDTH // LANES,),
        in_specs=[col(0), col(1), col(2), col(3), col(0), pl.BlockSpec((3, LANES), lambda j: (0, j))],
        out_specs=(pl.BlockSpec((4, seq, LANES), lambda j: (0, 0, j)),
                   pl.BlockSpec((SUBLANES, LANES), lambda j: (0, j))),
        out_shape=(jax.ShapeDtypeStruct((4, seq, CONV_WIDTH), BF16),
                   jax.ShapeDtypeStruct((SUBLANES, CONV_WIDTH), F32)),
        scratch_shapes=[pltpu.VMEM((seq + SUBLANES, LANES), F32), pltpu.VMEM((seq + SUBLANES, LANES), F32)],
        compiler_params=_compiler_params(48, ("arbitrary",)),
    )(proj, proj, proj, proj, dmix, conv_w)


def _tri_masks():
    r = lax.broadcasted_iota(jnp.int32, (TK, TK), 0)
    s = lax.broadcasted_iota(jnp.int32, (TK, TK), 1)
    return r, s


def _head_mask():
    lane = lax.broadcasted_iota(jnp.int32, (1, LANES), 1)
    return lane < HEAD_DIM


def _fill_tri(ref, ones):
    tri = jnp.where(ones, 1.0, 0.0).astype(BF16)
    ref[0:TK, :] = tri
    ref[TK:2 * TK, :] = tri


def _tile_lanes(a):
    return jnp.tile(a, (1, TK // LANES))


def _next_tile(tile):
    qb, n = tile
    more = n < qb
    return jnp.where(more, qb, qb + 1), jnp.where(more, n + 1, 0)


def _prev_tile(tile):
    qb, n = tile
    first = n == 0
    last_of_prev = jnp.maximum(qb - 1, 0)
    return jnp.where(first, last_of_prev, qb), jnp.where(first, last_of_prev, n - 1)


def _attn_fwd(proj):
    seq = proj.shape[0]
    n_q = seq // TQ
    assert seq // TK <= LANES

    n_kb = seq // TK
    heads = range(HEADS_PER_STEP)

    def key_rows(kb):
        return pl.ds(pl.multiple_of(jnp.clip(kb, 0, n_kb - 1) * TK, TK), TK)

    def query_rows(qb):
        return pl.ds(pl.multiple_of(jnp.clip(qb, 0, n_q - 1) * TQ, TQ), TQ)

    def body(q_ref, k_ref, v_ref, za_ref, o_ref, mix_ref, car_ref,
             qa_s, qb_s, k_s, v_s, suffix_s, carry_s, z_s, zm_s, hl_s, w_s, rsum_s, diff_s):
        head_a = _head_mask()
        r_i, s_i = _tri_masks()
        diff_s[...] = r_i - s_i
        _fill_tri(suffix_s, r_i > s_i)
        lane = lax.broadcasted_iota(jnp.int32, (1, LANES), 1)

        def prep(r, carry):
            rs = pl.ds(pl.multiple_of(r * TQ, TQ), TQ)
            q = q_ref[rs, :] * Q_SCALE
            qa_s[rs, :] = jnp.where(head_a, q, 0.0).astype(BF16)
            qb_s[rs, :] = jnp.where(head_a, 0.0, q).astype(BF16)
            k_s[rs, :] = k_ref[rs, :].astype(BF16)
            v_s[rs, :] = v_ref[rs, :].astype(BF16)
            o_ref[rs, :] = jnp.zeros((TQ, LANES), F32)
            for h in heads:
                car_ref[h, rs, :] = jnp.zeros((TQ, LANES), F32)
            return carry

        lax.fori_loop(0, n_q, prep, 0)
        carry_s[...] = jnp.zeros((HEADS_PER_STEP, TQ, LANES), F32)

        w_s[1] = jnp.zeros((HEADS_PER_STEP, TQ, TK), BF16)

        def stage_a(tile, slot):
            qb, n = tile
            kt = k_s[key_rows(qb - n), :]
            rs = query_rows(qb)
            z_s[slot, 0] = _dot_nt(qa_s[rs, :], kt)
            z_s[slot, 1] = _dot_nt(qb_s[rs, :], kt)

        def stage_b(tile, slot):
            mask = diff_s[...] > jnp.where(tile[1] == 0, 0, -TK)
            for h in heads:
                z2 = jnp.where(mask, z_s[slot, h] * LOG2_E, MASKED_LOGIT)
                sp = _softplus2(z2)
                zm_s[slot, h] = z2 - sp
                _split_store(hl_s, slot, h, sp)
                rsum_s[slot, h] = jnp.broadcast_to(jnp.sum(sp, axis=1, keepdims=True), (TQ, LANES))

        def stage_c(slot):
            return [_dot(hl_s[slot, h], suffix_s[...]) for h in heads]

        def stage_d(tile, slot, sufs):
            qb, n = tile
            rs = query_rows(qb)
            first = n == 0
            for h in heads:
                car = jnp.where(first, 0.0, carry_s[h])
                car_ref[h, rs, :] = jnp.where(lane == qb - n, car, car_ref[h, rs, :])
                w_s[slot, h] = jnp.exp2(zm_s[slot, h] - (sufs[h] + _tile_lanes(car))).astype(BF16)
                carry_s[h] = car + rsum_s[slot, h]

        def stage_e(tile, slot):
            qb, n = tile
            vt = v_s[key_rows(qb - n), :]
            pv = [_dot(w_s[slot, h], vt) for h in heads]
            rs = query_rows(qb)
            o_ref[rs, :] += jnp.where(head_a, pv[0], pv[1])

        zero = jnp.int32(0)
        stage_a((zero, zero), 0)
        stage_a(_next_tile((zero, zero)), 1)
        stage_b((zero, zero), 0)

        def trip(tile, slot):
            after = _next_tile(tile)
            stage_a(_next_tile(after), slot)
            sufs = stage_c(slot)
            stage_b(after, 1 - slot)
            stage_e(_prev_tile(tile), 1 - slot)
            stage_d(tile, slot, sufs)
            return after

        def two_trips(j, tile):
            return trip(trip(tile, 0), 1)

        n_tiles = n_q * (n_q + 1) // 2
        assert n_tiles % 2 == 0
        lax.fori_loop(0, n_tiles // 2, two_trips, (zero, zero))
        stage_e((jnp.int32(n_q - 1), jnp.int32(n_q - 1)), 1)

        def gate(r, carry):
            rs = pl.ds(pl.multiple_of(r * TQ, TQ), TQ)
            za = za_ref[rs, :]
            mix_ref[rs, :] = ((za * _sigmoid(za)) * o_ref[rs, :]).astype(BF16)
            return carry

        lax.fori_loop(0, n_q, gate, 0)

    col = lambda g: pl.BlockSpec((seq, LANES), _col_block(g), pipeline_mode=pl.Buffered(1))
    n_pairs = ATTN_WIDTH // LANES
    return pl.pallas_call(
        body, name="attn_fwd",
        grid=(n_pairs,),
        in_specs=[col(4), col(5), col(6), col(7)],
        out_specs=(pl.BlockSpec((seq, LANES), lambda p: (0, p)),
                   pl.BlockSpec((seq, LANES), lambda p: (0, p)),
                   pl.BlockSpec((HEADS_PER_STEP, seq, LANES), lambda p: (p, 0, 0))),
        out_shape=(jax.ShapeDtypeStruct((seq, ATTN_WIDTH), F32),
                   jax.ShapeDtypeStruct((seq, ATTN_WIDTH), BF16),
                   jax.ShapeDtypeStruct((n_pairs * HEADS_PER_STEP, seq, LANES), F32)),
        scratch_shapes=[pltpu.VMEM((seq, LANES), BF16), pltpu.VMEM((seq, LANES), BF16),
                        pltpu.VMEM((seq, LANES), BF16), pltpu.VMEM((seq, LANES), BF16),
                        pltpu.VMEM((2 * TK, TK), BF16),
                        pltpu.VMEM((HEADS_PER_STEP, TQ, LANES), F32),
                        pltpu.VMEM((2, HEADS_PER_STEP, TQ, TK), F32),
                        pltpu.VMEM((2, HEADS_PER_STEP, TQ, TK), F32),
                        pltpu.VMEM((2, HEADS_PER_STEP, TQ, 2 * TK), BF16),
                        pltpu.VMEM((2, HEADS_PER_STEP, TQ, TK), BF16),
                        pltpu.VMEM((2, HEADS_PER_STEP, TQ, LANES), F32),
                        pltpu.VMEM((TQ, TK), jnp.int32)],
        compiler_params=_compiler_params(48, ("arbitrary",)),
    )(proj, proj, proj, proj)


def _attn_bwd(proj, o, dmix, carries):
    seq = proj.shape[0]
    n_q = seq // TQ
    n_kb = seq // TK
    heads = range(HEADS_PER_STEP)

    def key_rows(kb):
        return pl.ds(pl.multiple_of(jnp.clip(kb, 0, n_kb - 1) * TK, TK), TK)

    def query_rows(qb):
        return pl.ds(pl.multiple_of(jnp.clip(qb, 0, n_q - 1) * TQ, TQ), TQ)

    def body(q_ref, k_ref, v_ref, za_ref, o_ref, dm_ref, car_ref, dp_ref,
             qa_s, qb_s, k_s, v_s, doa_s, dob_s, suffix_s, prefix_s, dq_s, ccar_s, dk_s, dv_s,
             z_s, zm_s, dlw_s, sg_s, hl_s, hl2_s, w_s, rsum_s, diff_s):
        head_a = _head_mask()
        r_i, s_i = _tri_masks()
        diff_s[...] = r_i - s_i
        _fill_tri(suffix_s, r_i > s_i)
        _fill_tri(prefix_s, r_i < s_i)
        lane = lax.broadcasted_iota(jnp.int32, (1, LANES), 1)

        def prep(r, carry):
            rs = pl.ds(pl.multiple_of(r * TQ, TQ), TQ)
            q = q_ref[rs, :] * Q_SCALE
            qa_s[rs, :] = jnp.where(head_a, q, 0.0).astype(BF16)
            qb_s[rs, :] = jnp.where(head_a, 0.0, q).astype(BF16)
            k_s[rs, :] = k_ref[rs, :].astype(BF16)
            v_s[rs, :] = v_ref[rs, :].astype(BF16)
            za = za_ref[rs, :]
            sg = _sigmoid(za)
            dm = dm_ref[rs, :]
            do = dm * (za * sg)
            doa_s[rs, :] = jnp.where(head_a, do, 0.0).astype(BF16)
            dob_s[rs, :] = jnp.where(head_a, 0.0, do).astype(BF16)
            dp_ref[3, rs, :] = (dm * o_ref[rs, :] * (sg * (1.0 + za * (1.0 - sg)))).astype(BF16)
            dq_s[rs, :] = jnp.zeros((TQ, LANES), F32)
            dk_s[rs, :] = jnp.zeros((TQ, LANES), F32)
            dv_s[rs, :] = jnp.zeros((TQ, LANES), F32)
            return carry

        lax.fori_loop(0, n_q, prep, 0)

        ccar_s[...] = jnp.zeros((HEADS_PER_STEP, TQ, LANES), F32)
        hl2_s[1] = jnp.zeros((HEADS_PER_STEP, TQ, 2 * TK), BF16)
        w_s[1] = jnp.zeros((HEADS_PER_STEP, TQ, TK), BF16)
        dlw_s[1] = jnp.zeros((HEADS_PER_STEP, TQ, TK), F32)
        sg_s[1] = jnp.zeros((HEADS_PER_STEP, TQ, TK), F32)
        rsum_s[1] = jnp.zeros((HEADS_PER_STEP, TQ, LANES), F32)

        def stage_a(tile, slot):
            qb, n = tile
            kt = k_s[key_rows(n), :]
            rs = query_rows(qb)
            z_s[slot, 0] = _dot_nt(qa_s[rs, :], kt)
            z_s[slot, 1] = _dot_nt(qb_s[rs, :], kt)

        def stage_b(tile, slot):
            mask = diff_s[...] > jnp.where(tile[1] == tile[0], 0, -TK)
            for h in heads:
                z2 = jnp.where(mask, z_s[slot, h] * LOG2_E, MASKED_LOGIT)
                sp = _softplus2(z2)
                zm_s[slot, h] = z2 - sp
                _split_store(hl_s, slot, h, sp)

        def stage_c(tile, slot):
            qb, n = tile
            vt = v_s[key_rows(n), :]
            rs = query_rows(qb)
            sufs = [_dot(hl_s[slot, h], suffix_s[...]) for h in heads]
            dws = [_dot_nt(doa_s[rs, :], vt), _dot_nt(dob_s[rs, :], vt)]
            return sufs, dws

        def stage_d(tile, slot, sufs, dws):
            qb, n = tile
            rs = query_rows(qb)
            for h in heads:
                cin = jnp.sum(jnp.where(lane == n, car_ref[h, rs, :], 0.0), axis=1, keepdims=True)
                zm = zm_s[slot, h]
                w = jnp.exp2(zm - (sufs[h] + cin))
                w_s[slot, h] = w.astype(BF16)
                sg_s[slot, h] = jnp.exp2(zm)
                dlw = dws[h] * w
                dlw_s[slot, h] = dlw
                _split_store(hl2_s, slot, h, dlw)
                rsum_s[slot, h] = jnp.broadcast_to(jnp.sum(dlw, axis=1, keepdims=True), (TQ, LANES))

        def stage_e(slot):
            return [_dot(hl2_s[slot, h], prefix_s[...]) for h in heads]

        def stage_f(tile, slot, pres):
            first = tile[1] == 0
            dzs = []
            for h in heads:
                ccar = jnp.where(first, 0.0, ccar_s[h])
                dlw = dlw_s[slot, h]
                dz = dlw - sg_s[slot, h] * (dlw + (pres[h] + _tile_lanes(ccar)))
                dzs.append(dz.astype(BF16))
                ccar_s[h] = ccar + rsum_s[slot, h]
            return dzs

        def stage_g(tile, slot, dzs):
            qb, n = tile
            cs = key_rows(n)
            rs = query_rows(qb)
            kt = k_s[cs, :]
            dq = [_dot(dzs[h], kt) for h in heads]
            dk = _dot_tn(dzs[0], qa_s[rs, :]) + _dot_tn(dzs[1], qb_s[rs, :])
            dv = _dot_tn(w_s[slot, 0], doa_s[rs, :]) + _dot_tn(w_s[slot, 1], dob_s[rs, :])
            dq_s[rs, :] += jnp.where(head_a, dq[0], dq[1])
            dk_s[cs, :] += dk
            dv_s[cs, :] += dv

        zero = jnp.int32(0)
        stage_a((zero, zero), 0)
        stage_a(_next_tile((zero, zero)), 1)
        stage_b((zero, zero), 0)

        def trip(tile, slot):
            after = _next_tile(tile)
            before = _prev_tile(tile)
            pres = stage_e(1 - slot)
            sufs, dws = stage_c(tile, slot)
            stage_a(_next_tile(after), slot)
            stage_b(after, 1 - slot)
            dzs = stage_f(before, 1 - slot, pres)
            stage_g(before, 1 - slot, dzs)
            stage_d(tile, slot, sufs, dws)
            return after

        def two_trips(j, tile):
            return trip(trip(tile, 0), 1)

        n_tiles = n_q * (n_q + 1) // 2
        assert n_tiles % 2 == 0
        lax.fori_loop(0, n_tiles // 2, two_trips, (zero, zero))
        last = (jnp.int32(n_q - 1), jnp.int32(n_q - 1))
        stage_g(last, 1, stage_f(last, 1, stage_e(1)))

        def finish(r, carry):
            rs = pl.ds(pl.multiple_of(r * TQ, TQ), TQ)
            dp_ref[0, rs, :] = (dq_s[rs, :] * Q_SCALE).astype(BF16)
            dp_ref[1, rs, :] = dk_s[rs, :].astype(BF16)
            dp_ref[2, rs, :] = dv_s[rs, :].astype(BF16)
            return carry

        lax.fori_loop(0, n_q, finish, 0)

    def col(g, n_sub=CONV_WIDTH // LANES):
        return pl.BlockSpec((seq, LANES), _col_block(g, n_sub), pipeline_mode=pl.Buffered(1))

    n_pairs = ATTN_WIDTH // LANES
    bf = lambda: pltpu.VMEM((seq, LANES), BF16)
    stage = lambda dtype, width=TK: pltpu.VMEM((2, HEADS_PER_STEP, TQ, width), dtype)
    return pl.pallas_call(
        body, name="attn_bwd",
        grid=(n_pairs,),
        in_specs=[col(4), col(5), col(6), col(7), col(0), col(1),
                  pl.BlockSpec((HEADS_PER_STEP, seq, LANES), lambda p: (p, 0, 0), pipeline_mode=pl.Buffered(1))],
        out_specs=pl.BlockSpec((4, seq, LANES), lambda p: (0, 0, p)),
        out_shape=jax.ShapeDtypeStruct((4, seq, ATTN_WIDTH), BF16),
        scratch_shapes=[bf(), bf(), bf(), bf(), bf(), bf(),
                        pltpu.VMEM((2 * TK, TK), BF16), pltpu.VMEM((2 * TK, TK), BF16),
                        pltpu.VMEM((seq, LANES), F32),
                        pltpu.VMEM((HEADS_PER_STEP, TQ, LANES), F32),
                        pltpu.VMEM((seq, LANES), F32), pltpu.VMEM((seq, LANES), F32),
                        stage(F32), stage(F32), stage(F32), stage(F32),
                        stage(BF16, 2 * TK), stage(BF16, 2 * TK), stage(BF16),
                        pltpu.VMEM((2, HEADS_PER_STEP, TQ, LANES), F32),
                        pltpu.VMEM((TQ, TK), jnp.int32)],
        compiler_params=_compiler_params(56, ("arbitrary",)),
    )(proj, proj, proj, proj, o, dmix, carries)


def _out_ln(mix_c, mix_a, woutg, x, target, gain, bias):
    seq = x.shape[0]
    tm = 256
    inv_d = 1.0 / D_MODEL

    def body(mc_ref, ma_ref, wo_ref, x_ref, t_ref, g_ref, b_ref, dxa_ref, dmix_ref, gw_ref, st_ref):
        @pl.when(pl.program_id(0) == 0)
        def _():
            gw_ref[...] = jnp.zeros(gw_ref.shape, F32)
            st_ref[...] = jnp.zeros(st_ref.shape, F32)

        mc = mc_ref[...]
        ma = ma_ref[...]
        sub = _dot(mc, wo_ref[0:CONV_WIDTH, :]) + _dot(ma, wo_ref[CONV_WIDTH:, :])
        r = DEEPNORM_ALPHA * x_ref[...] + sub
        mu = jnp.mean(r, axis=-1, keepdims=True)
        rc = r - mu
        var = jnp.mean(rc * rc, axis=-1, keepdims=True)
        rstd = lax.rsqrt(var + LN_EPS)
        xhat = rc * rstd
        gain_v = g_ref[...]
        diff = (xhat * gain_v + b_ref[...]) - t_ref[...]
        dy = diff * inv_d
        st_ref[0:1, :] += jnp.sum(dy * xhat, axis=0, keepdims=True)
        st_ref[1:2, :] += jnp.sum(dy, axis=0, keepdims=True)
        st_ref[2:3, :] += jnp.sum(diff * diff, axis=0, keepdims=True)
        dxh = dy * gain_v
        m1 = jnp.mean(dxh, axis=-1, keepdims=True)
        m2 = jnp.mean(dxh * xhat, axis=-1, keepdims=True)
        dr = rstd * (dxh - m1 - xhat * m2)
        dxa_ref[...] = DEEPNORM_ALPHA * dr
        drb = dr.astype(BF16)
        dmix_ref[...] = _dot_nt(drb, wo_ref[...])
        gw_ref[0:CONV_WIDTH, :] += _dot_tn(mc, drb)
        gw_ref[CONV_WIDTH:, :] += _dot_tn(ma, drb)

    rows_d = lambda: pl.BlockSpec((tm, D_MODEL), lambda i: (i, 0))
    rows_h = lambda: pl.BlockSpec((tm, CONV_WIDTH), lambda i: (i, 0))
    whole = lambda shape: pl.BlockSpec(shape, lambda i: (0, 0))
    return pl.pallas_call(
        body, name="out_ln",
        grid=(seq // tm,),
        in_specs=[rows_h(), rows_h(), whole((D_MODEL, D_MODEL)), rows_d(), rows_d(),
                  whole((1, D_MODEL)), whole((1, D_MODEL))],
        out_specs=(rows_d(), rows_d(), whole((D_MODEL, D_MODEL)), whole((SUBLANES, D_MODEL))),
        out_shape=(jax.ShapeDtypeStruct((seq, D_MODEL), F32), jax.ShapeDtypeStruct((seq, D_MODEL), F32),
                   jax.ShapeDtypeStruct((D_MODEL, D_MODEL), F32), jax.ShapeDtypeStruct((SUBLANES, D_MODEL), F32)),
        compiler_params=_compiler_params(48, ("arbitrary",)),
    )(mix_c, mix_a, woutg, x, target, gain, bias)


def _group_maps():
    half = N_GROUPS // 2
    return (lambda g: jnp.minimum(g, half - 1)), (lambda g: jnp.maximum(g - half, 0))


def _grad_w_in(xt, dpc, dpa):
    seq = xt.shape[1]
    half = N_GROUPS // 2
    in_conv, in_attn = _group_maps()

    def body(xt_ref, dc_ref, da_ref, out_ref):
        g = pl.program_id(0)

        @pl.when(g < half)
        def _():
            out_ref[...] = _dot(xt_ref[...], dc_ref[...])

        @pl.when(g >= half)
        def _():
            out_ref[...] = _dot(xt_ref[...], da_ref[...])

    return pl.pallas_call(
        body, name="grad_w_in",
        grid=(N_GROUPS,),
        in_specs=[pl.BlockSpec((D_MODEL, seq), lambda g: (0, 0), pipeline_mode=pl.Buffered(1)),
                  pl.BlockSpec((None, seq, CONV_WIDTH), lambda g: (in_conv(g), 0, 0)),
                  pl.BlockSpec((None, seq, CONV_WIDTH), lambda g: (in_attn(g), 0, 0))],
        out_specs=pl.BlockSpec((None, D_MODEL, CONV_WIDTH), lambda g: (g // 2, 0, g % 2)),
        out_shape=jax.ShapeDtypeStruct((N_CHIPS, D_MODEL, D_MODEL), F32),
        compiler_params=_compiler_params(48, ("arbitrary",)),
    )(xt, dpc, dpa)


def _grad_x(dxa, dpc, dpa, wing):
    seq = dxa.shape[0]
    tm = 256
    half = N_GROUPS // 2

    def body(dxa_ref, dc_ref, da_ref, w_ref, out_ref):
        acc = dxa_ref[...]
        for g in range(N_GROUPS):
            dp = dc_ref[g] if g < half else da_ref[g - half]
            cols = slice((g % 2) * CONV_WIDTH, (g % 2 + 1) * CONV_WIDTH)
            acc = acc + _dot_nt(dp, w_ref[g // 2, :, cols])
        out_ref[...] = acc

    return pl.pallas_call(
        body, name="grad_x",
        grid=(seq // tm,),
        in_specs=[pl.BlockSpec((tm, D_MODEL), lambda i: (i, 0)),
                  pl.BlockSpec((half, tm, CONV_WIDTH), lambda i: (0, i, 0)),
                  pl.BlockSpec((half, tm, CONV_WIDTH), lambda i: (0, i, 0)),
                  pl.BlockSpec((N_CHIPS, D_MODEL, D_MODEL), lambda i: (0, 0, 0), pipeline_mode=pl.Buffered(1))],
        out_specs=pl.BlockSpec((tm, D_MODEL), lambda i: (i, 0)),
        out_shape=jax.ShapeDtypeStruct((seq, D_MODEL), F32),
        compiler_params=_compiler_params(40, ("arbitrary",)),
    )(dxa, dpc, dpa, wing)


def _reduce_grads(gwin, gwout, pack):
    n_shards, n_rows_in, n_cols = gwin.shape
    n_rows_out = gwout.shape[1]
    half_rows = (n_rows_in // 2, n_rows_out // 2)
    row_chunk = 128

    def body(gwin_ref, gwout_ref, pack_ref, oin_ref, oout_ref, opack_ref,
             mine_in, mine_out, sib_in, sib_out, rin, rout, rpack,
             local_sems, sib_send, sib_recv, send_sems, recv_sems, pack_send, pack_recv, fin_send, fin_recv):
        x, y, c = lax.axis_index("x"), lax.axis_index("y"), lax.axis_index("c")
        me = 2 * x + y
        my_id = 4 * x + 2 * y + c
        chips = [(1 - x, y), (x, 1 - y), (1 - x, 1 - y)]
        sibling = (x, y, 1 - c)
        partial = (gwin_ref, gwout_ref)
        mine = (mine_in, mine_out)
        from_sib = (sib_in, sib_out)
        from_chips = (rin, rout)
        result = (oin_ref, oout_ref)

        def half(a, which):
            rows = half_rows[a]
            return pl.ds(pl.multiple_of(which * rows, rows), rows)

        sends = []
        loads = []
        for a in range(2):
            ld = pltpu.make_async_copy(partial[a].at[:, half(a, c), :], mine[a], local_sems.at[a])
            ld.start()
            loads.append(ld)
            cp = pltpu.make_async_remote_copy(
                src_ref=partial[a].at[:, half(a, 1 - c), :], dst_ref=from_sib[a],
                send_sem=sib_send.at[a], recv_sem=sib_recv.at[a], device_id=sibling, device_id_type=MESH)
            cp.start()
            sends.append(cp)

        rpack[my_id] = pack_ref[...]

        def pack_copy(rel, slot, to):
            return pltpu.make_async_remote_copy(
                src_ref=rpack.at[slot], dst_ref=rpack.at[slot],
                send_sem=pack_send.at[rel - 1], recv_sem=pack_recv.at[rel - 1],
                device_id=to, device_id_type=MESH)

        def related(rel):
            px = (1 - x) if rel & 4 else x
            py = (1 - y) if rel & 2 else y
            pc = (1 - c) if rel & 1 else c
            return px, py, pc

        for rel in range(1, N_DEVICES):
            cp = pack_copy(rel, my_id, related(rel))
            cp.start()
            sends.append(cp)

        for ld in loads:
            ld.wait()
        for a in range(2):
            pltpu.make_async_remote_copy(
                src_ref=partial[a].at[:, half(a, 1 - c), :], dst_ref=from_sib[a],
                send_sem=sib_send.at[a], recv_sem=sib_recv.at[a], device_id=sibling, device_id_type=MESH).wait_recv()

        def chip_sum(a, shard):
            def add(r, carry):
                rs = pl.ds(pl.multiple_of(r * row_chunk, row_chunk), row_chunk)
                mine[a][shard, rs, :] = mine[a][shard, rs, :] + from_sib[a][shard, rs, :]
                return carry

            lax.fori_loop(0, half_rows[a] // row_chunk, add, 0)

        def shard_copy(k, a, shard, to):
            return pltpu.make_async_remote_copy(
                src_ref=mine[a].at[shard], dst_ref=from_chips[a].at[k],
                send_sem=send_sems.at[2 * k + a], recv_sem=recv_sems.at[2 * k + a],
                device_id=to, device_id_type=MESH)

        for k, (px, py) in enumerate(chips):
            for a in range(2):
                chip_sum(a, 2 * px + py)
                cp = shard_copy(k, a, 2 * px + py, (px, py, c))
                cp.start()
                sends.append(cp)
        for a in range(2):
            chip_sum(a, me)

        for rel in range(1, N_DEVICES):
            px, py, pc = related(rel)
            pack_copy(rel, 4 * px + 2 * py + pc, (px, py, pc)).wait_recv()
        total = rpack[0]
        for i in range(1, N_DEVICES):
            total = total + rpack[i]
        opack_ref[...] = total

        for k, (px, py) in enumerate(chips):
            for a in range(2):
                shard_copy(k, a, me, (px, py, c)).wait_recv()

        def finish(a):
            def add(r, carry):
                rs = pl.ds(pl.multiple_of(r * row_chunk, row_chunk), row_chunk)
                dst = pl.ds(pl.multiple_of(c * half_rows[a] + r * row_chunk, row_chunk), row_chunk)
                result[a][dst, :] = ((mine[a][me, rs, :] + from_chips[a][0, rs, :])
                                     + from_chips[a][1, rs, :]) + from_chips[a][2, rs, :]
                return carry

            lax.fori_loop(0, half_rows[a] // row_chunk, add, 0)

        def final_copy(a, which):
            ref = result[a].at[half(a, which), :]
            return pltpu.make_async_remote_copy(
                src_ref=ref, dst_ref=ref, send_sem=fin_send.at[a], recv_sem=fin_recv.at[a],
                device_id=sibling, device_id_type=MESH)

        for a in range(2):
            finish(a)
            cp = final_copy(a, c)
            cp.start()
            sends.append(cp)
        for a in range(2):
            final_copy(a, 1 - c).wait_recv()
        for cp in sends:
            cp.wait_send()

    vmem = pl.BlockSpec(memory_space=pltpu.VMEM)
    hbm = pl.BlockSpec(memory_space=pl.ANY)
    in_shape = (n_rows_in, n_cols)
    out_shape = (n_rows_out, n_cols)
    half_in = (half_rows[0], n_cols)
    half_out = (half_rows[1], n_cols)
    return pl.pallas_call(
        body, name="reduce_grads",
        out_shape=(jax.ShapeDtypeStruct(in_shape, F32), jax.ShapeDtypeStruct(out_shape, F32),
                   jax.ShapeDtypeStruct(pack.shape, F32)),
        in_specs=[hbm, hbm, vmem], out_specs=(vmem, vmem, vmem),
        scratch_shapes=[pltpu.VMEM((n_shards,) + half_in, F32), pltpu.VMEM((n_shards,) + half_out, F32),
                        pltpu.VMEM((n_shards,) + half_in, F32), pltpu.VMEM((n_shards,) + half_out, F32),
                        pltpu.VMEM((N_CHIPS - 1,) + half_in, F32), pltpu.VMEM((N_CHIPS - 1,) + half_out, F32),
                        pltpu.VMEM((N_DEVICES,) + pack.shape, F32),
                        pltpu.SemaphoreType.DMA((2,)),
                        pltpu.SemaphoreType.DMA((2,)), pltpu.SemaphoreType.DMA((2,)),
                        pltpu.SemaphoreType.DMA((6,)), pltpu.SemaphoreType.DMA((6,)),
                        pltpu.SemaphoreType.DMA((N_DEVICES - 1,)), pltpu.SemaphoreType.DMA((N_DEVICES - 1,)),
                        pltpu.SemaphoreType.DMA((2,)), pltpu.SemaphoreType.DMA((2,))],
        compiler_params=_compiler_params(48),
    )(gwin, gwout, pack)


def _adamw(name, w, g, m, v):
    n_rows, n_cols = w.shape
    tr = 256 if n_rows % 256 == 0 else n_rows
    m_corr = 1.0 - ADAM_B1 ** ADAM_STEP
    v_corr = 1.0 - ADAM_B2 ** ADAM_STEP

    def body(w_ref, g_ref, m_ref, v_ref, d_ref, nm_ref, nv_ref):
        gv = g_ref[...]
        nm = ADAM_B1 * m_ref[...] + (1.0 - ADAM_B1) * gv
        nv = ADAM_B2 * v_ref[...] + (1.0 - ADAM_B2) * (gv * gv)
        nm_ref[...] = nm
        nv_ref[...] = nv
        d_ref[...] = -ADAM_LR * ((nm / m_corr) / (jnp.sqrt(nv / v_corr) + ADAM_EPS) + ADAM_WD * w_ref[...])

    blk = lambda: pl.BlockSpec((tr, n_cols), lambda i: (i, 0))
    shape = jax.ShapeDtypeStruct(w.shape, F32)
    return pl.pallas_call(
        body, name=name,
        grid=(n_rows // tr,),
        in_specs=[blk(), blk(), blk(), blk()], out_specs=(blk(), blk(), blk()),
        out_shape=(shape, shape, shape),
        compiler_params=_compiler_params(32, ("arbitrary",)),
    )(w, g, m, v)


def kernel(x, w_in, conv_w, w_out, ln_gain, ln_bias, loss_target, m_w_in, m_conv_w, m_w_out, m_ln_gain, m_ln_bias, v_w_in, v_conv_w, v_w_out, v_ln_gain, v_ln_bias):
    xs = x[0]
    target = loss_target[0]

    wing, woutg, cwg = _gather_weights(w_in[0], w_out[0], conv_w[0])
    conv_full = jnp.transpose(cwg, (1, 0, 2)).reshape(3, CONV_WIDTH)
    wout_full = woutg.reshape(D_MODEL, D_MODEL)

    proj, xt = _proj(xs, wing)
    mix_c = _conv_fwd(proj, conv_full)
    o, mix_a, carries = _attn_fwd(proj)
    dxa, dmix, gwout, stats = _out_ln(mix_c, mix_a, wout_full, xs, target, ln_gain, ln_bias)
    dpc, dconv = _conv_bwd(proj, dmix, conv_full)
    dpa = _attn_bwd(proj, o, dmix, carries)
    gwin = _grad_w_in(xt, dpc, dpa)
    grad_x = _grad_x(dxa, dpc, dpa, wing)

    pack = jnp.concatenate(
        [stats[0:2], jnp.pad(dconv[0:3], ((0, 0), (0, D_MODEL - CONV_WIDTH))), jnp.zeros((3, D_MODEL), F32)], axis=0)
    g_w_in, g_w_out, tot = _reduce_grads(gwin, gwout.reshape(N_CHIPS, D_MODEL // N_CHIPS, D_MODEL), pack)

    chip = 2 * lax.axis_index("x") + lax.axis_index("y")
    g_gain = tot[0:1]
    g_bias = tot[1:2]
    g_conv = lax.dynamic_slice(tot, (2, chip * LANES), (3, LANES))

    loss_local = (0.5 / D_MODEL) * jnp.sum(stats[2])
    loss = lax.psum(loss_local, ("x", "y", "c"))

    d_w_in, nm_w_in, nv_w_in = _adamw("adamw_w_in", w_in[0], g_w_in, m_w_in[0], v_w_in[0])
    d_w_out, nm_w_out, nv_w_out = _adamw("adamw_w_out", w_out[0], g_w_out, m_w_out[0], v_w_out[0])
    d_conv, nm_conv, nv_conv = _adamw("adamw_conv_w", conv_w[0], g_conv, m_conv_w[0], v_conv_w[0])
    d_gain, nm_gain, nv_gain = _adamw("adamw_ln_gain", ln_gain, g_gain, m_ln_gain, v_ln_gain)
    d_bias, nm_bias, nv_bias = _adamw("adamw_ln_bias", ln_bias, g_bias, m_ln_bias, v_ln_bias)

    lead = lambda a: a[None]
    return (loss, lead(grad_x),
            lead(g_w_in), lead(g_conv), lead(g_w_out), g_gain, g_bias,
            lead(d_w_in), lead(d_conv), lead(d_w_out), d_gain, d_bias,
            lead(nm_w_in), lead(nm_conv), lead(nm_w_out), nm_gain, nm_bias,
            lead(nv_w_in), lead(nv_conv), lead(nv_w_out), nv_gain, nv_bias)
```

```python
import functools

import jax
import jax.numpy as jnp
from jax import lax
from jax.experimental import pallas as pl
from jax.experimental.pallas import tpu as pltpu

F32 = jnp.float32
BF16 = jnp.bfloat16
MESH = pl.DeviceIdType.MESH

D_MODEL = 1024
CONV_WIDTH = 512
ATTN_WIDTH = 512
HEAD_DIM = 64
N_GROUPS = 8
N_CHIPS = 4
N_DEVICES = 8
LN_EPS = 1e-5
DEEPNORM_ALPHA = 2.0 ** 0.25
Q_SCALE = HEAD_DIM ** -0.5
ADAM_LR = 0.001
ADAM_B1 = 0.9
ADAM_B2 = 0.999
ADAM_EPS = 1e-08
ADAM_WD = 0.01
ADAM_STEP = 10

LANES = 128
SUBLANES = 8
V7X_VMEM_BYTES = 64 * 1024 * 1024
MIB = 1024 * 1024

TQ = 256
TK = 256
HEADS_PER_STEP = LANES // HEAD_DIM
CONV_ROWS = 512


def _compiler_params(vmem_mib, semantics=None):
    assert vmem_mib * MIB < V7X_VMEM_BYTES
    return pltpu.CompilerParams(dimension_semantics=semantics, vmem_limit_bytes=vmem_mib * MIB)


def _sigmoid(z):
    return 1.0 / (1.0 + jnp.exp(-z))


def _dot(a, b):
    return jnp.dot(a, b, preferred_element_type=F32)


def _dot_nt(a, b):
    return lax.dot_general(a, b, (((1,), (1,)), ((), ())), preferred_element_type=F32)


def _dot_tn(a, b):
    return lax.dot_general(a, b, (((0,), (0,)), ((), ())), preferred_element_type=F32)


def _truncate_to_bf16(a):
    bits = lax.bitcast_convert_type(a, jnp.uint32) & jnp.uint32(0xFFFF0000)
    return lax.bitcast_convert_type(bits, F32)


def _split_store(ref, slot, h, a):
    hi = _truncate_to_bf16(a)
    ref[slot, h, 0] = hi.astype(BF16)
    ref[slot, h, 1] = (a - hi).astype(BF16)


def _split_dot(ref, slot, h, tri):
    return _dot(ref[slot, h, 0], tri) + _dot(ref[slot, h, 1], tri)


MASKED_LOGIT = -1e30
LOG2_E = 1.4426950408889634


def _softplus2(z2):
    return jnp.maximum(z2, 0.0) + jnp.log2(1.0 + jnp.exp2(-jnp.abs(z2)))


def _gather_weights(w_in, w_out, conv_w):
    d_rows, d_cols = w_in.shape
    o_rows, o_cols = w_out.shape
    half_rows = (d_rows // 2, o_rows // 2)
    row_chunk = 128

    def body(win_ref, wout_ref, cw_ref, wing_ref, woutg_ref, cwg_ref, send_sems, recv_sems, pass_send, pass_recv):
        x, y, c = lax.axis_index("x"), lax.axis_index("y"), lax.axis_index("c")
        me = 2 * x + y
        srcs = (win_ref, wout_ref)
        bufs = (wing_ref, woutg_ref)

        def half(a, shard, which):
            rows = half_rows[a]
            return bufs[a].at[shard, pl.ds(pl.multiple_of(which * rows, rows), rows), :]

        def cast_half(which):
            for a in range(2):
                def cast(r, carry):
                    rows = pl.ds(pl.multiple_of(which * half_rows[a] + r * row_chunk, row_chunk), row_chunk)
                    bufs[a][me, rows, :] = srcs[a][rows, :].astype(BF16)
                    return carry

                lax.fori_loop(0, half_rows[a] // row_chunk, cast, 0)

        chips = [(1 - x, y), (x, 1 - y), (1 - x, 1 - y)]

        def ici_copy(k, a, shard, to):
            ref = cwg_ref.at[shard] if a == 2 else half(a, shard, c)
            return pltpu.make_async_remote_copy(
                src_ref=ref, dst_ref=ref, send_sem=send_sems.at[3 * k + a], recv_sem=recv_sems.at[3 * k + a],
                device_id=to, device_id_type=MESH)

        def pass_copy(k, a, shard, which):
            ref = half(a, shard, which)
            return pltpu.make_async_remote_copy(
                src_ref=ref, dst_ref=ref, send_sem=pass_send.at[2 * k + a], recv_sem=pass_recv.at[2 * k + a],
                device_id=(x, y, 1 - c), device_id_type=MESH)

        cast_half(c)
        cwg_ref[me] = cw_ref[...]
        sends = []
        for k, (px, py) in enumerate(chips):
            for a in range(3):
                cp = ici_copy(k, a, me, (px, py, c))
                cp.start()
                sends.append(cp)
        cast_half(1 - c)
        for k, (px, py) in enumerate(chips):
            for a in range(2):
                ici_copy(k, a, 2 * px + py, (px, py, c)).wait_recv()
                cp = pass_copy(k, a, 2 * px + py, c)
                cp.start()
                sends.append(cp)
        for k, (px, py) in enumerate(chips):
            ici_copy(k, 2, 2 * px + py, (px, py, c)).wait_recv()
            for a in range(2):
                pass_copy(k, a, 2 * px + py, 1 - c).wait_recv()
        for cp in sends:
            cp.wait_send()

    vmem = pl.BlockSpec(memory_space=pltpu.VMEM)
    return pl.pallas_call(
        body, name="gather_weights",
        out_shape=(jax.ShapeDtypeStruct((N_CHIPS, d_rows, d_cols), BF16),
                   jax.ShapeDtypeStruct((N_CHIPS, o_rows, o_cols), BF16),
                   jax.ShapeDtypeStruct((N_CHIPS,) + conv_w.shape, F32)),
        in_specs=[vmem, vmem, vmem], out_specs=(vmem, vmem, vmem),
        scratch_shapes=[pltpu.SemaphoreType.DMA((9,)), pltpu.SemaphoreType.DMA((9,)),
                        pltpu.SemaphoreType.DMA((6,)), pltpu.SemaphoreType.DMA((6,))],
        compiler_params=_compiler_params(32),
    )(w_in, w_out, conv_w)


def _proj(x, wing):
    seq = x.shape[0]
    tm = 256

    def body(x_ref, w_ref, p_ref, xt_ref):
        xv = x_ref[...]
        xb = xv.astype(BF16)
        for j in range(N_CHIPS):
            p_ref[:, j * D_MODEL:(j + 1) * D_MODEL] = _dot(xb, w_ref[j])
        xt_ref[...] = xv.T.astype(BF16)

    return pl.pallas_call(
        body, name="proj",
        grid=(seq // tm,),
        in_specs=[pl.BlockSpec((tm, D_MODEL), lambda i: (i, 0)),
                  pl.BlockSpec((N_CHIPS, D_MODEL, D_MODEL), lambda i: (0, 0, 0), pipeline_mode=pl.Buffered(1))],
        out_specs=(pl.BlockSpec((tm, N_CHIPS * D_MODEL), lambda i: (i, 0)),
                   pl.BlockSpec((D_MODEL, tm), lambda i: (0, i))),
        out_shape=(jax.ShapeDtypeStruct((seq, N_CHIPS * D_MODEL), F32),
                   jax.ShapeDtypeStruct((D_MODEL, seq), BF16)),
        compiler_params=_compiler_params(40, ("arbitrary",)),
    )(x, wing)


def _col_block(group, n_sub=CONV_WIDTH // LANES):
    return lambda j: (0, group * n_sub + j)


def _shift_down(ext, k, rows):
    return pltpu.roll(ext, k, 0)[SUBLANES:, :]


def _shift_up(ext, k, rows):
    return pltpu.roll(ext, rows + SUBLANES - k, 0)[:rows, :]


def _conv_fwd(proj, conv_w):
    seq = proj.shape[0]
    rows = CONV_ROWS
    n_chunks = seq // rows

    def body(b_ref, c_ref, h_ref, z_ref, w_ref, out_ref, u_s):
        u_s[0:SUBLANES, :] = jnp.zeros((SUBLANES, LANES), F32)

        def fill(r, carry):
            rs = pl.ds(pl.multiple_of(r * rows, rows), rows)
            u_s[pl.ds(pl.multiple_of(r * rows + SUBLANES, SUBLANES), rows), :] = c_ref[rs, :] * h_ref[rs, :]
            return carry

        lax.fori_loop(0, n_chunks, fill, 0)
        w = w_ref[...]

        def chunk(r, carry):
            r0 = pl.multiple_of(r * rows, rows)
            rs = pl.ds(r0, rows)
            ext = u_s[pl.ds(r0, rows + SUBLANES), :]
            u = ext[SUBLANES:, :]
            y = w[2:3, :] * u
            y = y + w[0:1, :] * _shift_down(ext, 2, rows)
            y = y + w[1:2, :] * _shift_down(ext, 1, rows)
            z = z_ref[rs, :]
            out_ref[rs, :] = ((z * _sigmoid(z)) * (b_ref[rs, :] * y)).astype(BF16)
            return carry

        lax.fori_loop(0, n_chunks, chunk, 0)

    col = lambda g: pl.BlockSpec((seq, LANES), _col_block(g))
    return pl.pallas_call(
        body, name="conv_fwd",
        grid=(CONV_WIDTH // LANES,),
        in_specs=[col(0), col(1), col(2), col(3), pl.BlockSpec((3, LANES), lambda j: (0, j))],
        out_specs=pl.BlockSpec((seq, LANES), lambda j: (0, j)),
        out_shape=jax.ShapeDtypeStruct((seq, CONV_WIDTH), BF16),
        scratch_shapes=[pltpu.VMEM((seq + SUBLANES, LANES), F32)],
        compiler_params=_compiler_params(40, ("arbitrary",)),
    )(proj, proj, proj, proj, conv_w)


def _conv_bwd(proj, dmix, conv_w):
    seq = proj.shape[0]
    rows = CONV_ROWS
    n_chunks = seq // rows

    def body(b_ref, c_ref, h_ref, z_ref, d_ref, w_ref, dp_ref, dw_ref, u_s, dy_s):
        u_s[0:SUBLANES, :] = jnp.zeros((SUBLANES, LANES), F32)
        dy_s[seq:seq + SUBLANES, :] = jnp.zeros((SUBLANES, LANES), F32)

        def fill(r, carry):
            r0 = pl.multiple_of(r * rows, rows)
            rs = pl.ds(r0, rows)
            u_s[pl.ds(pl.multiple_of(r0 + SUBLANES, SUBLANES), rows), :] = c_ref[rs, :] * h_ref[rs, :]
            z = z_ref[rs, :]
            dy_s[rs, :] = d_ref[rs, :] * (z * _sigmoid(z)) * b_ref[rs, :]
            return carry

        lax.fori_loop(0, n_chunks, fill, 0)
        w = w_ref[...]

        def chunk(r, acc):
            r0 = pl.multiple_of(r * rows, rows)
            rs = pl.ds(r0, rows)
            ext = u_s[pl.ds(r0, rows + SUBLANES), :]
            u = ext[SUBLANES:, :]
            um1 = _shift_down(ext, 1, rows)
            um2 = _shift_down(ext, 2, rows)
            y = w[2:3, :] * u
            y = y + w[0:1, :] * um2
            y = y + w[1:2, :] * um1
            z = z_ref[rs, :]
            b = b_ref[rs, :]
            dco = d_ref[rs, :]
            sg = _sigmoid(z)
            g = z * sg
            dp_ref[0, rs, :] = (dco * g * y).astype(BF16)
            dp_ref[3, rs, :] = (dco * b * y * (sg * (1.0 + z * (1.0 - sg)))).astype(BF16)
            ext_dy = dy_s[pl.ds(r0, rows + SUBLANES), :]
            dy = ext_dy[:rows, :]
            du = w[2:3, :] * dy + w[1:2, :] * _shift_up(ext_dy, 1, rows) + w[0:1, :] * _shift_up(ext_dy, 2, rows)
            dp_ref[1, rs, :] = (du * h_ref[rs, :]).astype(BF16)
            dp_ref[2, rs, :] = (du * c_ref[rs, :]).astype(BF16)
            a0, a1, a2 = acc
            return (a0 + jnp.sum(dy * um2, axis=0, keepdims=True),
                    a1 + jnp.sum(dy * um1, axis=0, keepdims=True),
                    a2 + jnp.sum(dy * u, axis=0, keepdims=True))

        zero = jnp.zeros((1, LANES), F32)
        a0, a1, a2 = lax.fori_loop(0, n_chunks, chunk, (zero, zero, zero))
        dw_ref[...] = jnp.concatenate([a0, a1, a2, jnp.zeros((SUBLANES - 3, LANES), F32)], axis=0)

    col = lambda g: pl.BlockSpec((seq, LANES), _col_block(g))
    return pl.pallas_call(
        body, name="conv_bwd",
        grid=(CONV_WIDTH // LANES,),
        in_specs=[col(0), col(1), col(2), col(3), col(0), pl.BlockSpec((3, LANES), lambda j: (0, j))],
        out_specs=(pl.BlockSpec((4, seq, LANES), lambda j: (0, 0, j)),
                   pl.BlockSpec((SUBLANES, LANES), lambda j: (0, j))),
        out_shape=(jax.ShapeDtypeStruct((4, seq, CONV_WIDTH), BF16),
                   jax.ShapeDtypeStruct((SUBLANES, CONV_WIDTH), F32)),
        scratch_shapes=[pltpu.VMEM((seq + SUBLANES, LANES), F32), pltpu.VMEM((seq + SUBLANES, LANES), F32)],
        compiler_params=_compiler_params(48, ("arbitrary",)),
    )(proj, proj, proj, proj, dmix, conv_w)


def _tri_masks():
    r = lax.broadcasted_iota(jnp.int32, (TK, TK), 0)
    s = lax.broadcasted_iota(jnp.int32, (TK, TK), 1)
    return r, s


def _head_mask():
    lane = lax.broadcasted_iota(jnp.int32, (1, LANES), 1)
    return lane < HEAD_DIM


def _fill_tri(ref, ones):
    ref[...] = jnp.where(ones, 1.0, 0.0).astype(BF16)


def _tile_lanes(a):
    return jnp.tile(a, (1, TK // LANES))


def _next_tile(tile):
    qb, n = tile
    more = n < qb
    return jnp.where(more, qb, qb + 1), jnp.where(more, n + 1, 0)


def _prev_tile(tile):
    qb, n = tile
    first = n == 0
    last_of_prev = jnp.maximum(qb - 1, 0)
    return jnp.where(first, last_of_prev, qb), jnp.where(first, last_of_prev, n - 1)


def _attn_fwd(proj):
    seq = proj.shape[0]
    n_q = seq // TQ
    assert seq // TK <= LANES

    n_kb = seq // TK
    heads = range(HEADS_PER_STEP)

    def key_rows(kb):
        return pl.ds(pl.multiple_of(jnp.clip(kb, 0, n_kb - 1) * TK, TK), TK)

    def query_rows(qb):
        return pl.ds(pl.multiple_of(jnp.clip(qb, 0, n_q - 1) * TQ, TQ), TQ)

    def body(q_ref, k_ref, v_ref, za_ref, o_ref, mix_ref, car_ref,
             qa_s, qb_s, k_s, v_s, suffix_s, carry_s, z_s, zm_s, hl_s, w_s, rsum_s, diff_s):
        head_a = _head_mask()
        r_i, s_i = _tri_masks()
        diff_s[...] = r_i - s_i
        _fill_tri(suffix_s, r_i > s_i)
        lane = lax.broadcasted_iota(jnp.int32, (1, LANES), 1)

        def prep(r, carry):
            rs = pl.ds(pl.multiple_of(r * TQ, TQ), TQ)
            q = q_ref[rs, :] * Q_SCALE
            qa_s[rs, :] = jnp.where(head_a, q, 0.0).astype(BF16)
            qb_s[rs, :] = jnp.where(head_a, 0.0, q).astype(BF16)
            k_s[rs, :] = k_ref[rs, :].astype(BF16)
            v_s[rs, :] = v_ref[rs, :].astype(BF16)
            o_ref[rs, :] = jnp.zeros((TQ, LANES), F32)
            for h in heads:
                car_ref[h, rs, :] = jnp.zeros((TQ, LANES), F32)
            return carry

        lax.fori_loop(0, n_q, prep, 0)
        carry_s[...] = jnp.zeros((HEADS_PER_STEP, TQ, LANES), F32)

        w_s[1] = jnp.zeros((HEADS_PER_STEP, TQ, TK), BF16)

        def stage_a(tile, slot):
            qb, n = tile
            kt = k_s[key_rows(qb - n), :]
            rs = query_rows(qb)
            z_s[slot, 0] = _dot_nt(qa_s[rs, :], kt)
            z_s[slot, 1] = _dot_nt(qb_s[rs, :], kt)

        def stage_b(tile, slot):
            mask = diff_s[...] > jnp.where(tile[1] == 0, 0, -TK)
            for h in heads:
                z2 = jnp.where(mask, z_s[slot, h] * LOG2_E, MASKED_LOGIT)
                sp = _softplus2(z2)
                zm_s[slot, h] = z2 - sp
                _split_store(hl_s, slot, h, sp)
                rsum_s[slot, h] = jnp.broadcast_to(jnp.sum(sp, axis=1, keepdims=True), (TQ, LANES))

        def stage_c(slot):
            return [_split_dot(hl_s, slot, h, suffix_s[...]) for h in heads]

        def stage_d(tile, slot, sufs):
            qb, n = tile
            rs = query_rows(qb)
            first = n == 0
            for h in heads:
                car = jnp.where(first, 0.0, carry_s[h])
                car_ref[h, rs, :] = jnp.where(lane == qb - n, car, car_ref[h, rs, :])
                w_s[slot, h] = jnp.exp2(zm_s[slot, h] - (sufs[h] + _tile_lanes(car))).astype(BF16)
                carry_s[h] = car + rsum_s[slot, h]

        def stage_e(tile, slot):
            qb, n = tile
            vt = v_s[key_rows(qb - n), :]
            pv = [_dot(w_s[slot, h], vt) for h in heads]
            rs = query_rows(qb)
            o_ref[rs, :] += jnp.where(head_a, pv[0], pv[1])

        zero = jnp.int32(0)
        stage_a((zero, zero), 0)
        stage_a(_next_tile((zero, zero)), 1)
        stage_b((zero, zero), 0)

        def trip(tile, slot):
            after = _next_tile(tile)
            stage_a(_next_tile(after), slot)
            sufs = stage_c(slot)
            stage_b(after, 1 - slot)
            stage_e(_prev_tile(tile), 1 - slot)
            stage_d(tile, slot, sufs)
            return after

        def two_trips(j, tile):
            return trip(trip(tile, 0), 1)

        n_tiles = n_q * (n_q + 1) // 2
        assert n_tiles % 2 == 0
        lax.fori_loop(0, n_tiles // 2, two_trips, (zero, zero))
        stage_e((jnp.int32(n_q - 1), jnp.int32(n_q - 1)), 1)

        def gate(r, carry):
            rs = pl.ds(pl.multiple_of(r * TQ, TQ), TQ)
            za = za_ref[rs, :]
            mix_ref[rs, :] = ((za * _sigmoid(za)) * o_ref[rs, :]).astype(BF16)
            return carry

        lax.fori_loop(0, n_q, gate, 0)

    col = lambda g: pl.BlockSpec((seq, LANES), _col_block(g), pipeline_mode=pl.Buffered(1))
    n_pairs = ATTN_WIDTH // LANES
    return pl.pallas_call(
        body, name="attn_fwd",
        grid=(n_pairs,),
        in_specs=[col(4), col(5), col(6), col(7)],
        out_specs=(pl.BlockSpec((seq, LANES), lambda p: (0, p)),
                   pl.BlockSpec((seq, LANES), lambda p: (0, p)),
                   pl.BlockSpec((HEADS_PER_STEP, seq, LANES), lambda p: (p, 0, 0))),
        out_shape=(jax.ShapeDtypeStruct((seq, ATTN_WIDTH), F32),
                   jax.ShapeDtypeStruct((seq, ATTN_WIDTH), BF16),
                   jax.ShapeDtypeStruct((n_pairs * HEADS_PER_STEP, seq, LANES), F32)),
        scratch_shapes=[pltpu.VMEM((seq, LANES), BF16), pltpu.VMEM((seq, LANES), BF16),
                        pltpu.VMEM((seq, LANES), BF16), pltpu.VMEM((seq, LANES), BF16),
                        pltpu.VMEM((TK, TK), BF16),
                        pltpu.VMEM((HEADS_PER_STEP, TQ, LANES), F32),
                        pltpu.VMEM((2, HEADS_PER_STEP, TQ, TK), F32),
                        pltpu.VMEM((2, HEADS_PER_STEP, TQ, TK), F32),
                        pltpu.VMEM((2, HEADS_PER_STEP, 2, TQ, TK), BF16),
                        pltpu.VMEM((2, HEADS_PER_STEP, TQ, TK), BF16),
                        pltpu.VMEM((2, HEADS_PER_STEP, TQ, LANES), F32),
                        pltpu.VMEM((TQ, TK), jnp.int32)],
        compiler_params=_compiler_params(48, ("arbitrary",)),
    )(proj, proj, proj, proj)


def _attn_bwd(proj, o, dmix, carries):
    seq = proj.shape[0]
    n_q = seq // TQ
    n_kb = seq // TK
    heads = range(HEADS_PER_STEP)

    def key_rows(kb):
        return pl.ds(pl.multiple_of(jnp.clip(kb, 0, n_kb - 1) * TK, TK), TK)

    def query_rows(qb):
        return pl.ds(pl.multiple_of(jnp.clip(qb, 0, n_q - 1) * TQ, TQ), TQ)

    def body(q_ref, k_ref, v_ref, za_ref, o_ref, dm_ref, car_ref, dp_ref,
             qa_s, qb_s, k_s, v_s, doa_s, dob_s, suffix_s, prefix_s, dq_s, ccar_s, dk_s, dv_s,
             z_s, zm_s, dlw_s, sg_s, hl_s, hl2_s, w_s, rsum_s, diff_s):
        head_a = _head_mask()
        r_i, s_i = _tri_masks()
        diff_s[...] = r_i - s_i
        _fill_tri(suffix_s, r_i > s_i)
        _fill_tri(prefix_s, r_i < s_i)
        lane = lax.broadcasted_iota(jnp.int32, (1, LANES), 1)

        def prep(r, carry):
            rs = pl.ds(pl.multiple_of(r * TQ, TQ), TQ)
            q = q_ref[rs, :] * Q_SCALE
            qa_s[rs, :] = jnp.where(head_a, q, 0.0).astype(BF16)
            qb_s[rs, :] = jnp.where(head_a, 0.0, q).astype(BF16)
            k_s[rs, :] = k_ref[rs, :].astype(BF16)
            v_s[rs, :] = v_ref[rs, :].astype(BF16)
            za = za_ref[rs, :]
            sg = _sigmoid(za)
            dm = dm_ref[rs, :]
            do = dm * (za * sg)
            doa_s[rs, :] = jnp.where(head_a, do, 0.0).astype(BF16)
            dob_s[rs, :] = jnp.where(head_a, 0.0, do).astype(BF16)
            dp_ref[3, rs, :] = (dm * o_ref[rs, :] * (sg * (1.0 + za * (1.0 - sg)))).astype(BF16)
            dq_s[rs, :] = jnp.zeros((TQ, LANES), F32)
            dk_s[rs, :] = jnp.zeros((TQ, LANES), F32)
            dv_s[rs, :] = jnp.zeros((TQ, LANES), F32)
            return carry

        lax.fori_loop(0, n_q, prep, 0)

        ccar_s[...] = jnp.zeros((HEADS_PER_STEP, TQ, LANES), F32)
        hl2_s[1] = jnp.zeros((HEADS_PER_STEP, 2, TQ, TK), BF16)
        w_s[1] = jnp.zeros((HEADS_PER_STEP, TQ, TK), BF16)
        dlw_s[1] = jnp.zeros((HEADS_PER_STEP, TQ, TK), F32)
        sg_s[1] = jnp.zeros((HEADS_PER_STEP, TQ, TK), F32)
        rsum_s[1] = jnp.zeros((HEADS_PER_STEP, TQ, LANES), F32)

        def stage_a(tile, slot):
            qb, n = tile
            kt = k_s[key_rows(n), :]
            rs = query_rows(qb)
            z_s[slot, 0] = _dot_nt(qa_s[rs, :], kt)
            z_s[slot, 1] = _dot_nt(qb_s[rs, :], kt)

        def stage_b(tile, slot):
            mask = diff_s[...] > jnp.where(tile[1] == tile[0], 0, -TK)
            for h in heads:
                z2 = jnp.where(mask, z_s[slot, h] * LOG2_E, MASKED_LOGIT)
                sp = _softplus2(z2)
                zm_s[slot, h] = z2 - sp
                _split_store(hl_s, slot, h, sp)

        def stage_c(tile, slot):
            qb, n = tile
            vt = v_s[key_rows(n), :]
            rs = query_rows(qb)
            sufs = [_split_dot(hl_s, slot, h, suffix_s[...]) for h in heads]
            dws = [_dot_nt(doa_s[rs, :], vt), _dot_nt(dob_s[rs, :], vt)]
            return sufs, dws

        def stage_d(tile, slot, sufs, dws):
            qb, n = tile
            rs = query_rows(qb)
            for h in heads:
                cin = jnp.sum(jnp.where(lane == n, car_ref[h, rs, :], 0.0), axis=1, keepdims=True)
                zm = zm_s[slot, h]
                w = jnp.exp2(zm - (sufs[h] + cin))
                w_s[slot, h] = w.astype(BF16)
                sg_s[slot, h] = jnp.exp2(zm)
                dlw = dws[h] * w
                dlw_s[slot, h] = dlw
                _split_store(hl2_s, slot, h, dlw)
                rsum_s[slot, h] = jnp.broadcast_to(jnp.sum(dlw, axis=1, keepdims=True), (TQ, LANES))

        def stage_e(slot):
            return [_split_dot(hl2_s, slot, h, prefix_s[...]) for h in heads]

        def stage_f(tile, slot, pres):
            first = tile[1] == 0
            dzs = []
            for h in heads:
                ccar = jnp.where(first, 0.0, ccar_s[h])
                dlw = dlw_s[slot, h]
                dz = dlw - sg_s[slot, h] * (dlw + (pres[h] + _tile_lanes(ccar)))
                dzs.append(dz.astype(BF16))
                ccar_s[h] = ccar + rsum_s[slot, h]
            return dzs

        def stage_g(tile, slot, dzs):
            qb, n = tile
            cs = key_rows(n)
            rs = query_rows(qb)
            kt = k_s[cs, :]
            dq = [_dot(dzs[h], kt) for h in heads]
            dk = _dot_tn(dzs[0], qa_s[rs, :]) + _dot_tn(dzs[1], qb_s[rs, :])
            dv = _dot_tn(w_s[slot, 0], doa_s[rs, :]) + _dot_tn(w_s[slot, 1], dob_s[rs, :])
            dq_s[rs, :] += jnp.where(head_a, dq[0], dq[1])
            dk_s[cs, :] += dk
            dv_s[cs, :] += dv

        zero = jnp.int32(0)
        stage_a((zero, zero), 0)
        stage_a(_next_tile((zero, zero)), 1)
        stage_b((zero, zero), 0)

        def trip(tile, slot):
            after = _next_tile(tile)
            before = _prev_tile(tile)
            pres = stage_e(1 - slot)
            sufs, dws = stage_c(tile, slot)
            stage_a(_next_tile(after), slot)
            stage_b(after, 1 - slot)
            dzs = stage_f(before, 1 - slot, pres)
            stage_g(before, 1 - slot, dzs)
            stage_d(tile, slot, sufs, dws)
            return after

        def two_trips(j, tile):
            return trip(trip(tile, 0), 1)

        n_tiles = n_q * (n_q + 1) // 2
        assert n_tiles % 2 == 0
        lax.fori_loop(0, n_tiles // 2, two_trips, (zero, zero))
        last = (jnp.int32(n_q - 1), jnp.int32(n_q - 1))
        stage_g(last, 1, stage_f(last, 1, stage_e(1)))

        def finish(r, carry):
            rs = pl.ds(pl.multiple_of(r * TQ, TQ), TQ)
            dp_ref[0, rs, :] = (dq_s[rs, :] * Q_SCALE).astype(BF16)
            dp_ref[1, rs, :] = dk_s[rs, :].astype(BF16)
            dp_ref[2, rs, :] = dv_s[rs, :].astype(BF16)
            return carry

        lax.fori_loop(0, n_q, finish, 0)

    def col(g, n_sub=CONV_WIDTH // LANES):
        return pl.BlockSpec((seq, LANES), _col_block(g, n_sub), pipeline_mode=pl.Buffered(1))

    n_pairs = ATTN_WIDTH // LANES
    bf = lambda: pltpu.VMEM((seq, LANES), BF16)
    stage = lambda dtype: pltpu.VMEM((2, HEADS_PER_STEP, TQ, TK), dtype)
    split_stage = lambda: pltpu.VMEM((2, HEADS_PER_STEP, 2, TQ, TK), BF16)
    return pl.pallas_call(
        body, name="attn_bwd",
        grid=(n_pairs,),
        in_specs=[col(4), col(5), col(6), col(7), col(0), col(1),
                  pl.BlockSpec((HEADS_PER_STEP, seq, LANES), lambda p: (p, 0, 0), pipeline_mode=pl.Buffered(1))],
        out_specs=pl.BlockSpec((4, seq, LANES), lambda p: (0, 0, p)),
        out_shape=jax.ShapeDtypeStruct((4, seq, ATTN_WIDTH), BF16),
        scratch_shapes=[bf(), bf(), bf(), bf(), bf(), bf(),
                        pltpu.VMEM((TK, TK), BF16), pltpu.VMEM((TK, TK), BF16),
                        pltpu.VMEM((seq, LANES), F32),
                        pltpu.VMEM((HEADS_PER_STEP, TQ, LANES), F32),
                        pltpu.VMEM((seq, LANES), F32), pltpu.VMEM((seq, LANES), F32),
                        stage(F32), stage(F32), stage(F32), stage(F32),
                        split_stage(), split_stage(), stage(BF16),
                        pltpu.VMEM((2, HEADS_PER_STEP, TQ, LANES), F32),
                        pltpu.VMEM((TQ, TK), jnp.int32)],
        compiler_params=_compiler_params(56, ("arbitrary",)),
    )(proj, proj, proj, proj, o, dmix, carries)


def _out_ln(mix_c, mix_a, woutg, x, target, gain, bias):
    seq = x.shape[0]
    tm = 256
    inv_d = 1.0 / D_MODEL

    def body(mc_ref, ma_ref, wo_ref, x_ref, t_ref, g_ref, b_ref, dxa_ref, dmix_ref, gw_ref, st_ref):
        @pl.when(pl.program_id(0) == 0)
        def _():
            gw_ref[...] = jnp.zeros(gw_ref.shape, F32)
            st_ref[...] = jnp.zeros(st_ref.shape, F32)

        mc = mc_ref[...]
        ma = ma_ref[...]
        sub = _dot(mc, wo_ref[0:CONV_WIDTH, :]) + _dot(ma, wo_ref[CONV_WIDTH:, :])
        r = DEEPNORM_ALPHA * x_ref[...] + sub
        mu = jnp.mean(r, axis=-1, keepdims=True)
        rc = r - mu
        var = jnp.mean(rc * rc, axis=-1, keepdims=True)
        rstd = lax.rsqrt(var + LN_EPS)
        xhat = rc * rstd
        gain_v = g_ref[...]
        diff = (xhat * gain_v + b_ref[...]) - t_ref[...]
        dy = diff * inv_d
        st_ref[0:1, :] += jnp.sum(dy * xhat, axis=0, keepdims=True)
        st_ref[1:2, :] += jnp.sum(dy, axis=0, keepdims=True)
        st_ref[2:3, :] += jnp.sum(diff * diff, axis=0, keepdims=True)
        dxh = dy * gain_v
        m1 = jnp.mean(dxh, axis=-1, keepdims=True)
        m2 = jnp.mean(dxh * xhat, axis=-1, keepdims=True)
        dr = rstd * (dxh - m1 - xhat * m2)
        dxa_ref[...] = DEEPNORM_ALPHA * dr
        drb = dr.astype(BF16)
        dmix_ref[...] = _dot_nt(drb, wo_ref[...])
        gw_ref[0:CONV_WIDTH, :] += _dot_tn(mc, drb)
        gw_ref[CONV_WIDTH:, :] += _dot_tn(ma, drb)

    rows_d = lambda: pl.BlockSpec((tm, D_MODEL), lambda i: (i, 0))
    rows_h = lambda: pl.BlockSpec((tm, CONV_WIDTH), lambda i: (i, 0))
    whole = lambda shape: pl.BlockSpec(shape, lambda i: (0, 0))
    return pl.pallas_call(
        body, name="out_ln",
        grid=(seq // tm,),
        in_specs=[rows_h(), rows_h(), whole((D_MODEL, D_MODEL)), rows_d(), rows_d(),
                  whole((1, D_MODEL)), whole((1, D_MODEL))],
        out_specs=(rows_d(), rows_d(), whole((D_MODEL, D_MODEL)), whole((SUBLANES, D_MODEL))),
        out_shape=(jax.ShapeDtypeStruct((seq, D_MODEL), F32), jax.ShapeDtypeStruct((seq, D_MODEL), F32),
                   jax.ShapeDtypeStruct((D_MODEL, D_MODEL), F32), jax.ShapeDtypeStruct((SUBLANES, D_MODEL), F32)),
        compiler_params=_compiler_params(48, ("arbitrary",)),
    )(mix_c, mix_a, woutg, x, target, gain, bias)


def _group_maps():
    half = N_GROUPS // 2
    return (lambda g: jnp.minimum(g, half - 1)), (lambda g: jnp.maximum(g - half, 0))


def _grad_w_in(xt, dpc, dpa):
    seq = xt.shape[1]
    half = N_GROUPS // 2
    in_conv, in_attn = _group_maps()

    def body(xt_ref, dc_ref, da_ref, out_ref):
        g = pl.program_id(0)

        @pl.when(g < half)
        def _():
            out_ref[...] = _dot(xt_ref[...], dc_ref[...])

        @pl.when(g >= half)
        def _():
            out_ref[...] = _dot(xt_ref[...], da_ref[...])

    return pl.pallas_call(
        body, name="grad_w_in",
        grid=(N_GROUPS,),
        in_specs=[pl.BlockSpec((D_MODEL, seq), lambda g: (0, 0), pipeline_mode=pl.Buffered(1)),
                  pl.BlockSpec((None, seq, CONV_WIDTH), lambda g: (in_conv(g), 0, 0)),
                  pl.BlockSpec((None, seq, CONV_WIDTH), lambda g: (in_attn(g), 0, 0))],
        out_specs=pl.BlockSpec((None, D_MODEL, CONV_WIDTH), lambda g: (g // 2, 0, g % 2)),
        out_shape=jax.ShapeDtypeStruct((N_CHIPS, D_MODEL, D_MODEL), F32),
        compiler_params=_compiler_params(48, ("arbitrary",)),
    )(xt, dpc, dpa)


def _grad_x(dxa, dpc, dpa, wing):
    seq = dxa.shape[0]
    tm = 256
    half = N_GROUPS // 2

    def body(dxa_ref, dc_ref, da_ref, w_ref, out_ref):
        acc = dxa_ref[...]
        for g in range(N_GROUPS):
            dp = dc_ref[g] if g < half else da_ref[g - half]
            cols = slice((g % 2) * CONV_WIDTH, (g % 2 + 1) * CONV_WIDTH)
            acc = acc + _dot_nt(dp, w_ref[g // 2, :, cols])
        out_ref[...] = acc

    return pl.pallas_call(
        body, name="grad_x",
        grid=(seq // tm,),
        in_specs=[pl.BlockSpec((tm, D_MODEL), lambda i: (i, 0)),
                  pl.BlockSpec((half, tm, CONV_WIDTH), lambda i: (0, i, 0)),
                  pl.BlockSpec((half, tm, CONV_WIDTH), lambda i: (0, i, 0)),
                  pl.BlockSpec((N_CHIPS, D_MODEL, D_MODEL), lambda i: (0, 0, 0), pipeline_mode=pl.Buffered(1))],
        out_specs=pl.BlockSpec((tm, D_MODEL), lambda i: (i, 0)),
        out_shape=jax.ShapeDtypeStruct((seq, D_MODEL), F32),
        compiler_params=_compiler_params(40, ("arbitrary",)),
    )(dxa, dpc, dpa, wing)


PACK_LOSS_ROW = 5


def _reduce_grads(gwin, gwout, pack):
    n_shards, n_rows_in, n_cols = gwin.shape
    n_rows_out = gwout.shape[1]
    half_rows = (n_rows_in // 2, n_rows_out // 2)
    row_chunk = 128

    def body(gwin_ref, gwout_ref, pack_ref, oin_ref, oout_ref, opack_ref,
             mine_in, mine_out, sib_in, sib_out, wire_in, wire_out, rin, rout, rpack,
             local_sems, sib_send, sib_recv, send_sems, recv_sems, pack_send, pack_recv, fin_send, fin_recv):
        x, y, c = lax.axis_index("x"), lax.axis_index("y"), lax.axis_index("c")
        me = 2 * x + y
        my_id = 4 * x + 2 * y + c
        chips = [(1 - x, y), (x, 1 - y), (1 - x, 1 - y)]
        sibling = (x, y, 1 - c)
        partial = (gwin_ref, gwout_ref)
        mine = (mine_in, mine_out)
        from_sib = (sib_in, sib_out)
        wire = (wire_in, wire_out)
        from_chips = (rin, rout)
        result = (oin_ref, oout_ref)

        def half(a, which):
            rows = half_rows[a]
            return pl.ds(pl.multiple_of(which * rows, rows), rows)

        sends = []
        loads = []
        for a in range(2):
            ld = pltpu.make_async_copy(partial[a].at[:, half(a, c), :], mine[a], local_sems.at[a])
            ld.start()
            loads.append(ld)
            cp = pltpu.make_async_remote_copy(
                src_ref=partial[a].at[:, half(a, 1 - c), :], dst_ref=from_sib[a],
                send_sem=sib_send.at[a], recv_sem=sib_recv.at[a], device_id=sibling, device_id_type=MESH)
            cp.start()
            sends.append(cp)

        rpack[my_id] = pack_ref[...]

        def pack_copy(rel, slot, to):
            return pltpu.make_async_remote_copy(
                src_ref=rpack.at[slot], dst_ref=rpack.at[slot],
                send_sem=pack_send.at[rel - 1], recv_sem=pack_recv.at[rel - 1],
                device_id=to, device_id_type=MESH)

        def related(rel):
            px = (1 - x) if rel & 4 else x
            py = (1 - y) if rel & 2 else y
            pc = (1 - c) if rel & 1 else c
            return px, py, pc

        for rel in range(1, N_DEVICES):
            cp = pack_copy(rel, my_id, related(rel))
            cp.start()
            sends.append(cp)

        for ld in loads:
            ld.wait()
        for a in range(2):
            pltpu.make_async_remote_copy(
                src_ref=partial[a].at[:, half(a, 1 - c), :], dst_ref=from_sib[a],
                send_sem=sib_send.at[a], recv_sem=sib_recv.at[a], device_id=sibling, device_id_type=MESH).wait_recv()

        def chip_sum(a, shard):
            def add(r, carry):
                rs = pl.ds(pl.multiple_of(r * row_chunk, row_chunk), row_chunk)
                mine[a][shard, rs, :] = mine[a][shard, rs, :] + from_sib[a][shard, rs, :]
                return carry

            lax.fori_loop(0, half_rows[a] // row_chunk, add, 0)

        def chip_sum_to_wire(a, shard, k):
            def add(r, carry):
                rs = pl.ds(pl.multiple_of(r * row_chunk, row_chunk), row_chunk)
                wire[a][k, rs, :] = (mine[a][shard, rs, :] + from_sib[a][shard, rs, :]).astype(BF16)
                return carry

            lax.fori_loop(0, half_rows[a] // row_chunk, add, 0)

        def shard_copy(k, a, to):
            return pltpu.make_async_remote_copy(
                src_ref=wire[a].at[k], dst_ref=from_chips[a].at[k],
                send_sem=send_sems.at[2 * k + a], recv_sem=recv_sems.at[2 * k + a],
                device_id=to, device_id_type=MESH)

        for k, (px, py) in enumerate(chips):
            for a in range(2):
                chip_sum_to_wire(a, 2 * px + py, k)
                cp = shard_copy(k, a, (px, py, c))
                cp.start()
                sends.append(cp)
        for a in range(2):
            chip_sum(a, me)

        for rel in range(1, N_DEVICES):
            px, py, pc = related(rel)
            pack_copy(rel, 4 * px + 2 * py + pc, (px, py, pc)).wait_recv()
        total = rpack[0]
        for i in range(1, N_DEVICES):
            total = total + rpack[i]
        opack_ref[...] = total
        sq_err = jnp.sum(total[PACK_LOSS_ROW:PACK_LOSS_ROW + 1, :], axis=1, keepdims=True)
        opack_ref[PACK_LOSS_ROW:PACK_LOSS_ROW + 1, :] = jnp.broadcast_to(sq_err * (0.5 / D_MODEL), (1, n_cols))

        for k, (px, py) in enumerate(chips):
            for a in range(2):
                shard_copy(k, a, (px, py, c)).wait_recv()

        def finish(a):
            def add(r, carry):
                rs = pl.ds(pl.multiple_of(r * row_chunk, row_chunk), row_chunk)
                dst = pl.ds(pl.multiple_of(c * half_rows[a] + r * row_chunk, row_chunk), row_chunk)
                result[a][dst, :] = ((mine[a][me, rs, :] + from_chips[a][0, rs, :].astype(F32))
                                     + from_chips[a][1, rs, :].astype(F32)) + from_chips[a][2, rs, :].astype(F32)
                return carry

            lax.fori_loop(0, half_rows[a] // row_chunk, add, 0)

        def final_copy(a, which):
            ref = result[a].at[half(a, which), :]
            return pltpu.make_async_remote_copy(
                src_ref=ref, dst_ref=ref, send_sem=fin_send.at[a], recv_sem=fin_recv.at[a],
                device_id=sibling, device_id_type=MESH)

        for a in range(2):
            finish(a)
            cp = final_copy(a, c)
            cp.start()
            sends.append(cp)
        for a in range(2):
            final_copy(a, 1 - c).wait_recv()
        for cp in sends:
            cp.wait_send()

    vmem = pl.BlockSpec(memory_space=pltpu.VMEM)
    hbm = pl.BlockSpec(memory_space=pl.ANY)
    in_shape = (n_rows_in, n_cols)
    out_shape = (n_rows_out, n_cols)
    half_in = (half_rows[0], n_cols)
    half_out = (half_rows[1], n_cols)
    return pl.pallas_call(
        body, name="reduce_grads",
        out_shape=(jax.ShapeDtypeStruct(in_shape, F32), jax.ShapeDtypeStruct(out_shape, F32),
                   jax.ShapeDtypeStruct(pack.shape, F32)),
        in_specs=[hbm, hbm, vmem], out_specs=(vmem, vmem, vmem),
        scratch_shapes=[pltpu.VMEM((n_shards,) + half_in, F32), pltpu.VMEM((n_shards,) + half_out, F32),
                        pltpu.VMEM((n_shards,) + half_in, F32), pltpu.VMEM((n_shards,) + half_out, F32),
                        pltpu.VMEM((N_CHIPS - 1,) + half_in, BF16), pltpu.VMEM((N_CHIPS - 1,) + half_out, BF16),
                        pltpu.VMEM((N_CHIPS - 1,) + half_in, BF16), pltpu.VMEM((N_CHIPS - 1,) + half_out, BF16),
                        pltpu.VMEM((N_DEVICES,) + pack.shape, F32),
                        pltpu.SemaphoreType.DMA((2,)),
                        pltpu.SemaphoreType.DMA((2,)), pltpu.SemaphoreType.DMA((2,)),
                        pltpu.SemaphoreType.DMA((6,)), pltpu.SemaphoreType.DMA((6,)),
                        pltpu.SemaphoreType.DMA((N_DEVICES - 1,)), pltpu.SemaphoreType.DMA((N_DEVICES - 1,)),
                        pltpu.SemaphoreType.DMA((2,)), pltpu.SemaphoreType.DMA((2,))],
        compiler_params=_compiler_params(48),
    )(gwin, gwout, pack)


def _adamw(name, w, g, m, v):
    n_rows, n_cols = w.shape
    tr = 256 if n_rows % 256 == 0 else n_rows
    m_corr = 1.0 - ADAM_B1 ** ADAM_STEP
    v_corr = 1.0 - ADAM_B2 ** ADAM_STEP

    def body(w_ref, g_ref, m_ref, v_ref, d_ref, nm_ref, nv_ref):
        gv = g_ref[...]
        nm = ADAM_B1 * m_ref[...] + (1.0 - ADAM_B1) * gv
        nv = ADAM_B2 * v_ref[...] + (1.0 - ADAM_B2) * (gv * gv)
        nm_ref[...] = nm
        nv_ref[...] = nv
        d_ref[...] = -ADAM_LR * ((nm / m_corr) / (jnp.sqrt(nv / v_corr) + ADAM_EPS) + ADAM_WD * w_ref[...])

    blk = lambda: pl.BlockSpec((tr, n_cols), lambda i: (i, 0))
    shape = jax.ShapeDtypeStruct(w.shape, F32)
    return pl.pallas_call(
        body, name=name,
        grid=(n_rows // tr,),
        in_specs=[blk(), blk(), blk(), blk()], out_specs=(blk(), blk(), blk()),
        out_shape=(shape, shape, shape),
        compiler_params=_compiler_params(32, ("arbitrary",)),
    )(w, g, m, v)


def kernel(x, w_in, conv_w, w_out, ln_gain, ln_bias, loss_target, m_w_in, m_conv_w, m_w_out, m_ln_gain, m_ln_bias, v_w_in, v_conv_w, v_w_out, v_ln_gain, v_ln_bias):
    xs = x[0]
    target = loss_target[0]

    wing, woutg, cwg = _gather_weights(w_in[0], w_out[0], conv_w[0])
    conv_full = jnp.transpose(cwg, (1, 0, 2)).reshape(3, CONV_WIDTH)
    wout_full = woutg.reshape(D_MODEL, D_MODEL)

    proj, xt = _proj(xs, wing)
    mix_c = _conv_fwd(proj, conv_full)
    o, mix_a, carries = _attn_fwd(proj)
    dxa, dmix, gwout, stats = _out_ln(mix_c, mix_a, wout_full, xs, target, ln_gain, ln_bias)
    dpc, dconv = _conv_bwd(proj, dmix, conv_full)
    dpa = _attn_bwd(proj, o, dmix, carries)
    gwin = _grad_w_in(xt, dpc, dpa)
    grad_x = _grad_x(dxa, dpc, dpa, wing)

    pack = jnp.concatenate(
        [stats[0:2], jnp.pad(dconv[0:3], ((0, 0), (0, D_MODEL - CONV_WIDTH))), stats[2:3],
         jnp.zeros((2, D_MODEL), F32)], axis=0)
    g_w_in, g_w_out, tot = _reduce_grads(gwin, gwout.reshape(N_CHIPS, D_MODEL // N_CHIPS, D_MODEL), pack)

    chip = 2 * lax.axis_index("x") + lax.axis_index("y")
    g_gain = tot[0:1]
    g_bias = tot[1:2]
    g_conv = lax.dynamic_slice(tot, (2, chip * LANES), (3, LANES))
    loss = tot[PACK_LOSS_ROW, 0]

    d_w_in, nm_w_in, nv_w_in = _adamw("adamw_w_in", w_in[0], g_w_in, m_w_in[0], v_w_in[0])
    d_w_out, nm_w_out, nv_w_out = _adamw("adamw_w_out", w_out[0], g_w_out, m_w_out[0], v_w_out[0])
    d_conv, nm_conv, nv_conv = _adamw("adamw_conv_w", conv_w[0], g_conv, m_conv_w[0], v_conv_w[0])
    d_gain, nm_gain, nv_gain = _adamw("adamw_ln_gain", ln_gain, g_gain, m_ln_gain, v_ln_gain)
    d_bias, nm_bias, nv_bias = _adamw("adamw_ln_bias", ln_bias, g_bias, m_ln_bias, v_ln_bias)

    lead = lambda a: a[None]
    return (loss, lead(grad_x),
            lead(g_w_in), lead(g_conv), lead(g_w_out), g_gain, g_bias,
            lead(d_w_in), lead(d_conv), lead(d_w_out), d_gain, d_bias,
            lead(nm_w_in), lead(nm_conv), lead(nm_w_out), nm_gain, nm_bias,
            lead(nv_w_in), lead(nv_conv), lead(nv_w_out), nv_gain, nv_bias)
```

```python
import functools

import jax
import jax.numpy as jnp
from jax import lax
from jax.experimental import pallas as pl
from jax.experimental.pallas import tpu as pltpu

F32 = jnp.float32
BF16 = jnp.bfloat16
MESH = pl.DeviceIdType.MESH

D_MODEL = 1024
CONV_WIDTH = 512
ATTN_WIDTH = 512
HEAD_DIM = 64
N_GROUPS = 8
N_CHIPS = 4
N_DEVICES = 8
LN_EPS = 1e-5
DEEPNORM_ALPHA = 2.0 ** 0.25
Q_SCALE = HEAD_DIM ** -0.5
ADAM_LR = 0.001
ADAM_B1 = 0.9
ADAM_B2 = 0.999
ADAM_EPS = 1e-08
ADAM_WD = 0.01
ADAM_STEP = 10

LANES = 128
SUBLANES = 8
V7X_VMEM_BYTES = 64 * 1024 * 1024
MIB = 1024 * 1024

TQ = 256
TK = 256
HEADS_PER_STEP = LANES // HEAD_DIM
CONV_ROWS = 512


def _compiler_params(vmem_mib, semantics=None):
    assert vmem_mib * MIB < V7X_VMEM_BYTES
    return pltpu.CompilerParams(dimension_semantics=semantics, vmem_limit_bytes=vmem_mib * MIB)


def _sigmoid(z):
    return 1.0 / (1.0 + jnp.exp(-z))


def _dot(a, b):
    return jnp.dot(a, b, preferred_element_type=F32)


def _dot_nt(a, b):
    return lax.dot_general(a, b, (((1,), (1,)), ((), ())), preferred_element_type=F32)


def _dot_tn(a, b):
    return lax.dot_general(a, b, (((0,), (0,)), ((), ())), preferred_element_type=F32)


def _truncate_to_bf16(a):
    bits = lax.bitcast_convert_type(a, jnp.uint32) & jnp.uint32(0xFFFF0000)
    return lax.bitcast_convert_type(bits, F32)


def _split_store(ref, slot, h, a):
    hi = _truncate_to_bf16(a)
    ref[slot, h, 0] = hi.astype(BF16)
    ref[slot, h, 1] = (a - hi).astype(BF16)


def _split_dot_stacked(ref, slot, tri):
    parts = ref[slot].reshape(HEADS_PER_STEP * 2 * TQ, TK)
    res = _dot(parts, tri).reshape(HEADS_PER_STEP, 2, TQ, TK)
    return [res[h, 0] + res[h, 1] for h in range(HEADS_PER_STEP)]


def _split_dot(ref, slot, h, tri):
    return _dot(ref[slot, h, 0], tri) + _dot(ref[slot, h, 1], tri)


MASKED_LOGIT = -1e30
LOG2_E = 1.4426950408889634


def _softplus2(z2):
    return jnp.maximum(z2, 0.0) + jnp.log2(1.0 + jnp.exp2(-jnp.abs(z2)))


def _gather_weights(w_in, w_out, conv_w):
    d_rows, d_cols = w_in.shape
    o_rows, o_cols = w_out.shape
    half_rows = (d_rows // 2, o_rows // 2)
    row_chunk = 128

    def body(win_ref, wout_ref, cw_ref, wing_ref, woutg_ref, cwg_ref, send_sems, recv_sems, pass_send, pass_recv):
        x, y, c = lax.axis_index("x"), lax.axis_index("y"), lax.axis_index("c")
        me = 2 * x + y
        srcs = (win_ref, wout_ref)
        bufs = (wing_ref, woutg_ref)

        def half(a, shard, which):
            rows = half_rows[a]
            return bufs[a].at[shard, pl.ds(pl.multiple_of(which * rows, rows), rows), :]

        def cast_half(which):
            for a in range(2):
                def cast(r, carry):
                    rows = pl.ds(pl.multiple_of(which * half_rows[a] + r * row_chunk, row_chunk), row_chunk)
                    bufs[a][me, rows, :] = srcs[a][rows, :].astype(BF16)
                    return carry

                lax.fori_loop(0, half_rows[a] // row_chunk, cast, 0)

        chips = [(1 - x, y), (x, 1 - y), (1 - x, 1 - y)]

        def ici_copy(k, a, shard, to):
            ref = cwg_ref.at[shard] if a == 2 else half(a, shard, c)
            return pltpu.make_async_remote_copy(
                src_ref=ref, dst_ref=ref, send_sem=send_sems.at[3 * k + a], recv_sem=recv_sems.at[3 * k + a],
                device_id=to, device_id_type=MESH)

        def pass_copy(k, a, shard, which):
            ref = half(a, shard, which)
            return pltpu.make_async_remote_copy(
                src_ref=ref, dst_ref=ref, send_sem=pass_send.at[2 * k + a], recv_sem=pass_recv.at[2 * k + a],
                device_id=(x, y, 1 - c), device_id_type=MESH)

        cast_half(c)
        cwg_ref[me] = cw_ref[...]
        sends = []
        for k, (px, py) in enumerate(chips):
            for a in range(3):
                cp = ici_copy(k, a, me, (px, py, c))
                cp.start()
                sends.append(cp)
        cast_half(1 - c)
        for k, (px, py) in enumerate(chips):
            for a in range(2):
                ici_copy(k, a, 2 * px + py, (px, py, c)).wait_recv()
                cp = pass_copy(k, a, 2 * px + py, c)
                cp.start()
                sends.append(cp)
        for k, (px, py) in enumerate(chips):
            ici_copy(k, 2, 2 * px + py, (px, py, c)).wait_recv()
            for a in range(2):
                pass_copy(k, a, 2 * px + py, 1 - c).wait_recv()
        for cp in sends:
            cp.wait_send()

    vmem = pl.BlockSpec(memory_space=pltpu.VMEM)
    return pl.pallas_call(
        body, name="gather_weights",
        out_shape=(jax.ShapeDtypeStruct((N_CHIPS, d_rows, d_cols), BF16),
                   jax.ShapeDtypeStruct((N_CHIPS, o_rows, o_cols), BF16),
                   jax.ShapeDtypeStruct((N_CHIPS,) + conv_w.shape, F32)),
        in_specs=[vmem, vmem, vmem], out_specs=(vmem, vmem, vmem),
        scratch_shapes=[pltpu.SemaphoreType.DMA((9,)), pltpu.SemaphoreType.DMA((9,)),
                        pltpu.SemaphoreType.DMA((6,)), pltpu.SemaphoreType.DMA((6,))],
        compiler_params=_compiler_params(32),
    )(w_in, w_out, conv_w)


def _proj(x, wing):
    seq = x.shape[0]
    tm = 256

    def body(x_ref, w_ref, p_ref, xt_ref):
        xv = x_ref[...]
        xb = xv.astype(BF16)
        for j in range(N_CHIPS):
            p_ref[:, j * D_MODEL:(j + 1) * D_MODEL] = _dot(xb, w_ref[j])
        xt_ref[...] = xv.T.astype(BF16)

    return pl.pallas_call(
        body, name="proj",
        grid=(seq // tm,),
        in_specs=[pl.BlockSpec((tm, D_MODEL), lambda i: (i, 0)),
                  pl.BlockSpec((N_CHIPS, D_MODEL, D_MODEL), lambda i: (0, 0, 0), pipeline_mode=pl.Buffered(1))],
        out_specs=(pl.BlockSpec((tm, N_CHIPS * D_MODEL), lambda i: (i, 0)),
                   pl.BlockSpec((D_MODEL, tm), lambda i: (0, i))),
        out_shape=(jax.ShapeDtypeStruct((seq, N_CHIPS * D_MODEL), F32),
                   jax.ShapeDtypeStruct((D_MODEL, seq), BF16)),
        compiler_params=_compiler_params(40, ("arbitrary",)),
    )(x, wing)


def _col_block(group, n_sub=CONV_WIDTH // LANES):
    return lambda j: (0, group * n_sub + j)


def _shift_down(ext, k, rows):
    return pltpu.roll(ext, k, 0)[SUBLANES:, :]


def _shift_up(ext, k, rows):
    return pltpu.roll(ext, rows + SUBLANES - k, 0)[:rows, :]


def _conv_fwd(proj, conv_w):
    seq = proj.shape[0]
    rows = CONV_ROWS
    n_chunks = seq // rows

    def body(b_ref, c_ref, h_ref, z_ref, w_ref, out_ref, u_s):
        u_s[0:SUBLANES, :] = jnp.zeros((SUBLANES, LANES), F32)

        def fill(r, carry):
            rs = pl.ds(pl.multiple_of(r * rows, rows), rows)
            u_s[pl.ds(pl.multiple_of(r * rows + SUBLANES, SUBLANES), rows), :] = c_ref[rs, :] * h_ref[rs, :]
            return carry

        lax.fori_loop(0, n_chunks, fill, 0)
        w = w_ref[...]

        def chunk(r, carry):
            r0 = pl.multiple_of(r * rows, rows)
            rs = pl.ds(r0, rows)
            ext = u_s[pl.ds(r0, rows + SUBLANES), :]
            u = ext[SUBLANES:, :]
            y = w[2:3, :] * u
            y = y + w[0:1, :] * _shift_down(ext, 2, rows)
            y = y + w[1:2, :] * _shift_down(ext, 1, rows)
            z = z_ref[rs, :]
            out_ref[rs, :] = ((z * _sigmoid(z)) * (b_ref[rs, :] * y)).astype(BF16)
            return carry

        lax.fori_loop(0, n_chunks, chunk, 0)

    col = lambda g: pl.BlockSpec((seq, LANES), _col_block(g))
    return pl.pallas_call(
        body, name="conv_fwd",
        grid=(CONV_WIDTH // LANES,),
        in_specs=[col(0), col(1), col(2), col(3), pl.BlockSpec((3, LANES), lambda j: (0, j))],
        out_specs=pl.BlockSpec((seq, LANES), lambda j: (0, j)),
        out_shape=jax.ShapeDtypeStruct((seq, CONV_WIDTH), BF16),
        scratch_shapes=[pltpu.VMEM((seq + SUBLANES, LANES), F32)],
        compiler_params=_compiler_params(40, ("arbitrary",)),
    )(proj, proj, proj, proj, conv_w)


def _conv_bwd(proj, dmix, conv_w):
    seq = proj.shape[0]
    rows = CONV_ROWS
    n_chunks = seq // rows

    def body(b_ref, c_ref, h_ref, z_ref, d_ref, w_ref, dp_ref, dw_ref, u_s, dy_s):
        u_s[0:SUBLANES, :] = jnp.zeros((SUBLANES, LANES), F32)
        dy_s[seq:seq + SUBLANES, :] = jnp.zeros((SUBLANES, LANES), F32)

        def fill(r, carry):
            r0 = pl.multiple_of(r * rows, rows)
            rs = pl.ds(r0, rows)
            u_s[pl.ds(pl.multiple_of(r0 + SUBLANES, SUBLANES), rows), :] = c_ref[rs, :] * h_ref[rs, :]
            z = z_ref[rs, :]
            dy_s[rs, :] = d_ref[rs, :] * (z * _sigmoid(z)) * b_ref[rs, :]
            return carry

        lax.fori_loop(0, n_chunks, fill, 0)
        w = w_ref[...]

        def chunk(r, acc):
            r0 = pl.multiple_of(r * rows, rows)
            rs = pl.ds(r0, rows)
            ext = u_s[pl.ds(r0, rows + SUBLANES), :]
            u = ext[SUBLANES:, :]
            um1 = _shift_down(ext, 1, rows)
            um2 = _shift_down(ext, 2, rows)
            y = w[2:3, :] * u
            y = y + w[0:1, :] * um2
            y = y + w[1:2, :] * um1
            z = z_ref[rs, :]
            b = b_ref[rs, :]
            dco = d_ref[rs, :]
            sg = _sigmoid(z)
            g = z * sg
            dp_ref[0, rs, :] = (dco * g * y).astype(BF16)
            dp_ref[3, rs, :] = (dco * b * y * (sg * (1.0 + z * (1.0 - sg)))).astype(BF16)
            ext_dy = dy_s[pl.ds(r0, rows + SUBLANES), :]
            dy = ext_dy[:rows, :]
            du = w[2:3, :] * dy + w[1:2, :] * _shift_up(ext_dy, 1, rows) + w[0:1, :] * _shift_up(ext_dy, 2, rows)
            dp_ref[1, rs, :] = (du * h_ref[rs, :]).astype(BF16)
            dp_ref[2, rs, :] = (du * c_ref[rs, :]).astype(BF16)
            a0, a1, a2 = acc
            return (a0 + jnp.sum(dy * um2, axis=0, keepdims=True),
                    a1 + jnp.sum(dy * um1, axis=0, keepdims=True),
                    a2 + jnp.sum(dy * u, axis=0, keepdims=True))

        zero = jnp.zeros((1, LANES), F32)
        a0, a1, a2 = lax.fori_loop(0, n_chunks, chunk, (zero, zero, zero))
        dw_ref[...] = jnp.concatenate([a0, a1, a2, jnp.zeros((SUBLANES - 3, LANES), F32)], axis=0)

    col = lambda g: pl.BlockSpec((seq, LANES), _col_block(g))
    return pl.pallas_call(
        body, name="conv_bwd",
        grid=(CONV_WIDTH // LANES,),
        in_specs=[col(0), col(1), col(2), col(3), col(0), pl.BlockSpec((3, LANES), lambda j: (0, j))],
        out_specs=(pl.BlockSpec((4, seq, LANES), lambda j: (0, 0, j)),
                   pl.BlockSpec((SUBLANES, LANES), lambda j: (0, j))),
        out_shape=(jax.ShapeDtypeStruct((4, seq, CONV_WIDTH), BF16),
                   jax.ShapeDtypeStruct((SUBLANES, CONV_WIDTH), F32)),
        scratch_shapes=[pltpu.VMEM((seq + SUBLANES, LANES), F32), pltpu.VMEM((seq + SUBLANES, LANES), F32)],
        compiler_params=_compiler_params(48, ("arbitrary",)),
    )(proj, proj, proj, proj, dmix, conv_w)


def _tri_masks():
    r = lax.broadcasted_iota(jnp.int32, (TK, TK), 0)
    s = lax.broadcasted_iota(jnp.int32, (TK, TK), 1)
    return r, s


def _head_mask():
    lane = lax.broadcasted_iota(jnp.int32, (1, LANES), 1)
    return lane < HEAD_DIM


def _fill_tri(ref, ones):
    ref[...] = jnp.where(ones, 1.0, 0.0).astype(BF16)


def _tile_lanes(a):
    return jnp.tile(a, (1, TK // LANES))


def _next_tile(tile):
    qb, n = tile
    more = n < qb
    return jnp.where(more, qb, qb + 1), jnp.where(more, n + 1, 0)


def _prev_tile(tile):
    qb, n = tile
    first = n == 0
    last_of_prev = jnp.maximum(qb - 1, 0)
    return jnp.where(first, last_of_prev, qb), jnp.where(first, last_of_prev, n - 1)


def _attn_fwd(proj):
    seq = proj.shape[0]
    n_q = seq // TQ
    assert seq // TK <= LANES

    n_kb = seq // TK
    heads = range(HEADS_PER_STEP)

    def key_rows(kb):
        return pl.ds(pl.multiple_of(jnp.clip(kb, 0, n_kb - 1) * TK, TK), TK)

    def query_rows(qb):
        return pl.ds(pl.multiple_of(jnp.clip(qb, 0, n_q - 1) * TQ, TQ), TQ)

    def body(q_ref, k_ref, v_ref, za_ref, o_ref, mix_ref, car_ref,
             qq_s, k_s, vt_s, ot_s, suffix_s, carry_s, z_s, zm_s, hl_s, w_s, rsum_s, diff_s):
        head_a = _head_mask()
        r_i, s_i = _tri_masks()
        diff_s[...] = r_i - s_i
        _fill_tri(suffix_s, r_i > s_i)
        lane = lax.broadcasted_iota(jnp.int32, (1, LANES), 1)

        def prep(r, carry):
            rs = pl.ds(pl.multiple_of(r * TQ, TQ), TQ)
            q = q_ref[rs, :] * Q_SCALE
            qq_s[r, 0] = jnp.where(head_a, q, 0.0).astype(BF16)
            qq_s[r, 1] = jnp.where(head_a, 0.0, q).astype(BF16)
            k_s[rs, :] = k_ref[rs, :].astype(BF16)
            vt_s[r] = v_ref[rs, :].T.astype(BF16)
            ot_s[:, rs] = jnp.zeros((LANES, TQ), F32)
            for h in heads:
                car_ref[h, rs, :] = jnp.zeros((TQ, LANES), F32)
            return carry

        lax.fori_loop(0, n_q, prep, 0)
        carry_s[...] = jnp.zeros((HEADS_PER_STEP, TQ, LANES), F32)

        w_s[1] = jnp.zeros((HEADS_PER_STEP, TQ, TK), BF16)

        def stage_a(tile, slot):
            qb, n = tile
            kt = k_s[key_rows(qb - n), :]
            both = qq_s[jnp.clip(qb, 0, n_q - 1)].reshape(HEADS_PER_STEP * TQ, LANES)
            z_s[slot] = _dot_nt(both, kt).reshape(HEADS_PER_STEP, TQ, TK)

        def stage_b(tile, slot):
            mask = diff_s[...] > jnp.where(tile[1] == 0, 0, -TK)
            for h in heads:
                z2 = jnp.where(mask, z_s[slot, h] * LOG2_E, MASKED_LOGIT)
                sp = _softplus2(z2)
                zm_s[slot, h] = z2 - sp
                _split_store(hl_s, slot, h, sp)
                rsum_s[slot, h] = jnp.broadcast_to(jnp.sum(sp, axis=1, keepdims=True), (TQ, LANES))

        def stage_c(slot):
            return _split_dot_stacked(hl_s, slot, suffix_s[...])

        def stage_d(tile, slot, sufs):
            qb, n = tile
            rs = query_rows(qb)
            first = n == 0
            for h in heads:
                car = jnp.where(first, 0.0, carry_s[h])
                car_ref[h, rs, :] = jnp.where(lane == qb - n, car, car_ref[h, rs, :])
                w_s[slot, h] = jnp.exp2(zm_s[slot, h] - (sufs[h] + _tile_lanes(car))).astype(BF16)
                carry_s[h] = car + rsum_s[slot, h]

        def stage_e(tile, slot):
            qb, n = tile
            kbi = jnp.clip(qb - n, 0, n_kb - 1)
            parts = [_dot_nt(vt_s[kbi, h * HEAD_DIM:(h + 1) * HEAD_DIM, :], w_s[slot, h]) for h in heads]
            ot_s[:, query_rows(qb)] += jnp.concatenate(parts, axis=0)

        zero = jnp.int32(0)
        stage_a((zero, zero), 0)
        stage_a(_next_tile((zero, zero)), 1)
        stage_b((zero, zero), 0)

        def trip(tile, slot):
            after = _next_tile(tile)
            stage_a(_next_tile(after), slot)
            sufs = stage_c(slot)
            stage_b(after, 1 - slot)
            stage_e(_prev_tile(tile), 1 - slot)
            stage_d(tile, slot, sufs)
            return after

        def two_trips(j, tile):
            return trip(trip(tile, 0), 1)

        n_tiles = n_q * (n_q + 1) // 2
        assert n_tiles % 2 == 0
        lax.fori_loop(0, n_tiles // 2, two_trips, (zero, zero))
        stage_e((jnp.int32(n_q - 1), jnp.int32(n_q - 1)), 1)

        def gate(r, carry):
            rs = pl.ds(pl.multiple_of(r * TQ, TQ), TQ)
            za = za_ref[rs, :]
            o = ot_s[:, rs].T
            o_ref[rs, :] = o
            mix_ref[rs, :] = ((za * _sigmoid(za)) * o).astype(BF16)
            return carry

        lax.fori_loop(0, n_q, gate, 0)

    col = lambda g: pl.BlockSpec((seq, LANES), _col_block(g), pipeline_mode=pl.Buffered(1))
    n_pairs = ATTN_WIDTH // LANES
    return pl.pallas_call(
        body, name="attn_fwd",
        grid=(n_pairs,),
        in_specs=[col(4), col(5), col(6), col(7)],
        out_specs=(pl.BlockSpec((seq, LANES), lambda p: (0, p)),
                   pl.BlockSpec((seq, LANES), lambda p: (0, p)),
                   pl.BlockSpec((HEADS_PER_STEP, seq, LANES), lambda p: (p, 0, 0))),
        out_shape=(jax.ShapeDtypeStruct((seq, ATTN_WIDTH), F32),
                   jax.ShapeDtypeStruct((seq, ATTN_WIDTH), BF16),
                   jax.ShapeDtypeStruct((n_pairs * HEADS_PER_STEP, seq, LANES), F32)),
        scratch_shapes=[pltpu.VMEM((n_q, HEADS_PER_STEP, TQ, LANES), BF16),
                        pltpu.VMEM((seq, LANES), BF16),
                        pltpu.VMEM((n_kb, LANES, TK), BF16), pltpu.VMEM((LANES, seq), F32),
                        pltpu.VMEM((TK, TK), BF16),
                        pltpu.VMEM((HEADS_PER_STEP, TQ, LANES), F32),
                        pltpu.VMEM((2, HEADS_PER_STEP, TQ, TK), F32),
                        pltpu.VMEM((2, HEADS_PER_STEP, TQ, TK), F32),
                        pltpu.VMEM((2, HEADS_PER_STEP, 2, TQ, TK), BF16),
                        pltpu.VMEM((2, HEADS_PER_STEP, TQ, TK), BF16),
                        pltpu.VMEM((2, HEADS_PER_STEP, TQ, LANES), F32),
                        pltpu.VMEM((TQ, TK), jnp.int32)],
        compiler_params=_compiler_params(48, ("arbitrary",)),
    )(proj, proj, proj, proj)


def _attn_bwd(proj, o, dmix, carries):
    seq = proj.shape[0]
    n_q = seq // TQ
    n_kb = seq // TK
    heads = range(HEADS_PER_STEP)

    def key_rows(kb):
        return pl.ds(pl.multiple_of(jnp.clip(kb, 0, n_kb - 1) * TK, TK), TK)

    def query_rows(qb):
        return pl.ds(pl.multiple_of(jnp.clip(qb, 0, n_q - 1) * TQ, TQ), TQ)

    def body(q_ref, k_ref, v_ref, za_ref, o_ref, dm_ref, car_ref, dp_ref,
             qq_s, dd_s, k_s, v_s, suffix_s, prefix_s, dq_s, ccar_s, dk_s, dv_s,
             z_s, zm_s, dlw_s, sg_s, hl_s, hl2_s, w_s, rsum_s, diff_s, qt_s, dot_s, kt_s):
        head_a = _head_mask()
        r_i, s_i = _tri_masks()
        diff_s[...] = r_i - s_i
        _fill_tri(suffix_s, r_i > s_i)
        _fill_tri(prefix_s, r_i < s_i)
        lane = lax.broadcasted_iota(jnp.int32, (1, LANES), 1)

        def prep(r, carry):
            rs = pl.ds(pl.multiple_of(r * TQ, TQ), TQ)
            q = q_ref[rs, :] * Q_SCALE
            qq_s[r, 0] = jnp.where(head_a, q, 0.0).astype(BF16)
            qq_s[r, 1] = jnp.where(head_a, 0.0, q).astype(BF16)
            qt_s[r] = q.T.astype(BF16)
            k = k_ref[rs, :]
            k_s[rs, :] = k.astype(BF16)
            kt_s[r] = k.T.astype(BF16)
            v_s[rs, :] = v_ref[rs, :].astype(BF16)
            za = za_ref[rs, :]
            sg = _sigmoid(za)
            dm = dm_ref[rs, :]
            do = dm * (za * sg)
            dd_s[r, 0] = jnp.where(head_a, do, 0.0).astype(BF16)
            dd_s[r, 1] = jnp.where(head_a, 0.0, do).astype(BF16)
            dot_s[r] = do.T.astype(BF16)
            dp_ref[3, rs, :] = (dm * o_ref[rs, :] * (sg * (1.0 + za * (1.0 - sg)))).astype(BF16)
            dq_s[:, rs] = jnp.zeros((LANES, TQ), F32)
            dk_s[:, rs] = jnp.zeros((LANES, TQ), F32)
            dv_s[:, rs] = jnp.zeros((LANES, TQ), F32)
            return carry

        lax.fori_loop(0, n_q, prep, 0)

        ccar_s[...] = jnp.zeros((HEADS_PER_STEP, TQ, LANES), F32)
        hl2_s[1] = jnp.zeros((HEADS_PER_STEP, 2, TQ, TK), BF16)
        w_s[1] = jnp.zeros((HEADS_PER_STEP, TQ, TK), BF16)
        dlw_s[1] = jnp.zeros((HEADS_PER_STEP, TQ, TK), F32)
        sg_s[1] = jnp.zeros((HEADS_PER_STEP, TQ, TK), F32)
        rsum_s[1] = jnp.zeros((HEADS_PER_STEP, TQ, LANES), F32)

        def stage_a(tile, slot):
            qb, n = tile
            kt = k_s[key_rows(n), :]
            both = qq_s[jnp.clip(qb, 0, n_q - 1)].reshape(HEADS_PER_STEP * TQ, LANES)
            z_s[slot] = _dot_nt(both, kt).reshape(HEADS_PER_STEP, TQ, TK)

        def stage_b(tile, slot):
            mask = diff_s[...] > jnp.where(tile[1] == tile[0], 0, -TK)
            for h in heads:
                z2 = jnp.where(mask, z_s[slot, h] * LOG2_E, MASKED_LOGIT)
                sp = _softplus2(z2)
                zm_s[slot, h] = z2 - sp
                _split_store(hl_s, slot, h, sp)

        def stage_c(tile, slot):
            qb, n = tile
            vt = v_s[key_rows(n), :]
            sufs = _split_dot_stacked(hl_s, slot, suffix_s[...])
            both = dd_s[jnp.clip(qb, 0, n_q - 1)].reshape(HEADS_PER_STEP * TQ, LANES)
            dws = _dot_nt(both, vt).reshape(HEADS_PER_STEP, TQ, TK)
            return sufs, dws

        def stage_d(tile, slot, sufs, dws):
            qb, n = tile
            rs = query_rows(qb)
            for h in heads:
                cin = jnp.sum(jnp.where(lane == n, car_ref[h, rs, :], 0.0), axis=1, keepdims=True)
                zm = zm_s[slot, h]
                w = jnp.exp2(zm - (sufs[h] + cin))
                w_s[slot, h] = w.astype(BF16)
                sg_s[slot, h] = jnp.exp2(zm)
                dlw = dws[h] * w
                dlw_s[slot, h] = dlw
                _split_store(hl2_s, slot, h, dlw)
                rsum_s[slot, h] = jnp.broadcast_to(jnp.sum(dlw, axis=1, keepdims=True), (TQ, LANES))

        def stage_e(slot):
            return _split_dot_stacked(hl2_s, slot, prefix_s[...])

        def stage_f(tile, slot, pres):
            first = tile[1] == 0
            dzs = []
            for h in heads:
                ccar = jnp.where(first, 0.0, ccar_s[h])
                dlw = dlw_s[slot, h]
                dz = dlw - sg_s[slot, h] * (dlw + (pres[h] + _tile_lanes(ccar)))
                dzs.append(dz.astype(BF16))
                ccar_s[h] = ccar + rsum_s[slot, h]
            return dzs

        def stage_g(tile, slot, dzs):
            qb, n = tile
            cs = key_rows(n)
            rs = query_rows(qb)
            ki = jnp.clip(n, 0, n_kb - 1)
            qi = jnp.clip(qb, 0, n_q - 1)
            dq, dk, dv = [], [], []
            for h in heads:
                dims = slice(h * HEAD_DIM, (h + 1) * HEAD_DIM)
                dq.append(_dot_nt(kt_s[ki, dims, :], dzs[h]))
                dk.append(_dot(qt_s[qi, dims, :], dzs[h]))
                dv.append(_dot(dot_s[qi, dims, :], w_s[slot, h]))
            dq_s[:, rs] += jnp.concatenate(dq, axis=0)
            dk_s[:, cs] += jnp.concatenate(dk, axis=0)
            dv_s[:, cs] += jnp.concatenate(dv, axis=0)

        zero = jnp.int32(0)
        stage_a((zero, zero), 0)
        stage_a(_next_tile((zero, zero)), 1)
        stage_b((zero, zero), 0)

        def trip(tile, slot):
            after = _next_tile(tile)
            before = _prev_tile(tile)
            pres = stage_e(1 - slot)
            sufs, dws = stage_c(tile, slot)
            stage_a(_next_tile(after), slot)
            stage_b(after, 1 - slot)
            dzs = stage_f(before, 1 - slot, pres)
            stage_g(before, 1 - slot, dzs)
            stage_d(tile, slot, sufs, dws)
            return after

        def two_trips(j, tile):
            return trip(trip(tile, 0), 1)

        n_tiles = n_q * (n_q + 1) // 2
        assert n_tiles % 2 == 0
        lax.fori_loop(0, n_tiles // 2, two_trips, (zero, zero))
        last = (jnp.int32(n_q - 1), jnp.int32(n_q - 1))
        stage_g(last, 1, stage_f(last, 1, stage_e(1)))

        def finish(r, carry):
            rs = pl.ds(pl.multiple_of(r * TQ, TQ), TQ)
            dp_ref[0, rs, :] = (dq_s[:, rs].T * Q_SCALE).astype(BF16)
            dp_ref[1, rs, :] = dk_s[:, rs].T.astype(BF16)
            dp_ref[2, rs, :] = dv_s[:, rs].T.astype(BF16)
            return carry

        lax.fori_loop(0, n_q, finish, 0)

    def col(g, n_sub=CONV_WIDTH // LANES):
        return pl.BlockSpec((seq, LANES), _col_block(g, n_sub), pipeline_mode=pl.Buffered(1))

    n_pairs = ATTN_WIDTH // LANES
    bf = lambda: pltpu.VMEM((seq, LANES), BF16)
    by_head = lambda: pltpu.VMEM((n_q, HEADS_PER_STEP, TQ, LANES), BF16)
    transposed = lambda: pltpu.VMEM((n_q, LANES, TQ), BF16)
    stage = lambda dtype: pltpu.VMEM((2, HEADS_PER_STEP, TQ, TK), dtype)
    split_stage = lambda: pltpu.VMEM((2, HEADS_PER_STEP, 2, TQ, TK), BF16)
    return pl.pallas_call(
        body, name="attn_bwd",
        grid=(n_pairs,),
        in_specs=[col(4), col(5), col(6), col(7), col(0), col(1),
                  pl.BlockSpec((HEADS_PER_STEP, seq, LANES), lambda p: (p, 0, 0), pipeline_mode=pl.Buffered(1))],
        out_specs=pl.BlockSpec((4, seq, LANES), lambda p: (0, 0, p)),
        out_shape=jax.ShapeDtypeStruct((4, seq, ATTN_WIDTH), BF16),
        scratch_shapes=[by_head(), by_head(), bf(), bf(),
                        pltpu.VMEM((TK, TK), BF16), pltpu.VMEM((TK, TK), BF16),
                        pltpu.VMEM((LANES, seq), F32),
                        pltpu.VMEM((HEADS_PER_STEP, TQ, LANES), F32),
                        pltpu.VMEM((LANES, seq), F32), pltpu.VMEM((LANES, seq), F32),
                        stage(F32), stage(F32), stage(F32), stage(F32),
                        split_stage(), split_stage(), stage(BF16),
                        pltpu.VMEM((2, HEADS_PER_STEP, TQ, LANES), F32),
                        pltpu.VMEM((TQ, TK), jnp.int32),
                        transposed(), transposed(), transposed()],
        compiler_params=_compiler_params(56, ("arbitrary",)),
    )(proj, proj, proj, proj, o, dmix, carries)


def _out_ln(mix_c, mix_a, woutg, x, target, gain, bias):
    seq = x.shape[0]
    tm = 256
    inv_d = 1.0 / D_MODEL

    def body(mc_ref, ma_ref, wo_ref, x_ref, t_ref, g_ref, b_ref, dxa_ref, dmix_ref, gw_ref, st_ref):
        @pl.when(pl.program_id(0) == 0)
        def _():
            gw_ref[...] = jnp.zeros(gw_ref.shape, F32)
            st_ref[...] = jnp.zeros(st_ref.shape, F32)

        mc = mc_ref[...]
        ma = ma_ref[...]
        sub = _dot(mc, wo_ref[0:CONV_WIDTH, :]) + _dot(ma, wo_ref[CONV_WIDTH:, :])
        r = DEEPNORM_ALPHA * x_ref[...] + sub
        mu = jnp.mean(r, axis=-1, keepdims=True)
        rc = r - mu
        var = jnp.mean(rc * rc, axis=-1, keepdims=True)
        rstd = lax.rsqrt(var + LN_EPS)
        xhat = rc * rstd
        gain_v = g_ref[...]
        diff = (xhat * gain_v + b_ref[...]) - t_ref[...]
        dy = diff * inv_d
        st_ref[0:1, :] += jnp.sum(dy * xhat, axis=0, keepdims=True)
        st_ref[1:2, :] += jnp.sum(dy, axis=0, keepdims=True)
        st_ref[2:3, :] += jnp.sum(diff * diff, axis=0, keepdims=True)
        dxh = dy * gain_v
        m1 = jnp.mean(dxh, axis=-1, keepdims=True)
        m2 = jnp.mean(dxh * xhat, axis=-1, keepdims=True)
        dr = rstd * (dxh - m1 - xhat * m2)
        dxa_ref[...] = DEEPNORM_ALPHA * dr
        drb = dr.astype(BF16)
        dmix_ref[...] = _dot_nt(drb, wo_ref[...])
        gw_ref[0:CONV_WIDTH, :] += _dot_tn(mc, drb)
        gw_ref[CONV_WIDTH:, :] += _dot_tn(ma, drb)

    rows_d = lambda: pl.BlockSpec((tm, D_MODEL), lambda i: (i, 0))
    rows_h = lambda: pl.BlockSpec((tm, CONV_WIDTH), lambda i: (i, 0))
    whole = lambda shape: pl.BlockSpec(shape, lambda i: (0, 0))
    return pl.pallas_call(
        body, name="out_ln",
        grid=(seq // tm,),
        in_specs=[rows_h(), rows_h(), whole((D_MODEL, D_MODEL)), rows_d(), rows_d(),
                  whole((1, D_MODEL)), whole((1, D_MODEL))],
        out_specs=(rows_d(), rows_d(), whole((D_MODEL, D_MODEL)), whole((SUBLANES, D_MODEL))),
        out_shape=(jax.ShapeDtypeStruct((seq, D_MODEL), F32), jax.ShapeDtypeStruct((seq, D_MODEL), F32),
                   jax.ShapeDtypeStruct((D_MODEL, D_MODEL), F32), jax.ShapeDtypeStruct((SUBLANES, D_MODEL), F32)),
        compiler_params=_compiler_params(48, ("arbitrary",)),
    )(mix_c, mix_a, woutg, x, target, gain, bias)


def _group_maps():
    half = N_GROUPS // 2
    return (lambda g: jnp.minimum(g, half - 1)), (lambda g: jnp.maximum(g - half, 0))


def _grad_w_in(xt, dpc, dpa):
    seq = xt.shape[1]
    half = N_GROUPS // 2
    in_conv, in_attn = _group_maps()

    def body(xt_ref, dc_ref, da_ref, out_ref):
        g = pl.program_id(0)

        @pl.when(g < half)
        def _():
            out_ref[...] = _dot(xt_ref[...], dc_ref[...])

        @pl.when(g >= half)
        def _():
            out_ref[...] = _dot(xt_ref[...], da_ref[...])

    return pl.pallas_call(
        body, name="grad_w_in",
        grid=(N_GROUPS,),
        in_specs=[pl.BlockSpec((D_MODEL, seq), lambda g: (0, 0), pipeline_mode=pl.Buffered(1)),
                  pl.BlockSpec((None, seq, CONV_WIDTH), lambda g: (in_conv(g), 0, 0)),
                  pl.BlockSpec((None, seq, CONV_WIDTH), lambda g: (in_attn(g), 0, 0))],
        out_specs=pl.BlockSpec((None, D_MODEL, CONV_WIDTH), lambda g: (g // 2, 0, g % 2)),
        out_shape=jax.ShapeDtypeStruct((N_CHIPS, D_MODEL, D_MODEL), F32),
        compiler_params=_compiler_params(48, ("arbitrary",)),
    )(xt, dpc, dpa)


def _grad_x(dxa, dpc, dpa, wing):
    seq = dxa.shape[0]
    tm = 256
    half = N_GROUPS // 2

    def body(dxa_ref, dc_ref, da_ref, w_ref, out_ref):
        acc = dxa_ref[...]
        for g in range(N_GROUPS):
            dp = dc_ref[g] if g < half else da_ref[g - half]
            cols = slice((g % 2) * CONV_WIDTH, (g % 2 + 1) * CONV_WIDTH)
            acc = acc + _dot_nt(dp, w_ref[g // 2, :, cols])
        out_ref[...] = acc

    return pl.pallas_call(
        body, name="grad_x",
        grid=(seq // tm,),
        in_specs=[pl.BlockSpec((tm, D_MODEL), lambda i: (i, 0)),
                  pl.BlockSpec((half, tm, CONV_WIDTH), lambda i: (0, i, 0)),
                  pl.BlockSpec((half, tm, CONV_WIDTH), lambda i: (0, i, 0)),
                  pl.BlockSpec((N_CHIPS, D_MODEL, D_MODEL), lambda i: (0, 0, 0), pipeline_mode=pl.Buffered(1))],
        out_specs=pl.BlockSpec((tm, D_MODEL), lambda i: (i, 0)),
        out_shape=jax.ShapeDtypeStruct((seq, D_MODEL), F32),
        compiler_params=_compiler_params(40, ("arbitrary",)),
    )(dxa, dpc, dpa, wing)


PACK_LOSS_ROW = 5


def _reduce_grads(gwin, gwout, pack):
    n_shards, n_rows_in, n_cols = gwin.shape
    n_rows_out = gwout.shape[1]
    half_rows = (n_rows_in // 2, n_rows_out // 2)
    row_chunk = 128

    def body(gwin_ref, gwout_ref, pack_ref, oin_ref, oout_ref, opack_ref,
             mine_in, mine_out, sib_in, sib_out, wire_in, wire_out, rin, rout, rpack,
             local_sems, sib_send, sib_recv, send_sems, recv_sems, pack_send, pack_recv, fin_send, fin_recv):
        x, y, c = lax.axis_index("x"), lax.axis_index("y"), lax.axis_index("c")
        me = 2 * x + y
        my_id = 4 * x + 2 * y + c
        chips = [(1 - x, y), (x, 1 - y), (1 - x, 1 - y)]
        sibling = (x, y, 1 - c)
        partial = (gwin_ref, gwout_ref)
        mine = (mine_in, mine_out)
        from_sib = (sib_in, sib_out)
        wire = (wire_in, wire_out)
        from_chips = (rin, rout)
        result = (oin_ref, oout_ref)

        def half(a, which):
            rows = half_rows[a]
            return pl.ds(pl.multiple_of(which * rows, rows), rows)

        sends = []
        loads = []
        for a in range(2):
            ld = pltpu.make_async_copy(partial[a].at[:, half(a, c), :], mine[a], local_sems.at[a])
            ld.start()
            loads.append(ld)
            cp = pltpu.make_async_remote_copy(
                src_ref=partial[a].at[:, half(a, 1 - c), :], dst_ref=from_sib[a],
                send_sem=sib_send.at[a], recv_sem=sib_recv.at[a], device_id=sibling, device_id_type=MESH)
            cp.start()
            sends.append(cp)

        rpack[my_id] = pack_ref[...]

        def pack_copy(rel, slot, to):
            return pltpu.make_async_remote_copy(
                src_ref=rpack.at[slot], dst_ref=rpack.at[slot],
                send_sem=pack_send.at[rel - 1], recv_sem=pack_recv.at[rel - 1],
                device_id=to, device_id_type=MESH)

        def related(rel):
            px = (1 - x) if rel & 4 else x
            py = (1 - y) if rel & 2 else y
            pc = (1 - c) if rel & 1 else c
            return px, py, pc

        for rel in range(1, N_DEVICES):
            cp = pack_copy(rel, my_id, related(rel))
            cp.start()
            sends.append(cp)

        for ld in loads:
            ld.wait()
        for a in range(2):
            pltpu.make_async_remote_copy(
                src_ref=partial[a].at[:, half(a, 1 - c), :], dst_ref=from_sib[a],
                send_sem=sib_send.at[a], recv_sem=sib_recv.at[a], device_id=sibling, device_id_type=MESH).wait_recv()

        def chip_sum(a, shard):
            def add(r, carry):
                rs = pl.ds(pl.multiple_of(r * row_chunk, row_chunk), row_chunk)
                mine[a][shard, rs, :] = mine[a][shard, rs, :] + from_sib[a][shard, rs, :]
                return carry

            lax.fori_loop(0, half_rows[a] // row_chunk, add, 0)

        def chip_sum_to_wire(a, shard, k):
            def add(r, carry):
                rs = pl.ds(pl.multiple_of(r * row_chunk, row_chunk), row_chunk)
                wire[a][k, rs, :] = (mine[a][shard, rs, :] + from_sib[a][shard, rs, :]).astype(BF16)
                return carry

            lax.fori_loop(0, half_rows[a] // row_chunk, add, 0)

        def shard_copy(k, a, to):
            return pltpu.make_async_remote_copy(
                src_ref=wire[a].at[k], dst_ref=from_chips[a].at[k],
                send_sem=send_sems.at[2 * k + a], recv_sem=recv_sems.at[2 * k + a],
                device_id=to, device_id_type=MESH)

        for k, (px, py) in enumerate(chips):
            for a in range(2):
                chip_sum_to_wire(a, 2 * px + py, k)
                cp = shard_copy(k, a, (px, py, c))
                cp.start()
                sends.append(cp)
        for a in range(2):
            chip_sum(a, me)

        for rel in range(1, N_DEVICES):
            px, py, pc = related(rel)
            pack_copy(rel, 4 * px + 2 * py + pc, (px, py, pc)).wait_recv()
        total = rpack[0]
        for i in range(1, N_DEVICES):
            total = total + rpack[i]
        opack_ref[...] = total
        sq_err = jnp.sum(total[PACK_LOSS_ROW:PACK_LOSS_ROW + 1, :], axis=1, keepdims=True)
        opack_ref[PACK_LOSS_ROW:PACK_LOSS_ROW + 1, :] = jnp.broadcast_to(sq_err * (0.5 / D_MODEL), (1, n_cols))

        for k, (px, py) in enumerate(chips):
            for a in range(2):
                shard_copy(k, a, (px, py, c)).wait_recv()

        def finish(a):
            def add(r, carry):
                rs = pl.ds(pl.multiple_of(r * row_chunk, row_chunk), row_chunk)
                dst = pl.ds(pl.multiple_of(c * half_rows[a] + r * row_chunk, row_chunk), row_chunk)
                result[a][dst, :] = ((mine[a][me, rs, :] + from_chips[a][0, rs, :].astype(F32))
                                     + from_chips[a][1, rs, :].astype(F32)) + from_chips[a][2, rs, :].astype(F32)
                return carry

            lax.fori_loop(0, half_rows[a] // row_chunk, add, 0)

        def final_copy(a, which):
            ref = result[a].at[half(a, which), :]
            return pltpu.make_async_remote_copy(
                src_ref=ref, dst_ref=ref, send_sem=fin_send.at[a], recv_sem=fin_recv.at[a],
                device_id=sibling, device_id_type=MESH)

        for a in range(2):
            finish(a)
            cp = final_copy(a, c)
            cp.start()
            sends.append(cp)
        for a in range(2):
            final_copy(a, 1 - c).wait_recv()
        for cp in sends:
            cp.wait_send()

    vmem = pl.BlockSpec(memory_space=pltpu.VMEM)
    hbm = pl.BlockSpec(memory_space=pl.ANY)
    in_shape = (n_rows_in, n_cols)
    out_shape = (n_rows_out, n_cols)
    half_in = (half_rows[0], n_cols)
    half_out = (half_rows[1], n_cols)
    return pl.pallas_call(
        body, name="reduce_grads",
        out_shape=(jax.ShapeDtypeStruct(in_shape, F32), jax.ShapeDtypeStruct(out_shape, F32),
                   jax.ShapeDtypeStruct(pack.shape, F32)),
        in_specs=[hbm, hbm, vmem], out_specs=(vmem, vmem, vmem),
        scratch_shapes=[pltpu.VMEM((n_shards,) + half_in, F32), pltpu.VMEM((n_shards,) + half_out, F32),
                        pltpu.VMEM((n_shards,) + half_in, F32), pltpu.VMEM((n_shards,) + half_out, F32),
                        pltpu.VMEM((N_CHIPS - 1,) + half_in, BF16), pltpu.VMEM((N_CHIPS - 1,) + half_out, BF16),
                        pltpu.VMEM((N_CHIPS - 1,) + half_in, BF16), pltpu.VMEM((N_CHIPS - 1,) + half_out, BF16),
                        pltpu.VMEM((N_DEVICES,) + pack.shape, F32),
                        pltpu.SemaphoreType.DMA((2,)),
                        pltpu.SemaphoreType.DMA((2,)), pltpu.SemaphoreType.DMA((2,)),
                        pltpu.SemaphoreType.DMA((6,)), pltpu.SemaphoreType.DMA((6,)),
                        pltpu.SemaphoreType.DMA((N_DEVICES - 1,)), pltpu.SemaphoreType.DMA((N_DEVICES - 1,)),
                        pltpu.SemaphoreType.DMA((2,)), pltpu.SemaphoreType.DMA((2,))],
        compiler_params=_compiler_params(48),
    )(gwin, gwout, pack)


def _adamw(name, w, g, m, v):
    n_rows, n_cols = w.shape
    tr = 256 if n_rows % 256 == 0 else n_rows
    m_corr = 1.0 - ADAM_B1 ** ADAM_STEP
    v_corr = 1.0 - ADAM_B2 ** ADAM_STEP

    def body(w_ref, g_ref, m_ref, v_ref, d_ref, nm_ref, nv_ref):
        gv = g_ref[...]
        nm = ADAM_B1 * m_ref[...] + (1.0 - ADAM_B1) * gv
        nv = ADAM_B2 * v_ref[...] + (1.0 - ADAM_B2) * (gv * gv)
        nm_ref[...] = nm
        nv_ref[...] = nv
        d_ref[...] = -ADAM_LR * ((nm / m_corr) / (jnp.sqrt(nv / v_corr) + ADAM_EPS) + ADAM_WD * w_ref[...])

    blk = lambda: pl.BlockSpec((tr, n_cols), lambda i: (i, 0))
    shape = jax.ShapeDtypeStruct(w.shape, F32)
    return pl.pallas_call(
        body, name=name,
        grid=(n_rows // tr,),
        in_specs=[blk(), blk(), blk(), blk()], out_specs=(blk(), blk(), blk()),
        out_shape=(shape, shape, shape),
        compiler_params=_compiler_params(32, ("arbitrary",)),
    )(w, g, m, v)


def kernel(x, w_in, conv_w, w_out, ln_gain, ln_bias, loss_target, m_w_in, m_conv_w, m_w_out, m_ln_gain, m_ln_bias, v_w_in, v_conv_w, v_w_out, v_ln_gain, v_ln_bias):
    xs = x[0]
    target = loss_target[0]

    wing, woutg, cwg = _gather_weights(w_in[0], w_out[0], conv_w[0])
    conv_full = jnp.transpose(cwg, (1, 0, 2)).reshape(3, CONV_WIDTH)
    wout_full = woutg.reshape(D_MODEL, D_MODEL)

    proj, xt = _proj(xs, wing)
    mix_c = _conv_fwd(proj, conv_full)
    o, mix_a, carries = _attn_fwd(proj)
    dxa, dmix, gwout, stats = _out_ln(mix_c, mix_a, wout_full, xs, target, ln_gain, ln_bias)
    dpc, dconv = _conv_bwd(proj, dmix, conv_full)
    dpa = _attn_bwd(proj, o, dmix, carries)
    gwin = _grad_w_in(xt, dpc, dpa)
    grad_x = _grad_x(dxa, dpc, dpa, wing)

    pack = jnp.concatenate(
        [stats[0:2], jnp.pad(dconv[0:3], ((0, 0), (0, D_MODEL - CONV_WIDTH))), stats[2:3],
         jnp.zeros((2, D_MODEL), F32)], axis=0)
    g_w_in, g_w_out, tot = _reduce_grads(gwin, gwout.reshape(N_CHIPS, D_MODEL // N_CHIPS, D_MODEL), pack)

    chip = 2 * lax.axis_index("x") + lax.axis_index("y")
    g_gain = tot[0:1]
    g_bias = tot[1:2]
    g_conv = lax.dynamic_slice(tot, (2, chip * LANES), (3, LANES))
    loss = tot[PACK_LOSS_ROW, 0]

    d_w_in, nm_w_in, nv_w_in = _adamw("adamw_w_in", w_in[0], g_w_in, m_w_in[0], v_w_in[0])
    d_w_out, nm_w_out, nv_w_out = _adamw("adamw_w_out", w_out[0], g_w_out, m_w_out[0], v_w_out[0])
    d_conv, nm_conv, nv_conv = _adamw("adamw_conv_w", conv_w[0], g_conv, m_conv_w[0], v_conv_w[0])
    d_gain, nm_gain, nv_gain = _adamw("adamw_ln_gain", ln_gain, g_gain, m_ln_gain, v_ln_gain)
    d_bias, nm_bias, nv_bias = _adamw("adamw_ln_bias", ln_bias, g_bias, m_ln_bias, v_ln_bias)

    lead = lambda a: a[None]
    return (loss, lead(grad_x),
            lead(g_w_in), lead(g_conv), lead(g_w_out), g_gain, g_bias,
            lead(d_w_in), lead(d_conv), lead(d_w_out), d_gain, d_bias,
            lead(nm_w_in), lead(nm_conv), lead(nm_w_out), nm_gain, nm_bias,
            lead(nv_w_in), lead(nv_conv), lead(nv_w_out), nv_gain, nv_bias)
```

```python
import functools

import jax
import jax.numpy as jnp
from jax import lax
from jax.experimental import pallas as pl
from jax.experimental.pallas import tpu as pltpu

F32 = jnp.float32
BF16 = jnp.bfloat16
MESH = pl.DeviceIdType.MESH

D_MODEL = 1024
CONV_WIDTH = 512
ATTN_WIDTH = 512
HEAD_DIM = 64
N_GROUPS = 8
N_CHIPS = 4
N_DEVICES = 8
LN_EPS = 1e-5
DEEPNORM_ALPHA = 2.0 ** 0.25
Q_SCALE = HEAD_DIM ** -0.5
ADAM_LR = 0.001
ADAM_B1 = 0.9
ADAM_B2 = 0.999
ADAM_EPS = 1e-08
ADAM_WD = 0.01
ADAM_STEP = 10

LANES = 128
SUBLANES = 8
V7X_VMEM_BYTES = 64 * 1024 * 1024
MIB = 1024 * 1024

TQ = 256
TK = 256
HEADS_PER_STEP = LANES // HEAD_DIM
CONV_ROWS = 512


def _compiler_params(vmem_mib, semantics=None):
    assert vmem_mib * MIB < V7X_VMEM_BYTES
    return pltpu.CompilerParams(dimension_semantics=semantics, vmem_limit_bytes=vmem_mib * MIB)


def _sigmoid(z):
    return 1.0 / (1.0 + jnp.exp(-z))


def _dot(a, b):
    return jnp.dot(a, b, preferred_element_type=F32)


def _dot_nt(a, b):
    return lax.dot_general(a, b, (((1,), (1,)), ((), ())), preferred_element_type=F32)


def _dot_tn(a, b):
    return lax.dot_general(a, b, (((0,), (0,)), ((), ())), preferred_element_type=F32)


def _truncate_to_bf16(a):
    bits = lax.bitcast_convert_type(a, jnp.uint32) & jnp.uint32(0xFFFF0000)
    return lax.bitcast_convert_type(bits, F32)


def _split_store(ref, slot, h, a):
    hi = _truncate_to_bf16(a)
    ref[slot, h, 0] = hi.astype(BF16)
    ref[slot, h, 1] = (a - hi).astype(BF16)


def _split_dot_stacked(ref, slot, tri):
    parts = ref[slot].reshape(HEADS_PER_STEP * 2 * TQ, TK)
    res = _dot(parts, tri).reshape(HEADS_PER_STEP, 2, TQ, TK)
    return [res[h, 0] + res[h, 1] for h in range(HEADS_PER_STEP)]


def _split_dot(ref, slot, h, tri):
    return _dot(ref[slot, h, 0], tri) + _dot(ref[slot, h, 1], tri)


MASKED_LOGIT = -1e30
LOG2_E = 1.4426950408889634


def _softplus2(z2):
    return jnp.maximum(z2, 0.0) + jnp.log2(1.0 + jnp.exp2(-jnp.abs(z2)))


def _gather_weights(w_in, w_out, conv_w):
    d_rows, d_cols = w_in.shape
    o_rows, o_cols = w_out.shape
    half_rows = (d_rows // 2, o_rows // 2)
    row_chunk = 128

    def body(win_ref, wout_ref, cw_ref, wing_ref, woutg_ref, cwg_ref, send_sems, recv_sems, pass_send, pass_recv):
        x, y, c = lax.axis_index("x"), lax.axis_index("y"), lax.axis_index("c")
        me = 2 * x + y
        srcs = (win_ref, wout_ref)
        bufs = (wing_ref, woutg_ref)

        def half(a, shard, which):
            rows = half_rows[a]
            return bufs[a].at[shard, pl.ds(pl.multiple_of(which * rows, rows), rows), :]

        def cast_half(which):
            for a in range(2):
                def cast(r, carry):
                    rows = pl.ds(pl.multiple_of(which * half_rows[a] + r * row_chunk, row_chunk), row_chunk)
                    bufs[a][me, rows, :] = srcs[a][rows, :].astype(BF16)
                    return carry

                lax.fori_loop(0, half_rows[a] // row_chunk, cast, 0)

        chips = [(1 - x, y), (x, 1 - y), (1 - x, 1 - y)]

        def ici_copy(k, a, shard, to):
            ref = cwg_ref.at[shard] if a == 2 else half(a, shard, c)
            return pltpu.make_async_remote_copy(
                src_ref=ref, dst_ref=ref, send_sem=send_sems.at[3 * k + a], recv_sem=recv_sems.at[3 * k + a],
                device_id=to, device_id_type=MESH)

        def pass_copy(k, a, shard, which):
            ref = half(a, shard, which)
            return pltpu.make_async_remote_copy(
                src_ref=ref, dst_ref=ref, send_sem=pass_send.at[2 * k + a], recv_sem=pass_recv.at[2 * k + a],
                device_id=(x, y, 1 - c), device_id_type=MESH)

        cast_half(c)
        cwg_ref[me] = cw_ref[...]
        sends = []
        for k, (px, py) in enumerate(chips):
            for a in range(3):
                cp = ici_copy(k, a, me, (px, py, c))
                cp.start()
                sends.append(cp)
        cast_half(1 - c)
        for k, (px, py) in enumerate(chips):
            for a in range(2):
                ici_copy(k, a, 2 * px + py, (px, py, c)).wait_recv()
                cp = pass_copy(k, a, 2 * px + py, c)
                cp.start()
                sends.append(cp)
        for k, (px, py) in enumerate(chips):
            ici_copy(k, 2, 2 * px + py, (px, py, c)).wait_recv()
            for a in range(2):
                pass_copy(k, a, 2 * px + py, 1 - c).wait_recv()
        for cp in sends:
            cp.wait_send()

    vmem = pl.BlockSpec(memory_space=pltpu.VMEM)
    return pl.pallas_call(
        body, name="gather_weights",
        out_shape=(jax.ShapeDtypeStruct((N_CHIPS, d_rows, d_cols), BF16),
                   jax.ShapeDtypeStruct((N_CHIPS, o_rows, o_cols), BF16),
                   jax.ShapeDtypeStruct((N_CHIPS,) + conv_w.shape, F32)),
        in_specs=[vmem, vmem, vmem], out_specs=(vmem, vmem, vmem),
        scratch_shapes=[pltpu.SemaphoreType.DMA((9,)), pltpu.SemaphoreType.DMA((9,)),
                        pltpu.SemaphoreType.DMA((6,)), pltpu.SemaphoreType.DMA((6,))],
        compiler_params=_compiler_params(32),
    )(w_in, w_out, conv_w)


def _proj(x, wing):
    seq = x.shape[0]
    tm = 256

    def body(x_ref, w_ref, p_ref, xt_ref):
        xv = x_ref[...]
        xb = xv.astype(BF16)
        for j in range(N_CHIPS):
            p_ref[:, j * D_MODEL:(j + 1) * D_MODEL] = _dot(xb, w_ref[j])
        xt_ref[...] = xv.T.astype(BF16)

    return pl.pallas_call(
        body, name="proj",
        grid=(seq // tm,),
        in_specs=[pl.BlockSpec((tm, D_MODEL), lambda i: (i, 0)),
                  pl.BlockSpec((N_CHIPS, D_MODEL, D_MODEL), lambda i: (0, 0, 0), pipeline_mode=pl.Buffered(1))],
        out_specs=(pl.BlockSpec((tm, N_CHIPS * D_MODEL), lambda i: (i, 0)),
                   pl.BlockSpec((D_MODEL, tm), lambda i: (0, i))),
        out_shape=(jax.ShapeDtypeStruct((seq, N_CHIPS * D_MODEL), F32),
                   jax.ShapeDtypeStruct((D_MODEL, seq), BF16)),
        compiler_params=_compiler_params(40, ("arbitrary",)),
    )(x, wing)


def _col_block(group, n_sub=CONV_WIDTH // LANES):
    return lambda j: (0, group * n_sub + j)


def _shift_down(ext, k, rows):
    return pltpu.roll(ext, k, 0)[SUBLANES:, :]


def _shift_up(ext, k, rows):
    return pltpu.roll(ext, rows + SUBLANES - k, 0)[:rows, :]


def _conv_fwd(proj, conv_w):
    seq = proj.shape[0]
    rows = CONV_ROWS
    n_chunks = seq // rows

    def body(b_ref, c_ref, h_ref, z_ref, w_ref, out_ref, u_s):
        u_s[0:SUBLANES, :] = jnp.zeros((SUBLANES, LANES), F32)

        def fill(r, carry):
            rs = pl.ds(pl.multiple_of(r * rows, rows), rows)
            u_s[pl.ds(pl.multiple_of(r * rows + SUBLANES, SUBLANES), rows), :] = c_ref[rs, :] * h_ref[rs, :]
            return carry

        lax.fori_loop(0, n_chunks, fill, 0)
        w = w_ref[...]

        def chunk(r, carry):
            r0 = pl.multiple_of(r * rows, rows)
            rs = pl.ds(r0, rows)
            ext = u_s[pl.ds(r0, rows + SUBLANES), :]
            u = ext[SUBLANES:, :]
            y = w[2:3, :] * u
            y = y + w[0:1, :] * _shift_down(ext, 2, rows)
            y = y + w[1:2, :] * _shift_down(ext, 1, rows)
            z = z_ref[rs, :]
            out_ref[rs, :] = ((z * _sigmoid(z)) * (b_ref[rs, :] * y)).astype(BF16)
            return carry

        lax.fori_loop(0, n_chunks, chunk, 0)

    col = lambda g: pl.BlockSpec((seq, LANES), _col_block(g))
    return pl.pallas_call(
        body, name="conv_fwd",
        grid=(CONV_WIDTH // LANES,),
        in_specs=[col(0), col(1), col(2), col(3), pl.BlockSpec((3, LANES), lambda j: (0, j))],
        out_specs=pl.BlockSpec((seq, LANES), lambda j: (0, j)),
        out_shape=jax.ShapeDtypeStruct((seq, CONV_WIDTH), BF16),
        scratch_shapes=[pltpu.VMEM((seq + SUBLANES, LANES), F32)],
        compiler_params=_compiler_params(40, ("arbitrary",)),
    )(proj, proj, proj, proj, conv_w)


def _conv_bwd(proj, dmix, conv_w):
    seq = proj.shape[0]
    rows = CONV_ROWS
    n_chunks = seq // rows

    def body(b_ref, c_ref, h_ref, z_ref, d_ref, w_ref, dp_ref, dw_ref, u_s, dy_s):
        u_s[0:SUBLANES, :] = jnp.zeros((SUBLANES, LANES), F32)
        dy_s[seq:seq + SUBLANES, :] = jnp.zeros((SUBLANES, LANES), F32)

        def fill(r, carry):
            r0 = pl.multiple_of(r * rows, rows)
            rs = pl.ds(r0, rows)
            u_s[pl.ds(pl.multiple_of(r0 + SUBLANES, SUBLANES), rows), :] = c_ref[rs, :] * h_ref[rs, :]
            z = z_ref[rs, :]
            dy_s[rs, :] = d_ref[rs, :] * (z * _sigmoid(z)) * b_ref[rs, :]
            return carry

        lax.fori_loop(0, n_chunks, fill, 0)
        w = w_ref[...]

        def chunk(r, acc):
            r0 = pl.multiple_of(r * rows, rows)
            rs = pl.ds(r0, rows)
            ext = u_s[pl.ds(r0, rows + SUBLANES), :]
            u = ext[SUBLANES:, :]
            um1 = _shift_down(ext, 1, rows)
            um2 = _shift_down(ext, 2, rows)
            y = w[2:3, :] * u
            y = y + w[0:1, :] * um2
            y = y + w[1:2, :] * um1
            z = z_ref[rs, :]
            b = b_ref[rs, :]
            dco = d_ref[rs, :]
            sg = _sigmoid(z)
            g = z * sg
            dp_ref[0, rs, :] = (dco * g * y).astype(BF16)
            dp_ref[3, rs, :] = (dco * b * y * (sg * (1.0 + z * (1.0 - sg)))).astype(BF16)
            ext_dy = dy_s[pl.ds(r0, rows + SUBLANES), :]
            dy = ext_dy[:rows, :]
            du = w[2:3, :] * dy + w[1:2, :] * _shift_up(ext_dy, 1, rows) + w[0:1, :] * _shift_up(ext_dy, 2, rows)
            dp_ref[1, rs, :] = (du * h_ref[rs, :]).astype(BF16)
            dp_ref[2, rs, :] = (du * c_ref[rs, :]).astype(BF16)
            a0, a1, a2 = acc
            return (a0 + jnp.sum(dy * um2, axis=0, keepdims=True),
                    a1 + jnp.sum(dy * um1, axis=0, keepdims=True),
                    a2 + jnp.sum(dy * u, axis=0, keepdims=True))

        zero = jnp.zeros((1, LANES), F32)
        a0, a1, a2 = lax.fori_loop(0, n_chunks, chunk, (zero, zero, zero))
        dw_ref[...] = jnp.concatenate([a0, a1, a2, jnp.zeros((SUBLANES - 3, LANES), F32)], axis=0)

    col = lambda g: pl.BlockSpec((seq, LANES), _col_block(g))
    return pl.pallas_call(
        body, name="conv_bwd",
        grid=(CONV_WIDTH // LANES,),
        in_specs=[col(0), col(1), col(2), col(3), col(0), pl.BlockSpec((3, LANES), lambda j: (0, j))],
        out_specs=(pl.BlockSpec((4, seq, LANES), lambda j: (0, 0, j)),
                   pl.BlockSpec((SUBLANES, LANES), lambda j: (0, j))),
        out_shape=(jax.ShapeDtypeStruct((4, seq, CONV_WIDTH), BF16),
                   jax.ShapeDtypeStruct((SUBLANES, CONV_WIDTH), F32)),
        scratch_shapes=[pltpu.VMEM((seq + SUBLANES, LANES), F32), pltpu.VMEM((seq + SUBLANES, LANES), F32)],
        compiler_params=_compiler_params(48, ("arbitrary",)),
    )(proj, proj, proj, proj, dmix, conv_w)


def _tri_masks():
    r = lax.broadcasted_iota(jnp.int32, (TK, TK), 0)
    s = lax.broadcasted_iota(jnp.int32, (TK, TK), 1)
    return r, s


def _head_mask():
    lane = lax.broadcasted_iota(jnp.int32, (1, LANES), 1)
    return lane < HEAD_DIM


def _fill_tri(ref, ones):
    ref[...] = jnp.where(ones, 1.0, 0.0).astype(BF16)


def _tile_lanes(a):
    return jnp.tile(a, (1, TK // LANES))


def _next_tile(tile):
    qb, n = tile
    more = n < qb
    return jnp.where(more, qb, qb + 1), jnp.where(more, n + 1, 0)


def _prev_tile(tile):
    qb, n = tile
    first = n == 0
    last_of_prev = jnp.maximum(qb - 1, 0)
    return jnp.where(first, last_of_prev, qb), jnp.where(first, last_of_prev, n - 1)


def _attn_fwd(proj):
    seq = proj.shape[0]
    n_q = seq // TQ
    assert seq // TK <= LANES

    n_kb = seq // TK
    heads = range(HEADS_PER_STEP)

    def key_rows(kb):
        return pl.ds(pl.multiple_of(jnp.clip(kb, 0, n_kb - 1) * TK, TK), TK)

    def query_rows(qb):
        return pl.ds(pl.multiple_of(jnp.clip(qb, 0, n_q - 1) * TQ, TQ), TQ)

    def body(q_ref, k_ref, v_ref, za_ref, o_ref, mix_ref, car_ref,
             qq_s, k_s, vt_s, ot_s, suffix_s, carry_s, z_s, zm_s, hl_s, w_s, rsum_s, diff_s):
        head_a = _head_mask()
        r_i, s_i = _tri_masks()
        diff_s[...] = r_i - s_i
        _fill_tri(suffix_s, r_i > s_i)
        lane = lax.broadcasted_iota(jnp.int32, (1, LANES), 1)

        def prep(r, carry):
            rs = pl.ds(pl.multiple_of(r * TQ, TQ), TQ)
            q = q_ref[rs, :] * Q_SCALE
            qq_s[r, 0] = jnp.where(head_a, q, 0.0).astype(BF16)
            qq_s[r, 1] = jnp.where(head_a, 0.0, q).astype(BF16)
            k_s[rs, :] = k_ref[rs, :].astype(BF16)
            vt_s[r] = v_ref[rs, :].T.astype(BF16)
            ot_s[:, rs] = jnp.zeros((LANES, TQ), F32)
            for h in heads:
                car_ref[h, rs, :] = jnp.zeros((TQ, LANES), F32)
            return carry

        lax.fori_loop(0, n_q, prep, 0)
        carry_s[...] = jnp.zeros((HEADS_PER_STEP, TQ, LANES), F32)

        w_s[1] = jnp.zeros((HEADS_PER_STEP, TQ, TK), BF16)

        def stage_a(tile, slot):
            qb, n = tile
            kt = k_s[key_rows(qb - n), :]
            both = qq_s[jnp.clip(qb, 0, n_q - 1)].reshape(HEADS_PER_STEP * TQ, LANES)
            z_s[slot] = _dot_nt(both, kt).reshape(HEADS_PER_STEP, TQ, TK)

        def stage_b(tile, slot):
            mask = diff_s[...] > jnp.where(tile[1] == 0, 0, -TK)
            for h in heads:
                z2 = jnp.where(mask, z_s[slot, h] * LOG2_E, MASKED_LOGIT)
                sp = _softplus2(z2)
                zm_s[slot, h] = z2 - sp
                _split_store(hl_s, slot, h, sp)
                rsum_s[slot, h] = jnp.broadcast_to(jnp.sum(sp, axis=1, keepdims=True), (TQ, LANES))

        def stage_c(slot):
            return _split_dot_stacked(hl_s, slot, suffix_s[...])

        def stage_d(tile, slot, sufs):
            qb, n = tile
            rs = query_rows(qb)
            first = n == 0
            for h in heads:
                car = jnp.where(first, 0.0, carry_s[h])
                car_ref[h, rs, :] = jnp.where(lane == qb - n, car, car_ref[h, rs, :])
                w_s[slot, h] = jnp.exp2(zm_s[slot, h] - (sufs[h] + _tile_lanes(car))).astype(BF16)
                carry_s[h] = car + rsum_s[slot, h]

        def stage_e(tile, slot):
            qb, n = tile
            kbi = jnp.clip(qb - n, 0, n_kb - 1)
            parts = [_dot_nt(vt_s[kbi, h * HEAD_DIM:(h + 1) * HEAD_DIM, :], w_s[slot, h]) for h in heads]
            ot_s[:, query_rows(qb)] += jnp.concatenate(parts, axis=0)

        zero = jnp.int32(0)
        stage_a((zero, zero), 0)
        stage_a(_next_tile((zero, zero)), 1)
        stage_b((zero, zero), 0)

        def trip(tile, slot):
            after = _next_tile(tile)
            stage_a(_next_tile(after), slot)
            sufs = stage_c(slot)
            stage_b(after, 1 - slot)
            stage_e(_prev_tile(tile), 1 - slot)
            stage_d(tile, slot, sufs)
            return after

        def two_trips(j, tile):
            return trip(trip(tile, 0), 1)

        n_tiles = n_q * (n_q + 1) // 2
        assert n_tiles % 2 == 0
        lax.fori_loop(0, n_tiles // 2, two_trips, (zero, zero))
        stage_e((jnp.int32(n_q - 1), jnp.int32(n_q - 1)), 1)

        def gate(r, carry):
            rs = pl.ds(pl.multiple_of(r * TQ, TQ), TQ)
            za = za_ref[rs, :]
            o = ot_s[:, rs].T
            o_ref[rs, :] = o
            mix_ref[rs, :] = ((za * _sigmoid(za)) * o).astype(BF16)
            return carry

        lax.fori_loop(0, n_q, gate, 0)

    col = lambda g: pl.BlockSpec((seq, LANES), _col_block(g), pipeline_mode=pl.Buffered(1))
    n_pairs = ATTN_WIDTH // LANES
    return pl.pallas_call(
        body, name="attn_fwd",
        grid=(n_pairs,),
        in_specs=[col(4), col(5), col(6), col(7)],
        out_specs=(pl.BlockSpec((seq, LANES), lambda p: (0, p)),
                   pl.BlockSpec((seq, LANES), lambda p: (0, p)),
                   pl.BlockSpec((HEADS_PER_STEP, seq, LANES), lambda p: (p, 0, 0))),
        out_shape=(jax.ShapeDtypeStruct((seq, ATTN_WIDTH), F32),
                   jax.ShapeDtypeStruct((seq, ATTN_WIDTH), BF16),
                   jax.ShapeDtypeStruct((n_pairs * HEADS_PER_STEP, seq, LANES), F32)),
        scratch_shapes=[pltpu.VMEM((n_q, HEADS_PER_STEP, TQ, LANES), BF16),
                        pltpu.VMEM((seq, LANES), BF16),
                        pltpu.VMEM((n_kb, LANES, TK), BF16), pltpu.VMEM((LANES, seq), F32),
                        pltpu.VMEM((TK, TK), BF16),
                        pltpu.VMEM((HEADS_PER_STEP, TQ, LANES), F32),
                        pltpu.VMEM((2, HEADS_PER_STEP, TQ, TK), F32),
                        pltpu.VMEM((2, HEADS_PER_STEP, TQ, TK), F32),
                        pltpu.VMEM((2, HEADS_PER_STEP, 2, TQ, TK), BF16),
                        pltpu.VMEM((2, HEADS_PER_STEP, TQ, TK), BF16),
                        pltpu.VMEM((2, HEADS_PER_STEP, TQ, LANES), F32),
                        pltpu.VMEM((TQ, TK), jnp.int32)],
        compiler_params=_compiler_params(48, ("arbitrary",)),
    )(proj, proj, proj, proj)


def _attn_bwd(proj, o, dmix, carries):
    seq = proj.shape[0]
    n_q = seq // TQ
    n_kb = seq // TK
    heads = range(HEADS_PER_STEP)

    def key_rows(kb):
        return pl.ds(pl.multiple_of(jnp.clip(kb, 0, n_kb - 1) * TK, TK), TK)

    def query_rows(qb):
        return pl.ds(pl.multiple_of(jnp.clip(qb, 0, n_q - 1) * TQ, TQ), TQ)

    def body(q_ref, k_ref, v_ref, za_ref, o_ref, dm_ref, car_ref, dp_ref,
             qq_s, dd_s, k_s, v_s, suffix_s, prefix_s, dq_s, ccar_s, dk_s, dv_s,
             z_s, zm_s, dlw_s, sg_s, hl_s, hl2_s, w_s, rsum_s, diff_s, qt_s, dot_s, kt_s):
        head_a = _head_mask()
        r_i, s_i = _tri_masks()
        diff_s[...] = r_i - s_i
        _fill_tri(suffix_s, r_i > s_i)
        _fill_tri(prefix_s, r_i < s_i)
        lane = lax.broadcasted_iota(jnp.int32, (1, LANES), 1)

        def prep(r, carry):
            rs = pl.ds(pl.multiple_of(r * TQ, TQ), TQ)
            q = q_ref[rs, :] * Q_SCALE
            qq_s[r, 0] = jnp.where(head_a, q, 0.0).astype(BF16)
            qq_s[r, 1] = jnp.where(head_a, 0.0, q).astype(BF16)
            qt_s[r] = q.T.astype(BF16)
            k = k_ref[rs, :]
            k_s[rs, :] = k.astype(BF16)
            kt_s[r] = k.T.astype(BF16)
            v_s[rs, :] = v_ref[rs, :].astype(BF16)
            za = za_ref[rs, :]
            sg = _sigmoid(za)
            dm = dm_ref[rs, :]
            do = dm * (za * sg)
            dd_s[r, 0] = jnp.where(head_a, do, 0.0).astype(BF16)
            dd_s[r, 1] = jnp.where(head_a, 0.0, do).astype(BF16)
            dot_s[r] = do.T.astype(BF16)
            dp_ref[3, rs, :] = (dm * o_ref[rs, :] * (sg * (1.0 + za * (1.0 - sg)))).astype(BF16)
            dq_s[:, rs] = jnp.zeros((LANES, TQ), F32)
            dk_s[:, rs] = jnp.zeros((LANES, TQ), F32)
            dv_s[:, rs] = jnp.zeros((LANES, TQ), F32)
            return carry

        lax.fori_loop(0, n_q, prep, 0)

        ccar_s[...] = jnp.zeros((HEADS_PER_STEP, TQ, LANES), F32)
        hl2_s[1] = jnp.zeros((HEADS_PER_STEP, 2, TQ, TK), BF16)
        w_s[1] = jnp.zeros((HEADS_PER_STEP, TQ, TK), BF16)
        dlw_s[1] = jnp.zeros((HEADS_PER_STEP, TQ, TK), F32)
        sg_s[1] = jnp.zeros((HEADS_PER_STEP, TQ, TK), F32)
        rsum_s[1] = jnp.zeros((HEADS_PER_STEP, TQ, LANES), F32)

        def stage_a(tile, slot):
            qb, n = tile
            kt = k_s[key_rows(n), :]
            both = qq_s[jnp.clip(qb, 0, n_q - 1)].reshape(HEADS_PER_STEP * TQ, LANES)
            z_s[slot] = _dot_nt(both, kt).reshape(HEADS_PER_STEP, TQ, TK)

        def stage_b(tile, slot):
            mask = diff_s[...] > jnp.where(tile[1] == tile[0], 0, -TK)
            for h in heads:
                z2 = jnp.where(mask, z_s[slot, h] * LOG2_E, MASKED_LOGIT)
                sp = _softplus2(z2)
                zm_s[slot, h] = z2 - sp
                _split_store(hl_s, slot, h, sp)

        def stage_c(tile, slot):
            qb, n = tile
            vt = v_s[key_rows(n), :]
            sufs = _split_dot_stacked(hl_s, slot, suffix_s[...])
            both = dd_s[jnp.clip(qb, 0, n_q - 1)].reshape(HEADS_PER_STEP * TQ, LANES)
            dws = _dot_nt(both, vt).reshape(HEADS_PER_STEP, TQ, TK)
            return sufs, dws

        def stage_d(tile, slot, sufs, dws):
            qb, n = tile
            rs = query_rows(qb)
            for h in heads:
                cin = jnp.sum(jnp.where(lane == n, car_ref[h, rs, :], 0.0), axis=1, keepdims=True)
                zm = zm_s[slot, h]
                w = jnp.exp2(zm - (sufs[h] + cin))
                w_s[slot, h] = w.astype(BF16)
                sg_s[slot, h] = jnp.exp2(zm)
                dlw = dws[h] * w
                dlw_s[slot, h] = dlw
                _split_store(hl2_s, slot, h, dlw)
                rsum_s[slot, h] = jnp.broadcast_to(jnp.sum(dlw, axis=1, keepdims=True), (TQ, LANES))

        def stage_e(slot):
            return _split_dot_stacked(hl2_s, slot, prefix_s[...])

        def stage_f(tile, slot, pres):
            first = tile[1] == 0
            dzs = []
            for h in heads:
                ccar = jnp.where(first, 0.0, ccar_s[h])
                dlw = dlw_s[slot, h]
                dz = dlw - sg_s[slot, h] * (dlw + (pres[h] + _tile_lanes(ccar)))
                dzs.append(dz.astype(BF16))
                ccar_s[h] = ccar + rsum_s[slot, h]
            return dzs

        def stage_g(tile, slot, dzs):
            qb, n = tile
            cs = key_rows(n)
            rs = query_rows(qb)
            ki = jnp.clip(n, 0, n_kb - 1)
            qi = jnp.clip(qb, 0, n_q - 1)
            dq, dk, dv = [], [], []
            for h in heads:
                dims = slice(h * HEAD_DIM, (h + 1) * HEAD_DIM)
                dq.append(_dot_nt(kt_s[ki, dims, :], dzs[h]))
                dk.append(_dot(qt_s[qi, dims, :], dzs[h]))
                dv.append(_dot(dot_s[qi, dims, :], w_s[slot, h]))
            dq_s[:, rs] += jnp.concatenate(dq, axis=0)
            dk_s[:, cs] += jnp.concatenate(dk, axis=0)
            dv_s[:, cs] += jnp.concatenate(dv, axis=0)

        zero = jnp.int32(0)
        stage_a((zero, zero), 0)
        stage_a(_next_tile((zero, zero)), 1)
        stage_b((zero, zero), 0)

        def trip(tile, slot):
            after = _next_tile(tile)
            before = _prev_tile(tile)
            pres = stage_e(1 - slot)
            sufs, dws = stage_c(tile, slot)
            stage_a(_next_tile(after), slot)
            stage_b(after, 1 - slot)
            dzs = stage_f(before, 1 - slot, pres)
            stage_g(before, 1 - slot, dzs)
            stage_d(tile, slot, sufs, dws)
            return after

        def two_trips(j, tile):
            return trip(trip(tile, 0), 1)

        n_tiles = n_q * (n_q + 1) // 2
        assert n_tiles % 2 == 0
        lax.fori_loop(0, n_tiles // 2, two_trips, (zero, zero))
        last = (jnp.int32(n_q - 1), jnp.int32(n_q - 1))
        stage_g(last, 1, stage_f(last, 1, stage_e(1)))

        def finish(r, carry):
            rs = pl.ds(pl.multiple_of(r * TQ, TQ), TQ)
            dp_ref[0, rs, :] = (dq_s[:, rs].T * Q_SCALE).astype(BF16)
            dp_ref[1, rs, :] = dk_s[:, rs].T.astype(BF16)
            dp_ref[2, rs, :] = dv_s[:, rs].T.astype(BF16)
            return carry

        lax.fori_loop(0, n_q, finish, 0)

    def col(g, n_sub=CONV_WIDTH // LANES):
        return pl.BlockSpec((seq, LANES), _col_block(g, n_sub), pipeline_mode=pl.Buffered(1))

    n_pairs = ATTN_WIDTH // LANES
    bf = lambda: pltpu.VMEM((seq, LANES), BF16)
    by_head = lambda: pltpu.VMEM((n_q, HEADS_PER_STEP, TQ, LANES), BF16)
    transposed = lambda: pltpu.VMEM((n_q, LANES, TQ), BF16)
    stage = lambda dtype: pltpu.VMEM((2, HEADS_PER_STEP, TQ, TK), dtype)
    split_stage = lambda: pltpu.VMEM((2, HEADS_PER_STEP, 2, TQ, TK), BF16)
    return pl.pallas_call(
        body, name="attn_bwd",
        grid=(n_pairs,),
        in_specs=[col(4), col(5), col(6), col(7), col(0), col(1),
                  pl.BlockSpec((HEADS_PER_STEP, seq, LANES), lambda p: (p, 0, 0), pipeline_mode=pl.Buffered(1))],
        out_specs=pl.BlockSpec((4, seq, LANES), lambda p: (0, 0, p)),
        out_shape=jax.ShapeDtypeStruct((4, seq, ATTN_WIDTH), BF16),
        scratch_shapes=[by_head(), by_head(), bf(), bf(),
                        pltpu.VMEM((TK, TK), BF16), pltpu.VMEM((TK, TK), BF16),
                        pltpu.VMEM((LANES, seq), F32),
                        pltpu.VMEM((HEADS_PER_STEP, TQ, LANES), F32),
                        pltpu.VMEM((LANES, seq), F32), pltpu.VMEM((LANES, seq), F32),
                        stage(F32), stage(F32), stage(F32), stage(F32),
                        split_stage(), split_stage(), stage(BF16),
                        pltpu.VMEM((2, HEADS_PER_STEP, TQ, LANES), F32),
                        pltpu.VMEM((TQ, TK), jnp.int32),
                        transposed(), transposed(), transposed()],
        compiler_params=_compiler_params(56, ("arbitrary",)),
    )(proj, proj, proj, proj, o, dmix, carries)


DEAD_CARRY_BITS = 160.0


def _attn_fwd_live(proj):
    seq = proj.shape[0]
    n_q = seq // TQ
    n_kb = seq // TK
    assert TQ == TK and n_kb <= LANES
    heads = range(HEADS_PER_STEP)

    def body(q_ref, k_ref, v_ref, za_ref, o_ref, mix_ref, car_ref, qq_s, k_s, vt_s, suffix_s, carry_s, diff_s):
        head_a = _head_mask()
        r_i, s_i = _tri_masks()
        diff_s[...] = r_i - s_i
        _fill_tri(suffix_s, r_i > s_i)
        lane = lax.broadcasted_iota(jnp.int32, (1, LANES), 1)

        def prep(r, carry):
            rs = pl.ds(pl.multiple_of(r * TQ, TQ), TQ)
            q = q_ref[rs, :] * Q_SCALE
            qq_s[r, 0] = jnp.where(head_a, q, 0.0).astype(BF16)
            qq_s[r, 1] = jnp.where(head_a, 0.0, q).astype(BF16)
            k_s[rs, :] = k_ref[rs, :].astype(BF16)
            vt_s[r] = v_ref[rs, :].T.astype(BF16)
            return carry

        lax.fori_loop(0, n_q, prep, 0)

        def q_block(qb, carry):
            rs = pl.ds(pl.multiple_of(qb * TQ, TQ), TQ)
            both = qq_s[qb].reshape(HEADS_PER_STEP * TQ, LANES)
            carry_s[...] = jnp.zeros((HEADS_PER_STEP, TQ, LANES), F32)
            for h in heads:
                car_ref[h, rs, :] = jnp.zeros((TQ, LANES), F32)

            def tile(n):
                kb = qb - n
                cs = pl.ds(pl.multiple_of(kb * TK, TK), TK)
                z = _dot_nt(both, k_s[cs, :]).reshape(HEADS_PER_STEP, TQ, TK)
                mask = diff_s[...] > jnp.where(n == 0, 0, -TK)
                zms, parts, cars = [], [], []
                for h in heads:
                    z2 = jnp.where(mask, z[h] * LOG2_E, MASKED_LOGIT)
                    sp = _softplus2(z2)
                    zms.append(z2 - sp)
                    hi = _truncate_to_bf16(sp)
                    parts += [hi.astype(BF16), (sp - hi).astype(BF16)]
                    car = carry_s[h]
                    cars.append(car)
                    car_ref[h, rs, :] = jnp.where(lane == kb, car, car_ref[h, rs, :])
                    carry_s[h] = car + jnp.sum(sp, axis=1, keepdims=True)
                suf = _dot(jnp.concatenate(parts, axis=0), suffix_s[...]).reshape(HEADS_PER_STEP, 2, TQ, TK)
                out = []
                for h in heads:
                    w = jnp.exp2(zms[h] - ((suf[h, 0] + suf[h, 1]) + _tile_lanes(cars[h]))).astype(BF16)
                    out.append(_dot_nt(vt_s[kb, h * HEAD_DIM:(h + 1) * HEAD_DIM, :], w))
                return jnp.concatenate(out, axis=0)

            def live(state):
                n, lowest, _ = state
                return jnp.logical_and(n <= qb, lowest < DEAD_CARRY_BITS)

            def step(state):
                n, _, acc = state
                acc = acc + tile(n)
                lowest = jnp.minimum(jnp.min(carry_s[0]), jnp.min(carry_s[1]))
                return n + 1, lowest, acc

            n_done, _, o_t = lax.while_loop(
                live, step, (jnp.int32(0), jnp.float32(0.0), jnp.zeros((LANES, TQ), F32)))
            for h in heads:
                car_ref[h, rs, :] = jnp.where(lane < qb + 1 - n_done, carry_s[h], car_ref[h, rs, :])
            o = o_t.T
            o_ref[rs, :] = o
            za = za_ref[rs, :]
            mix_ref[rs, :] = ((za * _sigmoid(za)) * o).astype(BF16)
            return carry

        lax.fori_loop(0, n_q, q_block, 0)

    col = lambda g: pl.BlockSpec((seq, LANES), _col_block(g))
    n_pairs = ATTN_WIDTH // LANES
    return pl.pallas_call(
        body, name="attn_fwd",
        grid=(n_pairs,),
        in_specs=[col(4), col(5), col(6), col(7)],
        out_specs=(pl.BlockSpec((seq, LANES), lambda p: (0, p)),
                   pl.BlockSpec((seq, LANES), lambda p: (0, p)),
                   pl.BlockSpec((HEADS_PER_STEP, seq, LANES), lambda p: (p, 0, 0))),
        out_shape=(jax.ShapeDtypeStruct((seq, ATTN_WIDTH), F32),
                   jax.ShapeDtypeStruct((seq, ATTN_WIDTH), BF16),
                   jax.ShapeDtypeStruct((n_pairs * HEADS_PER_STEP, seq, LANES), F32)),
        scratch_shapes=[pltpu.VMEM((n_q, HEADS_PER_STEP, TQ, LANES), BF16),
                        pltpu.VMEM((seq, LANES), BF16),
                        pltpu.VMEM((n_kb, LANES, TK), BF16),
                        pltpu.VMEM((TK, TK), BF16),
                        pltpu.VMEM((HEADS_PER_STEP, TQ, LANES), F32),
                        pltpu.VMEM((TQ, TK), jnp.int32)],
        compiler_params=_compiler_params(48, ("arbitrary",)),
    )(proj, proj, proj, proj)


def _attn_bwd_live(proj, o, dmix, carries):
    seq = proj.shape[0]
    n_q = seq // TQ
    n_kb = seq // TK
    assert TQ == TK and n_kb <= LANES
    heads = range(HEADS_PER_STEP)

    def body(q_ref, k_ref, v_ref, za_ref, o_ref, dm_ref, car_ref, dp_ref,
             qq_s, dd_s, k_s, v_s, qt_s, dot_s, kt_s, suffix_s, prefix_s, ccar_s, dq_s, dk_s, dv_s, diff_s):
        head_a = _head_mask()
        r_i, s_i = _tri_masks()
        diff_s[...] = r_i - s_i
        _fill_tri(suffix_s, r_i > s_i)
        _fill_tri(prefix_s, r_i < s_i)
        lane = lax.broadcasted_iota(jnp.int32, (1, LANES), 1)

        def prep(r, carry):
            rs = pl.ds(pl.multiple_of(r * TQ, TQ), TQ)
            q = q_ref[rs, :] * Q_SCALE
            qq_s[r, 0] = jnp.where(head_a, q, 0.0).astype(BF16)
            qq_s[r, 1] = jnp.where(head_a, 0.0, q).astype(BF16)
            qt_s[r] = q.T.astype(BF16)
            k = k_ref[rs, :]
            k_s[rs, :] = k.astype(BF16)
            kt_s[r] = k.T.astype(BF16)
            v_s[rs, :] = v_ref[rs, :].astype(BF16)
            za = za_ref[rs, :]
            sg = _sigmoid(za)
            dm = dm_ref[rs, :]
            do = dm * (za * sg)
            dd_s[r, 0] = jnp.where(head_a, do, 0.0).astype(BF16)
            dd_s[r, 1] = jnp.where(head_a, 0.0, do).astype(BF16)
            dot_s[r] = do.T.astype(BF16)
            dp_ref[3, rs, :] = (dm * o_ref[rs, :] * (sg * (1.0 + za * (1.0 - sg)))).astype(BF16)
            dq_s[:, rs] = jnp.zeros((LANES, TQ), F32)
            dk_s[:, rs] = jnp.zeros((LANES, TQ), F32)
            dv_s[:, rs] = jnp.zeros((LANES, TQ), F32)
            return carry

        lax.fori_loop(0, n_q, prep, 0)

        def q_block(qb, carry):
            rs = pl.ds(pl.multiple_of(qb * TQ, TQ), TQ)
            both_q = qq_s[qb].reshape(HEADS_PER_STEP * TQ, LANES)
            both_do = dd_s[qb].reshape(HEADS_PER_STEP * TQ, LANES)
            ccar_s[...] = jnp.zeros((HEADS_PER_STEP, TQ, LANES), F32)
            lowest = jnp.min(jnp.minimum(car_ref[0, rs, :], car_ref[1, rs, :]), axis=0, keepdims=True)
            dead = jnp.logical_and(lowest >= DEAD_CARRY_BITS, lane < qb)
            first_live = jnp.sum(jnp.where(dead, 1, 0))

            def tile(kb, acc):
                cs = pl.ds(pl.multiple_of(kb * TK, TK), TK)
                z = _dot_nt(both_q, k_s[cs, :]).reshape(HEADS_PER_STEP, TQ, TK)
                dw = _dot_nt(both_do, v_s[cs, :]).reshape(HEADS_PER_STEP, TQ, TK)
                mask = diff_s[...] > jnp.where(kb == qb, 0, -TK)
                zms, parts = [], []
                for h in heads:
                    z2 = jnp.where(mask, z[h] * LOG2_E, MASKED_LOGIT)
                    sp = _softplus2(z2)
                    zms.append(z2 - sp)
                    hi = _truncate_to_bf16(sp)
                    parts += [hi.astype(BF16), (sp - hi).astype(BF16)]
                suf = _dot(jnp.concatenate(parts, axis=0), suffix_s[...]).reshape(HEADS_PER_STEP, 2, TQ, TK)
                ws, dlws, parts = [], [], []
                for h in heads:
                    cin = jnp.sum(jnp.where(lane == kb, car_ref[h, rs, :], 0.0), axis=1, keepdims=True)
                    w = jnp.exp2(zms[h] - ((suf[h, 0] + suf[h, 1]) + cin))
                    ws.append(w.astype(BF16))
                    dlw = dw[h] * w
                    dlws.append(dlw)
                    hi = _truncate_to_bf16(dlw)
                    parts += [hi.astype(BF16), (dlw - hi).astype(BF16)]
                pre = _dot(jnp.concatenate(parts, axis=0), prefix_s[...]).reshape(HEADS_PER_STEP, 2, TQ, TK)
                dq_t, dk_t, dv_t = [], [], []
                for h in heads:
                    ccar = ccar_s[h]
                    before = (pre[h, 0] + pre[h, 1]) + _tile_lanes(ccar)
                    dz = (dlws[h] - jnp.exp2(zms[h]) * (dlws[h] + before)).astype(BF16)
                    ccar_s[h] = ccar + jnp.sum(dlws[h], axis=1, keepdims=True)
                    dims = slice(h * HEAD_DIM, (h + 1) * HEAD_DIM)
                    dq_t.append(_dot_nt(kt_s[kb, dims, :], dz))
                    dk_t.append(_dot(qt_s[qb, dims, :], dz))
                    dv_t.append(_dot(dot_s[qb, dims, :], ws[h]))
                dk_s[:, cs] += jnp.concatenate(dk_t, axis=0)
                dv_s[:, cs] += jnp.concatenate(dv_t, axis=0)
                return acc + jnp.concatenate(dq_t, axis=0)

            dq_s[:, rs] = lax.fori_loop(first_live, qb + 1, tile, jnp.zeros((LANES, TQ), F32))
            return carry

        lax.fori_loop(0, n_q, q_block, 0)

        def finish(r, carry):
            rs = pl.ds(pl.multiple_of(r * TQ, TQ), TQ)
            dp_ref[0, rs, :] = (dq_s[:, rs].T * Q_SCALE).astype(BF16)
            dp_ref[1, rs, :] = dk_s[:, rs].T.astype(BF16)
            dp_ref[2, rs, :] = dv_s[:, rs].T.astype(BF16)
            return carry

        lax.fori_loop(0, n_q, finish, 0)

    def col(g, n_sub=CONV_WIDTH // LANES):
        return pl.BlockSpec((seq, LANES), _col_block(g, n_sub), pipeline_mode=pl.Buffered(1))

    n_pairs = ATTN_WIDTH // LANES
    bf = lambda: pltpu.VMEM((seq, LANES), BF16)
    by_head = lambda: pltpu.VMEM((n_q, HEADS_PER_STEP, TQ, LANES), BF16)
    transposed = lambda: pltpu.VMEM((n_q, LANES, TQ), BF16)
    acc_t = lambda: pltpu.VMEM((LANES, seq), F32)
    return pl.pallas_call(
        body, name="attn_bwd",
        grid=(n_pairs,),
        in_specs=[col(4), col(5), col(6), col(7), col(0), col(1),
                  pl.BlockSpec((HEADS_PER_STEP, seq, LANES), lambda p: (p, 0, 0), pipeline_mode=pl.Buffered(1))],
        out_specs=pl.BlockSpec((4, seq, LANES), lambda p: (0, 0, p)),
        out_shape=jax.ShapeDtypeStruct((4, seq, ATTN_WIDTH), BF16),
        scratch_shapes=[by_head(), by_head(), bf(), bf(), transposed(), transposed(), transposed(),
                        pltpu.VMEM((TK, TK), BF16), pltpu.VMEM((TK, TK), BF16),
                        pltpu.VMEM((HEADS_PER_STEP, TQ, LANES), F32),
                        acc_t(), acc_t(), acc_t(),
                        pltpu.VMEM((TQ, TK), jnp.int32)],
        compiler_params=_compiler_params(56, ("arbitrary",)),
    )(proj, proj, proj, proj, o, dmix, carries)


def _out_ln(mix_c, mix_a, woutg, x, target, gain, bias):
    seq = x.shape[0]
    tm = 256
    inv_d = 1.0 / D_MODEL

    def body(mc_ref, ma_ref, wo_ref, x_ref, t_ref, g_ref, b_ref, dxa_ref, dmix_ref, gw_ref, st_ref):
        @pl.when(pl.program_id(0) == 0)
        def _():
            gw_ref[...] = jnp.zeros(gw_ref.shape, F32)
            st_ref[...] = jnp.zeros(st_ref.shape, F32)

        mc = mc_ref[...]
        ma = ma_ref[...]
        sub = _dot(mc, wo_ref[0:CONV_WIDTH, :]) + _dot(ma, wo_ref[CONV_WIDTH:, :])
        r = DEEPNORM_ALPHA * x_ref[...] + sub
        mu = jnp.mean(r, axis=-1, keepdims=True)
        rc = r - mu
        var = jnp.mean(rc * rc, axis=-1, keepdims=True)
        rstd = lax.rsqrt(var + LN_EPS)
        xhat = rc * rstd
        gain_v = g_ref[...]
        diff = (xhat * gain_v + b_ref[...]) - t_ref[...]
        dy = diff * inv_d
        st_ref[0:1, :] += jnp.sum(dy * xhat, axis=0, keepdims=True)
        st_ref[1:2, :] += jnp.sum(dy, axis=0, keepdims=True)
        st_ref[2:3, :] += jnp.sum(diff * diff, axis=0, keepdims=True)
        dxh = dy * gain_v
        m1 = jnp.mean(dxh, axis=-1, keepdims=True)
        m2 = jnp.mean(dxh * xhat, axis=-1, keepdims=True)
        dr = rstd * (dxh - m1 - xhat * m2)
        dxa_ref[...] = DEEPNORM_ALPHA * dr
        drb = dr.astype(BF16)
        dmix_ref[...] = _dot_nt(drb, wo_ref[...])
        gw_ref[0:CONV_WIDTH, :] += _dot_tn(mc, drb)
        gw_ref[CONV_WIDTH:, :] += _dot_tn(ma, drb)

    rows_d = lambda: pl.BlockSpec((tm, D_MODEL), lambda i: (i, 0))
    rows_h = lambda: pl.BlockSpec((tm, CONV_WIDTH), lambda i: (i, 0))
    whole = lambda shape: pl.BlockSpec(shape, lambda i: (0, 0))
    return pl.pallas_call(
        body, name="out_ln",
        grid=(seq // tm,),
        in_specs=[rows_h(), rows_h(), whole((D_MODEL, D_MODEL)), rows_d(), rows_d(),
                  whole((1, D_MODEL)), whole((1, D_MODEL))],
        out_specs=(rows_d(), rows_d(), whole((D_MODEL, D_MODEL)), whole((SUBLANES, D_MODEL))),
        out_shape=(jax.ShapeDtypeStruct((seq, D_MODEL), F32), jax.ShapeDtypeStruct((seq, D_MODEL), F32),
                   jax.ShapeDtypeStruct((D_MODEL, D_MODEL), F32), jax.ShapeDtypeStruct((SUBLANES, D_MODEL), F32)),
        compiler_params=_compiler_params(48, ("arbitrary",)),
    )(mix_c, mix_a, woutg, x, target, gain, bias)


def _group_maps():
    half = N_GROUPS // 2
    return (lambda g: jnp.minimum(g, half - 1)), (lambda g: jnp.maximum(g - half, 0))


def _grad_w_in(xt, dpc, dpa):
    seq = xt.shape[1]
    half = N_GROUPS // 2
    in_conv, in_attn = _group_maps()

    def body(xt_ref, dc_ref, da_ref, out_ref):
        g = pl.program_id(0)

        @pl.when(g < half)
        def _():
            out_ref[...] = _dot(xt_ref[...], dc_ref[...])

        @pl.when(g >= half)
        def _():
            out_ref[...] = _dot(xt_ref[...], da_ref[...])

    return pl.pallas_call(
        body, name="grad_w_in",
        grid=(N_GROUPS,),
        in_specs=[pl.BlockSpec((D_MODEL, seq), lambda g: (0, 0), pipeline_mode=pl.Buffered(1)),
                  pl.BlockSpec((None, seq, CONV_WIDTH), lambda g: (in_conv(g), 0, 0)),
                  pl.BlockSpec((None, seq, CONV_WIDTH), lambda g: (in_attn(g), 0, 0))],
        out_specs=pl.BlockSpec((None, D_MODEL, CONV_WIDTH), lambda g: (g // 2, 0, g % 2)),
        out_shape=jax.ShapeDtypeStruct((N_CHIPS, D_MODEL, D_MODEL), F32),
        compiler_params=_compiler_params(48, ("arbitrary",)),
    )(xt, dpc, dpa)


def _grad_x(dxa, dpc, dpa, wing):
    seq = dxa.shape[0]
    tm = 256
    half = N_GROUPS // 2

    def body(dxa_ref, dc_ref, da_ref, w_ref, out_ref):
        acc = dxa_ref[...]
        for g in range(N_GROUPS):
            dp = dc_ref[g] if g < half else da_ref[g - half]
            cols = slice((g % 2) * CONV_WIDTH, (g % 2 + 1) * CONV_WIDTH)
            acc = acc + _dot_nt(dp, w_ref[g // 2, :, cols])
        out_ref[...] = acc

    return pl.pallas_call(
        body, name="grad_x",
        grid=(seq // tm,),
        in_specs=[pl.BlockSpec((tm, D_MODEL), lambda i: (i, 0)),
                  pl.BlockSpec((half, tm, CONV_WIDTH), lambda i: (0, i, 0)),
                  pl.BlockSpec((half, tm, CONV_WIDTH), lambda i: (0, i, 0)),
                  pl.BlockSpec((N_CHIPS, D_MODEL, D_MODEL), lambda i: (0, 0, 0), pipeline_mode=pl.Buffered(1))],
        out_specs=pl.BlockSpec((tm, D_MODEL), lambda i: (i, 0)),
        out_shape=jax.ShapeDtypeStruct((seq, D_MODEL), F32),
        compiler_params=_compiler_params(40, ("arbitrary",)),
    )(dxa, dpc, dpa, wing)


PACK_LOSS_ROW = 5


def _reduce_grads(gwin, gwout, pack):
    n_shards, n_rows_in, n_cols = gwin.shape
    n_rows_out = gwout.shape[1]
    half_rows = (n_rows_in // 2, n_rows_out // 2)
    row_chunk = 128

    def body(gwin_ref, gwout_ref, pack_ref, oin_ref, oout_ref, opack_ref,
             mine_in, mine_out, sib_in, sib_out, wire_in, wire_out, rin, rout, rpack,
             local_sems, sib_send, sib_recv, send_sems, recv_sems, pack_send, pack_recv, fin_send, fin_recv):
        x, y, c = lax.axis_index("x"), lax.axis_index("y"), lax.axis_index("c")
        me = 2 * x + y
        my_id = 4 * x + 2 * y + c
        chips = [(1 - x, y), (x, 1 - y), (1 - x, 1 - y)]
        sibling = (x, y, 1 - c)
        partial = (gwin_ref, gwout_ref)
        mine = (mine_in, mine_out)
        from_sib = (sib_in, sib_out)
        wire = (wire_in, wire_out)
        from_chips = (rin, rout)
        result = (oin_ref, oout_ref)

        def half(a, which):
            rows = half_rows[a]
            return pl.ds(pl.multiple_of(which * rows, rows), rows)

        sends = []
        loads = []
        for a in range(2):
            ld = pltpu.make_async_copy(partial[a].at[:, half(a, c), :], mine[a], local_sems.at[a])
            ld.start()
            loads.append(ld)
            cp = pltpu.make_async_remote_copy(
                src_ref=partial[a].at[:, half(a, 1 - c), :], dst_ref=from_sib[a],
                send_sem=sib_send.at[a], recv_sem=sib_recv.at[a], device_id=sibling, device_id_type=MESH)
            cp.start()
            sends.append(cp)

        rpack[my_id] = pack_ref[...]

        def pack_copy(rel, slot, to):
            return pltpu.make_async_remote_copy(
                src_ref=rpack.at[slot], dst_ref=rpack.at[slot],
                send_sem=pack_send.at[rel - 1], recv_sem=pack_recv.at[rel - 1],
                device_id=to, device_id_type=MESH)

        def related(rel):
            px = (1 - x) if rel & 4 else x
            py = (1 - y) if rel & 2 else y
            pc = (1 - c) if rel & 1 else c
            return px, py, pc

        for rel in range(1, N_DEVICES):
            cp = pack_copy(rel, my_id, related(rel))
            cp.start()
            sends.append(cp)

        for ld in loads:
            ld.wait()
        for a in range(2):
            pltpu.make_async_remote_copy(
                src_ref=partial[a].at[:, half(a, 1 - c), :], dst_ref=from_sib[a],
                send_sem=sib_send.at[a], recv_sem=sib_recv.at[a], device_id=sibling, device_id_type=MESH).wait_recv()

        def chip_sum(a, shard):
            def add(r, carry):
                rs = pl.ds(pl.multiple_of(r * row_chunk, row_chunk), row_chunk)
                mine[a][shard, rs, :] = mine[a][shard, rs, :] + from_sib[a][shard, rs, :]
                return carry

            lax.fori_loop(0, half_rows[a] // row_chunk, add, 0)

        def chip_sum_to_wire(a, shard, k):
            def add(r, carry):
                rs = pl.ds(pl.multiple_of(r * row_chunk, row_chunk), row_chunk)
                wire[a][k, rs, :] = (mine[a][shard, rs, :] + from_sib[a][shard, rs, :]).astype(BF16)
                return carry

            lax.fori_loop(0, half_rows[a] // row_chunk, add, 0)

        def shard_copy(k, a, to):
            return pltpu.make_async_remote_copy(
                src_ref=wire[a].at[k], dst_ref=from_chips[a].at[k],
                send_sem=send_sems.at[2 * k + a], recv_sem=recv_sems.at[2 * k + a],
                device_id=to, device_id_type=MESH)

        for k, (px, py) in enumerate(chips):
            for a in range(2):
                chip_sum_to_wire(a, 2 * px + py, k)
                cp = shard_copy(k, a, (px, py, c))
                cp.start()
                sends.append(cp)
        for a in range(2):
            chip_sum(a, me)

        for rel in range(1, N_DEVICES):
            px, py, pc = related(rel)
            pack_copy(rel, 4 * px + 2 * py + pc, (px, py, pc)).wait_recv()
        total = rpack[0]
        for i in range(1, N_DEVICES):
            total = total + rpack[i]
        opack_ref[...] = total
        sq_err = jnp.sum(total[PACK_LOSS_ROW:PACK_LOSS_ROW + 1, :], axis=1, keepdims=True)
        opack_ref[PACK_LOSS_ROW:PACK_LOSS_ROW + 1, :] = jnp.broadcast_to(sq_err * (0.5 / D_MODEL), (1, n_cols))

        for k, (px, py) in enumerate(chips):
            for a in range(2):
                shard_copy(k, a, (px, py, c)).wait_recv()

        def finish(a):
            def add(r, carry):
                rs = pl.ds(pl.multiple_of(r * row_chunk, row_chunk), row_chunk)
                dst = pl.ds(pl.multiple_of(c * half_rows[a] + r * row_chunk, row_chunk), row_chunk)
                result[a][dst, :] = ((mine[a][me, rs, :] + from_chips[a][0, rs, :].astype(F32))
                                     + from_chips[a][1, rs, :].astype(F32)) + from_chips[a][2, rs, :].astype(F32)
                return carry

            lax.fori_loop(0, half_rows[a] // row_chunk, add, 0)

        def final_copy(a, which):
            ref = result[a].at[half(a, which), :]
            return pltpu.make_async_remote_copy(
                src_ref=ref, dst_ref=ref, send_sem=fin_send.at[a], recv_sem=fin_recv.at[a],
                device_id=sibling, device_id_type=MESH)

        for a in range(2):
            finish(a)
            cp = final_copy(a, c)
            cp.start()
            sends.append(cp)
        for a in range(2):
            final_copy(a, 1 - c).wait_recv()
        for cp in sends:
            cp.wait_send()

    vmem = pl.BlockSpec(memory_space=pltpu.VMEM)
    hbm = pl.BlockSpec(memory_space=pl.ANY)
    in_shape = (n_rows_in, n_cols)
    out_shape = (n_rows_out, n_cols)
    half_in = (half_rows[0], n_cols)
    half_out = (half_rows[1], n_cols)
    return pl.pallas_call(
        body, name="reduce_grads",
        out_shape=(jax.ShapeDtypeStruct(in_shape, F32), jax.ShapeDtypeStruct(out_shape, F32),
                   jax.ShapeDtypeStruct(pack.shape, F32)),
        in_specs=[hbm, hbm, vmem], out_specs=(vmem, vmem, vmem),
        scratch_shapes=[pltpu.VMEM((n_shards,) + half_in, F32), pltpu.VMEM((n_shards,) + half_out, F32),
                        pltpu.VMEM((n_shards,) + half_in, F32), pltpu.VMEM((n_shards,) + half_out, F32),
                        pltpu.VMEM((N_CHIPS - 1,) + half_in, BF16), pltpu.VMEM((N_CHIPS - 1,) + half_out, BF16),
                        pltpu.VMEM((N_CHIPS - 1,) + half_in, BF16), pltpu.VMEM((N_CHIPS - 1,) + half_out, BF16),
                        pltpu.VMEM((N_DEVICES,) + pack.shape, F32),
                        pltpu.SemaphoreType.DMA((2,)),
                        pltpu.SemaphoreType.DMA((2,)), pltpu.SemaphoreType.DMA((2,)),
                        pltpu.SemaphoreType.DMA((6,)), pltpu.SemaphoreType.DMA((6,)),
                        pltpu.SemaphoreType.DMA((N_DEVICES - 1,)), pltpu.SemaphoreType.DMA((N_DEVICES - 1,)),
                        pltpu.SemaphoreType.DMA((2,)), pltpu.SemaphoreType.DMA((2,))],
        compiler_params=_compiler_params(48),
    )(gwin, gwout, pack)


def _adamw(name, w, g, m, v):
    n_rows, n_cols = w.shape
    tr = 256 if n_rows % 256 == 0 else n_rows
    m_corr = 1.0 - ADAM_B1 ** ADAM_STEP
    v_corr = 1.0 - ADAM_B2 ** ADAM_STEP

    def body(w_ref, g_ref, m_ref, v_ref, d_ref, nm_ref, nv_ref):
        gv = g_ref[...]
        nm = ADAM_B1 * m_ref[...] + (1.0 - ADAM_B1) * gv
        nv = ADAM_B2 * v_ref[...] + (1.0 - ADAM_B2) * (gv * gv)
        nm_ref[...] = nm
        nv_ref[...] = nv
        d_ref[...] = -ADAM_LR * ((nm / m_corr) / (jnp.sqrt(nv / v_corr) + ADAM_EPS) + ADAM_WD * w_ref[...])

    blk = lambda: pl.BlockSpec((tr, n_cols), lambda i: (i, 0))
    shape = jax.ShapeDtypeStruct(w.shape, F32)
    return pl.pallas_call(
        body, name=name,
        grid=(n_rows // tr,),
        in_specs=[blk(), blk(), blk(), blk()], out_specs=(blk(), blk(), blk()),
        out_shape=(shape, shape, shape),
        compiler_params=_compiler_params(32, ("arbitrary",)),
    )(w, g, m, v)


def kernel(x, w_in, conv_w, w_out, ln_gain, ln_bias, loss_target, m_w_in, m_conv_w, m_w_out, m_ln_gain, m_ln_bias, v_w_in, v_conv_w, v_w_out, v_ln_gain, v_ln_bias):
    xs = x[0]
    target = loss_target[0]

    wing, woutg, cwg = _gather_weights(w_in[0], w_out[0], conv_w[0])
    conv_full = jnp.transpose(cwg, (1, 0, 2)).reshape(3, CONV_WIDTH)
    wout_full = woutg.reshape(D_MODEL, D_MODEL)

    proj, xt = _proj(xs, wing)
    mix_c = _conv_fwd(proj, conv_full)
    o, mix_a, carries = _attn_fwd_live(proj)
    dxa, dmix, gwout, stats = _out_ln(mix_c, mix_a, wout_full, xs, target, ln_gain, ln_bias)
    dpc, dconv = _conv_bwd(proj, dmix, conv_full)
    dpa = _attn_bwd_live(proj, o, dmix, carries)
    gwin = _grad_w_in(xt, dpc, dpa)
    grad_x = _grad_x(dxa, dpc, dpa, wing)

    pack = jnp.concatenate(
        [stats[0:2], jnp.pad(dconv[0:3], ((0, 0), (0, D_MODEL - CONV_WIDTH))), stats[2:3],
         jnp.zeros((2, D_MODEL), F32)], axis=0)
    g_w_in, g_w_out, tot = _reduce_grads(gwin, gwout.reshape(N_CHIPS, D_MODEL // N_CHIPS, D_MODEL), pack)

    chip = 2 * lax.axis_index("x") + lax.axis_index("y")
    g_gain = tot[0:1]
    g_bias = tot[1:2]
    g_conv = lax.dynamic_slice(tot, (2, chip * LANES), (3, LANES))
    loss = tot[PACK_LOSS_ROW, 0]

    d_w_in, nm_w_in, nv_w_in = _adamw("adamw_w_in", w_in[0], g_w_in, m_w_in[0], v_w_in[0])
    d_w_out, nm_w_out, nv_w_out = _adamw("adamw_w_out", w_out[0], g_w_out, m_w_out[0], v_w_out[0])
    d_conv, nm_conv, nv_conv = _adamw("adamw_conv_w", conv_w[0], g_conv, m_conv_w[0], v_conv_w[0])
    d_gain, nm_gain, nv_gain = _adamw("adamw_ln_gain", ln_gain, g_gain, m_ln_gain, v_ln_gain)
    d_bias, nm_bias, nv_bias = _adamw("adamw_ln_bias", ln_bias, g_bias, m_ln_bias, v_ln_bias)

    lead = lambda a: a[None]
    return (loss, lead(grad_x),
            lead(g_w_in), lead(g_conv), lead(g_w_out), g_gain, g_bias,
            lead(d_w_in), lead(d_conv), lead(d_w_out), d_gain, d_bias,
            lead(nm_w_in), lead(nm_conv), lead(nm_w_out), nm_gain, nm_bias,
            lead(nv_w_in), lead(nv_conv), lead(nv_w_out), nv_gain, nv_bias)
```

```python
import jax
import jax.numpy as jnp
from jax import lax
from jax.experimental import pallas as pl
from jax.experimental.pallas import tpu as pltpu

F32 = jnp.float32
BF16 = jnp.bfloat16
MESH = pl.DeviceIdType.MESH

D_MODEL = 1024
CONV_WIDTH = 512
ATTN_WIDTH = 512
HEAD_DIM = 64
N_GROUPS = 8
N_CHIPS = 4
N_DEVICES = 8
LN_EPS = 1e-5
DEEPNORM_ALPHA = 2.0 ** 0.25
Q_SCALE = HEAD_DIM ** -0.5
ADAM_LR = 0.001
ADAM_B1 = 0.9
ADAM_B2 = 0.999
ADAM_EPS = 1e-08
ADAM_WD = 0.01
ADAM_STEP = 10

LANES = 128
SUBLANES = 8
V7X_VMEM_BYTES = 64 * 1024 * 1024
MIB = 1024 * 1024

TQ = 256
TK = 256
HEADS_PER_STEP = LANES // HEAD_DIM
CONV_ROWS = 512


def _compiler_params(vmem_mib, semantics=None):
    assert vmem_mib * MIB < V7X_VMEM_BYTES
    return pltpu.CompilerParams(dimension_semantics=semantics, vmem_limit_bytes=vmem_mib * MIB)


def _sigmoid(z):
    return 1.0 / (1.0 + jnp.exp(-z))


def _dot(a, b):
    return jnp.dot(a, b, preferred_element_type=F32)


def _dot_nt(a, b):
    return lax.dot_general(a, b, (((1,), (1,)), ((), ())), preferred_element_type=F32)


def _dot_tn(a, b):
    return lax.dot_general(a, b, (((0,), (0,)), ((), ())), preferred_element_type=F32)


def _truncate_to_bf16(a):
    bits = lax.bitcast_convert_type(a, jnp.uint32) & jnp.uint32(0xFFFF0000)
    return lax.bitcast_convert_type(bits, F32)


def _split(a):
    hi = _truncate_to_bf16(a)
    return [hi.astype(BF16), (a - hi).astype(BF16)]


def _block_sums(parts, tri):
    n = len(parts) // 2
    res = _dot(jnp.concatenate(parts, axis=0), tri).reshape(n, 2, TQ, TK)
    return [res[i, 0] + res[i, 1] for i in range(n)]


MASKED_LOGIT = -1e30
LOG2_E = 1.4426950408889634


def _softplus2(z2):
    return jnp.maximum(z2, 0.0) + jnp.log2(1.0 + jnp.exp2(-jnp.abs(z2)))


def _gather_weights(w_in, w_out, conv_w):
    d_rows, d_cols = w_in.shape
    o_rows, o_cols = w_out.shape
    half_rows = (d_rows // 2, o_rows // 2)
    row_chunk = 128

    def body(win_ref, wout_ref, cw_ref, wing_ref, woutg_ref, cwg_ref, send_sems, recv_sems, pass_send, pass_recv):
        x, y, c = lax.axis_index("x"), lax.axis_index("y"), lax.axis_index("c")
        me = 2 * x + y
        srcs = (win_ref, wout_ref)
        bufs = (wing_ref, woutg_ref)

        def half(a, shard, which):
            rows = half_rows[a]
            return bufs[a].at[shard, pl.ds(pl.multiple_of(which * rows, rows), rows), :]

        def cast_half(which):
            for a in range(2):
                def cast(r, carry):
                    rows = pl.ds(pl.multiple_of(which * half_rows[a] + r * row_chunk, row_chunk), row_chunk)
                    bufs[a][me, rows, :] = srcs[a][rows, :].astype(BF16)
                    return carry

                lax.fori_loop(0, half_rows[a] // row_chunk, cast, 0)

        chips = [(1 - x, y), (x, 1 - y), (1 - x, 1 - y)]

        def ici_copy(k, a, shard, to):
            ref = cwg_ref.at[shard] if a == 2 else half(a, shard, c)
            return pltpu.make_async_remote_copy(
                src_ref=ref, dst_ref=ref, send_sem=send_sems.at[3 * k + a], recv_sem=recv_sems.at[3 * k + a],
                device_id=to, device_id_type=MESH)

        def pass_copy(k, a, shard, which):
            ref = half(a, shard, which)
            return pltpu.make_async_remote_copy(
                src_ref=ref, dst_ref=ref, send_sem=pass_send.at[2 * k + a], recv_sem=pass_recv.at[2 * k + a],
                device_id=(x, y, 1 - c), device_id_type=MESH)

        cast_half(c)
        cwg_ref[me] = cw_ref[...]
        sends = []
        for k, (px, py) in enumerate(chips):
            for a in range(3):
                cp = ici_copy(k, a, me, (px, py, c))
                cp.start()
                sends.append(cp)
        cast_half(1 - c)
        for k, (px, py) in enumerate(chips):
            for a in range(2):
                ici_copy(k, a, 2 * px + py, (px, py, c)).wait_recv()
                cp = pass_copy(k, a, 2 * px + py, c)
                cp.start()
                sends.append(cp)
        for k, (px, py) in enumerate(chips):
            ici_copy(k, 2, 2 * px + py, (px, py, c)).wait_recv()
            for a in range(2):
                pass_copy(k, a, 2 * px + py, 1 - c).wait_recv()
        for cp in sends:
            cp.wait_send()

    vmem = pl.BlockSpec(memory_space=pltpu.VMEM)
    return pl.pallas_call(
        body, name="gather_weights",
        out_shape=(jax.ShapeDtypeStruct((N_CHIPS, d_rows, d_cols), BF16),
                   jax.ShapeDtypeStruct((N_CHIPS, o_rows, o_cols), BF16),
                   jax.ShapeDtypeStruct((N_CHIPS,) + conv_w.shape, F32)),
        in_specs=[vmem, vmem, vmem], out_specs=(vmem, vmem, vmem),
        scratch_shapes=[pltpu.SemaphoreType.DMA((9,)), pltpu.SemaphoreType.DMA((9,)),
                        pltpu.SemaphoreType.DMA((6,)), pltpu.SemaphoreType.DMA((6,))],
        compiler_params=_compiler_params(32),
    )(w_in, w_out, conv_w)


def _proj(x, wing):
    seq = x.shape[0]
    tm = 256

    def body(x_ref, w_ref, p_ref, xt_ref):
        xv = x_ref[...]
        xb = xv.astype(BF16)
        for j in range(N_CHIPS):
            p_ref[:, j * D_MODEL:(j + 1) * D_MODEL] = _dot(xb, w_ref[j])
        xt_ref[...] = xv.T.astype(BF16)

    return pl.pallas_call(
        body, name="proj",
        grid=(seq // tm,),
        in_specs=[pl.BlockSpec((tm, D_MODEL), lambda i: (i, 0)),
                  pl.BlockSpec((N_CHIPS, D_MODEL, D_MODEL), lambda i: (0, 0, 0), pipeline_mode=pl.Buffered(1))],
        out_specs=(pl.BlockSpec((tm, N_CHIPS * D_MODEL), lambda i: (i, 0)),
                   pl.BlockSpec((D_MODEL, tm), lambda i: (0, i))),
        out_shape=(jax.ShapeDtypeStruct((seq, N_CHIPS * D_MODEL), F32),
                   jax.ShapeDtypeStruct((D_MODEL, seq), BF16)),
        compiler_params=_compiler_params(40, ("arbitrary",)),
    )(x, wing)


def _col_block(group, n_sub=CONV_WIDTH // LANES):
    return lambda j: (0, group * n_sub + j)


def _shift_down(ext, k, rows):
    return pltpu.roll(ext, k, 0)[SUBLANES:, :]


def _shift_up(ext, k, rows):
    return pltpu.roll(ext, rows + SUBLANES - k, 0)[:rows, :]


def _conv_fwd(proj, conv_w):
    seq = proj.shape[0]
    rows = CONV_ROWS
    n_chunks = seq // rows

    def body(b_ref, c_ref, h_ref, z_ref, w_ref, out_ref, u_s):
        u_s[0:SUBLANES, :] = jnp.zeros((SUBLANES, LANES), F32)

        def fill(r, carry):
            rs = pl.ds(pl.multiple_of(r * rows, rows), rows)
            u_s[pl.ds(pl.multiple_of(r * rows + SUBLANES, SUBLANES), rows), :] = c_ref[rs, :] * h_ref[rs, :]
            return carry

        lax.fori_loop(0, n_chunks, fill, 0)
        w = w_ref[...]

        def chunk(r, carry):
            r0 = pl.multiple_of(r * rows, rows)
            rs = pl.ds(r0, rows)
            ext = u_s[pl.ds(r0, rows + SUBLANES), :]
            u = ext[SUBLANES:, :]
            y = w[2:3, :] * u
            y = y + w[0:1, :] * _shift_down(ext, 2, rows)
            y = y + w[1:2, :] * _shift_down(ext, 1, rows)
            z = z_ref[rs, :]
            out_ref[rs, :] = ((z * _sigmoid(z)) * (b_ref[rs, :] * y)).astype(BF16)
            return carry

        lax.fori_loop(0, n_chunks, chunk, 0)

    col = lambda g: pl.BlockSpec((seq, LANES), _col_block(g))
    return pl.pallas_call(
        body, name="conv_fwd",
        grid=(CONV_WIDTH // LANES,),
        in_specs=[col(0), col(1), col(2), col(3), pl.BlockSpec((3, LANES), lambda j: (0, j))],
        out_specs=pl.BlockSpec((seq, LANES), lambda j: (0, j)),
        out_shape=jax.ShapeDtypeStruct((seq, CONV_WIDTH), BF16),
        scratch_shapes=[pltpu.VMEM((seq + SUBLANES, LANES), F32)],
        compiler_params=_compiler_params(40, ("arbitrary",)),
    )(proj, proj, proj, proj, conv_w)


def _conv_bwd(proj, dmix, conv_w):
    seq = proj.shape[0]
    rows = CONV_ROWS
    n_chunks = seq // rows

    def body(b_ref, c_ref, h_ref, z_ref, d_ref, w_ref, dp_ref, dw_ref, u_s, dy_s):
        u_s[0:SUBLANES, :] = jnp.zeros((SUBLANES, LANES), F32)
        dy_s[seq:seq + SUBLANES, :] = jnp.zeros((SUBLANES, LANES), F32)

        def fill(r, carry):
            r0 = pl.multiple_of(r * rows, rows)
            rs = pl.ds(r0, rows)
            u_s[pl.ds(pl.multiple_of(r0 + SUBLANES, SUBLANES), rows), :] = c_ref[rs, :] * h_ref[rs, :]
            z = z_ref[rs, :]
            dy_s[rs, :] = d_ref[rs, :] * (z * _sigmoid(z)) * b_ref[rs, :]
            return carry

        lax.fori_loop(0, n_chunks, fill, 0)
        w = w_ref[...]

        def chunk(r, acc):
            r0 = pl.multiple_of(r * rows, rows)
            rs = pl.ds(r0, rows)
            ext = u_s[pl.ds(r0, rows + SUBLANES), :]
            u = ext[SUBLANES:, :]
            um1 = _shift_down(ext, 1, rows)
            um2 = _shift_down(ext, 2, rows)
            y = w[2:3, :] * u
            y = y + w[0:1, :] * um2
            y = y + w[1:2, :] * um1
            z = z_ref[rs, :]
            b = b_ref[rs, :]
            dco = d_ref[rs, :]
            sg = _sigmoid(z)
            g = z * sg
            dp_ref[0, rs, :] = (dco * g * y).astype(BF16)
            dp_ref[3, rs, :] = (dco * b * y * (sg * (1.0 + z * (1.0 - sg)))).astype(BF16)
            ext_dy = dy_s[pl.ds(r0, rows + SUBLANES), :]
            dy = ext_dy[:rows, :]
            du = w[2:3, :] * dy + w[1:2, :] * _shift_up(ext_dy, 1, rows) + w[0:1, :] * _shift_up(ext_dy, 2, rows)
            dp_ref[1, rs, :] = (du * h_ref[rs, :]).astype(BF16)
            dp_ref[2, rs, :] = (du * c_ref[rs, :]).astype(BF16)
            a0, a1, a2 = acc
            return (a0 + jnp.sum(dy * um2, axis=0, keepdims=True),
                    a1 + jnp.sum(dy * um1, axis=0, keepdims=True),
                    a2 + jnp.sum(dy * u, axis=0, keepdims=True))

        zero = jnp.zeros((1, LANES), F32)
        a0, a1, a2 = lax.fori_loop(0, n_chunks, chunk, (zero, zero, zero))
        dw_ref[...] = jnp.concatenate([a0, a1, a2, jnp.zeros((SUBLANES - 3, LANES), F32)], axis=0)

    col = lambda g: pl.BlockSpec((seq, LANES), _col_block(g))
    return pl.pallas_call(
        body, name="conv_bwd",
        grid=(CONV_WIDTH // LANES,),
        in_specs=[col(0), col(1), col(2), col(3), col(0), pl.BlockSpec((3, LANES), lambda j: (0, j))],
        out_specs=(pl.BlockSpec((4, seq, LANES), lambda j: (0, 0, j)),
                   pl.BlockSpec((SUBLANES, LANES), lambda j: (0, j))),
        out_shape=(jax.ShapeDtypeStruct((4, seq, CONV_WIDTH), BF16),
                   jax.ShapeDtypeStruct((SUBLANES, CONV_WIDTH), F32)),
        scratch_shapes=[pltpu.VMEM((seq + SUBLANES, LANES), F32), pltpu.VMEM((seq + SUBLANES, LANES), F32)],
        compiler_params=_compiler_params(48, ("arbitrary",)),
    )(proj, proj, proj, proj, dmix, conv_w)


def _tri_masks():
    r = lax.broadcasted_iota(jnp.int32, (TK, TK), 0)
    s = lax.broadcasted_iota(jnp.int32, (TK, TK), 1)
    return r, s


def _head_mask():
    lane = lax.broadcasted_iota(jnp.int32, (1, LANES), 1)
    return lane < HEAD_DIM


def _fill_tri(ref, ones):
    ref[...] = jnp.where(ones, 1.0, 0.0).astype(BF16)


def _tile_lanes(a):
    return jnp.tile(a, (1, TK // LANES))


DEAD_CARRY_BITS = 160.0


def _attn_fwd(proj):
    seq = proj.shape[0]
    n_q = seq // TQ
    n_kb = seq // TK
    assert TQ == TK and n_kb <= LANES
    heads = range(HEADS_PER_STEP)

    def body(q_ref, k_ref, v_ref, za_ref, o_ref, mix_ref, car_ref, qq_s, k_s, vt_s, suffix_s, carry_s, diff_s, ot_s):
        head_a = _head_mask()
        r_i, s_i = _tri_masks()
        diff_s[...] = r_i - s_i
        _fill_tri(suffix_s, r_i > s_i)
        lane = lax.broadcasted_iota(jnp.int32, (1, LANES), 1)

        def prep(r, carry):
            rs = pl.ds(pl.multiple_of(r * TQ, TQ), TQ)
            q = q_ref[rs, :] * Q_SCALE
            qq_s[r, 0] = jnp.where(head_a, q, 0.0).astype(BF16)
            qq_s[r, 1] = jnp.where(head_a, 0.0, q).astype(BF16)
            k_s[rs, :] = k_ref[rs, :].astype(BF16)
            vt_s[r] = v_ref[rs, :].T.astype(BF16)
            return carry

        lax.fori_loop(0, n_q, prep, 0)

        def block_pair(j, carry):
            qbs = [2 * j, 2 * j + 1]
            rows = [pl.ds(pl.multiple_of(qb * TQ, TQ), TQ) for qb in qbs]
            both = [qq_s[qb].reshape(HEADS_PER_STEP * TQ, LANES) for qb in qbs]
            carry_s[...] = jnp.zeros(carry_s.shape, F32)
            ot_s[...] = jnp.zeros(ot_s.shape, F32)
            for rs in rows:
                for h in heads:
                    car_ref[h, rs, :] = jnp.zeros((TQ, LANES), F32)

            def tiles(items):
                kbs = [qbs[s] - n for s, n in items]
                cols = [pl.ds(pl.multiple_of(jnp.maximum(kb, 0) * TK, TK), TK) for kb in kbs]
                zs = [_dot_nt(both[s], k_s[cs, :]).reshape(HEADS_PER_STEP, TQ, TK) for (s, _), cs in zip(items, cols)]
                zms, cars, sufs = [], [], []
                for (s, n), kb, z in zip(items, kbs, zs):
                    mask = diff_s[...] > jnp.where(kb < 0, TK, jnp.where(n == 0, 0, -TK))
                    parts = []
                    for h in heads:
                        z2 = jnp.where(mask, z[h] * LOG2_E, MASKED_LOGIT)
                        sp = _softplus2(z2)
                        zms.append(z2 - sp)
                        parts += _split(sp)
                        car = carry_s[s, h]
                        cars.append(car)
                        car_ref[h, rows[s], :] = jnp.where(lane == kb, car, car_ref[h, rows[s], :])
                        carry_s[s, h] = car + jnp.sum(sp, axis=1, keepdims=True)
                    sufs += _block_sums(parts, suffix_s[...])
                for i, ((s, _), kb) in enumerate(zip(items, kbs)):
                    out = []
                    for h in heads:
                        j2 = i * HEADS_PER_STEP + h
                        w = jnp.exp2(zms[j2] - (sufs[j2] + _tile_lanes(cars[j2]))).astype(BF16)
                        out.append(_dot_nt(vt_s[jnp.maximum(kb, 0), h * HEAD_DIM:(h + 1) * HEAD_DIM, :], w))
                    ot_s[s] += jnp.concatenate(out, axis=0)

            tiles([(0, 0), (0, 1), (1, 0), (1, 1)])

            for s, (qb, rs) in enumerate(zip(qbs, rows)):
                def lowest_carry():
                    return jnp.min(jnp.minimum(carry_s[s, 0], carry_s[s, 1]))

                def live(state):
                    n, lowest = state
                    return jnp.logical_and(n <= qb, lowest < DEAD_CARRY_BITS)

                def step(state):
                    n, _ = state
                    tiles([(s, n)])
                    return n + 1, lowest_carry()

                n_done, _ = lax.while_loop(live, step, (jnp.int32(2), lowest_carry()))
                for h in heads:
                    car_ref[h, rs, :] = jnp.where(lane < qb + 1 - n_done, carry_s[s, h], car_ref[h, rs, :])
                o = ot_s[s].T
                o_ref[rs, :] = o
                za = za_ref[rs, :]
                mix_ref[rs, :] = ((za * _sigmoid(za)) * o).astype(BF16)
            return carry

        assert n_q % 2 == 0
        lax.fori_loop(0, n_q // 2, block_pair, 0)

    col = lambda g: pl.BlockSpec((seq, LANES), _col_block(g))
    n_pairs = ATTN_WIDTH // LANES
    return pl.pallas_call(
        body, name="attn_fwd",
        grid=(n_pairs,),
        in_specs=[col(4), col(5), col(6), col(7)],
        out_specs=(pl.BlockSpec((seq, LANES), lambda p: (0, p)),
                   pl.BlockSpec((seq, LANES), lambda p: (0, p)),
                   pl.BlockSpec((HEADS_PER_STEP, seq, LANES), lambda p: (p, 0, 0))),
        out_shape=(jax.ShapeDtypeStruct((seq, ATTN_WIDTH), F32),
                   jax.ShapeDtypeStruct((seq, ATTN_WIDTH), BF16),
                   jax.ShapeDtypeStruct((n_pairs * HEADS_PER_STEP, seq, LANES), F32)),
        scratch_shapes=[pltpu.VMEM((n_q, HEADS_PER_STEP, TQ, LANES), BF16),
                        pltpu.VMEM((seq, LANES), BF16),
                        pltpu.VMEM((n_kb, LANES, TK), BF16),
                        pltpu.VMEM((TK, TK), BF16),
                        pltpu.VMEM((2, HEADS_PER_STEP, TQ, LANES), F32),
                        pltpu.VMEM((TQ, TK), jnp.int32),
                        pltpu.VMEM((2, LANES, TQ), F32)],
        compiler_params=_compiler_params(48, ("arbitrary",)),
    )(proj, proj, proj, proj)


def _attn_bwd(proj, o, dmix, carries):
    seq = proj.shape[0]
    n_q = seq // TQ
    n_kb = seq // TK
    assert TQ == TK and n_kb <= LANES
    heads = range(HEADS_PER_STEP)

    def body(q_ref, k_ref, v_ref, za_ref, o_ref, dm_ref, car_ref, dp_ref,
             qq_s, dd_s, k_s, v_s, qt_s, dot_s, kt_s, suffix_s, prefix_s, ccar_s, dq_s, dk_s, dv_s, diff_s):
        head_a = _head_mask()
        r_i, s_i = _tri_masks()
        diff_s[...] = r_i - s_i
        _fill_tri(suffix_s, r_i > s_i)
        _fill_tri(prefix_s, r_i < s_i)
        lane = lax.broadcasted_iota(jnp.int32, (1, LANES), 1)

        def prep(r, carry):
            rs = pl.ds(pl.multiple_of(r * TQ, TQ), TQ)
            q = q_ref[rs, :] * Q_SCALE
            qq_s[r, 0] = jnp.where(head_a, q, 0.0).astype(BF16)
            qq_s[r, 1] = jnp.where(head_a, 0.0, q).astype(BF16)
            qt_s[r] = q.T.astype(BF16)
            k = k_ref[rs, :]
            k_s[rs, :] = k.astype(BF16)
            kt_s[r] = k.T.astype(BF16)
            v_s[rs, :] = v_ref[rs, :].astype(BF16)
            za = za_ref[rs, :]
            sg = _sigmoid(za)
            dm = dm_ref[rs, :]
            do = dm * (za * sg)
            dd_s[r, 0] = jnp.where(head_a, do, 0.0).astype(BF16)
            dd_s[r, 1] = jnp.where(head_a, 0.0, do).astype(BF16)
            dot_s[r] = do.T.astype(BF16)
            dp_ref[3, rs, :] = (dm * o_ref[rs, :] * (sg * (1.0 + za * (1.0 - sg)))).astype(BF16)
            dq_s[:, rs] = jnp.zeros((LANES, TQ), F32)
            dk_s[:, rs] = jnp.zeros((LANES, TQ), F32)
            dv_s[:, rs] = jnp.zeros((LANES, TQ), F32)
            return carry

        lax.fori_loop(0, n_q, prep, 0)

        def block_pair(j, carry):
            qbs = [2 * j, 2 * j + 1]
            rows = [pl.ds(pl.multiple_of(qb * TQ, TQ), TQ) for qb in qbs]
            both_q = [qq_s[qb].reshape(HEADS_PER_STEP * TQ, LANES) for qb in qbs]
            both_do = [dd_s[qb].reshape(HEADS_PER_STEP * TQ, LANES) for qb in qbs]
            ccar_s[...] = jnp.zeros(ccar_s.shape, F32)

            def tiles(items):
                cols = [pl.ds(pl.multiple_of(jnp.maximum(kb, 0) * TK, TK), TK) for _, kb in items]
                zs = [_dot_nt(both_q[s], k_s[cs, :]).reshape(HEADS_PER_STEP, TQ, TK) for (s, _), cs in zip(items, cols)]
                dws = [_dot_nt(both_do[s], v_s[cs, :]).reshape(HEADS_PER_STEP, TQ, TK)
                       for (s, _), cs in zip(items, cols)]
                zms, sufs = [], []
                for (s, kb), z in zip(items, zs):
                    mask = diff_s[...] > jnp.where(kb < 0, TK, jnp.where(kb == qbs[s], 0, -TK))
                    parts = []
                    for h in heads:
                        z2 = jnp.where(mask, z[h] * LOG2_E, MASKED_LOGIT)
                        sp = _softplus2(z2)
                        zms.append(z2 - sp)
                        parts += _split(sp)
                    sufs += _block_sums(parts, suffix_s[...])
                ws, dlws, pres = [], [], []
                for i, (s, kb) in enumerate(items):
                    parts = []
                    for h in heads:
                        j2 = i * HEADS_PER_STEP + h
                        cin = jnp.sum(jnp.where(lane == kb, car_ref[h, rows[s], :], 0.0), axis=1, keepdims=True)
                        w = jnp.exp2(zms[j2] - (sufs[j2] + cin))
                        ws.append(w.astype(BF16))
                        dlw = dws[i][h] * w
                        dlws.append(dlw)
                        parts += _split(dlw)
                    pres += _block_sums(parts, prefix_s[...])
                for i, ((s, kb), cs) in enumerate(zip(items, cols)):
                    dq_t, dk_t, dv_t = [], [], []
                    for h in heads:
                        j2 = i * HEADS_PER_STEP + h
                        ccar = ccar_s[s, h]
                        dz = (dlws[j2] - jnp.exp2(zms[j2]) * (dlws[j2] + (pres[j2] + _tile_lanes(ccar)))).astype(BF16)
                        ccar_s[s, h] = ccar + jnp.sum(dlws[j2], axis=1, keepdims=True)
                        dims = slice(h * HEAD_DIM, (h + 1) * HEAD_DIM)
                        dq_t.append(_dot_nt(kt_s[jnp.maximum(kb, 0), dims, :], dz))
                        dk_t.append(_dot(qt_s[qbs[s], dims, :], dz))
                        dv_t.append(_dot(dot_s[qbs[s], dims, :], ws[j2]))
                    dq_s[:, rows[s]] += jnp.concatenate(dq_t, axis=0)
                    dk_s[:, cs] += jnp.concatenate(dk_t, axis=0)
                    dv_s[:, cs] += jnp.concatenate(dv_t, axis=0)

            for s, (qb, rs) in enumerate(zip(qbs, rows)):
                lowest = jnp.min(jnp.minimum(car_ref[0, rs, :], car_ref[1, rs, :]), axis=0, keepdims=True)
                dead = jnp.logical_and(lowest >= DEAD_CARRY_BITS, lane < qb)
                first_live = jnp.sum(jnp.where(dead, 1, 0))

                def one(kb, c2):
                    tiles([(s, kb)])
                    return c2

                lax.fori_loop(first_live, qb - 1, one, 0)

            for s, qb in enumerate(qbs):
                tiles([(s, qb - 1), (s, qb)])
            return carry

        assert n_q % 2 == 0
        lax.fori_loop(0, n_q // 2, block_pair, 0)

        def finish(r, carry):
            rs = pl.ds(pl.multiple_of(r * TQ, TQ), TQ)
            dp_ref[0, rs, :] = (dq_s[:, rs].T * Q_SCALE).astype(BF16)
            dp_ref[1, rs, :] = dk_s[:, rs].T.astype(BF16)
            dp_ref[2, rs, :] = dv_s[:, rs].T.astype(BF16)
            return carry

        lax.fori_loop(0, n_q, finish, 0)

    def col(g, n_sub=CONV_WIDTH // LANES):
        return pl.BlockSpec((seq, LANES), _col_block(g, n_sub), pipeline_mode=pl.Buffered(1))

    n_pairs = ATTN_WIDTH // LANES
    bf = lambda: pltpu.VMEM((seq, LANES), BF16)
    by_head = lambda: pltpu.VMEM((n_q, HEADS_PER_STEP, TQ, LANES), BF16)
    transposed = lambda: pltpu.VMEM((n_q, LANES, TQ), BF16)
    acc_t = lambda: pltpu.VMEM((LANES, seq), F32)
    return pl.pallas_call(
        body, name="attn_bwd",
        grid=(n_pairs,),
        in_specs=[col(4), col(5), col(6), col(7), col(0), col(1),
                  pl.BlockSpec((HEADS_PER_STEP, seq, LANES), lambda p: (p, 0, 0), pipeline_mode=pl.Buffered(1))],
        out_specs=pl.BlockSpec((4, seq, LANES), lambda p: (0, 0, p)),
        out_shape=jax.ShapeDtypeStruct((4, seq, ATTN_WIDTH), BF16),
        scratch_shapes=[by_head(), by_head(), bf(), bf(), transposed(), transposed(), transposed(),
                        pltpu.VMEM((TK, TK), BF16), pltpu.VMEM((TK, TK), BF16),
                        pltpu.VMEM((2, HEADS_PER_STEP, TQ, LANES), F32),
                        acc_t(), acc_t(), acc_t(),
                        pltpu.VMEM((TQ, TK), jnp.int32)],
        compiler_params=_compiler_params(56, ("arbitrary",)),
    )(proj, proj, proj, proj, o, dmix, carries)


def _out_ln(mix_c, mix_a, woutg, x, target, gain, bias):
    seq = x.shape[0]
    tm = 256
    inv_d = 1.0 / D_MODEL

    def body(mc_ref, ma_ref, wo_ref, x_ref, t_ref, g_ref, b_ref, dxa_ref, dmix_ref, gw_ref, st_ref):
        @pl.when(pl.program_id(0) == 0)
        def _():
            gw_ref[...] = jnp.zeros(gw_ref.shape, F32)
            st_ref[...] = jnp.zeros(st_ref.shape, F32)

        mc = mc_ref[...]
        ma = ma_ref[...]
        sub = _dot(mc, wo_ref[0:CONV_WIDTH, :]) + _dot(ma, wo_ref[CONV_WIDTH:, :])
        r = DEEPNORM_ALPHA * x_ref[...] + sub
        mu = jnp.mean(r, axis=-1, keepdims=True)
        rc = r - mu
        var = jnp.mean(rc * rc, axis=-1, keepdims=True)
        rstd = lax.rsqrt(var + LN_EPS)
        xhat = rc * rstd
        gain_v = g_ref[...]
        diff = (xhat * gain_v + b_ref[...]) - t_ref[...]
        dy = diff * inv_d
        st_ref[0:1, :] += jnp.sum(dy * xhat, axis=0, keepdims=True)
        st_ref[1:2, :] += jnp.sum(dy, axis=0, keepdims=True)
        st_ref[2:3, :] += jnp.sum(diff * diff, axis=0, keepdims=True)
        dxh = dy * gain_v
        m1 = jnp.mean(dxh, axis=-1, keepdims=True)
        m2 = jnp.mean(dxh * xhat, axis=-1, keepdims=True)
        dr = rstd * (dxh - m1 - xhat * m2)
        dxa_ref[...] = DEEPNORM_ALPHA * dr
        drb = dr.astype(BF16)
        dmix_ref[...] = _dot_nt(drb, wo_ref[...])
        gw_ref[0:CONV_WIDTH, :] += _dot_tn(mc, drb)
        gw_ref[CONV_WIDTH:, :] += _dot_tn(ma, drb)

    rows_d = lambda: pl.BlockSpec((tm, D_MODEL), lambda i: (i, 0))
    rows_h = lambda: pl.BlockSpec((tm, CONV_WIDTH), lambda i: (i, 0))
    whole = lambda shape: pl.BlockSpec(shape, lambda i: (0, 0))
    return pl.pallas_call(
        body, name="out_ln",
        grid=(seq // tm,),
        in_specs=[rows_h(), rows_h(), whole((D_MODEL, D_MODEL)), rows_d(), rows_d(),
                  whole((1, D_MODEL)), whole((1, D_MODEL))],
        out_specs=(rows_d(), rows_d(), whole((D_MODEL, D_MODEL)), whole((SUBLANES, D_MODEL))),
        out_shape=(jax.ShapeDtypeStruct((seq, D_MODEL), F32), jax.ShapeDtypeStruct((seq, D_MODEL), F32),
                   jax.ShapeDtypeStruct((D_MODEL, D_MODEL), F32), jax.ShapeDtypeStruct((SUBLANES, D_MODEL), F32)),
        compiler_params=_compiler_params(48, ("arbitrary",)),
    )(mix_c, mix_a, woutg, x, target, gain, bias)


def _group_maps():
    half = N_GROUPS // 2
    return (lambda g: jnp.minimum(g, half - 1)), (lambda g: jnp.maximum(g - half, 0))


def _grad_w_in(xt, dpc, dpa):
    seq = xt.shape[1]
    half = N_GROUPS // 2
    in_conv, in_attn = _group_maps()

    def body(xt_ref, dc_ref, da_ref, out_ref):
        g = pl.program_id(0)

        @pl.when(g < half)
        def _():
            out_ref[...] = _dot(xt_ref[...], dc_ref[...])

        @pl.when(g >= half)
        def _():
            out_ref[...] = _dot(xt_ref[...], da_ref[...])

    return pl.pallas_call(
        body, name="grad_w_in",
        grid=(N_GROUPS,),
        in_specs=[pl.BlockSpec((D_MODEL, seq), lambda g: (0, 0), pipeline_mode=pl.Buffered(1)),
                  pl.BlockSpec((None, seq, CONV_WIDTH), lambda g: (in_conv(g), 0, 0)),
                  pl.BlockSpec((None, seq, CONV_WIDTH), lambda g: (in_attn(g), 0, 0))],
        out_specs=pl.BlockSpec((None, D_MODEL, CONV_WIDTH), lambda g: (g // 2, 0, g % 2)),
        out_shape=jax.ShapeDtypeStruct((N_CHIPS, D_MODEL, D_MODEL), F32),
        compiler_params=_compiler_params(48, ("arbitrary",)),
    )(xt, dpc, dpa)


def _grad_x(dxa, dpc, dpa, wing):
    seq = dxa.shape[0]
    tm = 256
    half = N_GROUPS // 2

    def body(dxa_ref, dc_ref, da_ref, w_ref, out_ref):
        acc = dxa_ref[...]
        for g in range(N_GROUPS):
            dp = dc_ref[g] if g < half else da_ref[g - half]
            cols = slice((g % 2) * CONV_WIDTH, (g % 2 + 1) * CONV_WIDTH)
            acc = acc + _dot_nt(dp, w_ref[g // 2, :, cols])
        out_ref[...] = acc

    return pl.pallas_call(
        body, name="grad_x",
        grid=(seq // tm,),
        in_specs=[pl.BlockSpec((tm, D_MODEL), lambda i: (i, 0)),
                  pl.BlockSpec((half, tm, CONV_WIDTH), lambda i: (0, i, 0)),
                  pl.BlockSpec((half, tm, CONV_WIDTH), lambda i: (0, i, 0)),
                  pl.BlockSpec((N_CHIPS, D_MODEL, D_MODEL), lambda i: (0, 0, 0), pipeline_mode=pl.Buffered(1))],
        out_specs=pl.BlockSpec((tm, D_MODEL), lambda i: (i, 0)),
        out_shape=jax.ShapeDtypeStruct((seq, D_MODEL), F32),
        compiler_params=_compiler_params(40, ("arbitrary",)),
    )(dxa, dpc, dpa, wing)


PACK_LOSS_ROW = 5


def _reduce_grads(gwin, gwout, pack):
    n_shards, n_rows_in, n_cols = gwin.shape
    n_rows_out = gwout.shape[1]
    half_rows = (n_rows_in // 2, n_rows_out // 2)
    row_chunk = 128

    def body(gwin_ref, gwout_ref, pack_ref, oin_ref, oout_ref, opack_ref,
             mine_in, mine_out, sib_in, sib_out, wire_in, wire_out, rin, rout, rpack,
             local_sems, sib_send, sib_recv, send_sems, recv_sems, pack_send, pack_recv, fin_send, fin_recv):
        x, y, c = lax.axis_index("x"), lax.axis_index("y"), lax.axis_index("c")
        me = 2 * x + y
        my_id = 4 * x + 2 * y + c
        chips = [(1 - x, y), (x, 1 - y), (1 - x, 1 - y)]
        sibling = (x, y, 1 - c)
        partial = (gwin_ref, gwout_ref)
        mine = (mine_in, mine_out)
        from_sib = (sib_in, sib_out)
        wire = (wire_in, wire_out)
        from_chips = (rin, rout)
        result = (oin_ref, oout_ref)

        def half(a, which):
            rows = half_rows[a]
            return pl.ds(pl.multiple_of(which * rows, rows), rows)

        sends = []
        loads = []
        for a in range(2):
            ld = pltpu.make_async_copy(partial[a].at[:, half(a, c), :], mine[a], local_sems.at[a])
            ld.start()
            loads.append(ld)
            cp = pltpu.make_async_remote_copy(
                src_ref=partial[a].at[:, half(a, 1 - c), :], dst_ref=from_sib[a],
                send_sem=sib_send.at[a], recv_sem=sib_recv.at[a], device_id=sibling, device_id_type=MESH)
            cp.start()
            sends.append(cp)

        rpack[my_id] = pack_ref[...]

        def pack_copy(rel, slot, to):
            return pltpu.make_async_remote_copy(
                src_ref=rpack.at[slot], dst_ref=rpack.at[slot],
                send_sem=pack_send.at[rel - 1], recv_sem=pack_recv.at[rel - 1],
                device_id=to, device_id_type=MESH)

        def related(rel):
            px = (1 - x) if rel & 4 else x
            py = (1 - y) if rel & 2 else y
            pc = (1 - c) if rel & 1 else c
            return px, py, pc

        for rel in range(1, N_DEVICES):
            cp = pack_copy(rel, my_id, related(rel))
            cp.start()
            sends.append(cp)

        for ld in loads:
            ld.wait()
        for a in range(2):
            pltpu.make_async_remote_copy(
                src_ref=partial[a].at[:, half(a, 1 - c), :], dst_ref=from_sib[a],
                send_sem=sib_send.at[a], recv_sem=sib_recv.at[a], device_id=sibling, device_id_type=MESH).wait_recv()

        def chip_sum(a, shard):
            def add(r, carry):
                rs = pl.ds(pl.multiple_of(r * row_chunk, row_chunk), row_chunk)
                mine[a][shard, rs, :] = mine[a][shard, rs, :] + from_sib[a][shard, rs, :]
                return carry

            lax.fori_loop(0, half_rows[a] // row_chunk, add, 0)

        def chip_sum_to_wire(a, shard, k):
            def add(r, carry):
                rs = pl.ds(pl.multiple_of(r * row_chunk, row_chunk), row_chunk)
                wire[a][k, rs, :] = (mine[a][shard, rs, :] + from_sib[a][shard, rs, :]).astype(BF16)
                return carry

            lax.fori_loop(0, half_rows[a] // row_chunk, add, 0)

        def shard_copy(k, a, to):
            return pltpu.make_async_remote_copy(
                src_ref=wire[a].at[k], dst_ref=from_chips[a].at[k],
                send_sem=send_sems.at[2 * k + a], recv_sem=recv_sems.at[2 * k + a],
                device_id=to, device_id_type=MESH)

        for k, (px, py) in enumerate(chips):
            for a in range(2):
                chip_sum_to_wire(a, 2 * px + py, k)
                cp = shard_copy(k, a, (px, py, c))
                cp.start()
                sends.append(cp)
        for a in range(2):
            chip_sum(a, me)

        for rel in range(1, N_DEVICES):
            px, py, pc = related(rel)
            pack_copy(rel, 4 * px + 2 * py + pc, (px, py, pc)).wait_recv()
        total = rpack[0]
        for i in range(1, N_DEVICES):
            total = total + rpack[i]
        opack_ref[...] = total
        sq_err = jnp.sum(total[PACK_LOSS_ROW:PACK_LOSS_ROW + 1, :], axis=1, keepdims=True)
        opack_ref[PACK_LOSS_ROW:PACK_LOSS_ROW + 1, :] = jnp.broadcast_to(sq_err * (0.5 / D_MODEL), (1, n_cols))

        for k, (px, py) in enumerate(chips):
            for a in range(2):
                shard_copy(k, a, (px, py, c)).wait_recv()

        def finish(a):
            def add(r, carry):
                rs = pl.ds(pl.multiple_of(r * row_chunk, row_chunk), row_chunk)
                dst = pl.ds(pl.multiple_of(c * half_rows[a] + r * row_chunk, row_chunk), row_chunk)
                result[a][dst, :] = ((mine[a][me, rs, :] + from_chips[a][0, rs, :].astype(F32))
                                     + from_chips[a][1, rs, :].astype(F32)) + from_chips[a][2, rs, :].astype(F32)
                return carry

            lax.fori_loop(0, half_rows[a] // row_chunk, add, 0)

        def final_copy(a, which):
            ref = result[a].at[half(a, which), :]
            return pltpu.make_async_remote_copy(
                src_ref=ref, dst_ref=ref, send_sem=fin_send.at[a], recv_sem=fin_recv.at[a],
                device_id=sibling, device_id_type=MESH)

        for a in range(2):
            finish(a)
            cp = final_copy(a, c)
            cp.start()
            sends.append(cp)
        for a in range(2):
            final_copy(a, 1 - c).wait_recv()
        for cp in sends:
            cp.wait_send()

    vmem = pl.BlockSpec(memory_space=pltpu.VMEM)
    hbm = pl.BlockSpec(memory_space=pl.ANY)
    in_shape = (n_rows_in, n_cols)
    out_shape = (n_rows_out, n_cols)
    half_in = (half_rows[0], n_cols)
    half_out = (half_rows[1], n_cols)
    return pl.pallas_call(
        body, name="reduce_grads",
        out_shape=(jax.ShapeDtypeStruct(in_shape, F32), jax.ShapeDtypeStruct(out_shape, F32),
                   jax.ShapeDtypeStruct(pack.shape, F32)),
        in_specs=[hbm, hbm, vmem], out_specs=(vmem, vmem, vmem),
        scratch_shapes=[pltpu.VMEM((n_shards,) + half_in, F32), pltpu.VMEM((n_shards,) + half_out, F32),
                        pltpu.VMEM((n_shards,) + half_in, F32), pltpu.VMEM((n_shards,) + half_out, F32),
                        pltpu.VMEM((N_CHIPS - 1,) + half_in, BF16), pltpu.VMEM((N_CHIPS - 1,) + half_out, BF16),
                        pltpu.VMEM((N_CHIPS - 1,) + half_in, BF16), pltpu.VMEM((N_CHIPS - 1,) + half_out, BF16),
                        pltpu.VMEM((N_DEVICES,) + pack.shape, F32),
                        pltpu.SemaphoreType.DMA((2,)),
                        pltpu.SemaphoreType.DMA((2,)), pltpu.SemaphoreType.DMA((2,)),
                        pltpu.SemaphoreType.DMA((6,)), pltpu.SemaphoreType.DMA((6,)),
                        pltpu.SemaphoreType.DMA((N_DEVICES - 1,)), pltpu.SemaphoreType.DMA((N_DEVICES - 1,)),
                        pltpu.SemaphoreType.DMA((2,)), pltpu.SemaphoreType.DMA((2,))],
        compiler_params=_compiler_params(48),
    )(gwin, gwout, pack)


def _adamw(name, w, g, m, v):
    n_rows, n_cols = w.shape
    tr = 256 if n_rows % 256 == 0 else n_rows
    m_corr = 1.0 - ADAM_B1 ** ADAM_STEP
    v_corr = 1.0 - ADAM_B2 ** ADAM_STEP

    def body(w_ref, g_ref, m_ref, v_ref, d_ref, nm_ref, nv_ref):
        gv = g_ref[...]
        nm = ADAM_B1 * m_ref[...] + (1.0 - ADAM_B1) * gv
        nv = ADAM_B2 * v_ref[...] + (1.0 - ADAM_B2) * (gv * gv)
        nm_ref[...] = nm
        nv_ref[...] = nv
        d_ref[...] = -ADAM_LR * ((nm / m_corr) / (jnp.sqrt(nv / v_corr) + ADAM_EPS) + ADAM_WD * w_ref[...])

    blk = lambda: pl.BlockSpec((tr, n_cols), lambda i: (i, 0))
    shape = jax.ShapeDtypeStruct(w.shape, F32)
    return pl.pallas_call(
        body, name=name,
        grid=(n_rows // tr,),
        in_specs=[blk(), blk(), blk(), blk()], out_specs=(blk(), blk(), blk()),
        out_shape=(shape, shape, shape),
        compiler_params=_compiler_params(32, ("arbitrary",)),
    )(w, g, m, v)


def kernel(x, w_in, conv_w, w_out, ln_gain, ln_bias, loss_target, m_w_in, m_conv_w, m_w_out, m_ln_gain, m_ln_bias, v_w_in, v_conv_w, v_w_out, v_ln_gain, v_ln_bias):
    xs = x[0]
    target = loss_target[0]

    wing, woutg, cwg = _gather_weights(w_in[0], w_out[0], conv_w[0])
    conv_full = jnp.transpose(cwg, (1, 0, 2)).reshape(3, CONV_WIDTH)
    wout_full = woutg.reshape(D_MODEL, D_MODEL)

    proj, xt = _proj(xs, wing)
    mix_c = _conv_fwd(proj, conv_full)
    o, mix_a, carries = _attn_fwd(proj)
    dxa, dmix, gwout, stats = _out_ln(mix_c, mix_a, wout_full, xs, target, ln_gain, ln_bias)
    dpc, dconv = _conv_bwd(proj, dmix, conv_full)
    dpa = _attn_bwd(proj, o, dmix, carries)
    gwin = _grad_w_in(xt, dpc, dpa)
    grad_x = _grad_x(dxa, dpc, dpa, wing)

    pack = jnp.concatenate(
        [stats[0:2], jnp.pad(dconv[0:3], ((0, 0), (0, D_MODEL - CONV_WIDTH))), stats[2:3],
         jnp.zeros((2, D_MODEL), F32)], axis=0)
    g_w_in, g_w_out, tot = _reduce_grads(gwin, gwout.reshape(N_CHIPS, D_MODEL // N_CHIPS, D_MODEL), pack)

    chip = 2 * lax.axis_index("x") + lax.axis_index("y")
    g_gain = tot[0:1]
    g_bias = tot[1:2]
    g_conv = lax.dynamic_slice(tot, (2, chip * LANES), (3, LANES))
    loss = tot[PACK_LOSS_ROW, 0]

    d_w_in, nm_w_in, nv_w_in = _adamw("adamw_w_in", w_in[0], g_w_in, m_w_in[0], v_w_in[0])
    d_w_out, nm_w_out, nv_w_out = _adamw("adamw_w_out", w_out[0], g_w_out, m_w_out[0], v_w_out[0])
    d_conv, nm_conv, nv_conv = _adamw("adamw_conv_w", conv_w[0], g_conv, m_conv_w[0], v_conv_w[0])
    d_gain, nm_gain, nv_gain = _adamw("adamw_ln_gain", ln_gain, g_gain, m_ln_gain, v_ln_gain)
    d_bias, nm_bias, nv_bias = _adamw("adamw_ln_bias", ln_bias, g_bias, m_ln_bias, v_ln_bias)

    lead = lambda a: a[None]
    return (loss, lead(grad_x),
            lead(g_w_in), lead(g_conv), lead(g_w_out), g_gain, g_bias,
            lead(d_w_in), lead(d_conv), lead(d_w_out), d_gain, d_bias,
            lead(nm_w_in), lead(nm_conv), lead(nm_w_out), nm_gain, nm_bias,
            lead(nv_w_in), lead(nv_conv), lead(nv_w_out), nv_gain, nv_bias)
```

```python
import jax
import jax.numpy as jnp
from jax import lax
from jax.experimental import pallas as pl
from jax.experimental.pallas import tpu as pltpu

F32 = jnp.float32
BF16 = jnp.bfloat16
MESH = pl.DeviceIdType.MESH

D_MODEL = 1024
CONV_WIDTH = 512
ATTN_WIDTH = 512
HEAD_DIM = 64
N_GROUPS = 8
N_CHIPS = 4
N_DEVICES = 8
LN_EPS = 1e-5
DEEPNORM_ALPHA = 2.0 ** 0.25
Q_SCALE = HEAD_DIM ** -0.5
ADAM_LR = 0.001
ADAM_B1 = 0.9
ADAM_B2 = 0.999
ADAM_EPS = 1e-08
ADAM_WD = 0.01
ADAM_STEP = 10

LANES = 128
SUBLANES = 8
V7X_VMEM_BYTES = 64 * 1024 * 1024
MIB = 1024 * 1024

TQ = 256
TK = 256
HEADS_PER_STEP = LANES // HEAD_DIM
CONV_ROWS = 512


def _compiler_params(vmem_mib, semantics=None):
    assert vmem_mib * MIB < V7X_VMEM_BYTES
    return pltpu.CompilerParams(dimension_semantics=semantics, vmem_limit_bytes=vmem_mib * MIB)


def _sigmoid(z):
    return 1.0 / (1.0 + jnp.exp(-z))


def _dot(a, b):
    return jnp.dot(a, b, preferred_element_type=F32)


def _dot_nt(a, b):
    return lax.dot_general(a, b, (((1,), (1,)), ((), ())), preferred_element_type=F32)


def _dot_tn(a, b):
    return lax.dot_general(a, b, (((0,), (0,)), ((), ())), preferred_element_type=F32)


def _truncate_to_bf16(a):
    bits = lax.bitcast_convert_type(a, jnp.uint32) & jnp.uint32(0xFFFF0000)
    return lax.bitcast_convert_type(bits, F32)


def _split(a):
    hi = _truncate_to_bf16(a)
    return [hi.astype(BF16), (a - hi).astype(BF16)]


def _block_sums(parts, tri):
    n = len(parts) // 2
    res = _dot(jnp.concatenate(parts, axis=0), tri).reshape(n, 2, TQ, TK)
    return [res[i, 0] + res[i, 1] for i in range(n)]


MASKED_LOGIT = -1e30
LOG2_E = 1.4426950408889634


def _softplus2(z2):
    return jnp.maximum(z2, 0.0) + jnp.log2(1.0 + jnp.exp2(-jnp.abs(z2)))


def _gather_weights(w_in, w_out, conv_w):
    d_rows, d_cols = w_in.shape
    o_rows, o_cols = w_out.shape
    half_rows = (d_rows // 2, o_rows // 2)
    row_chunk = 128

    def body(win_ref, wout_ref, cw_ref, wing_ref, woutg_ref, cwg_ref, send_sems, recv_sems, pass_send, pass_recv):
        x, y, c = lax.axis_index("x"), lax.axis_index("y"), lax.axis_index("c")
        me = 2 * x + y
        srcs = (win_ref, wout_ref)
        bufs = (wing_ref, woutg_ref)

        def half(a, shard, which):
            rows = half_rows[a]
            return bufs[a].at[shard, pl.ds(pl.multiple_of(which * rows, rows), rows), :]

        def cast_half(which):
            for a in range(2):
                def cast(r, carry):
                    rows = pl.ds(pl.multiple_of(which * half_rows[a] + r * row_chunk, row_chunk), row_chunk)
                    bufs[a][me, rows, :] = srcs[a][rows, :].astype(BF16)
                    return carry

                lax.fori_loop(0, half_rows[a] // row_chunk, cast, 0)

        chips = [(1 - x, y), (x, 1 - y), (1 - x, 1 - y)]

        def ici_copy(k, a, shard, to):
            ref = cwg_ref.at[shard] if a == 2 else half(a, shard, c)
            return pltpu.make_async_remote_copy(
                src_ref=ref, dst_ref=ref, send_sem=send_sems.at[3 * k + a], recv_sem=recv_sems.at[3 * k + a],
                device_id=to, device_id_type=MESH)

        def pass_copy(k, a, shard, which):
            ref = half(a, shard, which)
            return pltpu.make_async_remote_copy(
                src_ref=ref, dst_ref=ref, send_sem=pass_send.at[2 * k + a], recv_sem=pass_recv.at[2 * k + a],
                device_id=(x, y, 1 - c), device_id_type=MESH)

        cast_half(c)
        cwg_ref[me] = cw_ref[...]
        sends = []
        for k, (px, py) in enumerate(chips):
            for a in range(3):
                cp = ici_copy(k, a, me, (px, py, c))
                cp.start()
                sends.append(cp)
        cast_half(1 - c)
        for k, (px, py) in enumerate(chips):
            for a in range(2):
                ici_copy(k, a, 2 * px + py, (px, py, c)).wait_recv()
                cp = pass_copy(k, a, 2 * px + py, c)
                cp.start()
                sends.append(cp)
        for k, (px, py) in enumerate(chips):
            ici_copy(k, 2, 2 * px + py, (px, py, c)).wait_recv()
            for a in range(2):
                pass_copy(k, a, 2 * px + py, 1 - c).wait_recv()
        for cp in sends:
            cp.wait_send()

    vmem = pl.BlockSpec(memory_space=pltpu.VMEM)
    return pl.pallas_call(
        body, name="gather_weights",
        out_shape=(jax.ShapeDtypeStruct((N_CHIPS, d_rows, d_cols), BF16),
                   jax.ShapeDtypeStruct((N_CHIPS, o_rows, o_cols), BF16),
                   jax.ShapeDtypeStruct((N_CHIPS,) + conv_w.shape, F32)),
        in_specs=[vmem, vmem, vmem], out_specs=(vmem, vmem, vmem),
        scratch_shapes=[pltpu.SemaphoreType.DMA((9,)), pltpu.SemaphoreType.DMA((9,)),
                        pltpu.SemaphoreType.DMA((6,)), pltpu.SemaphoreType.DMA((6,))],
        compiler_params=_compiler_params(32),
    )(w_in, w_out, conv_w)


def _proj(x, wing):
    seq = x.shape[0]
    tm = 512

    def body(x_ref, w_ref, p_ref, xt_ref):
        xv = x_ref[...]
        xb = xv.astype(BF16)
        for j in range(N_CHIPS):
            p_ref[:, j * D_MODEL:(j + 1) * D_MODEL] = _dot(xb, w_ref[j])
        xt_ref[...] = xv.T.astype(BF16)

    return pl.pallas_call(
        body, name="proj",
        grid=(seq // tm,),
        in_specs=[pl.BlockSpec((tm, D_MODEL), lambda i: (i, 0)),
                  pl.BlockSpec((N_CHIPS, D_MODEL, D_MODEL), lambda i: (0, 0, 0), pipeline_mode=pl.Buffered(1))],
        out_specs=(pl.BlockSpec((tm, N_CHIPS * D_MODEL), lambda i: (i, 0)),
                   pl.BlockSpec((D_MODEL, tm), lambda i: (0, i))),
        out_shape=(jax.ShapeDtypeStruct((seq, N_CHIPS * D_MODEL), F32),
                   jax.ShapeDtypeStruct((D_MODEL, seq), BF16)),
        compiler_params=_compiler_params(48, ("arbitrary",)),
    )(x, wing)


def _col_block(group, n_sub=CONV_WIDTH // LANES):
    return lambda j: (0, group * n_sub + j)


def _shift_down(ext, k, rows):
    return pltpu.roll(ext, k, 0)[SUBLANES:, :]


def _shift_up(ext, k, rows):
    return pltpu.roll(ext, rows + SUBLANES - k, 0)[:rows, :]


def _conv_fwd(proj, conv_w):
    seq = proj.shape[0]
    rows = CONV_ROWS
    n_chunks = seq // rows

    def body(b_ref, c_ref, h_ref, z_ref, w_ref, out_ref, u_s):
        u_s[0:SUBLANES, :] = jnp.zeros((SUBLANES, LANES), F32)

        def fill(r, carry):
            rs = pl.ds(pl.multiple_of(r * rows, rows), rows)
            u_s[pl.ds(pl.multiple_of(r * rows + SUBLANES, SUBLANES), rows), :] = c_ref[rs, :] * h_ref[rs, :]
            return carry

        lax.fori_loop(0, n_chunks, fill, 0)
        w = w_ref[...]

        def chunk(r, carry):
            r0 = pl.multiple_of(r * rows, rows)
            rs = pl.ds(r0, rows)
            ext = u_s[pl.ds(r0, rows + SUBLANES), :]
            u = ext[SUBLANES:, :]
            y = w[2:3, :] * u
            y = y + w[0:1, :] * _shift_down(ext, 2, rows)
            y = y + w[1:2, :] * _shift_down(ext, 1, rows)
            z = z_ref[rs, :]
            out_ref[rs, :] = ((z * _sigmoid(z)) * (b_ref[rs, :] * y)).astype(BF16)
            return carry

        lax.fori_loop(0, n_chunks, chunk, 0)

    col = lambda g: pl.BlockSpec((seq, LANES), _col_block(g))
    return pl.pallas_call(
        body, name="conv_fwd",
        grid=(CONV_WIDTH // LANES,),
        in_specs=[col(0), col(1), col(2), col(3), pl.BlockSpec((3, LANES), lambda j: (0, j))],
        out_specs=pl.BlockSpec((seq, LANES), lambda j: (0, j)),
        out_shape=jax.ShapeDtypeStruct((seq, CONV_WIDTH), BF16),
        scratch_shapes=[pltpu.VMEM((seq + SUBLANES, LANES), F32)],
        compiler_params=_compiler_params(40, ("arbitrary",)),
    )(proj, proj, proj, proj, conv_w)


def _conv_bwd(proj, dmix, conv_w):
    seq = proj.shape[0]
    rows = CONV_ROWS
    n_chunks = seq // rows

    def body(b_ref, c_ref, h_ref, z_ref, d_ref, w_ref, dp_ref, dw_ref, u_s, dy_s):
        u_s[0:SUBLANES, :] = jnp.zeros((SUBLANES, LANES), F32)
        dy_s[seq:seq + SUBLANES, :] = jnp.zeros((SUBLANES, LANES), F32)

        def fill(r, carry):
            r0 = pl.multiple_of(r * rows, rows)
            rs = pl.ds(r0, rows)
            u_s[pl.ds(pl.multiple_of(r0 + SUBLANES, SUBLANES), rows), :] = c_ref[rs, :] * h_ref[rs, :]
            z = z_ref[rs, :]
            dy_s[rs, :] = d_ref[rs, :] * (z * _sigmoid(z)) * b_ref[rs, :]
            return carry

        lax.fori_loop(0, n_chunks, fill, 0)
        w = w_ref[...]

        def chunk(r, acc):
            r0 = pl.multiple_of(r * rows, rows)
            rs = pl.ds(r0, rows)
            ext = u_s[pl.ds(r0, rows + SUBLANES), :]
            u = ext[SUBLANES:, :]
            um1 = _shift_down(ext, 1, rows)
            um2 = _shift_down(ext, 2, rows)
            y = w[2:3, :] * u
            y = y + w[0:1, :] * um2
            y = y + w[1:2, :] * um1
            z = z_ref[rs, :]
            b = b_ref[rs, :]
            dco = d_ref[rs, :]
            sg = _sigmoid(z)
            g = z * sg
            dp_ref[0, rs, :] = (dco * g * y).astype(BF16)
            dp_ref[3, rs, :] = (dco * b * y * (sg * (1.0 + z * (1.0 - sg)))).astype(BF16)
            ext_dy = dy_s[pl.ds(r0, rows + SUBLANES), :]
            dy = ext_dy[:rows, :]
            du = w[2:3, :] * dy + w[1:2, :] * _shift_up(ext_dy, 1, rows) + w[0:1, :] * _shift_up(ext_dy, 2, rows)
            dp_ref[1, rs, :] = (du * h_ref[rs, :]).astype(BF16)
            dp_ref[2, rs, :] = (du * c_ref[rs, :]).astype(BF16)
            a0, a1, a2 = acc
            return (a0 + jnp.sum(dy * um2, axis=0, keepdims=True),
                    a1 + jnp.sum(dy * um1, axis=0, keepdims=True),
                    a2 + jnp.sum(dy * u, axis=0, keepdims=True))

        zero = jnp.zeros((1, LANES), F32)
        a0, a1, a2 = lax.fori_loop(0, n_chunks, chunk, (zero, zero, zero))
        dw_ref[...] = jnp.concatenate([a0, a1, a2, jnp.zeros((SUBLANES - 3, LANES), F32)], axis=0)

    col = lambda g: pl.BlockSpec((seq, LANES), _col_block(g))
    return pl.pallas_call(
        body, name="conv_bwd",
        grid=(CONV_WIDTH // LANES,),
        in_specs=[col(0), col(1), col(2), col(3), col(0), pl.BlockSpec((3, LANES), lambda j: (0, j))],
        out_specs=(pl.BlockSpec((4, seq, LANES), lambda j: (0, 0, j)),
                   pl.BlockSpec((SUBLANES, LANES), lambda j: (0, j))),
        out_shape=(jax.ShapeDtypeStruct((4, seq, CONV_WIDTH), BF16),
                   jax.ShapeDtypeStruct((SUBLANES, CONV_WIDTH), F32)),
        scratch_shapes=[pltpu.VMEM((seq + SUBLANES, LANES), F32), pltpu.VMEM((seq + SUBLANES, LANES), F32)],
        compiler_params=_compiler_params(48, ("arbitrary",)),
    )(proj, proj, proj, proj, dmix, conv_w)


def _tri_masks():
    r = lax.broadcasted_iota(jnp.int32, (TK, TK), 0)
    s = lax.broadcasted_iota(jnp.int32, (TK, TK), 1)
    return r, s


def _head_mask():
    lane = lax.broadcasted_iota(jnp.int32, (1, LANES), 1)
    return lane < HEAD_DIM


def _fill_tri(ref, ones):
    ref[...] = jnp.where(ones, 1.0, 0.0).astype(BF16)


def _tile_lanes(a):
    return jnp.tile(a, (1, TK // LANES))


DEAD_CARRY_BITS = 160.0


def _attn_fwd(proj):
    seq = proj.shape[0]
    n_q = seq // TQ
    n_kb = seq // TK
    assert TQ == TK and n_kb <= LANES
    heads = range(HEADS_PER_STEP)

    def body(q_ref, k_ref, v_ref, za_ref, o_ref, mix_ref, car_ref, qq_s, k_s, vt_s, suffix_s, carry_s, diff_s, ot_s):
        head_a = _head_mask()
        r_i, s_i = _tri_masks()
        diff_s[...] = r_i - s_i
        _fill_tri(suffix_s, r_i > s_i)
        lane = lax.broadcasted_iota(jnp.int32, (1, LANES), 1)

        def prep(r, carry):
            rs = pl.ds(pl.multiple_of(r * TQ, TQ), TQ)
            q = q_ref[rs, :] * Q_SCALE
            qq_s[r, 0] = jnp.where(head_a, q, 0.0).astype(BF16)
            qq_s[r, 1] = jnp.where(head_a, 0.0, q).astype(BF16)
            k_s[rs, :] = k_ref[rs, :].astype(BF16)
            vt_s[r] = v_ref[rs, :].T.astype(BF16)
            return carry

        lax.fori_loop(0, n_q, prep, 0)

        def block_pair(j, carry):
            qbs = [2 * j, 2 * j + 1]
            rows = [pl.ds(pl.multiple_of(qb * TQ, TQ), TQ) for qb in qbs]
            both = [qq_s[qb].reshape(HEADS_PER_STEP * TQ, LANES) for qb in qbs]
            carry_s[...] = jnp.zeros(carry_s.shape, F32)
            ot_s[...] = jnp.zeros(ot_s.shape, F32)
            for rs in rows:
                for h in heads:
                    car_ref[h, rs, :] = jnp.zeros((TQ, LANES), F32)

            def tiles(items):
                kbs = [qbs[s] - n for s, n in items]
                cols = [pl.ds(pl.multiple_of(jnp.maximum(kb, 0) * TK, TK), TK) for kb in kbs]
                zs = [_dot_nt(both[s], k_s[cs, :]).reshape(HEADS_PER_STEP, TQ, TK) for (s, _), cs in zip(items, cols)]
                zms, cars, sufs = [], [], []
                for (s, n), kb, z in zip(items, kbs, zs):
                    mask = diff_s[...] > jnp.where(kb < 0, TK, jnp.where(n == 0, 0, -TK))
                    parts = []
                    for h in heads:
                        z2 = jnp.where(mask, z[h] * LOG2_E, MASKED_LOGIT)
                        sp = _softplus2(z2)
                        zms.append(z2 - sp)
                        parts += _split(sp)
                        car = carry_s[s, h]
                        cars.append(car)
                        car_ref[h, rows[s], :] = jnp.where(lane == kb, car, car_ref[h, rows[s], :])
                        carry_s[s, h] = car + jnp.sum(sp, axis=1, keepdims=True)
                    sufs += _block_sums(parts, suffix_s[...])
                for i, ((s, _), kb) in enumerate(zip(items, kbs)):
                    out = []
                    for h in heads:
                        j2 = i * HEADS_PER_STEP + h
                        w = jnp.exp2(zms[j2] - (sufs[j2] + _tile_lanes(cars[j2]))).astype(BF16)
                        out.append(_dot_nt(vt_s[jnp.maximum(kb, 0), h * HEAD_DIM:(h + 1) * HEAD_DIM, :], w))
                    ot_s[s] += jnp.concatenate(out, axis=0)

            tiles([(0, 0), (0, 1), (1, 0), (1, 1)])

            for s, (qb, rs) in enumerate(zip(qbs, rows)):
                def lowest_carry():
                    return jnp.min(jnp.minimum(carry_s[s, 0], carry_s[s, 1]))

                def live(state):
                    n, lowest = state
                    return jnp.logical_and(n <= qb, lowest < DEAD_CARRY_BITS)

                def step(state):
                    n, _ = state
                    tiles([(s, n)])
                    return n + 1, lowest_carry()

                n_done, _ = lax.while_loop(live, step, (jnp.int32(2), lowest_carry()))
                for h in heads:
                    car_ref[h, rs, :] = jnp.where(lane < qb + 1 - n_done, carry_s[s, h], car_ref[h, rs, :])
                o = ot_s[s].T
                o_ref[rs, :] = o
                za = za_ref[rs, :]
                mix_ref[rs, :] = ((za * _sigmoid(za)) * o).astype(BF16)
            return carry

        assert n_q % 2 == 0
        lax.fori_loop(0, n_q // 2, block_pair, 0)

    col = lambda g: pl.BlockSpec((seq, LANES), _col_block(g))
    n_pairs = ATTN_WIDTH // LANES
    return pl.pallas_call(
        body, name="attn_fwd",
        grid=(n_pairs,),
        in_specs=[col(4), col(5), col(6), col(7)],
        out_specs=(pl.BlockSpec((seq, LANES), lambda p: (0, p)),
                   pl.BlockSpec((seq, LANES), lambda p: (0, p)),
                   pl.BlockSpec((HEADS_PER_STEP, seq, LANES), lambda p: (p, 0, 0))),
        out_shape=(jax.ShapeDtypeStruct((seq, ATTN_WIDTH), F32),
                   jax.ShapeDtypeStruct((seq, ATTN_WIDTH), BF16),
                   jax.ShapeDtypeStruct((n_pairs * HEADS_PER_STEP, seq, LANES), F32)),
        scratch_shapes=[pltpu.VMEM((n_q, HEADS_PER_STEP, TQ, LANES), BF16),
                        pltpu.VMEM((seq, LANES), BF16),
                        pltpu.VMEM((n_kb, LANES, TK), BF16),
                        pltpu.VMEM((TK, TK), BF16),
                        pltpu.VMEM((2, HEADS_PER_STEP, TQ, LANES), F32),
                        pltpu.VMEM((TQ, TK), jnp.int32),
                        pltpu.VMEM((2, LANES, TQ), F32)],
        compiler_params=_compiler_params(48, ("arbitrary",)),
    )(proj, proj, proj, proj)


def _attn_bwd(proj, o, dmix, carries):
    seq = proj.shape[0]
    n_q = seq // TQ
    n_kb = seq // TK
    assert TQ == TK and n_kb <= LANES
    heads = range(HEADS_PER_STEP)

    def body(q_ref, k_ref, v_ref, za_ref, o_ref, dm_ref, car_ref, dp_ref,
             qq_s, dd_s, k_s, v_s, qt_s, dot_s, kt_s, suffix_s, prefix_s, ccar_s, dq_s, dk_s, dv_s, diff_s):
        head_a = _head_mask()
        r_i, s_i = _tri_masks()
        diff_s[...] = r_i - s_i
        _fill_tri(suffix_s, r_i > s_i)
        _fill_tri(prefix_s, r_i < s_i)
        lane = lax.broadcasted_iota(jnp.int32, (1, LANES), 1)

        def prep(r, carry):
            rs = pl.ds(pl.multiple_of(r * TQ, TQ), TQ)
            q = q_ref[rs, :] * Q_SCALE
            qq_s[r, 0] = jnp.where(head_a, q, 0.0).astype(BF16)
            qq_s[r, 1] = jnp.where(head_a, 0.0, q).astype(BF16)
            qt_s[r] = q.T.astype(BF16)
            k = k_ref[rs, :]
            k_s[rs, :] = k.astype(BF16)
            kt_s[r] = k.T.astype(BF16)
            v_s[rs, :] = v_ref[rs, :].astype(BF16)
            za = za_ref[rs, :]
            sg = _sigmoid(za)
            dm = dm_ref[rs, :]
            do = dm * (za * sg)
            dd_s[r, 0] = jnp.where(head_a, do, 0.0).astype(BF16)
            dd_s[r, 1] = jnp.where(head_a, 0.0, do).astype(BF16)
            dot_s[r] = do.T.astype(BF16)
            dp_ref[3, rs, :] = (dm * o_ref[rs, :] * (sg * (1.0 + za * (1.0 - sg)))).astype(BF16)
            dq_s[:, rs] = jnp.zeros((LANES, TQ), F32)
            dk_s[:, rs] = jnp.zeros((LANES, TQ), F32)
            dv_s[:, rs] = jnp.zeros((LANES, TQ), F32)
            return carry

        lax.fori_loop(0, n_q, prep, 0)

        def block_pair(j, carry):
            qbs = [2 * j, 2 * j + 1]
            rows = [pl.ds(pl.multiple_of(qb * TQ, TQ), TQ) for qb in qbs]
            both_q = [qq_s[qb].reshape(HEADS_PER_STEP * TQ, LANES) for qb in qbs]
            both_do = [dd_s[qb].reshape(HEADS_PER_STEP * TQ, LANES) for qb in qbs]
            ccar_s[...] = jnp.zeros(ccar_s.shape, F32)

            def tiles(items):
                cols = [pl.ds(pl.multiple_of(jnp.maximum(kb, 0) * TK, TK), TK) for _, kb in items]
                zs = [_dot_nt(both_q[s], k_s[cs, :]).reshape(HEADS_PER_STEP, TQ, TK) for (s, _), cs in zip(items, cols)]
                dws = [_dot_nt(both_do[s], v_s[cs, :]).reshape(HEADS_PER_STEP, TQ, TK)
                       for (s, _), cs in zip(items, cols)]
                zms, sufs = [], []
                for (s, kb), z in zip(items, zs):
                    mask = diff_s[...] > jnp.where(kb < 0, TK, jnp.where(kb == qbs[s], 0, -TK))
                    parts = []
                    for h in heads:
                        z2 = jnp.where(mask, z[h] * LOG2_E, MASKED_LOGIT)
                        sp = _softplus2(z2)
                        zms.append(z2 - sp)
                        parts += _split(sp)
                    sufs += _block_sums(parts, suffix_s[...])
                ws, dlws, pres = [], [], []
                for i, (s, kb) in enumerate(items):
                    parts = []
                    for h in heads:
                        j2 = i * HEADS_PER_STEP + h
                        cin = jnp.sum(jnp.where(lane == kb, car_ref[h, rows[s], :], 0.0), axis=1, keepdims=True)
                        w = jnp.exp2(zms[j2] - (sufs[j2] + cin))
                        ws.append(w.astype(BF16))
                        dlw = dws[i][h] * w
                        dlws.append(dlw)
                        parts += _split(dlw)
                    pres += _block_sums(parts, prefix_s[...])
                for i, ((s, kb), cs) in enumerate(zip(items, cols)):
                    dq_t, dk_t, dv_t = [], [], []
                    for h in heads:
                        j2 = i * HEADS_PER_STEP + h
                        ccar = ccar_s[s, h]
                        dz = (dlws[j2] - jnp.exp2(zms[j2]) * (dlws[j2] + (pres[j2] + _tile_lanes(ccar)))).astype(BF16)
                        ccar_s[s, h] = ccar + jnp.sum(dlws[j2], axis=1, keepdims=True)
                        dims = slice(h * HEAD_DIM, (h + 1) * HEAD_DIM)
                        dq_t.append(_dot_nt(kt_s[jnp.maximum(kb, 0), dims, :], dz))
                        dk_t.append(_dot(qt_s[qbs[s], dims, :], dz))
                        dv_t.append(_dot(dot_s[qbs[s], dims, :], ws[j2]))
                    dq_s[:, rows[s]] += jnp.concatenate(dq_t, axis=0)
                    dk_s[:, cs] += jnp.concatenate(dk_t, axis=0)
                    dv_s[:, cs] += jnp.concatenate(dv_t, axis=0)

            for s, (qb, rs) in enumerate(zip(qbs, rows)):
                lowest = jnp.min(jnp.minimum(car_ref[0, rs, :], car_ref[1, rs, :]), axis=0, keepdims=True)
                dead = jnp.logical_and(lowest >= DEAD_CARRY_BITS, lane < qb)
                first_live = jnp.sum(jnp.where(dead, 1, 0))

                def one(kb, c2):
                    tiles([(s, kb)])
                    return c2

                lax.fori_loop(first_live, qb - 1, one, 0)

            for s, qb in enumerate(qbs):
                tiles([(s, qb - 1), (s, qb)])
            return carry

        assert n_q % 2 == 0
        lax.fori_loop(0, n_q // 2, block_pair, 0)

        def finish(r, carry):
            rs = pl.ds(pl.multiple_of(r * TQ, TQ), TQ)
            dp_ref[0, rs, :] = (dq_s[:, rs].T * Q_SCALE).astype(BF16)
            dp_ref[1, rs, :] = dk_s[:, rs].T.astype(BF16)
            dp_ref[2, rs, :] = dv_s[:, rs].T.astype(BF16)
            return carry

        lax.fori_loop(0, n_q, finish, 0)

    def col(g, n_sub=CONV_WIDTH // LANES):
        return pl.BlockSpec((seq, LANES), _col_block(g, n_sub), pipeline_mode=pl.Buffered(1))

    n_pairs = ATTN_WIDTH // LANES
    bf = lambda: pltpu.VMEM((seq, LANES), BF16)
    by_head = lambda: pltpu.VMEM((n_q, HEADS_PER_STEP, TQ, LANES), BF16)
    transposed = lambda: pltpu.VMEM((n_q, LANES, TQ), BF16)
    acc_t = lambda: pltpu.VMEM((LANES, seq), F32)
    return pl.pallas_call(
        body, name="attn_bwd",
        grid=(n_pairs,),
        in_specs=[col(4), col(5), col(6), col(7), col(0), col(1),
                  pl.BlockSpec((HEADS_PER_STEP, seq, LANES), lambda p: (p, 0, 0), pipeline_mode=pl.Buffered(1))],
        out_specs=pl.BlockSpec((4, seq, LANES), lambda p: (0, 0, p)),
        out_shape=jax.ShapeDtypeStruct((4, seq, ATTN_WIDTH), BF16),
        scratch_shapes=[by_head(), by_head(), bf(), bf(), transposed(), transposed(), transposed(),
                        pltpu.VMEM((TK, TK), BF16), pltpu.VMEM((TK, TK), BF16),
                        pltpu.VMEM((2, HEADS_PER_STEP, TQ, LANES), F32),
                        acc_t(), acc_t(), acc_t(),
                        pltpu.VMEM((TQ, TK), jnp.int32)],
        compiler_params=_compiler_params(56, ("arbitrary",)),
    )(proj, proj, proj, proj, o, dmix, carries)


def _out_ln(mix_c, mix_a, woutg, x, target, gain, bias):
    seq = x.shape[0]
    tm = 512
    inv_d = 1.0 / D_MODEL

    def body(mc_ref, ma_ref, wo_ref, x_ref, t_ref, g_ref, b_ref, dxa_ref, dmix_ref, gw_ref, st_ref):
        @pl.when(pl.program_id(0) == 0)
        def _():
            gw_ref[...] = jnp.zeros(gw_ref.shape, F32)
            st_ref[...] = jnp.zeros(st_ref.shape, F32)

        mc = mc_ref[...]
        ma = ma_ref[...]
        sub = _dot(mc, wo_ref[0:CONV_WIDTH, :]) + _dot(ma, wo_ref[CONV_WIDTH:, :])
        r = DEEPNORM_ALPHA * x_ref[...] + sub
        mu = jnp.mean(r, axis=-1, keepdims=True)
        rc = r - mu
        var = jnp.mean(rc * rc, axis=-1, keepdims=True)
        rstd = lax.rsqrt(var + LN_EPS)
        xhat = rc * rstd
        gain_v = g_ref[...]
        diff = (xhat * gain_v + b_ref[...]) - t_ref[...]
        dy = diff * inv_d
        st_ref[0:1, :] += jnp.sum(dy * xhat, axis=0, keepdims=True)
        st_ref[1:2, :] += jnp.sum(dy, axis=0, keepdims=True)
        st_ref[2:3, :] += jnp.sum(diff * diff, axis=0, keepdims=True)
        dxh = dy * gain_v
        m1 = jnp.mean(dxh, axis=-1, keepdims=True)
        m2 = jnp.mean(dxh * xhat, axis=-1, keepdims=True)
        dr = rstd * (dxh - m1 - xhat * m2)
        dxa_ref[...] = DEEPNORM_ALPHA * dr
        drb = dr.astype(BF16)
        dmix_ref[...] = _dot_nt(drb, wo_ref[...])
        gw_ref[0:CONV_WIDTH, :] += _dot_tn(mc, drb)
        gw_ref[CONV_WIDTH:, :] += _dot_tn(ma, drb)

    rows_d = lambda: pl.BlockSpec((tm, D_MODEL), lambda i: (i, 0))
    rows_h = lambda: pl.BlockSpec((tm, CONV_WIDTH), lambda i: (i, 0))
    whole = lambda shape: pl.BlockSpec(shape, lambda i: (0, 0))
    return pl.pallas_call(
        body, name="out_ln",
        grid=(seq // tm,),
        in_specs=[rows_h(), rows_h(), whole((D_MODEL, D_MODEL)), rows_d(), rows_d(),
                  whole((1, D_MODEL)), whole((1, D_MODEL))],
        out_specs=(rows_d(), rows_d(), whole((D_MODEL, D_MODEL)), whole((SUBLANES, D_MODEL))),
        out_shape=(jax.ShapeDtypeStruct((seq, D_MODEL), F32), jax.ShapeDtypeStruct((seq, D_MODEL), F32),
                   jax.ShapeDtypeStruct((D_MODEL, D_MODEL), F32), jax.ShapeDtypeStruct((SUBLANES, D_MODEL), F32)),
        compiler_params=_compiler_params(48, ("arbitrary",)),
    )(mix_c, mix_a, woutg, x, target, gain, bias)


def _group_maps():
    half = N_GROUPS // 2
    return (lambda g: jnp.minimum(g, half - 1)), (lambda g: jnp.maximum(g - half, 0))


def _grad_w_in(xt, dpc, dpa):
    seq = xt.shape[1]
    half = N_GROUPS // 2
    in_conv, in_attn = _group_maps()

    def body(xt_ref, dc_ref, da_ref, out_ref):
        g = pl.program_id(0)

        @pl.when(g < half)
        def _():
            out_ref[...] = _dot(xt_ref[...], dc_ref[...])

        @pl.when(g >= half)
        def _():
            out_ref[...] = _dot(xt_ref[...], da_ref[...])

    return pl.pallas_call(
        body, name="grad_w_in",
        grid=(N_GROUPS,),
        in_specs=[pl.BlockSpec((D_MODEL, seq), lambda g: (0, 0), pipeline_mode=pl.Buffered(1)),
                  pl.BlockSpec((None, seq, CONV_WIDTH), lambda g: (in_conv(g), 0, 0)),
                  pl.BlockSpec((None, seq, CONV_WIDTH), lambda g: (in_attn(g), 0, 0))],
        out_specs=pl.BlockSpec((None, D_MODEL, CONV_WIDTH), lambda g: (g // 2, 0, g % 2)),
        out_shape=jax.ShapeDtypeStruct((N_CHIPS, D_MODEL, D_MODEL), F32),
        compiler_params=_compiler_params(48, ("arbitrary",)),
    )(xt, dpc, dpa)


def _grad_x(dxa, dpc, dpa, wing):
    seq = dxa.shape[0]
    tm = 512
    half = N_GROUPS // 2

    def body(dxa_ref, dc_ref, da_ref, w_ref, out_ref):
        acc = dxa_ref[...]
        for g in range(N_GROUPS):
            dp = dc_ref[g] if g < half else da_ref[g - half]
            cols = slice((g % 2) * CONV_WIDTH, (g % 2 + 1) * CONV_WIDTH)
            acc = acc + _dot_nt(dp, w_ref[g // 2, :, cols])
        out_ref[...] = acc

    return pl.pallas_call(
        body, name="grad_x",
        grid=(seq // tm,),
        in_specs=[pl.BlockSpec((tm, D_MODEL), lambda i: (i, 0)),
                  pl.BlockSpec((half, tm, CONV_WIDTH), lambda i: (0, i, 0)),
                  pl.BlockSpec((half, tm, CONV_WIDTH), lambda i: (0, i, 0)),
                  pl.BlockSpec((N_CHIPS, D_MODEL, D_MODEL), lambda i: (0, 0, 0), pipeline_mode=pl.Buffered(1))],
        out_specs=pl.BlockSpec((tm, D_MODEL), lambda i: (i, 0)),
        out_shape=jax.ShapeDtypeStruct((seq, D_MODEL), F32),
        compiler_params=_compiler_params(40, ("arbitrary",)),
    )(dxa, dpc, dpa, wing)


PACK_LOSS_ROW = 5


def _reduce_grads(gwin, gwout, pack):
    n_shards, n_rows_in, n_cols = gwin.shape
    n_rows_out = gwout.shape[1]
    half_rows = (n_rows_in // 2, n_rows_out // 2)
    row_chunk = 128

    def body(gwin_ref, gwout_ref, pack_ref, oin_ref, oout_ref, opack_ref,
             mine_in, mine_out, sib_in, sib_out, wire_in, wire_out, rin, rout, rpack,
             local_sems, sib_send, sib_recv, send_sems, recv_sems, pack_send, pack_recv, fin_send, fin_recv):
        x, y, c = lax.axis_index("x"), lax.axis_index("y"), lax.axis_index("c")
        me = 2 * x + y
        my_id = 4 * x + 2 * y + c
        chips = [(1 - x, y), (x, 1 - y), (1 - x, 1 - y)]
        sibling = (x, y, 1 - c)
        partial = (gwin_ref, gwout_ref)
        mine = (mine_in, mine_out)
        from_sib = (sib_in, sib_out)
        wire = (wire_in, wire_out)
        from_chips = (rin, rout)
        result = (oin_ref, oout_ref)

        def half(a, which):
            rows = half_rows[a]
            return pl.ds(pl.multiple_of(which * rows, rows), rows)

        sends = []
        loads = []
        for a in range(2):
            ld = pltpu.make_async_copy(partial[a].at[:, half(a, c), :], mine[a], local_sems.at[a])
            ld.start()
            loads.append(ld)
            cp = pltpu.make_async_remote_copy(
                src_ref=partial[a].at[:, half(a, 1 - c), :], dst_ref=from_sib[a],
                send_sem=sib_send.at[a], recv_sem=sib_recv.at[a], device_id=sibling, device_id_type=MESH)
            cp.start()
            sends.append(cp)

        rpack[my_id] = pack_ref[...]

        def pack_copy(rel, slot, to):
            return pltpu.make_async_remote_copy(
                src_ref=rpack.at[slot], dst_ref=rpack.at[slot],
                send_sem=pack_send.at[rel - 1], recv_sem=pack_recv.at[rel - 1],
                device_id=to, device_id_type=MESH)

        def related(rel):
            px = (1 - x) if rel & 4 else x
            py = (1 - y) if rel & 2 else y
            pc = (1 - c) if rel & 1 else c
            return px, py, pc

        for rel in range(1, N_DEVICES):
            cp = pack_copy(rel, my_id, related(rel))
            cp.start()
            sends.append(cp)

        for ld in loads:
            ld.wait()
        for a in range(2):
            pltpu.make_async_remote_copy(
                src_ref=partial[a].at[:, half(a, 1 - c), :], dst_ref=from_sib[a],
                send_sem=sib_send.at[a], recv_sem=sib_recv.at[a], device_id=sibling, device_id_type=MESH).wait_recv()

        def chip_sum(a, shard):
            def add(r, carry):
                rs = pl.ds(pl.multiple_of(r * row_chunk, row_chunk), row_chunk)
                mine[a][shard, rs, :] = mine[a][shard, rs, :] + from_sib[a][shard, rs, :]
                return carry

            lax.fori_loop(0, half_rows[a] // row_chunk, add, 0)

        def chip_sum_to_wire(a, shard, k):
            def add(r, carry):
                rs = pl.ds(pl.multiple_of(r * row_chunk, row_chunk), row_chunk)
                wire[a][k, rs, :] = (mine[a][shard, rs, :] + from_sib[a][shard, rs, :]).astype(BF16)
                return carry

            lax.fori_loop(0, half_rows[a] // row_chunk, add, 0)

        def shard_copy(k, a, to):
            return pltpu.make_async_remote_copy(
                src_ref=wire[a].at[k], dst_ref=from_chips[a].at[k],
                send_sem=send_sems.at[2 * k + a], recv_sem=recv_sems.at[2 * k + a],
                device_id=to, device_id_type=MESH)

        for k, (px, py) in enumerate(chips):
            for a in range(2):
                chip_sum_to_wire(a, 2 * px + py, k)
                cp = shard_copy(k, a, (px, py, c))
                cp.start()
                sends.append(cp)
        for a in range(2):
            chip_sum(a, me)

        for rel in range(1, N_DEVICES):
            px, py, pc = related(rel)
            pack_copy(rel, 4 * px + 2 * py + pc, (px, py, pc)).wait_recv()
        total = rpack[0]
        for i in range(1, N_DEVICES):
            total = total + rpack[i]
        opack_ref[...] = total
        sq_err = jnp.sum(total[PACK_LOSS_ROW:PACK_LOSS_ROW + 1, :], axis=1, keepdims=True)
        opack_ref[PACK_LOSS_ROW:PACK_LOSS_ROW + 1, :] = jnp.broadcast_to(sq_err * (0.5 / D_MODEL), (1, n_cols))

        for k, (px, py) in enumerate(chips):
            for a in range(2):
                shard_copy(k, a, (px, py, c)).wait_recv()

        def finish(a):
            def add(r, carry):
                rs = pl.ds(pl.multiple_of(r * row_chunk, row_chunk), row_chunk)
                dst = pl.ds(pl.multiple_of(c * half_rows[a] + r * row_chunk, row_chunk), row_chunk)
                result[a][dst, :] = ((mine[a][me, rs, :] + from_chips[a][0, rs, :].astype(F32))
                                     + from_chips[a][1, rs, :].astype(F32)) + from_chips[a][2, rs, :].astype(F32)
                return carry

            lax.fori_loop(0, half_rows[a] // row_chunk, add, 0)

        def final_copy(a, which):
            ref = result[a].at[half(a, which), :]
            return pltpu.make_async_remote_copy(
                src_ref=ref, dst_ref=ref, send_sem=fin_send.at[a], recv_sem=fin_recv.at[a],
                device_id=sibling, device_id_type=MESH)

        for a in range(2):
            finish(a)
            cp = final_copy(a, c)
            cp.start()
            sends.append(cp)
        for a in range(2):
            final_copy(a, 1 - c).wait_recv()
        for cp in sends:
            cp.wait_send()

    vmem = pl.BlockSpec(memory_space=pltpu.VMEM)
    hbm = pl.BlockSpec(memory_space=pl.ANY)
    in_shape = (n_rows_in, n_cols)
    out_shape = (n_rows_out, n_cols)
    half_in = (half_rows[0], n_cols)
    half_out = (half_rows[1], n_cols)
    return pl.pallas_call(
        body, name="reduce_grads",
        out_shape=(jax.ShapeDtypeStruct(in_shape, F32), jax.ShapeDtypeStruct(out_shape, F32),
                   jax.ShapeDtypeStruct(pack.shape, F32)),
        in_specs=[hbm, hbm, vmem], out_specs=(vmem, vmem, vmem),
        scratch_shapes=[pltpu.VMEM((n_shards,) + half_in, F32), pltpu.VMEM((n_shards,) + half_out, F32),
                        pltpu.VMEM((n_shards,) + half_in, F32), pltpu.VMEM((n_shards,) + half_out, F32),
                        pltpu.VMEM((N_CHIPS - 1,) + half_in, BF16), pltpu.VMEM((N_CHIPS - 1,) + half_out, BF16),
                        pltpu.VMEM((N_CHIPS - 1,) + half_in, BF16), pltpu.VMEM((N_CHIPS - 1,) + half_out, BF16),
                        pltpu.VMEM((N_DEVICES,) + pack.shape, F32),
                        pltpu.SemaphoreType.DMA((2,)),
                        pltpu.SemaphoreType.DMA((2,)), pltpu.SemaphoreType.DMA((2,)),
                        pltpu.SemaphoreType.DMA((6,)), pltpu.SemaphoreType.DMA((6,)),
                        pltpu.SemaphoreType.DMA((N_DEVICES - 1,)), pltpu.SemaphoreType.DMA((N_DEVICES - 1,)),
                        pltpu.SemaphoreType.DMA((2,)), pltpu.SemaphoreType.DMA((2,))],
        compiler_params=_compiler_params(48),
    )(gwin, gwout, pack)


def _adamw(name, w, g, m, v):
    n_rows, n_cols = w.shape
    tr = 256 if n_rows % 256 == 0 else n_rows
    m_corr = 1.0 - ADAM_B1 ** ADAM_STEP
    v_corr = 1.0 - ADAM_B2 ** ADAM_STEP

    def body(w_ref, g_ref, m_ref, v_ref, d_ref, nm_ref, nv_ref):
        gv = g_ref[...]
        nm = ADAM_B1 * m_ref[...] + (1.0 - ADAM_B1) * gv
        nv = ADAM_B2 * v_ref[...] + (1.0 - ADAM_B2) * (gv * gv)
        nm_ref[...] = nm
        nv_ref[...] = nv
        d_ref[...] = -ADAM_LR * ((nm / m_corr) / (jnp.sqrt(nv / v_corr) + ADAM_EPS) + ADAM_WD * w_ref[...])

    blk = lambda: pl.BlockSpec((tr, n_cols), lambda i: (i, 0))
    shape = jax.ShapeDtypeStruct(w.shape, F32)
    return pl.pallas_call(
        body, name=name,
        grid=(n_rows // tr,),
        in_specs=[blk(), blk(), blk(), blk()], out_specs=(blk(), blk(), blk()),
        out_shape=(shape, shape, shape),
        compiler_params=_compiler_params(32, ("arbitrary",)),
    )(w, g, m, v)


def kernel(x, w_in, conv_w, w_out, ln_gain, ln_bias, loss_target, m_w_in, m_conv_w, m_w_out, m_ln_gain, m_ln_bias, v_w_in, v_conv_w, v_w_out, v_ln_gain, v_ln_bias):
    xs = x[0]
    target = loss_target[0]

    wing, woutg, cwg = _gather_weights(w_in[0], w_out[0], conv_w[0])
    conv_full = jnp.transpose(cwg, (1, 0, 2)).reshape(3, CONV_WIDTH)
    wout_full = woutg.reshape(D_MODEL, D_MODEL)

    proj, xt = _proj(xs, wing)
    mix_c = _conv_fwd(proj, conv_full)
    o, mix_a, carries = _attn_fwd(proj)
    dxa, dmix, gwout, stats = _out_ln(mix_c, mix_a, wout_full, xs, target, ln_gain, ln_bias)
    dpc, dconv = _conv_bwd(proj, dmix, conv_full)
    dpa = _attn_bwd(proj, o, dmix, carries)
    gwin = _grad_w_in(xt, dpc, dpa)
    grad_x = _grad_x(dxa, dpc, dpa, wing)

    pack = jnp.concatenate(
        [stats[0:2], jnp.pad(dconv[0:3], ((0, 0), (0, D_MODEL - CONV_WIDTH))), stats[2:3],
         jnp.zeros((2, D_MODEL), F32)], axis=0)
    g_w_in, g_w_out, tot = _reduce_grads(gwin, gwout.reshape(N_CHIPS, D_MODEL // N_CHIPS, D_MODEL), pack)

    chip = 2 * lax.axis_index("x") + lax.axis_index("y")
    g_gain = tot[0:1]
    g_bias = tot[1:2]
    g_conv = lax.dynamic_slice(tot, (2, chip * LANES), (3, LANES))
    loss = tot[PACK_LOSS_ROW, 0]

    d_w_in, nm_w_in, nv_w_in = _adamw("adamw_w_in", w_in[0], g_w_in, m_w_in[0], v_w_in[0])
    d_w_out, nm_w_out, nv_w_out = _adamw("adamw_w_out", w_out[0], g_w_out, m_w_out[0], v_w_out[0])
    d_conv, nm_conv, nv_conv = _adamw("adamw_conv_w", conv_w[0], g_conv, m_conv_w[0], v_conv_w[0])
    d_gain, nm_gain, nv_gain = _adamw("adamw_ln_gain", ln_gain, g_gain, m_ln_gain, v_ln_gain)
    d_bias, nm_bias, nv_bias = _adamw("adamw_ln_bias", ln_bias, g_bias, m_ln_bias, v_ln_bias)

    lead = lambda a: a[None]
    return (loss, lead(grad_x),
            lead(g_w_in), lead(g_conv), lead(g_w_out), g_gain, g_bias,
            lead(d_w_in), lead(d_conv), lead(d_w_out), d_gain, d_bias,
            lead(nm_w_in), lead(nm_conv), lead(nm_w_out), nm_gain, nm_bias,
            lead(nv_w_in), lead(nv_conv), lead(nv_w_out), nv_gain, nv_bias)
```

```python
import jax
import jax.numpy as jnp
from jax import lax
from jax.experimental import pallas as pl
from jax.experimental.pallas import tpu as pltpu

F32 = jnp.float32
BF16 = jnp.bfloat16
MESH = pl.DeviceIdType.MESH

D_MODEL = 1024
CONV_WIDTH = 512
ATTN_WIDTH = 512
HEAD_DIM = 64
N_GROUPS = 8
N_CHIPS = 4
N_DEVICES = 8
LN_EPS = 1e-5
DEEPNORM_ALPHA = 2.0 ** 0.25
Q_SCALE = HEAD_DIM ** -0.5
ADAM_LR = 0.001
ADAM_B1 = 0.9
ADAM_B2 = 0.999
ADAM_EPS = 1e-08
ADAM_WD = 0.01
ADAM_STEP = 10

LANES = 128
SUBLANES = 8
V7X_VMEM_BYTES = 64 * 1024 * 1024
MIB = 1024 * 1024

TQ = 256
TK = 256
HEADS_PER_STEP = LANES // HEAD_DIM
CONV_ROWS = 512


def _compiler_params(vmem_mib, semantics=None):
    assert vmem_mib * MIB < V7X_VMEM_BYTES
    return pltpu.CompilerParams(dimension_semantics=semantics, vmem_limit_bytes=vmem_mib * MIB)


def _sigmoid(z):
    return 1.0 / (1.0 + jnp.exp(-z))


def _dot(a, b):
    return jnp.dot(a, b, preferred_element_type=F32)


def _dot_nt(a, b):
    return lax.dot_general(a, b, (((1,), (1,)), ((), ())), preferred_element_type=F32)


def _dot_tn(a, b):
    return lax.dot_general(a, b, (((0,), (0,)), ((), ())), preferred_element_type=F32)


def _truncate_to_bf16(a):
    bits = lax.bitcast_convert_type(a, jnp.uint32) & jnp.uint32(0xFFFF0000)
    return lax.bitcast_convert_type(bits, F32)


def _split(a):
    hi = _truncate_to_bf16(a)
    return [hi.astype(BF16), (a - hi).astype(BF16)]


def _block_sums(parts, tri):
    n = len(parts) // 2
    res = _dot(jnp.concatenate(parts, axis=0), tri).reshape(n, 2, TQ, TK)
    return [res[i, 0] + res[i, 1] for i in range(n)]


MASKED_LOGIT = -1e30
LOG2_E = 1.4426950408889634


def _softplus2(z2):
    return jnp.maximum(z2, 0.0) + jnp.log2(1.0 + jnp.exp2(-jnp.abs(z2)))


def _gather_weights(w_in, w_out, conv_w):
    d_rows, d_cols = w_in.shape
    o_rows, o_cols = w_out.shape
    half_rows = (d_rows // 2, o_rows // 2)
    row_chunk = 128

    def body(win_ref, wout_ref, cw_ref, wing_ref, woutg_ref, cwg_ref, send_sems, recv_sems, pass_send, pass_recv):
        x, y, c = lax.axis_index("x"), lax.axis_index("y"), lax.axis_index("c")
        me = 2 * x + y
        srcs = (win_ref, wout_ref)
        bufs = (wing_ref, woutg_ref)

        def half(a, shard, which):
            rows = half_rows[a]
            return bufs[a].at[shard, pl.ds(pl.multiple_of(which * rows, rows), rows), :]

        def cast_half(which):
            for a in range(2):
                def cast(r, carry):
                    rows = pl.ds(pl.multiple_of(which * half_rows[a] + r * row_chunk, row_chunk), row_chunk)
                    bufs[a][me, rows, :] = srcs[a][rows, :].astype(BF16)
                    return carry

                lax.fori_loop(0, half_rows[a] // row_chunk, cast, 0)

        chips = [(1 - x, y), (x, 1 - y), (1 - x, 1 - y)]

        def ici_copy(k, a, shard, to):
            ref = cwg_ref.at[shard] if a == 2 else half(a, shard, c)
            return pltpu.make_async_remote_copy(
                src_ref=ref, dst_ref=ref, send_sem=send_sems.at[3 * k + a], recv_sem=recv_sems.at[3 * k + a],
                device_id=to, device_id_type=MESH)

        def pass_copy(k, a, shard, which):
            ref = half(a, shard, which)
            return pltpu.make_async_remote_copy(
                src_ref=ref, dst_ref=ref, send_sem=pass_send.at[2 * k + a], recv_sem=pass_recv.at[2 * k + a],
                device_id=(x, y, 1 - c), device_id_type=MESH)

        cast_half(c)
        cwg_ref[me] = cw_ref[...]
        sends = []
        for k, (px, py) in enumerate(chips):
            for a in range(3):
                cp = ici_copy(k, a, me, (px, py, c))
                cp.start()
                sends.append(cp)
        cast_half(1 - c)
        for k, (px, py) in enumerate(chips):
            for a in range(2):
                ici_copy(k, a, 2 * px + py, (px, py, c)).wait_recv()
                cp = pass_copy(k, a, 2 * px + py, c)
                cp.start()
                sends.append(cp)
        for k, (px, py) in enumerate(chips):
            ici_copy(k, 2, 2 * px + py, (px, py, c)).wait_recv()
            for a in range(2):
                pass_copy(k, a, 2 * px + py, 1 - c).wait_recv()
        for cp in sends:
            cp.wait_send()

    vmem = pl.BlockSpec(memory_space=pltpu.VMEM)
    return pl.pallas_call(
        body, name="gather_weights",
        out_shape=(jax.ShapeDtypeStruct((N_CHIPS, d_rows, d_cols), BF16),
                   jax.ShapeDtypeStruct((N_CHIPS, o_rows, o_cols), BF16),
                   jax.ShapeDtypeStruct((N_CHIPS,) + conv_w.shape, F32)),
        in_specs=[vmem, vmem, vmem], out_specs=(vmem, vmem, vmem),
        scratch_shapes=[pltpu.SemaphoreType.DMA((9,)), pltpu.SemaphoreType.DMA((9,)),
                        pltpu.SemaphoreType.DMA((6,)), pltpu.SemaphoreType.DMA((6,))],
        compiler_params=_compiler_params(32),
    )(w_in, w_out, conv_w)


def _proj(x, wing):
    seq = x.shape[0]
    tm = 512
    w = CONV_WIDTH

    def body(x_ref, w_ref, conv_ref, qkv_ref, za_ref, xt_ref):
        xv = x_ref[...]
        xb = xv.astype(BF16)
        for j in range(2):
            conv_ref[:, j * D_MODEL:(j + 1) * D_MODEL] = _dot(xb, w_ref[j])
        qk = _dot(xb, w_ref[2])
        qkv_ref[:, 0:w] = (qk[:, 0:w] * Q_SCALE).astype(BF16)
        qkv_ref[:, w:2 * w] = qk[:, w:].astype(BF16)
        vz = _dot(xb, w_ref[3])
        qkv_ref[:, 2 * w:] = vz[:, 0:w].astype(BF16)
        za_ref[...] = vz[:, w:]
        xt_ref[...] = xv.T.astype(BF16)

    rows = lambda width: pl.BlockSpec((tm, width), lambda i: (i, 0))
    return pl.pallas_call(
        body, name="proj",
        grid=(seq // tm,),
        in_specs=[rows(D_MODEL),
                  pl.BlockSpec((N_CHIPS, D_MODEL, D_MODEL), lambda i: (0, 0, 0), pipeline_mode=pl.Buffered(1))],
        out_specs=(rows(4 * w), rows(3 * w), rows(w), pl.BlockSpec((D_MODEL, tm), lambda i: (0, i))),
        out_shape=(jax.ShapeDtypeStruct((seq, 4 * w), F32), jax.ShapeDtypeStruct((seq, 3 * w), BF16),
                   jax.ShapeDtypeStruct((seq, w), F32), jax.ShapeDtypeStruct((D_MODEL, seq), BF16)),
        compiler_params=_compiler_params(48, ("arbitrary",)),
    )(x, wing)


def _col_block(group, n_sub=CONV_WIDTH // LANES):
    return lambda j: (0, group * n_sub + j)


def _shift_down(ext, k, rows):
    return pltpu.roll(ext, k, 0)[SUBLANES:, :]


def _shift_up(ext, k, rows):
    return pltpu.roll(ext, rows + SUBLANES - k, 0)[:rows, :]


def _conv_fwd(proj, conv_w):
    seq = proj.shape[0]
    rows = CONV_ROWS
    n_chunks = seq // rows

    def body(b_ref, c_ref, h_ref, z_ref, w_ref, out_ref, u_s):
        u_s[0:SUBLANES, :] = jnp.zeros((SUBLANES, LANES), F32)

        def fill(r, carry):
            rs = pl.ds(pl.multiple_of(r * rows, rows), rows)
            u_s[pl.ds(pl.multiple_of(r * rows + SUBLANES, SUBLANES), rows), :] = c_ref[rs, :] * h_ref[rs, :]
            return carry

        lax.fori_loop(0, n_chunks, fill, 0)
        w = w_ref[...]

        def chunk(r, carry):
            r0 = pl.multiple_of(r * rows, rows)
            rs = pl.ds(r0, rows)
            ext = u_s[pl.ds(r0, rows + SUBLANES), :]
            u = ext[SUBLANES:, :]
            y = w[2:3, :] * u
            y = y + w[0:1, :] * _shift_down(ext, 2, rows)
            y = y + w[1:2, :] * _shift_down(ext, 1, rows)
            z = z_ref[rs, :]
            out_ref[rs, :] = ((z * _sigmoid(z)) * (b_ref[rs, :] * y)).astype(BF16)
            return carry

        lax.fori_loop(0, n_chunks, chunk, 0)

    col = lambda g: pl.BlockSpec((seq, LANES), _col_block(g))
    return pl.pallas_call(
        body, name="conv_fwd",
        grid=(CONV_WIDTH // LANES,),
        in_specs=[col(0), col(1), col(2), col(3), pl.BlockSpec((3, LANES), lambda j: (0, j))],
        out_specs=pl.BlockSpec((seq, LANES), lambda j: (0, j)),
        out_shape=jax.ShapeDtypeStruct((seq, CONV_WIDTH), BF16),
        scratch_shapes=[pltpu.VMEM((seq + SUBLANES, LANES), F32)],
        compiler_params=_compiler_params(40, ("arbitrary",)),
    )(proj, proj, proj, proj, conv_w)


def _conv_bwd(proj, dmix, conv_w):
    seq = proj.shape[0]
    rows = CONV_ROWS
    n_chunks = seq // rows

    def body(b_ref, c_ref, h_ref, z_ref, d_ref, w_ref, dp_ref, dw_ref, u_s, dy_s):
        u_s[0:SUBLANES, :] = jnp.zeros((SUBLANES, LANES), F32)
        dy_s[seq:seq + SUBLANES, :] = jnp.zeros((SUBLANES, LANES), F32)

        def fill(r, carry):
            r0 = pl.multiple_of(r * rows, rows)
            rs = pl.ds(r0, rows)
            u_s[pl.ds(pl.multiple_of(r0 + SUBLANES, SUBLANES), rows), :] = c_ref[rs, :] * h_ref[rs, :]
            z = z_ref[rs, :]
            dy_s[rs, :] = d_ref[rs, :] * (z * _sigmoid(z)) * b_ref[rs, :]
            return carry

        lax.fori_loop(0, n_chunks, fill, 0)
        w = w_ref[...]

        def chunk(r, acc):
            r0 = pl.multiple_of(r * rows, rows)
            rs = pl.ds(r0, rows)
            ext = u_s[pl.ds(r0, rows + SUBLANES), :]
            u = ext[SUBLANES:, :]
            um1 = _shift_down(ext, 1, rows)
            um2 = _shift_down(ext, 2, rows)
            y = w[2:3, :] * u
            y = y + w[0:1, :] * um2
            y = y + w[1:2, :] * um1
            z = z_ref[rs, :]
            b = b_ref[rs, :]
            dco = d_ref[rs, :]
            sg = _sigmoid(z)
            g = z * sg
            dp_ref[0, rs, :] = (dco * g * y).astype(BF16)
            dp_ref[3, rs, :] = (dco * b * y * (sg * (1.0 + z * (1.0 - sg)))).astype(BF16)
            ext_dy = dy_s[pl.ds(r0, rows + SUBLANES), :]
            dy = ext_dy[:rows, :]
            du = w[2:3, :] * dy + w[1:2, :] * _shift_up(ext_dy, 1, rows) + w[0:1, :] * _shift_up(ext_dy, 2, rows)
            dp_ref[1, rs, :] = (du * h_ref[rs, :]).astype(BF16)
            dp_ref[2, rs, :] = (du * c_ref[rs, :]).astype(BF16)
            a0, a1, a2 = acc
            return (a0 + jnp.sum(dy * um2, axis=0, keepdims=True),
                    a1 + jnp.sum(dy * um1, axis=0, keepdims=True),
                    a2 + jnp.sum(dy * u, axis=0, keepdims=True))

        zero = jnp.zeros((1, LANES), F32)
        a0, a1, a2 = lax.fori_loop(0, n_chunks, chunk, (zero, zero, zero))
        dw_ref[...] = jnp.concatenate([a0, a1, a2, jnp.zeros((SUBLANES - 3, LANES), F32)], axis=0)

    col = lambda g: pl.BlockSpec((seq, LANES), _col_block(g))
    return pl.pallas_call(
        body, name="conv_bwd",
        grid=(CONV_WIDTH // LANES,),
        in_specs=[col(0), col(1), col(2), col(3), col(0), pl.BlockSpec((3, LANES), lambda j: (0, j))],
        out_specs=(pl.BlockSpec((4, seq, LANES), lambda j: (0, 0, j)),
                   pl.BlockSpec((SUBLANES, LANES), lambda j: (0, j))),
        out_shape=(jax.ShapeDtypeStruct((4, seq, CONV_WIDTH), BF16),
                   jax.ShapeDtypeStruct((SUBLANES, CONV_WIDTH), F32)),
        scratch_shapes=[pltpu.VMEM((seq + SUBLANES, LANES), F32), pltpu.VMEM((seq + SUBLANES, LANES), F32)],
        compiler_params=_compiler_params(48, ("arbitrary",)),
    )(proj, proj, proj, proj, dmix, conv_w)


def _tri_masks():
    r = lax.broadcasted_iota(jnp.int32, (TK, TK), 0)
    s = lax.broadcasted_iota(jnp.int32, (TK, TK), 1)
    return r, s


def _head_mask():
    lane = lax.broadcasted_iota(jnp.int32, (1, LANES), 1)
    return lane < HEAD_DIM


def _fill_tri(ref, ones):
    ref[...] = jnp.where(ones, 1.0, 0.0).astype(BF16)


def _tile_lanes(a):
    return jnp.tile(a, (1, TK // LANES))


DEAD_CARRY_BITS = 160.0


def _attn_fwd(qkv, z_attn):
    seq = qkv.shape[0]
    n_q = seq // TQ
    n_kb = seq // TK
    assert TQ == TK and n_kb <= LANES
    heads = range(HEADS_PER_STEP)

    def body(q_ref, k_ref, v_ref, za_ref, o_ref, mix_ref, car_ref, qq_s, vt_s, suffix_s, carry_s, diff_s, ot_s):
        head_a = _head_mask()
        r_i, s_i = _tri_masks()
        diff_s[...] = r_i - s_i
        _fill_tri(suffix_s, r_i > s_i)
        lane = lax.broadcasted_iota(jnp.int32, (1, LANES), 1)

        def prep(r, carry):
            rs = pl.ds(pl.multiple_of(r * TQ, TQ), TQ)
            q = q_ref[rs, :]
            qq_s[r, 0] = jnp.where(head_a, q, jnp.zeros_like(q))
            qq_s[r, 1] = jnp.where(head_a, jnp.zeros_like(q), q)
            vt_s[r] = v_ref[rs, :].astype(F32).T.astype(BF16)
            return carry

        lax.fori_loop(0, n_q, prep, 0)

        def block_pair(j, carry):
            qbs = [2 * j, 2 * j + 1]
            rows = [pl.ds(pl.multiple_of(qb * TQ, TQ), TQ) for qb in qbs]
            both = [qq_s[qb].reshape(HEADS_PER_STEP * TQ, LANES) for qb in qbs]
            carry_s[...] = jnp.zeros(carry_s.shape, F32)
            ot_s[...] = jnp.zeros(ot_s.shape, F32)
            for rs in rows:
                for h in heads:
                    car_ref[h, rs, :] = jnp.zeros((TQ, LANES), F32)

            def tiles(items):
                kbs = [qbs[s] - n for s, n in items]
                cols = [pl.ds(pl.multiple_of(jnp.maximum(kb, 0) * TK, TK), TK) for kb in kbs]
                zs = [_dot_nt(both[s], k_ref[cs, :]).reshape(HEADS_PER_STEP, TQ, TK) for (s, _), cs in zip(items, cols)]
                zms, cars, sufs = [], [], []
                for (s, n), kb, z in zip(items, kbs, zs):
                    mask = diff_s[...] > jnp.where(kb < 0, TK, jnp.where(n == 0, 0, -TK))
                    parts = []
                    for h in heads:
                        z2 = jnp.where(mask, z[h] * LOG2_E, MASKED_LOGIT)
                        sp = _softplus2(z2)
                        zms.append(z2 - sp)
                        parts += _split(sp)
                        car = carry_s[s, h]
                        cars.append(car)
                        car_ref[h, rows[s], :] = jnp.where(lane == kb, car, car_ref[h, rows[s], :])
                        carry_s[s, h] = car + jnp.sum(sp, axis=1, keepdims=True)
                    sufs += _block_sums(parts, suffix_s[...])
                for i, ((s, _), kb) in enumerate(zip(items, kbs)):
                    out = []
                    for h in heads:
                        j2 = i * HEADS_PER_STEP + h
                        w = jnp.exp2(zms[j2] - (sufs[j2] + _tile_lanes(cars[j2]))).astype(BF16)
                        out.append(_dot_nt(vt_s[jnp.maximum(kb, 0), h * HEAD_DIM:(h + 1) * HEAD_DIM, :], w))
                    ot_s[s] += jnp.concatenate(out, axis=0)

            tiles([(0, 0), (0, 1), (1, 0), (1, 1)])

            for s, (qb, rs) in enumerate(zip(qbs, rows)):
                def lowest_carry():
                    return jnp.min(jnp.minimum(carry_s[s, 0], carry_s[s, 1]))

                def live(state):
                    n, lowest = state
                    return jnp.logical_and(n <= qb, lowest < DEAD_CARRY_BITS)

                def step(state):
                    n, _ = state
                    tiles([(s, n)])
                    return n + 1, lowest_carry()

                n_done, _ = lax.while_loop(live, step, (jnp.int32(2), lowest_carry()))
                for h in heads:
                    car_ref[h, rs, :] = jnp.where(lane < qb + 1 - n_done, carry_s[s, h], car_ref[h, rs, :])
                o = ot_s[s].T
                o_ref[rs, :] = o
                za = za_ref[rs, :]
                mix_ref[rs, :] = ((za * _sigmoid(za)) * o).astype(BF16)
            return carry

        assert n_q % 2 == 0
        lax.fori_loop(0, n_q // 2, block_pair, 0)

    col = lambda g: pl.BlockSpec((seq, LANES), _col_block(g))
    n_pairs = ATTN_WIDTH // LANES
    return pl.pallas_call(
        body, name="attn_fwd",
        grid=(n_pairs,),
        in_specs=[col(0), col(1), col(2), col(0)],
        out_specs=(pl.BlockSpec((seq, LANES), lambda p: (0, p)),
                   pl.BlockSpec((seq, LANES), lambda p: (0, p)),
                   pl.BlockSpec((HEADS_PER_STEP, seq, LANES), lambda p: (p, 0, 0))),
        out_shape=(jax.ShapeDtypeStruct((seq, ATTN_WIDTH), F32),
                   jax.ShapeDtypeStruct((seq, ATTN_WIDTH), BF16),
                   jax.ShapeDtypeStruct((n_pairs * HEADS_PER_STEP, seq, LANES), F32)),
        scratch_shapes=[pltpu.VMEM((n_q, HEADS_PER_STEP, TQ, LANES), BF16),
                        pltpu.VMEM((n_kb, LANES, TK), BF16),
                        pltpu.VMEM((TK, TK), BF16),
                        pltpu.VMEM((2, HEADS_PER_STEP, TQ, LANES), F32),
                        pltpu.VMEM((TQ, TK), jnp.int32),
                        pltpu.VMEM((2, LANES, TQ), F32)],
        compiler_params=_compiler_params(48, ("arbitrary",)),
    )(qkv, qkv, qkv, z_attn)


def _attn_bwd(qkv, z_attn, o, dmix, carries):
    seq = qkv.shape[0]
    n_q = seq // TQ
    n_kb = seq // TK
    assert TQ == TK and n_kb <= LANES
    heads = range(HEADS_PER_STEP)

    def body(q_ref, k_ref, v_ref, za_ref, o_ref, dm_ref, car_ref, dp_ref,
             qq_s, dd_s, qt_s, dot_s, kt_s, suffix_s, prefix_s, ccar_s, dq_s, dk_s, dv_s, diff_s):
        head_a = _head_mask()
        r_i, s_i = _tri_masks()
        diff_s[...] = r_i - s_i
        _fill_tri(suffix_s, r_i > s_i)
        _fill_tri(prefix_s, r_i < s_i)
        lane = lax.broadcasted_iota(jnp.int32, (1, LANES), 1)

        def prep(r, carry):
            rs = pl.ds(pl.multiple_of(r * TQ, TQ), TQ)
            q = q_ref[rs, :]
            qq_s[r, 0] = jnp.where(head_a, q, jnp.zeros_like(q))
            qq_s[r, 1] = jnp.where(head_a, jnp.zeros_like(q), q)
            qt_s[r] = q.astype(F32).T.astype(BF16)
            kt_s[r] = k_ref[rs, :].astype(F32).T.astype(BF16)
            za = za_ref[rs, :]
            sg = _sigmoid(za)
            dm = dm_ref[rs, :]
            do = dm * (za * sg)
            dd_s[r, 0] = jnp.where(head_a, do, 0.0).astype(BF16)
            dd_s[r, 1] = jnp.where(head_a, 0.0, do).astype(BF16)
            dot_s[r] = do.T.astype(BF16)
            dp_ref[3, rs, :] = (dm * o_ref[rs, :] * (sg * (1.0 + za * (1.0 - sg)))).astype(BF16)
            dq_s[:, rs] = jnp.zeros((LANES, TQ), F32)
            dk_s[:, rs] = jnp.zeros((LANES, TQ), F32)
            dv_s[:, rs] = jnp.zeros((LANES, TQ), F32)
            return carry

        lax.fori_loop(0, n_q, prep, 0)

        def block_pair(j, carry):
            qbs = [2 * j, 2 * j + 1]
            rows = [pl.ds(pl.multiple_of(qb * TQ, TQ), TQ) for qb in qbs]
            both_q = [qq_s[qb].reshape(HEADS_PER_STEP * TQ, LANES) for qb in qbs]
            both_do = [dd_s[qb].reshape(HEADS_PER_STEP * TQ, LANES) for qb in qbs]
            ccar_s[...] = jnp.zeros(ccar_s.shape, F32)

            def tiles(items):
                cols = [pl.ds(pl.multiple_of(jnp.maximum(kb, 0) * TK, TK), TK) for _, kb in items]
                zs = [_dot_nt(both_q[s], k_ref[cs, :]).reshape(HEADS_PER_STEP, TQ, TK) for (s, _), cs in zip(items, cols)]
                dws = [_dot_nt(both_do[s], v_ref[cs, :]).reshape(HEADS_PER_STEP, TQ, TK)
                       for (s, _), cs in zip(items, cols)]
                zms, sufs = [], []
                for (s, kb), z in zip(items, zs):
                    mask = diff_s[...] > jnp.where(kb < 0, TK, jnp.where(kb == qbs[s], 0, -TK))
                    parts = []
                    for h in heads:
                        z2 = jnp.where(mask, z[h] * LOG2_E, MASKED_LOGIT)
                        sp = _softplus2(z2)
                        zms.append(z2 - sp)
                        parts += _split(sp)
                    sufs += _block_sums(parts, suffix_s[...])
                ws, dlws, pres = [], [], []
                for i, (s, kb) in enumerate(items):
                    parts = []
                    for h in heads:
                        j2 = i * HEADS_PER_STEP + h
                        cin = jnp.sum(jnp.where(lane == kb, car_ref[h, rows[s], :], 0.0), axis=1, keepdims=True)
                        w = jnp.exp2(zms[j2] - (sufs[j2] + cin))
                        ws.append(w.astype(BF16))
                        dlw = dws[i][h] * w
                        dlws.append(dlw)
                        parts += _split(dlw)
                    pres += _block_sums(parts, prefix_s[...])
                for i, ((s, kb), cs) in enumerate(zip(items, cols)):
                    dq_t, dk_t, dv_t = [], [], []
                    for h in heads:
                        j2 = i * HEADS_PER_STEP + h
                        ccar = ccar_s[s, h]
                        dz = (dlws[j2] - jnp.exp2(zms[j2]) * (dlws[j2] + (pres[j2] + _tile_lanes(ccar)))).astype(BF16)
                        ccar_s[s, h] = ccar + jnp.sum(dlws[j2], axis=1, keepdims=True)
                        dims = slice(h * HEAD_DIM, (h + 1) * HEAD_DIM)
                        dq_t.append(_dot_nt(kt_s[jnp.maximum(kb, 0), dims, :], dz))
                        dk_t.append(_dot(qt_s[qbs[s], dims, :], dz))
                        dv_t.append(_dot(dot_s[qbs[s], dims, :], ws[j2]))
                    dq_s[:, rows[s]] += jnp.concatenate(dq_t, axis=0)
                    dk_s[:, cs] += jnp.concatenate(dk_t, axis=0)
                    dv_s[:, cs] += jnp.concatenate(dv_t, axis=0)

            for s, (qb, rs) in enumerate(zip(qbs, rows)):
                lowest = jnp.min(jnp.minimum(car_ref[0, rs, :], car_ref[1, rs, :]), axis=0, keepdims=True)
                dead = jnp.logical_and(lowest >= DEAD_CARRY_BITS, lane < qb)
                first_live = jnp.sum(jnp.where(dead, 1, 0))

                def one(kb, c2):
                    tiles([(s, kb)])
                    return c2

                lax.fori_loop(first_live, qb - 1, one, 0)

            for s, qb in enumerate(qbs):
                tiles([(s, qb - 1), (s, qb)])
            return carry

        assert n_q % 2 == 0
        lax.fori_loop(0, n_q // 2, block_pair, 0)

        def finish(r, carry):
            rs = pl.ds(pl.multiple_of(r * TQ, TQ), TQ)
            dp_ref[0, rs, :] = (dq_s[:, rs].T * Q_SCALE).astype(BF16)
            dp_ref[1, rs, :] = dk_s[:, rs].T.astype(BF16)
            dp_ref[2, rs, :] = dv_s[:, rs].T.astype(BF16)
            return carry

        lax.fori_loop(0, n_q, finish, 0)

    def col(g, n_sub=CONV_WIDTH // LANES):
        return pl.BlockSpec((seq, LANES), _col_block(g, n_sub), pipeline_mode=pl.Buffered(1))

    n_pairs = ATTN_WIDTH // LANES
    by_head = lambda: pltpu.VMEM((n_q, HEADS_PER_STEP, TQ, LANES), BF16)
    transposed = lambda: pltpu.VMEM((n_q, LANES, TQ), BF16)
    acc_t = lambda: pltpu.VMEM((LANES, seq), F32)
    return pl.pallas_call(
        body, name="attn_bwd",
        grid=(n_pairs,),
        in_specs=[col(0), col(1), col(2), col(0), col(0), col(1),
                  pl.BlockSpec((HEADS_PER_STEP, seq, LANES), lambda p: (p, 0, 0), pipeline_mode=pl.Buffered(1))],
        out_specs=pl.BlockSpec((4, seq, LANES), lambda p: (0, 0, p)),
        out_shape=jax.ShapeDtypeStruct((4, seq, ATTN_WIDTH), BF16),
        scratch_shapes=[by_head(), by_head(), transposed(), transposed(), transposed(),
                        pltpu.VMEM((TK, TK), BF16), pltpu.VMEM((TK, TK), BF16),
                        pltpu.VMEM((2, HEADS_PER_STEP, TQ, LANES), F32),
                        acc_t(), acc_t(), acc_t(),
                        pltpu.VMEM((TQ, TK), jnp.int32)],
        compiler_params=_compiler_params(56, ("arbitrary",)),
    )(qkv, qkv, qkv, z_attn, o, dmix, carries)


def _out_ln(mix_c, mix_a, woutg, x, target, gain, bias):
    seq = x.shape[0]
    tm = 512
    inv_d = 1.0 / D_MODEL

    def body(mc_ref, ma_ref, wo_ref, x_ref, t_ref, g_ref, b_ref, dxa_ref, dmix_ref, gw_ref, st_ref):
        @pl.when(pl.program_id(0) == 0)
        def _():
            gw_ref[...] = jnp.zeros(gw_ref.shape, F32)
            st_ref[...] = jnp.zeros(st_ref.shape, F32)

        mc = mc_ref[...]
        ma = ma_ref[...]
        sub = _dot(mc, wo_ref[0:CONV_WIDTH, :]) + _dot(ma, wo_ref[CONV_WIDTH:, :])
        r = DEEPNORM_ALPHA * x_ref[...] + sub
        mu = jnp.mean(r, axis=-1, keepdims=True)
        rc = r - mu
        var = jnp.mean(rc * rc, axis=-1, keepdims=True)
        rstd = lax.rsqrt(var + LN_EPS)
        xhat = rc * rstd
        gain_v = g_ref[...]
        diff = (xhat * gain_v + b_ref[...]) - t_ref[...]
        dy = diff * inv_d
        st_ref[0:1, :] += jnp.sum(dy * xhat, axis=0, keepdims=True)
        st_ref[1:2, :] += jnp.sum(dy, axis=0, keepdims=True)
        st_ref[2:3, :] += jnp.sum(diff * diff, axis=0, keepdims=True)
        dxh = dy * gain_v
        m1 = jnp.mean(dxh, axis=-1, keepdims=True)
        m2 = jnp.mean(dxh * xhat, axis=-1, keepdims=True)
        dr = rstd * (dxh - m1 - xhat * m2)
        dxa_ref[...] = DEEPNORM_ALPHA * dr
        drb = dr.astype(BF16)
        dmix_ref[...] = _dot_nt(drb, wo_ref[...])
        gw_ref[0:CONV_WIDTH, :] += _dot_tn(mc, drb)
        gw_ref[CONV_WIDTH:, :] += _dot_tn(ma, drb)

    rows_d = lambda: pl.BlockSpec((tm, D_MODEL), lambda i: (i, 0))
    rows_h = lambda: pl.BlockSpec((tm, CONV_WIDTH), lambda i: (i, 0))
    whole = lambda shape: pl.BlockSpec(shape, lambda i: (0, 0))
    return pl.pallas_call(
        body, name="out_ln",
        grid=(seq // tm,),
        in_specs=[rows_h(), rows_h(), whole((D_MODEL, D_MODEL)), rows_d(), rows_d(),
                  whole((1, D_MODEL)), whole((1, D_MODEL))],
        out_specs=(rows_d(), rows_d(), whole((D_MODEL, D_MODEL)), whole((SUBLANES, D_MODEL))),
        out_shape=(jax.ShapeDtypeStruct((seq, D_MODEL), F32), jax.ShapeDtypeStruct((seq, D_MODEL), F32),
                   jax.ShapeDtypeStruct((D_MODEL, D_MODEL), F32), jax.ShapeDtypeStruct((SUBLANES, D_MODEL), F32)),
        compiler_params=_compiler_params(48, ("arbitrary",)),
    )(mix_c, mix_a, woutg, x, target, gain, bias)


def _group_maps():
    half = N_GROUPS // 2
    return (lambda g: jnp.minimum(g, half - 1)), (lambda g: jnp.maximum(g - half, 0))


def _grad_w_in(xt, dpc, dpa):
    seq = xt.shape[1]
    half = N_GROUPS // 2
    in_conv, in_attn = _group_maps()

    def body(xt_ref, dc_ref, da_ref, out_ref):
        g = pl.program_id(0)

        @pl.when(g < half)
        def _():
            out_ref[...] = _dot(xt_ref[...], dc_ref[...])

        @pl.when(g >= half)
        def _():
            out_ref[...] = _dot(xt_ref[...], da_ref[...])

    return pl.pallas_call(
        body, name="grad_w_in",
        grid=(N_GROUPS,),
        in_specs=[pl.BlockSpec((D_MODEL, seq), lambda g: (0, 0), pipeline_mode=pl.Buffered(1)),
                  pl.BlockSpec((None, seq, CONV_WIDTH), lambda g: (in_conv(g), 0, 0)),
                  pl.BlockSpec((None, seq, CONV_WIDTH), lambda g: (in_attn(g), 0, 0))],
        out_specs=pl.BlockSpec((None, D_MODEL, CONV_WIDTH), lambda g: (g // 2, 0, g % 2)),
        out_shape=jax.ShapeDtypeStruct((N_CHIPS, D_MODEL, D_MODEL), F32),
        compiler_params=_compiler_params(48, ("arbitrary",)),
    )(xt, dpc, dpa)


def _grad_x(dxa, dpc, dpa, wing):
    seq = dxa.shape[0]
    tm = 512
    half = N_GROUPS // 2

    def body(dxa_ref, dc_ref, da_ref, w_ref, out_ref):
        acc = dxa_ref[...]
        for g in range(N_GROUPS):
            dp = dc_ref[g] if g < half else da_ref[g - half]
            cols = slice((g % 2) * CONV_WIDTH, (g % 2 + 1) * CONV_WIDTH)
            acc = acc + _dot_nt(dp, w_ref[g // 2, :, cols])
        out_ref[...] = acc

    return pl.pallas_call(
        body, name="grad_x",
        grid=(seq // tm,),
        in_specs=[pl.BlockSpec((tm, D_MODEL), lambda i: (i, 0)),
                  pl.BlockSpec((half, tm, CONV_WIDTH), lambda i: (0, i, 0)),
                  pl.BlockSpec((half, tm, CONV_WIDTH), lambda i: (0, i, 0)),
                  pl.BlockSpec((N_CHIPS, D_MODEL, D_MODEL), lambda i: (0, 0, 0), pipeline_mode=pl.Buffered(1))],
        out_specs=pl.BlockSpec((tm, D_MODEL), lambda i: (i, 0)),
        out_shape=jax.ShapeDtypeStruct((seq, D_MODEL), F32),
        compiler_params=_compiler_params(40, ("arbitrary",)),
    )(dxa, dpc, dpa, wing)


PACK_LOSS_ROW = 5


def _reduce_grads(gwin, gwout, pack):
    n_shards, n_rows_in, n_cols = gwin.shape
    n_rows_out = gwout.shape[1]
    half_rows = (n_rows_in // 2, n_rows_out // 2)
    row_chunk = 128

    def body(gwin_ref, gwout_ref, pack_ref, oin_ref, oout_ref, opack_ref,
             mine_in, mine_out, sib_in, sib_out, wire_in, wire_out, rin, rout, rpack,
             local_sems, sib_send, sib_recv, send_sems, recv_sems, pack_send, pack_recv, fin_send, fin_recv):
        x, y, c = lax.axis_index("x"), lax.axis_index("y"), lax.axis_index("c")
        me = 2 * x + y
        my_id = 4 * x + 2 * y + c
        chips = [(1 - x, y), (x, 1 - y), (1 - x, 1 - y)]
        sibling = (x, y, 1 - c)
        partial = (gwin_ref, gwout_ref)
        mine = (mine_in, mine_out)
        from_sib = (sib_in, sib_out)
        wire = (wire_in, wire_out)
        from_chips = (rin, rout)
        result = (oin_ref, oout_ref)

        def half(a, which):
            rows = half_rows[a]
            return pl.ds(pl.multiple_of(which * rows, rows), rows)

        order = [2 * px + py for px, py in chips] + [me]

        def swap_copy(i, a):
            shard = order[i]
            return pltpu.make_async_remote_copy(
                src_ref=partial[a].at[shard, half(a, 1 - c), :], dst_ref=from_sib[a].at[shard],
                send_sem=sib_send.at[2 * i + a], recv_sem=sib_recv.at[2 * i + a],
                device_id=sibling, device_id_type=MESH)

        sends = []
        loads = []
        for a in range(2):
            ld = pltpu.make_async_copy(partial[a].at[:, half(a, c), :], mine[a], local_sems.at[a])
            ld.start()
            loads.append(ld)
        for i in range(n_shards):
            for a in range(2):
                cp = swap_copy(i, a)
                cp.start()
                sends.append(cp)

        rpack[my_id] = pack_ref[...]

        def pack_copy(rel, slot, to):
            return pltpu.make_async_remote_copy(
                src_ref=rpack.at[slot], dst_ref=rpack.at[slot],
                send_sem=pack_send.at[rel - 1], recv_sem=pack_recv.at[rel - 1],
                device_id=to, device_id_type=MESH)

        def related(rel):
            px = (1 - x) if rel & 4 else x
            py = (1 - y) if rel & 2 else y
            pc = (1 - c) if rel & 1 else c
            return px, py, pc

        for rel in range(1, N_DEVICES):
            cp = pack_copy(rel, my_id, related(rel))
            cp.start()
            sends.append(cp)

        for ld in loads:
            ld.wait()

        def chip_sum(a, shard):
            def add(r, carry):
                rs = pl.ds(pl.multiple_of(r * row_chunk, row_chunk), row_chunk)
                mine[a][shard, rs, :] = mine[a][shard, rs, :] + from_sib[a][shard, rs, :]
                return carry

            lax.fori_loop(0, half_rows[a] // row_chunk, add, 0)

        def chip_sum_to_wire(a, shard, k):
            def add(r, carry):
                rs = pl.ds(pl.multiple_of(r * row_chunk, row_chunk), row_chunk)
                wire[a][k, rs, :] = (mine[a][shard, rs, :] + from_sib[a][shard, rs, :]).astype(BF16)
                return carry

            lax.fori_loop(0, half_rows[a] // row_chunk, add, 0)

        def shard_copy(k, a, to):
            return pltpu.make_async_remote_copy(
                src_ref=wire[a].at[k], dst_ref=from_chips[a].at[k],
                send_sem=send_sems.at[2 * k + a], recv_sem=recv_sems.at[2 * k + a],
                device_id=to, device_id_type=MESH)

        for k, (px, py) in enumerate(chips):
            for a in range(2):
                swap_copy(k, a).wait_recv()
                chip_sum_to_wire(a, 2 * px + py, k)
                cp = shard_copy(k, a, (px, py, c))
                cp.start()
                sends.append(cp)
        for a in range(2):
            swap_copy(n_shards - 1, a).wait_recv()
            chip_sum(a, me)

        for rel in range(1, N_DEVICES):
            px, py, pc = related(rel)
            pack_copy(rel, 4 * px + 2 * py + pc, (px, py, pc)).wait_recv()
        total = rpack[0]
        for i in range(1, N_DEVICES):
            total = total + rpack[i]
        opack_ref[...] = total
        sq_err = jnp.sum(total[PACK_LOSS_ROW:PACK_LOSS_ROW + 1, :], axis=1, keepdims=True)
        opack_ref[PACK_LOSS_ROW:PACK_LOSS_ROW + 1, :] = jnp.broadcast_to(sq_err * (0.5 / D_MODEL), (1, n_cols))

        for k, (px, py) in enumerate(chips):
            for a in range(2):
                shard_copy(k, a, (px, py, c)).wait_recv()

        def finish(a):
            def add(r, carry):
                rs = pl.ds(pl.multiple_of(r * row_chunk, row_chunk), row_chunk)
                dst = pl.ds(pl.multiple_of(c * half_rows[a] + r * row_chunk, row_chunk), row_chunk)
                result[a][dst, :] = ((mine[a][me, rs, :] + from_chips[a][0, rs, :].astype(F32))
                                     + from_chips[a][1, rs, :].astype(F32)) + from_chips[a][2, rs, :].astype(F32)
                return carry

            lax.fori_loop(0, half_rows[a] // row_chunk, add, 0)

        def final_copy(a, which):
            ref = result[a].at[half(a, which), :]
            return pltpu.make_async_remote_copy(
                src_ref=ref, dst_ref=ref, send_sem=fin_send.at[a], recv_sem=fin_recv.at[a],
                device_id=sibling, device_id_type=MESH)

        for a in range(2):
            finish(a)
            cp = final_copy(a, c)
            cp.start()
            sends.append(cp)
        for a in range(2):
            final_copy(a, 1 - c).wait_recv()
        for cp in sends:
            cp.wait_send()

    vmem = pl.BlockSpec(memory_space=pltpu.VMEM)
    hbm = pl.BlockSpec(memory_space=pl.ANY)
    in_shape = (n_rows_in, n_cols)
    out_shape = (n_rows_out, n_cols)
    half_in = (half_rows[0], n_cols)
    half_out = (half_rows[1], n_cols)
    return pl.pallas_call(
        body, name="reduce_grads",
        out_shape=(jax.ShapeDtypeStruct(in_shape, F32), jax.ShapeDtypeStruct(out_shape, F32),
                   jax.ShapeDtypeStruct(pack.shape, F32)),
        in_specs=[hbm, hbm, vmem], out_specs=(vmem, vmem, vmem),
        scratch_shapes=[pltpu.VMEM((n_shards,) + half_in, F32), pltpu.VMEM((n_shards,) + half_out, F32),
                        pltpu.VMEM((n_shards,) + half_in, F32), pltpu.VMEM((n_shards,) + half_out, F32),
                        pltpu.VMEM((N_CHIPS - 1,) + half_in, BF16), pltpu.VMEM((N_CHIPS - 1,) + half_out, BF16),
                        pltpu.VMEM((N_CHIPS - 1,) + half_in, BF16), pltpu.VMEM((N_CHIPS - 1,) + half_out, BF16),
                        pltpu.VMEM((N_DEVICES,) + pack.shape, F32),
                        pltpu.SemaphoreType.DMA((2,)),
                        pltpu.SemaphoreType.DMA((2 * n_shards,)), pltpu.SemaphoreType.DMA((2 * n_shards,)),
                        pltpu.SemaphoreType.DMA((6,)), pltpu.SemaphoreType.DMA((6,)),
                        pltpu.SemaphoreType.DMA((N_DEVICES - 1,)), pltpu.SemaphoreType.DMA((N_DEVICES - 1,)),
                        pltpu.SemaphoreType.DMA((2,)), pltpu.SemaphoreType.DMA((2,))],
        compiler_params=_compiler_params(48),
    )(gwin, gwout, pack)


def _adamw(name, w, g, m, v):
    n_rows, n_cols = w.shape
    tr = 256 if n_rows % 256 == 0 else n_rows
    m_corr = 1.0 - ADAM_B1 ** ADAM_STEP
    v_corr = 1.0 - ADAM_B2 ** ADAM_STEP

    def body(w_ref, g_ref, m_ref, v_ref, d_ref, nm_ref, nv_ref):
        gv = g_ref[...]
        nm = ADAM_B1 * m_ref[...] + (1.0 - ADAM_B1) * gv
        nv = ADAM_B2 * v_ref[...] + (1.0 - ADAM_B2) * (gv * gv)
        nm_ref[...] = nm
        nv_ref[...] = nv
        d_ref[...] = -ADAM_LR * ((nm / m_corr) / (jnp.sqrt(nv / v_corr) + ADAM_EPS) + ADAM_WD * w_ref[...])

    blk = lambda: pl.BlockSpec((tr, n_cols), lambda i: (i, 0))
    shape = jax.ShapeDtypeStruct(w.shape, F32)
    return pl.pallas_call(
        body, name=name,
        grid=(n_rows // tr,),
        in_specs=[blk(), blk(), blk(), blk()], out_specs=(blk(), blk(), blk()),
        out_shape=(shape, shape, shape),
        compiler_params=_compiler_params(32, ("arbitrary",)),
    )(w, g, m, v)


def kernel(x, w_in, conv_w, w_out, ln_gain, ln_bias, loss_target, m_w_in, m_conv_w, m_w_out, m_ln_gain, m_ln_bias, v_w_in, v_conv_w, v_w_out, v_ln_gain, v_ln_bias):
    xs = x[0]
    target = loss_target[0]

    wing, woutg, cwg = _gather_weights(w_in[0], w_out[0], conv_w[0])
    conv_full = jnp.transpose(cwg, (1, 0, 2)).reshape(3, CONV_WIDTH)
    wout_full = woutg.reshape(D_MODEL, D_MODEL)

    p_conv, qkv, z_attn, xt = _proj(xs, wing)
    mix_c = _conv_fwd(p_conv, conv_full)
    o, mix_a, carries = _attn_fwd(qkv, z_attn)
    dxa, dmix, gwout, stats = _out_ln(mix_c, mix_a, wout_full, xs, target, ln_gain, ln_bias)
    dpc, dconv = _conv_bwd(p_conv, dmix, conv_full)
    dpa = _attn_bwd(qkv, z_attn, o, dmix, carries)
    gwin = _grad_w_in(xt, dpc, dpa)
    grad_x = _grad_x(dxa, dpc, dpa, wing)

    pack = jnp.concatenate(
        [stats[0:2], jnp.pad(dconv[0:3], ((0, 0), (0, D_MODEL - CONV_WIDTH))), stats[2:3],
         jnp.zeros((2, D_MODEL), F32)], axis=0)
    g_w_in, g_w_out, tot = _reduce_grads(gwin, gwout.reshape(N_CHIPS, D_MODEL // N_CHIPS, D_MODEL), pack)

    chip = 2 * lax.axis_index("x") + lax.axis_index("y")
    g_gain = tot[0:1]
    g_bias = tot[1:2]
    g_conv = lax.dynamic_slice(tot, (2, chip * LANES), (3, LANES))
    loss = tot[PACK_LOSS_ROW, 0]

    d_w_in, nm_w_in, nv_w_in = _adamw("adamw_w_in", w_in[0], g_w_in, m_w_in[0], v_w_in[0])
    d_w_out, nm_w_out, nv_w_out = _adamw("adamw_w_out", w_out[0], g_w_out, m_w_out[0], v_w_out[0])
    d_conv, nm_conv, nv_conv = _adamw("adamw_conv_w", conv_w[0], g_conv, m_conv_w[0], v_conv_w[0])
    d_gain, nm_gain, nv_gain = _adamw("adamw_ln_gain", ln_gain, g_gain, m_ln_gain, v_ln_gain)
    d_bias, nm_bias, nv_bias = _adamw("adamw_ln_bias", ln_bias, g_bias, m_ln_bias, v_ln_bias)

    lead = lambda a: a[None]
    return (loss, lead(grad_x),
            lead(g_w_in), lead(g_conv), lead(g_w_out), g_gain, g_bias,
            lead(d_w_in), lead(d_conv), lead(d_w_out), d_gain, d_bias,
            lead(nm_w_in), lead(nm_conv), lead(nm_w_out), nm_gain, nm_bias,
            lead(nv_w_in), lead(nv_conv), lead(nv_w_out), nv_gain, nv_bias)
```

```python
import jax
import jax.numpy as jnp
from jax import lax
from jax.experimental import pallas as pl
from jax.experimental.pallas import tpu as pltpu

F32 = jnp.float32
BF16 = jnp.bfloat16
MESH = pl.DeviceIdType.MESH

D_MODEL = 1024
CONV_WIDTH = 512
ATTN_WIDTH = 512
HEAD_DIM = 64
N_GROUPS = 8
N_CHIPS = 4
N_DEVICES = 8
LN_EPS = 1e-5
DEEPNORM_ALPHA = 2.0 ** 0.25
Q_SCALE = HEAD_DIM ** -0.5
ADAM_LR = 0.001
ADAM_B1 = 0.9
ADAM_B2 = 0.999
ADAM_EPS = 1e-08
ADAM_WD = 0.01
ADAM_STEP = 10

LANES = 128
SUBLANES = 8
V7X_VMEM_BYTES = 64 * 1024 * 1024
MIB = 1024 * 1024

TQ = 256
TK = 256
HEADS_PER_STEP = LANES // HEAD_DIM
CONV_ROWS = 512


def _compiler_params(vmem_mib, semantics=None):
    assert vmem_mib * MIB < V7X_VMEM_BYTES
    return pltpu.CompilerParams(dimension_semantics=semantics, vmem_limit_bytes=vmem_mib * MIB)


def _sigmoid(z):
    return 1.0 / (1.0 + jnp.exp(-z))


def _dot(a, b):
    return jnp.dot(a, b, preferred_element_type=F32)


def _dot_nt(a, b):
    return lax.dot_general(a, b, (((1,), (1,)), ((), ())), preferred_element_type=F32)


def _dot_tn(a, b):
    return lax.dot_general(a, b, (((0,), (0,)), ((), ())), preferred_element_type=F32)


def _truncate_to_bf16(a):
    bits = lax.bitcast_convert_type(a, jnp.uint32) & jnp.uint32(0xFFFF0000)
    return lax.bitcast_convert_type(bits, F32)


def _split(a):
    hi = _truncate_to_bf16(a)
    return [hi.astype(BF16), (a - hi).astype(BF16)]


def _block_sums(parts, tri):
    n = len(parts) // 2
    res = _dot(jnp.concatenate(parts, axis=0), tri).reshape(n, 2, TQ, TK)
    return [res[i, 0] + res[i, 1] for i in range(n)]


MASKED_LOGIT = -1e30
LOG2_E = 1.4426950408889634


def _softplus2(z2):
    return jnp.maximum(z2, 0.0) + jnp.log2(1.0 + jnp.exp2(-jnp.abs(z2)))


def _gather_weights(w_in, w_out, conv_w):
    d_rows, d_cols = w_in.shape
    o_rows, o_cols = w_out.shape
    half_rows = (d_rows // 2, o_rows // 2)
    row_chunk = 128

    def body(win_ref, wout_ref, cw_ref, wing_ref, woutg_ref, cwg_ref, send_sems, recv_sems, pass_send, pass_recv):
        x, y, c = lax.axis_index("x"), lax.axis_index("y"), lax.axis_index("c")
        me = 2 * x + y
        srcs = (win_ref, wout_ref)
        bufs = (wing_ref, woutg_ref)

        def half(a, shard, which):
            rows = half_rows[a]
            return bufs[a].at[shard, pl.ds(pl.multiple_of(which * rows, rows), rows), :]

        def cast_half(which):
            for a in range(2):
                def cast(r, carry):
                    rows = pl.ds(pl.multiple_of(which * half_rows[a] + r * row_chunk, row_chunk), row_chunk)
                    bufs[a][me, rows, :] = srcs[a][rows, :].astype(BF16)
                    return carry

                lax.fori_loop(0, half_rows[a] // row_chunk, cast, 0)

        chips = [(1 - x, y), (x, 1 - y), (1 - x, 1 - y)]

        def ici_copy(k, a, shard, to):
            ref = cwg_ref.at[shard] if a == 2 else half(a, shard, c)
            return pltpu.make_async_remote_copy(
                src_ref=ref, dst_ref=ref, send_sem=send_sems.at[3 * k + a], recv_sem=recv_sems.at[3 * k + a],
                device_id=to, device_id_type=MESH)

        def pass_copy(k, a, shard, which):
            ref = half(a, shard, which)
            return pltpu.make_async_remote_copy(
                src_ref=ref, dst_ref=ref, send_sem=pass_send.at[2 * k + a], recv_sem=pass_recv.at[2 * k + a],
                device_id=(x, y, 1 - c), device_id_type=MESH)

        cast_half(c)
        cwg_ref[me] = cw_ref[...]
        sends = []
        for k, (px, py) in enumerate(chips):
            for a in range(3):
                cp = ici_copy(k, a, me, (px, py, c))
                cp.start()
                sends.append(cp)
        cast_half(1 - c)
        for k, (px, py) in enumerate(chips):
            for a in range(2):
                ici_copy(k, a, 2 * px + py, (px, py, c)).wait_recv()
                cp = pass_copy(k, a, 2 * px + py, c)
                cp.start()
                sends.append(cp)
        for k, (px, py) in enumerate(chips):
            ici_copy(k, 2, 2 * px + py, (px, py, c)).wait_recv()
            for a in range(2):
                pass_copy(k, a, 2 * px + py, 1 - c).wait_recv()
        for cp in sends:
            cp.wait_send()

    vmem = pl.BlockSpec(memory_space=pltpu.VMEM)
    return pl.pallas_call(
        body, name="gather_weights",
        out_shape=(jax.ShapeDtypeStruct((N_CHIPS, d_rows, d_cols), BF16),
                   jax.ShapeDtypeStruct((N_CHIPS, o_rows, o_cols), BF16),
                   jax.ShapeDtypeStruct((N_CHIPS,) + conv_w.shape, F32)),
        in_specs=[vmem, vmem, vmem], out_specs=(vmem, vmem, vmem),
        scratch_shapes=[pltpu.SemaphoreType.DMA((9,)), pltpu.SemaphoreType.DMA((9,)),
                        pltpu.SemaphoreType.DMA((6,)), pltpu.SemaphoreType.DMA((6,))],
        compiler_params=_compiler_params(32),
    )(w_in, w_out, conv_w)


def _proj(x, wing):
    seq = x.shape[0]
    tm = 512
    w = CONV_WIDTH

    def body(x_ref, w_ref, conv_ref, qkv_ref, za_ref, xt_ref):
        xv = x_ref[...]
        xb = xv.astype(BF16)
        for j in range(2):
            conv_ref[:, j * D_MODEL:(j + 1) * D_MODEL] = _dot(xb, w_ref[j])
        qk = _dot(xb, w_ref[2])
        qkv_ref[:, 0:w] = (qk[:, 0:w] * Q_SCALE).astype(BF16)
        qkv_ref[:, w:2 * w] = qk[:, w:].astype(BF16)
        vz = _dot(xb, w_ref[3])
        qkv_ref[:, 2 * w:] = vz[:, 0:w].astype(BF16)
        za_ref[...] = vz[:, w:]
        xt_ref[...] = xv.T.astype(BF16)

    rows = lambda width: pl.BlockSpec((tm, width), lambda i: (i, 0))
    return pl.pallas_call(
        body, name="proj",
        grid=(seq // tm,),
        in_specs=[rows(D_MODEL),
                  pl.BlockSpec((N_CHIPS, D_MODEL, D_MODEL), lambda i: (0, 0, 0), pipeline_mode=pl.Buffered(1))],
        out_specs=(rows(4 * w), rows(3 * w), rows(w), pl.BlockSpec((D_MODEL, tm), lambda i: (0, i))),
        out_shape=(jax.ShapeDtypeStruct((seq, 4 * w), F32), jax.ShapeDtypeStruct((seq, 3 * w), BF16),
                   jax.ShapeDtypeStruct((seq, w), F32), jax.ShapeDtypeStruct((D_MODEL, seq), BF16)),
        compiler_params=_compiler_params(48, ("arbitrary",)),
    )(x, wing)


def _col_block(group, n_sub=CONV_WIDTH // LANES):
    return lambda j: (0, group * n_sub + j)


def _shift_down(ext, k, rows):
    return pltpu.roll(ext, k, 0)[SUBLANES:, :]


def _shift_up(ext, k, rows):
    return pltpu.roll(ext, rows + SUBLANES - k, 0)[:rows, :]


def _conv_fwd(proj, conv_w):
    seq = proj.shape[0]
    rows = CONV_ROWS
    n_chunks = seq // rows

    def body(b_ref, c_ref, h_ref, z_ref, w_ref, out_ref, u_s):
        u_s[0:SUBLANES, :] = jnp.zeros((SUBLANES, LANES), F32)

        def fill(r, carry):
            rs = pl.ds(pl.multiple_of(r * rows, rows), rows)
            u_s[pl.ds(pl.multiple_of(r * rows + SUBLANES, SUBLANES), rows), :] = c_ref[rs, :] * h_ref[rs, :]
            return carry

        lax.fori_loop(0, n_chunks, fill, 0)
        w = w_ref[...]

        def chunk(r, carry):
            r0 = pl.multiple_of(r * rows, rows)
            rs = pl.ds(r0, rows)
            ext = u_s[pl.ds(r0, rows + SUBLANES), :]
            u = ext[SUBLANES:, :]
            y = w[2:3, :] * u
            y = y + w[0:1, :] * _shift_down(ext, 2, rows)
            y = y + w[1:2, :] * _shift_down(ext, 1, rows)
            z = z_ref[rs, :]
            out_ref[rs, :] = ((z * _sigmoid(z)) * (b_ref[rs, :] * y)).astype(BF16)
            return carry

        lax.fori_loop(0, n_chunks, chunk, 0)

    col = lambda g: pl.BlockSpec((seq, LANES), _col_block(g))
    return pl.pallas_call(
        body, name="conv_fwd",
        grid=(CONV_WIDTH // LANES,),
        in_specs=[col(0), col(1), col(2), col(3), pl.BlockSpec((3, LANES), lambda j: (0, j))],
        out_specs=pl.BlockSpec((seq, LANES), lambda j: (0, j)),
        out_shape=jax.ShapeDtypeStruct((seq, CONV_WIDTH), BF16),
        scratch_shapes=[pltpu.VMEM((seq + SUBLANES, LANES), F32)],
        compiler_params=_compiler_params(40, ("arbitrary",)),
    )(proj, proj, proj, proj, conv_w)


def _conv_bwd(proj, dmix, conv_w):
    seq = proj.shape[0]
    rows = CONV_ROWS
    n_chunks = seq // rows

    def body(b_ref, c_ref, h_ref, z_ref, d_ref, w_ref, dp_ref, dw_ref, u_s, dy_s):
        u_s[0:SUBLANES, :] = jnp.zeros((SUBLANES, LANES), F32)
        dy_s[seq:seq + SUBLANES, :] = jnp.zeros((SUBLANES, LANES), F32)

        def fill(r, carry):
            r0 = pl.multiple_of(r * rows, rows)
            rs = pl.ds(r0, rows)
            u_s[pl.ds(pl.multiple_of(r0 + SUBLANES, SUBLANES), rows), :] = c_ref[rs, :] * h_ref[rs, :]
            z = z_ref[rs, :]
            dy_s[rs, :] = d_ref[rs, :] * (z * _sigmoid(z)) * b_ref[rs, :]
            return carry

        lax.fori_loop(0, n_chunks, fill, 0)
        w = w_ref[...]

        def chunk(r, acc):
            r0 = pl.multiple_of(r * rows, rows)
            rs = pl.ds(r0, rows)
            ext = u_s[pl.ds(r0, rows + SUBLANES), :]
            u = ext[SUBLANES:, :]
            um1 = _shift_down(ext, 1, rows)
            um2 = _shift_down(ext, 2, rows)
            y = w[2:3, :] * u
            y = y + w[0:1, :] * um2
            y = y + w[1:2, :] * um1
            z = z_ref[rs, :]
            b = b_ref[rs, :]
            dco = d_ref[rs, :]
            sg = _sigmoid(z)
            g = z * sg
            dp_ref[0, rs, :] = (dco * g * y).astype(BF16)
            dp_ref[3, rs, :] = (dco * b * y * (sg * (1.0 + z * (1.0 - sg)))).astype(BF16)
            ext_dy = dy_s[pl.ds(r0, rows + SUBLANES), :]
            dy = ext_dy[:rows, :]
            du = w[2:3, :] * dy + w[1:2, :] * _shift_up(ext_dy, 1, rows) + w[0:1, :] * _shift_up(ext_dy, 2, rows)
            dp_ref[1, rs, :] = (du * h_ref[rs, :]).astype(BF16)
            dp_ref[2, rs, :] = (du * c_ref[rs, :]).astype(BF16)
            a0, a1, a2 = acc
            return (a0 + jnp.sum(dy * um2, axis=0, keepdims=True),
                    a1 + jnp.sum(dy * um1, axis=0, keepdims=True),
                    a2 + jnp.sum(dy * u, axis=0, keepdims=True))

        zero = jnp.zeros((1, LANES), F32)
        a0, a1, a2 = lax.fori_loop(0, n_chunks, chunk, (zero, zero, zero))
        dw_ref[...] = jnp.concatenate([a0, a1, a2, jnp.zeros((SUBLANES - 3, LANES), F32)], axis=0)

    col = lambda g: pl.BlockSpec((seq, LANES), _col_block(g))
    return pl.pallas_call(
        body, name="conv_bwd",
        grid=(CONV_WIDTH // LANES,),
        in_specs=[col(0), col(1), col(2), col(3), col(0), pl.BlockSpec((3, LANES), lambda j: (0, j))],
        out_specs=(pl.BlockSpec((4, seq, LANES), lambda j: (0, 0, j)),
                   pl.BlockSpec((SUBLANES, LANES), lambda j: (0, j))),
        out_shape=(jax.ShapeDtypeStruct((4, seq, CONV_WIDTH), BF16),
                   jax.ShapeDtypeStruct((SUBLANES, CONV_WIDTH), F32)),
        scratch_shapes=[pltpu.VMEM((seq + SUBLANES, LANES), F32), pltpu.VMEM((seq + SUBLANES, LANES), F32)],
        compiler_params=_compiler_params(48, ("arbitrary",)),
    )(proj, proj, proj, proj, dmix, conv_w)


def _tri_masks():
    r = lax.broadcasted_iota(jnp.int32, (TK, TK), 0)
    s = lax.broadcasted_iota(jnp.int32, (TK, TK), 1)
    return r, s


def _head_mask():
    lane = lax.broadcasted_iota(jnp.int32, (1, LANES), 1)
    return lane < HEAD_DIM


def _fill_tri(ref, ones):
    ref[...] = jnp.where(ones, 1.0, 0.0).astype(BF16)


def _tile_lanes(a):
    return jnp.tile(a, (1, TK // LANES))


DEAD_CARRY_BITS = 160.0


def _attn_fwd(qkv, z_attn):
    seq = qkv.shape[0]
    n_q = seq // TQ
    n_kb = seq // TK
    assert TQ == TK and n_kb <= LANES
    heads = range(HEADS_PER_STEP)

    def body(q_ref, k_ref, v_ref, za_ref, o_ref, mix_ref, car_ref, qq_s, vt_s, suffix_s, carry_s, diff_s, ot_s):
        head_a = _head_mask()
        r_i, s_i = _tri_masks()
        diff_s[...] = r_i - s_i
        _fill_tri(suffix_s, r_i > s_i)
        lane = lax.broadcasted_iota(jnp.int32, (1, LANES), 1)

        def prep(r, carry):
            rs = pl.ds(pl.multiple_of(r * TQ, TQ), TQ)
            q = q_ref[rs, :]
            qq_s[r, 0] = jnp.where(head_a, q, jnp.zeros_like(q))
            qq_s[r, 1] = jnp.where(head_a, jnp.zeros_like(q), q)
            vt_s[r] = v_ref[rs, :].astype(F32).T.astype(BF16)
            return carry

        lax.fori_loop(0, n_q, prep, 0)

        def block_pair(j, carry):
            qbs = [2 * j, 2 * j + 1]
            rows = [pl.ds(pl.multiple_of(qb * TQ, TQ), TQ) for qb in qbs]
            both = [qq_s[qb].reshape(HEADS_PER_STEP * TQ, LANES) for qb in qbs]
            carry_s[...] = jnp.zeros(carry_s.shape, F32)
            ot_s[...] = jnp.zeros(ot_s.shape, F32)
            for rs in rows:
                for h in heads:
                    car_ref[h, rs, :] = jnp.zeros((TQ, LANES), F32)

            def tiles(items):
                kbs = [qbs[s] - n for s, n in items]
                cols = [pl.ds(pl.multiple_of(jnp.maximum(kb, 0) * TK, TK), TK) for kb in kbs]
                zs = [_dot_nt(both[s], k_ref[cs, :]).reshape(HEADS_PER_STEP, TQ, TK) for (s, _), cs in zip(items, cols)]
                zms, cars, sufs = [], [], []
                for (s, n), kb, z in zip(items, kbs, zs):
                    mask = diff_s[...] > jnp.where(kb < 0, TK, jnp.where(n == 0, 0, -TK))
                    parts = []
                    for h in heads:
                        z2 = jnp.where(mask, z[h] * LOG2_E, MASKED_LOGIT)
                        sp = _softplus2(z2)
                        zms.append(z2 - sp)
                        parts += _split(sp)
                        car = carry_s[s, h]
                        cars.append(car)
                        car_ref[h, rows[s], :] = jnp.where(lane == kb, car, car_ref[h, rows[s], :])
                        carry_s[s, h] = car + jnp.sum(sp, axis=1, keepdims=True)
                    sufs += _block_sums(parts, suffix_s[...])
                for i, ((s, _), kb) in enumerate(zip(items, kbs)):
                    out = []
                    for h in heads:
                        j2 = i * HEADS_PER_STEP + h
                        w = jnp.exp2(zms[j2] - (sufs[j2] + _tile_lanes(cars[j2]))).astype(BF16)
                        out.append(_dot_nt(vt_s[jnp.maximum(kb, 0), h * HEAD_DIM:(h + 1) * HEAD_DIM, :], w))
                    ot_s[s] += jnp.concatenate(out, axis=0)

            tiles([(0, 0), (0, 1), (1, 0), (1, 1)])

            for s, (qb, rs) in enumerate(zip(qbs, rows)):
                def lowest_carry():
                    return jnp.min(jnp.minimum(carry_s[s, 0], carry_s[s, 1]))

                def live(state):
                    n, lowest = state
                    return jnp.logical_and(n <= qb, lowest < DEAD_CARRY_BITS)

                def step(state):
                    n, _ = state
                    tiles([(s, n)])
                    return n + 1, lowest_carry()

                n_done, _ = lax.while_loop(live, step, (jnp.int32(2), lowest_carry()))
                for h in heads:
                    car_ref[h, rs, :] = jnp.where(lane < qb + 1 - n_done, carry_s[s, h], car_ref[h, rs, :])
                o = ot_s[s].T
                o_ref[rs, :] = o
                za = za_ref[rs, :]
                mix_ref[rs, :] = ((za * _sigmoid(za)) * o).astype(BF16)
            return carry

        assert n_q % 2 == 0
        lax.fori_loop(0, n_q // 2, block_pair, 0)

    col = lambda g: pl.BlockSpec((seq, LANES), _col_block(g))
    n_pairs = ATTN_WIDTH // LANES
    return pl.pallas_call(
        body, name="attn_fwd",
        grid=(n_pairs,),
        in_specs=[col(0), col(1), col(2), col(0)],
        out_specs=(pl.BlockSpec((seq, LANES), lambda p: (0, p)),
                   pl.BlockSpec((seq, LANES), lambda p: (0, p)),
                   pl.BlockSpec((HEADS_PER_STEP, seq, LANES), lambda p: (p, 0, 0))),
        out_shape=(jax.ShapeDtypeStruct((seq, ATTN_WIDTH), F32),
                   jax.ShapeDtypeStruct((seq, ATTN_WIDTH), BF16),
                   jax.ShapeDtypeStruct((n_pairs * HEADS_PER_STEP, seq, LANES), F32)),
        scratch_shapes=[pltpu.VMEM((n_q, HEADS_PER_STEP, TQ, LANES), BF16),
                        pltpu.VMEM((n_kb, LANES, TK), BF16),
                        pltpu.VMEM((TK, TK), BF16),
                        pltpu.VMEM((2, HEADS_PER_STEP, TQ, LANES), F32),
                        pltpu.VMEM((TQ, TK), jnp.int32),
                        pltpu.VMEM((2, LANES, TQ), F32)],
        compiler_params=_compiler_params(48, ("arbitrary",)),
    )(qkv, qkv, qkv, z_attn)


def _attn_bwd(qkv, z_attn, o, dmix, carries):
    seq = qkv.shape[0]
    n_q = seq // TQ
    n_kb = seq // TK
    assert TQ == TK and n_kb <= LANES
    heads = range(HEADS_PER_STEP)

    def body(q_ref, k_ref, v_ref, za_ref, o_ref, dm_ref, car_ref, dp_ref,
             qq_s, dd_s, qt_s, dot_s, kt_s, suffix_s, prefix_s, ccar_s, dq_s, dk_s, dv_s, diff_s):
        head_a = _head_mask()
        r_i, s_i = _tri_masks()
        diff_s[...] = r_i - s_i
        _fill_tri(suffix_s, r_i > s_i)
        _fill_tri(prefix_s, r_i < s_i)
        lane = lax.broadcasted_iota(jnp.int32, (1, LANES), 1)

        def prep(r, carry):
            rs = pl.ds(pl.multiple_of(r * TQ, TQ), TQ)
            q = q_ref[rs, :]
            qq_s[r, 0] = jnp.where(head_a, q, jnp.zeros_like(q))
            qq_s[r, 1] = jnp.where(head_a, jnp.zeros_like(q), q)
            qt_s[r] = q.astype(F32).T.astype(BF16)
            kt_s[r] = k_ref[rs, :].astype(F32).T.astype(BF16)
            za = za_ref[rs, :]
            sg = _sigmoid(za)
            dm = dm_ref[rs, :]
            do = dm * (za * sg)
            dd_s[r, 0] = jnp.where(head_a, do, 0.0).astype(BF16)
            dd_s[r, 1] = jnp.where(head_a, 0.0, do).astype(BF16)
            dot_s[r] = do.T.astype(BF16)
            dp_ref[3, rs, :] = (dm * o_ref[rs, :] * (sg * (1.0 + za * (1.0 - sg)))).astype(BF16)
            dq_s[:, rs] = jnp.zeros((LANES, TQ), F32)
            dk_s[:, rs] = jnp.zeros((LANES, TQ), F32)
            dv_s[:, rs] = jnp.zeros((LANES, TQ), F32)
            return carry

        lax.fori_loop(0, n_q, prep, 0)

        def block_pair(j, carry):
            qbs = [2 * j, 2 * j + 1]
            rows = [pl.ds(pl.multiple_of(qb * TQ, TQ), TQ) for qb in qbs]
            both_q = [qq_s[qb].reshape(HEADS_PER_STEP * TQ, LANES) for qb in qbs]
            both_do = [dd_s[qb].reshape(HEADS_PER_STEP * TQ, LANES) for qb in qbs]
            ccar_s[...] = jnp.zeros(ccar_s.shape, F32)

            def tiles(items):
                cols = [pl.ds(pl.multiple_of(jnp.maximum(kb, 0) * TK, TK), TK) for _, kb in items]
                zs = [_dot_nt(both_q[s], k_ref[cs, :]).reshape(HEADS_PER_STEP, TQ, TK) for (s, _), cs in zip(items, cols)]
                dws = [_dot_nt(both_do[s], v_ref[cs, :]).reshape(HEADS_PER_STEP, TQ, TK)
                       for (s, _), cs in zip(items, cols)]
                zms, sufs = [], []
                for (s, kb), z in zip(items, zs):
                    mask = diff_s[...] > jnp.where(kb < 0, TK, jnp.where(kb == qbs[s], 0, -TK))
                    parts = []
                    for h in heads:
                        z2 = jnp.where(mask, z[h] * LOG2_E, MASKED_LOGIT)
                        sp = _softplus2(z2)
                        zms.append(z2 - sp)
                        parts += _split(sp)
                    sufs += _block_sums(parts, suffix_s[...])
                ws, dlws, pres = [], [], []
                for i, (s, kb) in enumerate(items):
                    parts = []
                    for h in heads:
                        j2 = i * HEADS_PER_STEP + h
                        cin = jnp.sum(jnp.where(lane == kb, car_ref[h, rows[s], :], 0.0), axis=1, keepdims=True)
                        w = jnp.exp2(zms[j2] - (sufs[j2] + cin))
                        ws.append(w.astype(BF16))
                        dlw = dws[i][h] * w
                        dlws.append(dlw)
                        parts += _split(dlw)
                    pres += _block_sums(parts, prefix_s[...])
                for i, ((s, kb), cs) in enumerate(zip(items, cols)):
                    dq_t, dk_t, dv_t = [], [], []
                    for h in heads:
                        j2 = i * HEADS_PER_STEP + h
                        ccar = ccar_s[s, h]
                        dz = (dlws[j2] - jnp.exp2(zms[j2]) * (dlws[j2] + (pres[j2] + _tile_lanes(ccar)))).astype(BF16)
                        ccar_s[s, h] = ccar + jnp.sum(dlws[j2], axis=1, keepdims=True)
                        dims = slice(h * HEAD_DIM, (h + 1) * HEAD_DIM)
                        dq_t.append(_dot_nt(kt_s[jnp.maximum(kb, 0), dims, :], dz))
                        dk_t.append(_dot(qt_s[qbs[s], dims, :], dz))
                        dv_t.append(_dot(dot_s[qbs[s], dims, :], ws[j2]))
                    dq_s[:, rows[s]] += jnp.concatenate(dq_t, axis=0)
                    dk_s[:, cs] += jnp.concatenate(dk_t, axis=0)
                    dv_s[:, cs] += jnp.concatenate(dv_t, axis=0)

            for s, (qb, rs) in enumerate(zip(qbs, rows)):
                lowest = jnp.min(jnp.minimum(car_ref[0, rs, :], car_ref[1, rs, :]), axis=0, keepdims=True)
                dead = jnp.logical_and(lowest >= DEAD_CARRY_BITS, lane < qb)
                first_live = jnp.sum(jnp.where(dead, 1, 0))

                def one(kb, c2):
                    tiles([(s, kb)])
                    return c2

                lax.fori_loop(first_live, qb - 1, one, 0)

            for s, qb in enumerate(qbs):
                tiles([(s, qb - 1), (s, qb)])
            return carry

        assert n_q % 2 == 0
        lax.fori_loop(0, n_q // 2, block_pair, 0)

        def finish(r, carry):
            rs = pl.ds(pl.multiple_of(r * TQ, TQ), TQ)
            dp_ref[0, rs, :] = (dq_s[:, rs].T * Q_SCALE).astype(BF16)
            dp_ref[1, rs, :] = dk_s[:, rs].T.astype(BF16)
            dp_ref[2, rs, :] = dv_s[:, rs].T.astype(BF16)
            return carry

        lax.fori_loop(0, n_q, finish, 0)

    def col(g, n_sub=CONV_WIDTH // LANES):
        return pl.BlockSpec((seq, LANES), _col_block(g, n_sub))

    n_pairs = ATTN_WIDTH // LANES
    by_head = lambda: pltpu.VMEM((n_q, HEADS_PER_STEP, TQ, LANES), BF16)
    transposed = lambda: pltpu.VMEM((n_q, LANES, TQ), BF16)
    acc_t = lambda: pltpu.VMEM((LANES, seq), F32)
    return pl.pallas_call(
        body, name="attn_bwd",
        grid=(n_pairs,),
        in_specs=[col(0), col(1), col(2), col(0), col(0), col(1),
                  pl.BlockSpec((HEADS_PER_STEP, seq, LANES), lambda p: (p, 0, 0))],
        out_specs=pl.BlockSpec((4, seq, LANES), lambda p: (0, 0, p)),
        out_shape=jax.ShapeDtypeStruct((4, seq, ATTN_WIDTH), BF16),
        scratch_shapes=[by_head(), by_head(), transposed(), transposed(), transposed(),
                        pltpu.VMEM((TK, TK), BF16), pltpu.VMEM((TK, TK), BF16),
                        pltpu.VMEM((2, HEADS_PER_STEP, TQ, LANES), F32),
                        acc_t(), acc_t(), acc_t(),
                        pltpu.VMEM((TQ, TK), jnp.int32)],
        compiler_params=_compiler_params(56, ("arbitrary",)),
    )(qkv, qkv, qkv, z_attn, o, dmix, carries)


def _out_ln(mix_c, mix_a, woutg, x, target, gain, bias):
    seq = x.shape[0]
    tm = 512
    inv_d = 1.0 / D_MODEL

    def body(mc_ref, ma_ref, wo_ref, x_ref, t_ref, g_ref, b_ref, dxa_ref, dmix_ref, gw_ref, st_ref):
        @pl.when(pl.program_id(0) == 0)
        def _():
            gw_ref[...] = jnp.zeros(gw_ref.shape, F32)
            st_ref[...] = jnp.zeros(st_ref.shape, F32)

        mc = mc_ref[...]
        ma = ma_ref[...]
        sub = _dot(mc, wo_ref[0:CONV_WIDTH, :]) + _dot(ma, wo_ref[CONV_WIDTH:, :])
        r = DEEPNORM_ALPHA * x_ref[...] + sub
        mu = jnp.mean(r, axis=-1, keepdims=True)
        rc = r - mu
        var = jnp.mean(rc * rc, axis=-1, keepdims=True)
        rstd = lax.rsqrt(var + LN_EPS)
        xhat = rc * rstd
        gain_v = g_ref[...]
        diff = (xhat * gain_v + b_ref[...]) - t_ref[...]
        dy = diff * inv_d
        st_ref[0:1, :] += jnp.sum(dy * xhat, axis=0, keepdims=True)
        st_ref[1:2, :] += jnp.sum(dy, axis=0, keepdims=True)
        st_ref[2:3, :] += jnp.sum(diff * diff, axis=0, keepdims=True)
        dxh = dy * gain_v
        m1 = jnp.mean(dxh, axis=-1, keepdims=True)
        m2 = jnp.mean(dxh * xhat, axis=-1, keepdims=True)
        dr = rstd * (dxh - m1 - xhat * m2)
        dxa_ref[...] = DEEPNORM_ALPHA * dr
        drb = dr.astype(BF16)
        dmix_ref[...] = _dot_nt(drb, wo_ref[...])
        gw_ref[0:CONV_WIDTH, :] += _dot_tn(mc, drb)
        gw_ref[CONV_WIDTH:, :] += _dot_tn(ma, drb)

    rows_d = lambda: pl.BlockSpec((tm, D_MODEL), lambda i: (i, 0))
    rows_h = lambda: pl.BlockSpec((tm, CONV_WIDTH), lambda i: (i, 0))
    whole = lambda shape: pl.BlockSpec(shape, lambda i: (0, 0))
    return pl.pallas_call(
        body, name="out_ln",
        grid=(seq // tm,),
        in_specs=[rows_h(), rows_h(), whole((D_MODEL, D_MODEL)), rows_d(), rows_d(),
                  whole((1, D_MODEL)), whole((1, D_MODEL))],
        out_specs=(rows_d(), rows_d(), whole((D_MODEL, D_MODEL)), whole((SUBLANES, D_MODEL))),
        out_shape=(jax.ShapeDtypeStruct((seq, D_MODEL), F32), jax.ShapeDtypeStruct((seq, D_MODEL), F32),
                   jax.ShapeDtypeStruct((D_MODEL, D_MODEL), F32), jax.ShapeDtypeStruct((SUBLANES, D_MODEL), F32)),
        compiler_params=_compiler_params(48, ("arbitrary",)),
    )(mix_c, mix_a, woutg, x, target, gain, bias)


def _group_maps():
    half = N_GROUPS // 2
    return (lambda g: jnp.minimum(g, half - 1)), (lambda g: jnp.maximum(g - half, 0))


def _grad_w_in(xt, dpc, dpa):
    seq = xt.shape[1]
    half = N_GROUPS // 2
    in_conv, in_attn = _group_maps()

    def body(xt_ref, dc_ref, da_ref, out_ref):
        g = pl.program_id(0)

        @pl.when(g < half)
        def _():
            out_ref[...] = _dot(xt_ref[...], dc_ref[...])

        @pl.when(g >= half)
        def _():
            out_ref[...] = _dot(xt_ref[...], da_ref[...])

    return pl.pallas_call(
        body, name="grad_w_in",
        grid=(N_GROUPS,),
        in_specs=[pl.BlockSpec((D_MODEL, seq), lambda g: (0, 0), pipeline_mode=pl.Buffered(1)),
                  pl.BlockSpec((None, seq, CONV_WIDTH), lambda g: (in_conv(g), 0, 0)),
                  pl.BlockSpec((None, seq, CONV_WIDTH), lambda g: (in_attn(g), 0, 0))],
        out_specs=pl.BlockSpec((None, D_MODEL, CONV_WIDTH), lambda g: (g // 2, 0, g % 2)),
        out_shape=jax.ShapeDtypeStruct((N_CHIPS, D_MODEL, D_MODEL), F32),
        compiler_params=_compiler_params(48, ("arbitrary",)),
    )(xt, dpc, dpa)


def _grad_x(dxa, dpc, dpa, wing):
    seq = dxa.shape[0]
    tm = 512
    half = N_GROUPS // 2

    def body(dxa_ref, dc_ref, da_ref, w_ref, out_ref):
        acc = dxa_ref[...]
        for g in range(N_GROUPS):
            dp = dc_ref[g] if g < half else da_ref[g - half]
            cols = slice((g % 2) * CONV_WIDTH, (g % 2 + 1) * CONV_WIDTH)
            acc = acc + _dot_nt(dp, w_ref[g // 2, :, cols])
        out_ref[...] = acc

    return pl.pallas_call(
        body, name="grad_x",
        grid=(seq // tm,),
        in_specs=[pl.BlockSpec((tm, D_MODEL), lambda i: (i, 0)),
                  pl.BlockSpec((half, tm, CONV_WIDTH), lambda i: (0, i, 0)),
                  pl.BlockSpec((half, tm, CONV_WIDTH), lambda i: (0, i, 0)),
                  pl.BlockSpec((N_CHIPS, D_MODEL, D_MODEL), lambda i: (0, 0, 0), pipeline_mode=pl.Buffered(1))],
        out_specs=pl.BlockSpec((tm, D_MODEL), lambda i: (i, 0)),
        out_shape=jax.ShapeDtypeStruct((seq, D_MODEL), F32),
        compiler_params=_compiler_params(40, ("arbitrary",)),
    )(dxa, dpc, dpa, wing)


PACK_LOSS_ROW = 5


def _reduce_grads(gwin, gwout, pack):
    n_shards, n_rows_in, n_cols = gwin.shape
    n_rows_out = gwout.shape[1]
    half_rows = (n_rows_in // 2, n_rows_out // 2)
    row_chunk = 128

    def body(gwin_ref, gwout_ref, pack_ref, oin_ref, oout_ref, opack_ref,
             mine_in, mine_out, sib_in, sib_out, wire_in, wire_out, rin, rout, rpack,
             local_sems, sib_send, sib_recv, send_sems, recv_sems, pack_send, pack_recv, fin_send, fin_recv):
        x, y, c = lax.axis_index("x"), lax.axis_index("y"), lax.axis_index("c")
        me = 2 * x + y
        my_id = 4 * x + 2 * y + c
        chips = [(1 - x, y), (x, 1 - y), (1 - x, 1 - y)]
        sibling = (x, y, 1 - c)
        partial = (gwin_ref, gwout_ref)
        mine = (mine_in, mine_out)
        from_sib = (sib_in, sib_out)
        wire = (wire_in, wire_out)
        from_chips = (rin, rout)
        result = (oin_ref, oout_ref)

        def half(a, which):
            rows = half_rows[a]
            return pl.ds(pl.multiple_of(which * rows, rows), rows)

        order = [2 * px + py for px, py in chips] + [me]

        def swap_copy(i, a):
            shard = order[i]
            return pltpu.make_async_remote_copy(
                src_ref=partial[a].at[shard, half(a, 1 - c), :], dst_ref=from_sib[a].at[shard],
                send_sem=sib_send.at[2 * i + a], recv_sem=sib_recv.at[2 * i + a],
                device_id=sibling, device_id_type=MESH)

        sends = []
        loads = []
        for a in range(2):
            ld = pltpu.make_async_copy(partial[a].at[:, half(a, c), :], mine[a], local_sems.at[a])
            ld.start()
            loads.append(ld)
        for i in range(n_shards):
            for a in range(2):
                cp = swap_copy(i, a)
                cp.start()
                sends.append(cp)

        rpack[my_id] = pack_ref[...]

        def pack_copy(rel, slot, to):
            return pltpu.make_async_remote_copy(
                src_ref=rpack.at[slot], dst_ref=rpack.at[slot],
                send_sem=pack_send.at[rel - 1], recv_sem=pack_recv.at[rel - 1],
                device_id=to, device_id_type=MESH)

        def related(rel):
            px = (1 - x) if rel & 4 else x
            py = (1 - y) if rel & 2 else y
            pc = (1 - c) if rel & 1 else c
            return px, py, pc

        for rel in range(1, N_DEVICES):
            cp = pack_copy(rel, my_id, related(rel))
            cp.start()
            sends.append(cp)

        for ld in loads:
            ld.wait()

        def chip_sum(a, shard):
            def add(r, carry):
                rs = pl.ds(pl.multiple_of(r * row_chunk, row_chunk), row_chunk)
                mine[a][shard, rs, :] = mine[a][shard, rs, :] + from_sib[a][shard, rs, :]
                return carry

            lax.fori_loop(0, half_rows[a] // row_chunk, add, 0)

        def chip_sum_to_wire(a, shard, k):
            def add(r, carry):
                rs = pl.ds(pl.multiple_of(r * row_chunk, row_chunk), row_chunk)
                wire[a][k, rs, :] = (mine[a][shard, rs, :] + from_sib[a][shard, rs, :]).astype(BF16)
                return carry

            lax.fori_loop(0, half_rows[a] // row_chunk, add, 0)

        def shard_copy(k, a, to):
            return pltpu.make_async_remote_copy(
                src_ref=wire[a].at[k], dst_ref=from_chips[a].at[k],
                send_sem=send_sems.at[2 * k + a], recv_sem=recv_sems.at[2 * k + a],
                device_id=to, device_id_type=MESH)

        for k, (px, py) in enumerate(chips):
            for a in range(2):
                swap_copy(k, a).wait_recv()
                chip_sum_to_wire(a, 2 * px + py, k)
                cp = shard_copy(k, a, (px, py, c))
                cp.start()
                sends.append(cp)
        for a in range(2):
            swap_copy(n_shards - 1, a).wait_recv()
            chip_sum(a, me)

        for rel in range(1, N_DEVICES):
            px, py, pc = related(rel)
            pack_copy(rel, 4 * px + 2 * py + pc, (px, py, pc)).wait_recv()
        total = rpack[0]
        for i in range(1, N_DEVICES):
            total = total + rpack[i]
        opack_ref[...] = total
        sq_err = jnp.sum(total[PACK_LOSS_ROW:PACK_LOSS_ROW + 1, :], axis=1, keepdims=True)
        opack_ref[PACK_LOSS_ROW:PACK_LOSS_ROW + 1, :] = jnp.broadcast_to(sq_err * (0.5 / D_MODEL), (1, n_cols))

        for k, (px, py) in enumerate(chips):
            for a in range(2):
                shard_copy(k, a, (px, py, c)).wait_recv()

        def finish(a):
            def add(r, carry):
                rs = pl.ds(pl.multiple_of(r * row_chunk, row_chunk), row_chunk)
                dst = pl.ds(pl.multiple_of(c * half_rows[a] + r * row_chunk, row_chunk), row_chunk)
                result[a][dst, :] = ((mine[a][me, rs, :] + from_chips[a][0, rs, :].astype(F32))
                                     + from_chips[a][1, rs, :].astype(F32)) + from_chips[a][2, rs, :].astype(F32)
                return carry

            lax.fori_loop(0, half_rows[a] // row_chunk, add, 0)

        def final_copy(a, which):
            ref = result[a].at[half(a, which), :]
            return pltpu.make_async_remote_copy(
                src_ref=ref, dst_ref=ref, send_sem=fin_send.at[a], recv_sem=fin_recv.at[a],
                device_id=sibling, device_id_type=MESH)

        for a in range(2):
            finish(a)
            cp = final_copy(a, c)
            cp.start()
            sends.append(cp)
        for a in range(2):
            final_copy(a, 1 - c).wait_recv()
        for cp in sends:
            cp.wait_send()

    vmem = pl.BlockSpec(memory_space=pltpu.VMEM)
    hbm = pl.BlockSpec(memory_space=pl.ANY)
    in_shape = (n_rows_in, n_cols)
    out_shape = (n_rows_out, n_cols)
    half_in = (half_rows[0], n_cols)
    half_out = (half_rows[1], n_cols)
    return pl.pallas_call(
        body, name="reduce_grads",
        out_shape=(jax.ShapeDtypeStruct(in_shape, F32), jax.ShapeDtypeStruct(out_shape, F32),
                   jax.ShapeDtypeStruct(pack.shape, F32)),
        in_specs=[hbm, hbm, vmem], out_specs=(vmem, vmem, vmem),
        scratch_shapes=[pltpu.VMEM((n_shards,) + half_in, F32), pltpu.VMEM((n_shards,) + half_out, F32),
                        pltpu.VMEM((n_shards,) + half_in, F32), pltpu.VMEM((n_shards,) + half_out, F32),
                        pltpu.VMEM((N_CHIPS - 1,) + half_in, BF16), pltpu.VMEM((N_CHIPS - 1,) + half_out, BF16),
                        pltpu.VMEM((N_CHIPS - 1,) + half_in, BF16), pltpu.VMEM((N_CHIPS - 1,) + half_out, BF16),
                        pltpu.VMEM((N_DEVICES,) + pack.shape, F32),
                        pltpu.SemaphoreType.DMA((2,)),
                        pltpu.SemaphoreType.DMA((2 * n_shards,)), pltpu.SemaphoreType.DMA((2 * n_shards,)),
                        pltpu.SemaphoreType.DMA((6,)), pltpu.SemaphoreType.DMA((6,)),
                        pltpu.SemaphoreType.DMA((N_DEVICES - 1,)), pltpu.SemaphoreType.DMA((N_DEVICES - 1,)),
                        pltpu.SemaphoreType.DMA((2,)), pltpu.SemaphoreType.DMA((2,))],
        compiler_params=_compiler_params(48),
    )(gwin, gwout, pack)


def _adamw(name, w, g, m, v):
    n_rows, n_cols = w.shape
    tr = 256 if n_rows % 256 == 0 else n_rows
    m_corr = 1.0 - ADAM_B1 ** ADAM_STEP
    v_corr = 1.0 - ADAM_B2 ** ADAM_STEP

    def body(w_ref, g_ref, m_ref, v_ref, d_ref, nm_ref, nv_ref):
        gv = g_ref[...]
        nm = ADAM_B1 * m_ref[...] + (1.0 - ADAM_B1) * gv
        nv = ADAM_B2 * v_ref[...] + (1.0 - ADAM_B2) * (gv * gv)
        nm_ref[...] = nm
        nv_ref[...] = nv
        d_ref[...] = -ADAM_LR * ((nm / m_corr) / (jnp.sqrt(nv / v_corr) + ADAM_EPS) + ADAM_WD * w_ref[...])

    blk = lambda: pl.BlockSpec((tr, n_cols), lambda i: (i, 0))
    shape = jax.ShapeDtypeStruct(w.shape, F32)
    return pl.pallas_call(
        body, name=name,
        grid=(n_rows // tr,),
        in_specs=[blk(), blk(), blk(), blk()], out_specs=(blk(), blk(), blk()),
        out_shape=(shape, shape, shape),
        compiler_params=_compiler_params(32, ("arbitrary",)),
    )(w, g, m, v)


def kernel(x, w_in, conv_w, w_out, ln_gain, ln_bias, loss_target, m_w_in, m_conv_w, m_w_out, m_ln_gain, m_ln_bias, v_w_in, v_conv_w, v_w_out, v_ln_gain, v_ln_bias):
    xs = x[0]
    target = loss_target[0]

    wing, woutg, cwg = _gather_weights(w_in[0], w_out[0], conv_w[0])
    conv_full = jnp.transpose(cwg, (1, 0, 2)).reshape(3, CONV_WIDTH)
    wout_full = woutg.reshape(D_MODEL, D_MODEL)

    p_conv, qkv, z_attn, xt = _proj(xs, wing)
    mix_c = _conv_fwd(p_conv, conv_full)
    o, mix_a, carries = _attn_fwd(qkv, z_attn)
    dxa, dmix, gwout, stats = _out_ln(mix_c, mix_a, wout_full, xs, target, ln_gain, ln_bias)
    dpc, dconv = _conv_bwd(p_conv, dmix, conv_full)
    dpa = _attn_bwd(qkv, z_attn, o, dmix, carries)
    gwin = _grad_w_in(xt, dpc, dpa)
    grad_x = _grad_x(dxa, dpc, dpa, wing)

    pack = jnp.concatenate(
        [stats[0:2], jnp.pad(dconv[0:3], ((0, 0), (0, D_MODEL - CONV_WIDTH))), stats[2:3],
         jnp.zeros((2, D_MODEL), F32)], axis=0)
    g_w_in, g_w_out, tot = _reduce_grads(gwin, gwout.reshape(N_CHIPS, D_MODEL // N_CHIPS, D_MODEL), pack)

    chip = 2 * lax.axis_index("x") + lax.axis_index("y")
    g_gain = tot[0:1]
    g_bias = tot[1:2]
    g_conv = lax.dynamic_slice(tot, (2, chip * LANES), (3, LANES))
    loss = tot[PACK_LOSS_ROW, 0]

    d_w_in, nm_w_in, nv_w_in = _adamw("adamw_w_in", w_in[0], g_w_in, m_w_in[0], v_w_in[0])
    d_w_out, nm_w_out, nv_w_out = _adamw("adamw_w_out", w_out[0], g_w_out, m_w_out[0], v_w_out[0])
    d_conv, nm_conv, nv_conv = _adamw("adamw_conv_w", conv_w[0], g_conv, m_conv_w[0], v_conv_w[0])
    d_gain, nm_gain, nv_gain = _adamw("adamw_ln_gain", ln_gain, g_gain, m_ln_gain, v_ln_gain)
    d_bias, nm_bias, nv_bias = _adamw("adamw_ln_bias", ln_bias, g_bias, m_ln_bias, v_ln_bias)

    lead = lambda a: a[None]
    return (loss, lead(grad_x),
            lead(g_w_in), lead(g_conv), lead(g_w_out), g_gain, g_bias,
            lead(d_w_in), lead(d_conv), lead(d_w_out), d_gain, d_bias,
            lead(nm_w_in), lead(nm_conv), lead(nm_w_out), nm_gain, nm_bias,
            lead(nv_w_in), lead(nv_conv), lead(nv_w_out), nv_gain, nv_bias)
```

```python
import jax
import jax.numpy as jnp
from jax import lax
from jax.experimental import pallas as pl
from jax.experimental.pallas import tpu as pltpu

F32 = jnp.float32
BF16 = jnp.bfloat16
MESH = pl.DeviceIdType.MESH

D_MODEL = 1024
CONV_WIDTH = 512
ATTN_WIDTH = 512
HEAD_DIM = 64
N_GROUPS = 8
N_CHIPS = 4
N_DEVICES = 8
LN_EPS = 1e-5
DEEPNORM_ALPHA = 2.0 ** 0.25
Q_SCALE = HEAD_DIM ** -0.5
ADAM_LR = 0.001
ADAM_B1 = 0.9
ADAM_B2 = 0.999
ADAM_EPS = 1e-08
ADAM_WD = 0.01
ADAM_STEP = 10

LANES = 128
SUBLANES = 8
V7X_VMEM_BYTES = 64 * 1024 * 1024
MIB = 1024 * 1024

TQ = 256
TK = 256
HEADS_PER_STEP = LANES // HEAD_DIM
CONV_ROWS = 512


def _compiler_params(vmem_mib, semantics=None):
    assert vmem_mib * MIB < V7X_VMEM_BYTES
    return pltpu.CompilerParams(dimension_semantics=semantics, vmem_limit_bytes=vmem_mib * MIB)


def _sigmoid(z):
    return 1.0 / (1.0 + jnp.exp(-z))


def _dot(a, b):
    return jnp.dot(a, b, preferred_element_type=F32)


def _dot_nt(a, b):
    return lax.dot_general(a, b, (((1,), (1,)), ((), ())), preferred_element_type=F32)


def _dot_tn(a, b):
    return lax.dot_general(a, b, (((0,), (0,)), ((), ())), preferred_element_type=F32)


def _truncate_to_bf16(a):
    bits = lax.bitcast_convert_type(a, jnp.uint32) & jnp.uint32(0xFFFF0000)
    return lax.bitcast_convert_type(bits, F32)


def _split(a):
    hi = _truncate_to_bf16(a)
    return [hi.astype(BF16), (a - hi).astype(BF16)]


def _block_sums(parts, tri):
    n = len(parts) // 2
    res = _dot(jnp.concatenate(parts, axis=0), tri).reshape(n, 2, TQ, TK)
    return [res[i, 0] + res[i, 1] for i in range(n)]


MASKED_LOGIT = -1e30
LOG2_E = 1.4426950408889634


def _softplus2(z2):
    return jnp.maximum(z2, 0.0) + jnp.log2(1.0 + jnp.exp2(-jnp.abs(z2)))


def _gather_weights(w_in, w_out, conv_w):
    d_rows, d_cols = w_in.shape
    o_rows, o_cols = w_out.shape
    half_rows = (d_rows // 2, o_rows // 2)
    row_chunk = 128

    def body(win_ref, wout_ref, cw_ref, wing_ref, woutg_ref, cwg_ref, cast_in, cast_out,
             send_sems, recv_sems, pass_send, pass_recv, store_sems):
        x, y, c = lax.axis_index("x"), lax.axis_index("y"), lax.axis_index("c")
        me = 2 * x + y
        srcs = (win_ref, wout_ref)
        bufs = (wing_ref, woutg_ref)
        casts = (cast_in, cast_out)

        def rows_of(a, which):
            rows = half_rows[a]
            return pl.ds(pl.multiple_of(which * rows, rows), rows)

        def half(a, shard, which):
            return bufs[a].at[shard, rows_of(a, which), :]

        def cast_half(which):
            for a in range(2):
                def cast(r, carry):
                    rows = pl.ds(pl.multiple_of(which * half_rows[a] + r * row_chunk, row_chunk), row_chunk)
                    casts[a][rows, :] = srcs[a][rows, :].astype(BF16)
                    return carry

                lax.fori_loop(0, half_rows[a] // row_chunk, cast, 0)

        chips = [(1 - x, y), (x, 1 - y), (1 - x, 1 - y)]

        def ici_copy(k, a, shard, to):
            if a == 2:
                src = dst = cwg_ref.at[shard]
            else:
                src, dst = casts[a].at[rows_of(a, c), :], half(a, shard, c)
            return pltpu.make_async_remote_copy(
                src_ref=src, dst_ref=dst, send_sem=send_sems.at[3 * k + a], recv_sem=recv_sems.at[3 * k + a],
                device_id=to, device_id_type=MESH)

        def pass_copy(k, a, shard, which):
            ref = half(a, shard, which)
            return pltpu.make_async_remote_copy(
                src_ref=ref, dst_ref=ref, send_sem=pass_send.at[2 * k + a], recv_sem=pass_recv.at[2 * k + a],
                device_id=(x, y, 1 - c), device_id_type=MESH)

        cast_half(c)
        cwg_ref[me] = cw_ref[...]
        sends = []
        for k, (px, py) in enumerate(chips):
            for a in range(3):
                cp = ici_copy(k, a, me, (px, py, c))
                cp.start()
                sends.append(cp)
        cast_half(1 - c)
        stores = []
        for a in range(2):
            st = pltpu.make_async_copy(casts[a], bufs[a].at[me], store_sems.at[a])
            st.start()
            stores.append(st)
        for k, (px, py) in enumerate(chips):
            for a in range(2):
                ici_copy(k, a, 2 * px + py, (px, py, c)).wait_recv()
                cp = pass_copy(k, a, 2 * px + py, c)
                cp.start()
                sends.append(cp)
        for k, (px, py) in enumerate(chips):
            ici_copy(k, 2, 2 * px + py, (px, py, c)).wait_recv()
            for a in range(2):
                pass_copy(k, a, 2 * px + py, 1 - c).wait_recv()
        for cp in sends:
            cp.wait_send()
        for st in stores:
            st.wait()

    vmem = pl.BlockSpec(memory_space=pltpu.VMEM)
    hbm = pl.BlockSpec(memory_space=pl.ANY)
    return pl.pallas_call(
        body, name="gather_weights",
        out_shape=(jax.ShapeDtypeStruct((N_CHIPS, d_rows, d_cols), BF16),
                   jax.ShapeDtypeStruct((N_CHIPS, o_rows, o_cols), BF16),
                   jax.ShapeDtypeStruct((N_CHIPS,) + conv_w.shape, F32)),
        in_specs=[vmem, vmem, vmem], out_specs=(hbm, hbm, vmem),
        scratch_shapes=[pltpu.VMEM((d_rows, d_cols), BF16), pltpu.VMEM((o_rows, o_cols), BF16),
                        pltpu.SemaphoreType.DMA((9,)), pltpu.SemaphoreType.DMA((9,)),
                        pltpu.SemaphoreType.DMA((6,)), pltpu.SemaphoreType.DMA((6,)),
                        pltpu.SemaphoreType.DMA((2,))],
        compiler_params=_compiler_params(32),
    )(w_in, w_out, conv_w)


def _proj(x, wing):
    seq = x.shape[0]
    tm = 512
    w = CONV_WIDTH

    def body(x_ref, w_ref, conv_ref, qkv_ref, za_ref, xt_ref):
        xv = x_ref[...]
        xb = xv.astype(BF16)
        for j in range(2):
            conv_ref[:, j * D_MODEL:(j + 1) * D_MODEL] = _dot(xb, w_ref[j])
        qk = _dot(xb, w_ref[2])
        qkv_ref[:, 0:w] = (qk[:, 0:w] * Q_SCALE).astype(BF16)
        qkv_ref[:, w:2 * w] = qk[:, w:].astype(BF16)
        vz = _dot(xb, w_ref[3])
        qkv_ref[:, 2 * w:] = vz[:, 0:w].astype(BF16)
        za_ref[...] = vz[:, w:]
        xt_ref[...] = xv.T.astype(BF16)

    rows = lambda width: pl.BlockSpec((tm, width), lambda i: (i, 0))
    return pl.pallas_call(
        body, name="proj",
        grid=(seq // tm,),
        in_specs=[rows(D_MODEL),
                  pl.BlockSpec((N_CHIPS, D_MODEL, D_MODEL), lambda i: (0, 0, 0), pipeline_mode=pl.Buffered(1))],
        out_specs=(rows(4 * w), rows(3 * w), rows(w), pl.BlockSpec((D_MODEL, tm), lambda i: (0, i))),
        out_shape=(jax.ShapeDtypeStruct((seq, 4 * w), F32), jax.ShapeDtypeStruct((seq, 3 * w), BF16),
                   jax.ShapeDtypeStruct((seq, w), F32), jax.ShapeDtypeStruct((D_MODEL, seq), BF16)),
        compiler_params=_compiler_params(48, ("arbitrary",)),
    )(x, wing)


def _col_block(group, n_sub=CONV_WIDTH // LANES):
    return lambda j: (0, group * n_sub + j)


def _shift_down(ext, k, rows):
    return pltpu.roll(ext, k, 0)[SUBLANES:, :]


def _shift_up(ext, k, rows):
    return pltpu.roll(ext, rows + SUBLANES - k, 0)[:rows, :]


def _conv_fwd(proj, conv_w):
    seq = proj.shape[0]
    rows = CONV_ROWS
    n_chunks = seq // rows

    def body(b_ref, c_ref, h_ref, z_ref, w_ref, out_ref, u_s):
        u_s[0:SUBLANES, :] = jnp.zeros((SUBLANES, LANES), F32)

        def fill(r, carry):
            rs = pl.ds(pl.multiple_of(r * rows, rows), rows)
            u_s[pl.ds(pl.multiple_of(r * rows + SUBLANES, SUBLANES), rows), :] = c_ref[rs, :] * h_ref[rs, :]
            return carry

        lax.fori_loop(0, n_chunks, fill, 0)
        w = w_ref[...]

        def chunk(r, carry):
            r0 = pl.multiple_of(r * rows, rows)
            rs = pl.ds(r0, rows)
            ext = u_s[pl.ds(r0, rows + SUBLANES), :]
            u = ext[SUBLANES:, :]
            y = w[2:3, :] * u
            y = y + w[0:1, :] * _shift_down(ext, 2, rows)
            y = y + w[1:2, :] * _shift_down(ext, 1, rows)
            z = z_ref[rs, :]
            out_ref[rs, :] = ((z * _sigmoid(z)) * (b_ref[rs, :] * y)).astype(BF16)
            return carry

        lax.fori_loop(0, n_chunks, chunk, 0)

    col = lambda g: pl.BlockSpec((seq, LANES), _col_block(g))
    return pl.pallas_call(
        body, name="conv_fwd",
        grid=(CONV_WIDTH // LANES,),
        in_specs=[col(0), col(1), col(2), col(3), pl.BlockSpec((3, LANES), lambda j: (0, j))],
        out_specs=pl.BlockSpec((seq, LANES), lambda j: (0, j)),
        out_shape=jax.ShapeDtypeStruct((seq, CONV_WIDTH), BF16),
        scratch_shapes=[pltpu.VMEM((seq + SUBLANES, LANES), F32)],
        compiler_params=_compiler_params(40, ("arbitrary",)),
    )(proj, proj, proj, proj, conv_w)


def _conv_bwd(proj, dmix, conv_w):
    seq = proj.shape[0]
    rows = CONV_ROWS
    n_chunks = seq // rows

    def body(b_ref, c_ref, h_ref, z_ref, d_ref, w_ref, dp_ref, dw_ref, u_s, dy_s):
        u_s[0:SUBLANES, :] = jnp.zeros((SUBLANES, LANES), F32)
        dy_s[seq:seq + SUBLANES, :] = jnp.zeros((SUBLANES, LANES), F32)

        def fill(r, carry):
            r0 = pl.multiple_of(r * rows, rows)
            rs = pl.ds(r0, rows)
            u_s[pl.ds(pl.multiple_of(r0 + SUBLANES, SUBLANES), rows), :] = c_ref[rs, :] * h_ref[rs, :]
            z = z_ref[rs, :]
            dy_s[rs, :] = d_ref[rs, :] * (z * _sigmoid(z)) * b_ref[rs, :]
            return carry

        lax.fori_loop(0, n_chunks, fill, 0)
        w = w_ref[...]

        def chunk(r, acc):
            r0 = pl.multiple_of(r * rows, rows)
            rs = pl.ds(r0, rows)
            ext = u_s[pl.ds(r0, rows + SUBLANES), :]
            u = ext[SUBLANES:, :]
            um1 = _shift_down(ext, 1, rows)
            um2 = _shift_down(ext, 2, rows)
            y = w[2:3, :] * u
            y = y + w[0:1, :] * um2
            y = y + w[1:2, :] * um1
            z = z_ref[rs, :]
            b = b_ref[rs, :]
            dco = d_ref[rs, :]
            sg = _sigmoid(z)
            g = z * sg
            dp_ref[0, rs, :] = (dco * g * y).astype(BF16)
            dp_ref[3, rs, :] = (dco * b * y * (sg * (1.0 + z * (1.0 - sg)))).astype(BF16)
            ext_dy = dy_s[pl.ds(r0, rows + SUBLANES), :]
            dy = ext_dy[:rows, :]
            du = w[2:3, :] * dy + w[1:2, :] * _shift_up(ext_dy, 1, rows) + w[0:1, :] * _shift_up(ext_dy, 2, rows)
            dp_ref[1, rs, :] = (du * h_ref[rs, :]).astype(BF16)
            dp_ref[2, rs, :] = (du * c_ref[rs, :]).astype(BF16)
            a0, a1, a2 = acc
            return (a0 + jnp.sum(dy * um2, axis=0, keepdims=True),
                    a1 + jnp.sum(dy * um1, axis=0, keepdims=True),
                    a2 + jnp.sum(dy * u, axis=0, keepdims=True))

        zero = jnp.zeros((1, LANES), F32)
        a0, a1, a2 = lax.fori_loop(0, n_chunks, chunk, (zero, zero, zero))
        dw_ref[...] = jnp.concatenate([a0, a1, a2, jnp.zeros((SUBLANES - 3, LANES), F32)], axis=0)

    col = lambda g: pl.BlockSpec((seq, LANES), _col_block(g))
    return pl.pallas_call(
        body, name="conv_bwd",
        grid=(CONV_WIDTH // LANES,),
        in_specs=[col(0), col(1), col(2), col(3), col(0), pl.BlockSpec((3, LANES), lambda j: (0, j))],
        out_specs=(pl.BlockSpec((4, seq, LANES), lambda j: (0, 0, j)),
                   pl.BlockSpec((SUBLANES, LANES), lambda j: (0, j))),
        out_shape=(jax.ShapeDtypeStruct((4, seq, CONV_WIDTH), BF16),
                   jax.ShapeDtypeStruct((SUBLANES, CONV_WIDTH), F32)),
        scratch_shapes=[pltpu.VMEM((seq + SUBLANES, LANES), F32), pltpu.VMEM((seq + SUBLANES, LANES), F32)],
        compiler_params=_compiler_params(48, ("arbitrary",)),
    )(proj, proj, proj, proj, dmix, conv_w)


def _tri_masks():
    r = lax.broadcasted_iota(jnp.int32, (TK, TK), 0)
    s = lax.broadcasted_iota(jnp.int32, (TK, TK), 1)
    return r, s


def _head_mask():
    lane = lax.broadcasted_iota(jnp.int32, (1, LANES), 1)
    return lane < HEAD_DIM


def _fill_tri(ref, ones):
    ref[...] = jnp.where(ones, 1.0, 0.0).astype(BF16)


def _tile_lanes(a):
    return jnp.tile(a, (1, TK // LANES))


DEAD_CARRY_BITS = 160.0


def _attn_fwd(qkv, z_attn):
    seq = qkv.shape[0]
    n_q = seq // TQ
    n_kb = seq // TK
    assert TQ == TK and n_kb <= LANES
    heads = range(HEADS_PER_STEP)

    def body(q_ref, k_ref, v_ref, za_ref, o_ref, mix_ref, car_ref, qq_s, vt_s, suffix_s, carry_s, diff_s, ot_s):
        head_a = _head_mask()
        r_i, s_i = _tri_masks()
        diff_s[...] = r_i - s_i
        _fill_tri(suffix_s, r_i > s_i)
        lane = lax.broadcasted_iota(jnp.int32, (1, LANES), 1)

        def prep(r, carry):
            rs = pl.ds(pl.multiple_of(r * TQ, TQ), TQ)
            q = q_ref[rs, :]
            qq_s[r, 0] = jnp.where(head_a, q, jnp.zeros_like(q))
            qq_s[r, 1] = jnp.where(head_a, jnp.zeros_like(q), q)
            vt_s[r] = v_ref[rs, :].astype(F32).T.astype(BF16)
            return carry

        lax.fori_loop(0, n_q, prep, 0)

        def block_pair(j, carry):
            qbs = [2 * j, 2 * j + 1]
            rows = [pl.ds(pl.multiple_of(qb * TQ, TQ), TQ) for qb in qbs]
            both = [qq_s[qb].reshape(HEADS_PER_STEP * TQ, LANES) for qb in qbs]
            carry_s[...] = jnp.zeros(carry_s.shape, F32)
            ot_s[...] = jnp.zeros(ot_s.shape, F32)
            for rs in rows:
                for h in heads:
                    car_ref[h, rs, :] = jnp.zeros((TQ, LANES), F32)

            def tiles(items):
                kbs = [qbs[s] - n for s, n in items]
                cols = [pl.ds(pl.multiple_of(jnp.maximum(kb, 0) * TK, TK), TK) for kb in kbs]
                zs = [_dot_nt(both[s], k_ref[cs, :]).reshape(HEADS_PER_STEP, TQ, TK) for (s, _), cs in zip(items, cols)]
                zms, cars, sufs = [], [], []
                for (s, n), kb, z in zip(items, kbs, zs):
                    mask = diff_s[...] > jnp.where(kb < 0, TK, jnp.where(n == 0, 0, -TK))
                    parts = []
                    for h in heads:
                        z2 = jnp.where(mask, z[h] * LOG2_E, MASKED_LOGIT)
                        sp = _softplus2(z2)
                        zms.append(z2 - sp)
                        parts += _split(sp)
                        car = carry_s[s, h]
                        cars.append(car)
                        car_ref[h, rows[s], :] = jnp.where(lane == kb, car, car_ref[h, rows[s], :])
                        carry_s[s, h] = car + jnp.sum(sp, axis=1, keepdims=True)
                    sufs += _block_sums(parts, suffix_s[...])
                for i, ((s, _), kb) in enumerate(zip(items, kbs)):
                    out = []
                    for h in heads:
                        j2 = i * HEADS_PER_STEP + h
                        w = jnp.exp2(zms[j2] - (sufs[j2] + _tile_lanes(cars[j2]))).astype(BF16)
                        out.append(_dot_nt(vt_s[jnp.maximum(kb, 0), h * HEAD_DIM:(h + 1) * HEAD_DIM, :], w))
                    ot_s[s] += jnp.concatenate(out, axis=0)

            tiles([(0, 0), (0, 1), (1, 0), (1, 1)])

            for s, (qb, rs) in enumerate(zip(qbs, rows)):
                def lowest_carry():
                    return jnp.min(jnp.minimum(carry_s[s, 0], carry_s[s, 1]))

                def live(state):
                    n, lowest = state
                    return jnp.logical_and(n <= qb, lowest < DEAD_CARRY_BITS)

                def step(state):
                    n, _ = state
                    tiles([(s, n)])
                    return n + 1, lowest_carry()

                n_done, _ = lax.while_loop(live, step, (jnp.int32(2), lowest_carry()))
                for h in heads:
                    car_ref[h, rs, :] = jnp.where(lane < qb + 1 - n_done, carry_s[s, h], car_ref[h, rs, :])
                o = ot_s[s].T
                o_ref[rs, :] = o
                za = za_ref[rs, :]
                mix_ref[rs, :] = ((za * _sigmoid(za)) * o).astype(BF16)
            return carry

        assert n_q % 2 == 0
        lax.fori_loop(0, n_q // 2, block_pair, 0)

    col = lambda g: pl.BlockSpec((seq, LANES), _col_block(g))
    n_pairs = ATTN_WIDTH // LANES
    return pl.pallas_call(
        body, name="attn_fwd",
        grid=(n_pairs,),
        in_specs=[col(0), col(1), col(2), col(0)],
        out_specs=(pl.BlockSpec((seq, LANES), lambda p: (0, p)),
                   pl.BlockSpec((seq, LANES), lambda p: (0, p)),
                   pl.BlockSpec((HEADS_PER_STEP, seq, LANES), lambda p: (p, 0, 0))),
        out_shape=(jax.ShapeDtypeStruct((seq, ATTN_WIDTH), F32),
                   jax.ShapeDtypeStruct((seq, ATTN_WIDTH), BF16),
                   jax.ShapeDtypeStruct((n_pairs * HEADS_PER_STEP, seq, LANES), F32)),
        scratch_shapes=[pltpu.VMEM((n_q, HEADS_PER_STEP, TQ, LANES), BF16),
                        pltpu.VMEM((n_kb, LANES, TK), BF16),
                        pltpu.VMEM((TK, TK), BF16),
                        pltpu.VMEM((2, HEADS_PER_STEP, TQ, LANES), F32),
                        pltpu.VMEM((TQ, TK), jnp.int32),
                        pltpu.VMEM((2, LANES, TQ), F32)],
        compiler_params=_compiler_params(48, ("arbitrary",)),
    )(qkv, qkv, qkv, z_attn)


def _attn_bwd(qkv, z_attn, o, dmix, carries):
    seq = qkv.shape[0]
    n_q = seq // TQ
    n_kb = seq // TK
    assert TQ == TK and n_kb <= LANES
    heads = range(HEADS_PER_STEP)

    def body(q_ref, k_ref, v_ref, za_ref, o_ref, dm_ref, car_ref, dp_ref,
             qq_s, dd_s, qt_s, dot_s, kt_s, suffix_s, prefix_s, ccar_s, dq_s, dk_s, dv_s, diff_s):
        head_a = _head_mask()
        r_i, s_i = _tri_masks()
        diff_s[...] = r_i - s_i
        _fill_tri(suffix_s, r_i > s_i)
        _fill_tri(prefix_s, r_i < s_i)
        lane = lax.broadcasted_iota(jnp.int32, (1, LANES), 1)

        def prep(r, carry):
            rs = pl.ds(pl.multiple_of(r * TQ, TQ), TQ)
            q = q_ref[rs, :]
            qq_s[r, 0] = jnp.where(head_a, q, jnp.zeros_like(q))
            qq_s[r, 1] = jnp.where(head_a, jnp.zeros_like(q), q)
            qt_s[r] = q.astype(F32).T.astype(BF16)
            kt_s[r] = k_ref[rs, :].astype(F32).T.astype(BF16)
            za = za_ref[rs, :]
            sg = _sigmoid(za)
            dm = dm_ref[rs, :]
            do = dm * (za * sg)
            dd_s[r, 0] = jnp.where(head_a, do, 0.0).astype(BF16)
            dd_s[r, 1] = jnp.where(head_a, 0.0, do).astype(BF16)
            dot_s[r] = do.T.astype(BF16)
            dp_ref[3, rs, :] = (dm * o_ref[rs, :] * (sg * (1.0 + za * (1.0 - sg)))).astype(BF16)
            dq_s[:, rs] = jnp.zeros((LANES, TQ), F32)
            dk_s[:, rs] = jnp.zeros((LANES, TQ), F32)
            dv_s[:, rs] = jnp.zeros((LANES, TQ), F32)
            return carry

        lax.fori_loop(0, n_q, prep, 0)

        def block_pair(j, carry):
            qbs = [2 * j, 2 * j + 1]
            rows = [pl.ds(pl.multiple_of(qb * TQ, TQ), TQ) for qb in qbs]
            both_q = [qq_s[qb].reshape(HEADS_PER_STEP * TQ, LANES) for qb in qbs]
            both_do = [dd_s[qb].reshape(HEADS_PER_STEP * TQ, LANES) for qb in qbs]
            ccar_s[...] = jnp.zeros(ccar_s.shape, F32)

            def tiles(items):
                cols = [pl.ds(pl.multiple_of(jnp.maximum(kb, 0) * TK, TK), TK) for _, kb in items]
                zs = [_dot_nt(both_q[s], k_ref[cs, :]).reshape(HEADS_PER_STEP, TQ, TK) for (s, _), cs in zip(items, cols)]
                dws = [_dot_nt(both_do[s], v_ref[cs, :]).reshape(HEADS_PER_STEP, TQ, TK)
                       for (s, _), cs in zip(items, cols)]
                zms, sufs = [], []
                for (s, kb), z in zip(items, zs):
                    mask = diff_s[...] > jnp.where(kb < 0, TK, jnp.where(kb == qbs[s], 0, -TK))
                    parts = []
                    for h in heads:
                        z2 = jnp.where(mask, z[h] * LOG2_E, MASKED_LOGIT)
                        sp = _softplus2(z2)
                        zms.append(z2 - sp)
                        parts += _split(sp)
                    sufs += _block_sums(parts, suffix_s[...])
                ws, dlws, pres = [], [], []
                for i, (s, kb) in enumerate(items):
                    parts = []
                    for h in heads:
                        j2 = i * HEADS_PER_STEP + h
                        cin = jnp.sum(jnp.where(lane == kb, car_ref[h, rows[s], :], 0.0), axis=1, keepdims=True)
                        w = jnp.exp2(zms[j2] - (sufs[j2] + cin))
                        ws.append(w.astype(BF16))
                        dlw = dws[i][h] * w
                        dlws.append(dlw)
                        parts += _split(dlw)
                    pres += _block_sums(parts, prefix_s[...])
                for i, ((s, kb), cs) in enumerate(zip(items, cols)):
                    dq_t, dk_t, dv_t = [], [], []
                    for h in heads:
                        j2 = i * HEADS_PER_STEP + h
                        ccar = ccar_s[s, h]
                        dz = (dlws[j2] - jnp.exp2(zms[j2]) * (dlws[j2] + (pres[j2] + _tile_lanes(ccar)))).astype(BF16)
                        ccar_s[s, h] = ccar + jnp.sum(dlws[j2], axis=1, keepdims=True)
                        dims = slice(h * HEAD_DIM, (h + 1) * HEAD_DIM)
                        dq_t.append(_dot_nt(kt_s[jnp.maximum(kb, 0), dims, :], dz))
                        dk_t.append(_dot(qt_s[qbs[s], dims, :], dz))
                        dv_t.append(_dot(dot_s[qbs[s], dims, :], ws[j2]))
                    dq_s[:, rows[s]] += jnp.concatenate(dq_t, axis=0)
                    dk_s[:, cs] += jnp.concatenate(dk_t, axis=0)
                    dv_s[:, cs] += jnp.concatenate(dv_t, axis=0)

            for s, (qb, rs) in enumerate(zip(qbs, rows)):
                lowest = jnp.min(jnp.minimum(car_ref[0, rs, :], car_ref[1, rs, :]), axis=0, keepdims=True)
                dead = jnp.logical_and(lowest >= DEAD_CARRY_BITS, lane < qb)
                first_live = jnp.sum(jnp.where(dead, 1, 0))

                def one(kb, c2):
                    tiles([(s, kb)])
                    return c2

                lax.fori_loop(first_live, qb - 1, one, 0)

            for s, qb in enumerate(qbs):
                tiles([(s, qb - 1), (s, qb)])
            return carry

        assert n_q % 2 == 0
        lax.fori_loop(0, n_q // 2, block_pair, 0)

        def finish(r, carry):
            rs = pl.ds(pl.multiple_of(r * TQ, TQ), TQ)
            dp_ref[0, rs, :] = (dq_s[:, rs].T * Q_SCALE).astype(BF16)
            dp_ref[1, rs, :] = dk_s[:, rs].T.astype(BF16)
            dp_ref[2, rs, :] = dv_s[:, rs].T.astype(BF16)
            return carry

        lax.fori_loop(0, n_q, finish, 0)

    def col(g, n_sub=CONV_WIDTH // LANES):
        return pl.BlockSpec((seq, LANES), _col_block(g, n_sub))

    n_pairs = ATTN_WIDTH // LANES
    by_head = lambda: pltpu.VMEM((n_q, HEADS_PER_STEP, TQ, LANES), BF16)
    transposed = lambda: pltpu.VMEM((n_q, LANES, TQ), BF16)
    acc_t = lambda: pltpu.VMEM((LANES, seq), F32)
    return pl.pallas_call(
        body, name="attn_bwd",
        grid=(n_pairs,),
        in_specs=[col(0), col(1), col(2), col(0), col(0), col(1),
                  pl.BlockSpec((HEADS_PER_STEP, seq, LANES), lambda p: (p, 0, 0))],
        out_specs=pl.BlockSpec((4, seq, LANES), lambda p: (0, 0, p)),
        out_shape=jax.ShapeDtypeStruct((4, seq, ATTN_WIDTH), BF16),
        scratch_shapes=[by_head(), by_head(), transposed(), transposed(), transposed(),
                        pltpu.VMEM((TK, TK), BF16), pltpu.VMEM((TK, TK), BF16),
                        pltpu.VMEM((2, HEADS_PER_STEP, TQ, LANES), F32),
                        acc_t(), acc_t(), acc_t(),
                        pltpu.VMEM((TQ, TK), jnp.int32)],
        compiler_params=_compiler_params(56, ("arbitrary",)),
    )(qkv, qkv, qkv, z_attn, o, dmix, carries)


def _out_ln(mix_c, mix_a, woutg, x, target, gain, bias):
    seq = x.shape[0]
    tm = 512
    inv_d = 1.0 / D_MODEL

    def body(mc_ref, ma_ref, wo_ref, x_ref, t_ref, g_ref, b_ref, dxa_ref, dmix_ref, gw_ref, st_ref):
        @pl.when(pl.program_id(0) == 0)
        def _():
            gw_ref[...] = jnp.zeros(gw_ref.shape, F32)
            st_ref[...] = jnp.zeros(st_ref.shape, F32)

        mc = mc_ref[...]
        ma = ma_ref[...]
        sub = _dot(mc, wo_ref[0:CONV_WIDTH, :]) + _dot(ma, wo_ref[CONV_WIDTH:, :])
        r = DEEPNORM_ALPHA * x_ref[...] + sub
        mu = jnp.mean(r, axis=-1, keepdims=True)
        rc = r - mu
        var = jnp.mean(rc * rc, axis=-1, keepdims=True)
        rstd = lax.rsqrt(var + LN_EPS)
        xhat = rc * rstd
        gain_v = g_ref[...]
        diff = (xhat * gain_v + b_ref[...]) - t_ref[...]
        dy = diff * inv_d
        st_ref[0:1, :] += jnp.sum(dy * xhat, axis=0, keepdims=True)
        st_ref[1:2, :] += jnp.sum(dy, axis=0, keepdims=True)
        st_ref[2:3, :] += jnp.sum(diff * diff, axis=0, keepdims=True)
        dxh = dy * gain_v
        m1 = jnp.mean(dxh, axis=-1, keepdims=True)
        m2 = jnp.mean(dxh * xhat, axis=-1, keepdims=True)
        dr = rstd * (dxh - m1 - xhat * m2)
        dxa_ref[...] = DEEPNORM_ALPHA * dr
        drb = dr.astype(BF16)
        dmix_ref[...] = _dot_nt(drb, wo_ref[...])
        gw_ref[0:CONV_WIDTH, :] += _dot_tn(mc, drb)
        gw_ref[CONV_WIDTH:, :] += _dot_tn(ma, drb)

    rows_d = lambda: pl.BlockSpec((tm, D_MODEL), lambda i: (i, 0))
    rows_h = lambda: pl.BlockSpec((tm, CONV_WIDTH), lambda i: (i, 0))
    whole = lambda shape: pl.BlockSpec(shape, lambda i: (0, 0))
    return pl.pallas_call(
        body, name="out_ln",
        grid=(seq // tm,),
        in_specs=[rows_h(), rows_h(), whole((D_MODEL, D_MODEL)), rows_d(), rows_d(),
                  whole((1, D_MODEL)), whole((1, D_MODEL))],
        out_specs=(rows_d(), rows_d(), whole((D_MODEL, D_MODEL)), whole((SUBLANES, D_MODEL))),
        out_shape=(jax.ShapeDtypeStruct((seq, D_MODEL), F32), jax.ShapeDtypeStruct((seq, D_MODEL), F32),
                   jax.ShapeDtypeStruct((D_MODEL, D_MODEL), F32), jax.ShapeDtypeStruct((SUBLANES, D_MODEL), F32)),
        compiler_params=_compiler_params(48, ("arbitrary",)),
    )(mix_c, mix_a, woutg, x, target, gain, bias)


def _group_maps():
    half = N_GROUPS // 2
    return (lambda g: jnp.minimum(g, half - 1)), (lambda g: jnp.maximum(g - half, 0))


def _grad_w_in(xt, dpc, dpa):
    seq = xt.shape[1]
    half = N_GROUPS // 2
    in_conv, in_attn = _group_maps()

    def body(xt_ref, dc_ref, da_ref, out_ref):
        g = pl.program_id(0)

        @pl.when(g < half)
        def _():
            out_ref[...] = _dot(xt_ref[...], dc_ref[...])

        @pl.when(g >= half)
        def _():
            out_ref[...] = _dot(xt_ref[...], da_ref[...])

    return pl.pallas_call(
        body, name="grad_w_in",
        grid=(N_GROUPS,),
        in_specs=[pl.BlockSpec((D_MODEL, seq), lambda g: (0, 0), pipeline_mode=pl.Buffered(1)),
                  pl.BlockSpec((None, seq, CONV_WIDTH), lambda g: (in_conv(g), 0, 0)),
                  pl.BlockSpec((None, seq, CONV_WIDTH), lambda g: (in_attn(g), 0, 0))],
        out_specs=pl.BlockSpec((None, D_MODEL, CONV_WIDTH), lambda g: (g // 2, 0, g % 2)),
        out_shape=jax.ShapeDtypeStruct((N_CHIPS, D_MODEL, D_MODEL), F32),
        compiler_params=_compiler_params(48, ("arbitrary",)),
    )(xt, dpc, dpa)


def _grad_x(dxa, dpc, dpa, wing):
    seq = dxa.shape[0]
    tm = 512
    half = N_GROUPS // 2

    def body(dxa_ref, dc_ref, da_ref, w_ref, out_ref):
        acc = dxa_ref[...]
        for g in range(N_GROUPS):
            dp = dc_ref[g] if g < half else da_ref[g - half]
            cols = slice((g % 2) * CONV_WIDTH, (g % 2 + 1) * CONV_WIDTH)
            acc = acc + _dot_nt(dp, w_ref[g // 2, :, cols])
        out_ref[...] = acc

    return pl.pallas_call(
        body, name="grad_x",
        grid=(seq // tm,),
        in_specs=[pl.BlockSpec((tm, D_MODEL), lambda i: (i, 0)),
                  pl.BlockSpec((half, tm, CONV_WIDTH), lambda i: (0, i, 0)),
                  pl.BlockSpec((half, tm, CONV_WIDTH), lambda i: (0, i, 0)),
                  pl.BlockSpec((N_CHIPS, D_MODEL, D_MODEL), lambda i: (0, 0, 0), pipeline_mode=pl.Buffered(1))],
        out_specs=pl.BlockSpec((tm, D_MODEL), lambda i: (i, 0)),
        out_shape=jax.ShapeDtypeStruct((seq, D_MODEL), F32),
        compiler_params=_compiler_params(40, ("arbitrary",)),
    )(dxa, dpc, dpa, wing)


PACK_LOSS_ROW = 5


def _reduce_grads(gwin, gwout, pack):
    n_shards, n_rows_in, n_cols = gwin.shape
    n_rows_out = gwout.shape[1]
    half_rows = (n_rows_in // 2, n_rows_out // 2)
    row_chunk = 128

    def body(gwin_ref, gwout_ref, pack_ref, oin_ref, oout_ref, opack_ref,
             mine_in, mine_out, sib_in, sib_out, wire_in, wire_out, rin, rout, rpack,
             local_sems, sib_send, sib_recv, send_sems, recv_sems, pack_send, pack_recv, fin_send, fin_recv):
        x, y, c = lax.axis_index("x"), lax.axis_index("y"), lax.axis_index("c")
        me = 2 * x + y
        my_id = 4 * x + 2 * y + c
        chips = [(1 - x, y), (x, 1 - y), (1 - x, 1 - y)]
        sibling = (x, y, 1 - c)
        partial = (gwin_ref, gwout_ref)
        mine = (mine_in, mine_out)
        from_sib = (sib_in, sib_out)
        wire = (wire_in, wire_out)
        from_chips = (rin, rout)
        result = (oin_ref, oout_ref)

        def half(a, which):
            rows = half_rows[a]
            return pl.ds(pl.multiple_of(which * rows, rows), rows)

        order = [2 * px + py for px, py in chips] + [me]

        def swap_copy(i, a):
            shard = order[i]
            return pltpu.make_async_remote_copy(
                src_ref=partial[a].at[shard, half(a, 1 - c), :], dst_ref=from_sib[a].at[shard],
                send_sem=sib_send.at[2 * i + a], recv_sem=sib_recv.at[2 * i + a],
                device_id=sibling, device_id_type=MESH)

        sends = []
        loads = []
        for a in range(2):
            ld = pltpu.make_async_copy(partial[a].at[:, half(a, c), :], mine[a], local_sems.at[a])
            ld.start()
            loads.append(ld)
        for i in range(n_shards):
            for a in range(2):
                cp = swap_copy(i, a)
                cp.start()
                sends.append(cp)

        rpack[my_id] = pack_ref[...]

        def pack_copy(rel, slot, to):
            return pltpu.make_async_remote_copy(
                src_ref=rpack.at[slot], dst_ref=rpack.at[slot],
                send_sem=pack_send.at[rel - 1], recv_sem=pack_recv.at[rel - 1],
                device_id=to, device_id_type=MESH)

        def related(rel):
            px = (1 - x) if rel & 4 else x
            py = (1 - y) if rel & 2 else y
            pc = (1 - c) if rel & 1 else c
            return px, py, pc

        for rel in range(1, N_DEVICES):
            cp = pack_copy(rel, my_id, related(rel))
            cp.start()
            sends.append(cp)

        for ld in loads:
            ld.wait()

        def chip_sum(a, shard):
            def add(r, carry):
                rs = pl.ds(pl.multiple_of(r * row_chunk, row_chunk), row_chunk)
                mine[a][shard, rs, :] = mine[a][shard, rs, :] + from_sib[a][shard, rs, :]
                return carry

            lax.fori_loop(0, half_rows[a] // row_chunk, add, 0)

        def chip_sum_to_wire(a, shard, k):
            def add(r, carry):
                rs = pl.ds(pl.multiple_of(r * row_chunk, row_chunk), row_chunk)
                wire[a][k, rs, :] = (mine[a][shard, rs, :] + from_sib[a][shard, rs, :]).astype(BF16)
                return carry

            lax.fori_loop(0, half_rows[a] // row_chunk, add, 0)

        def shard_copy(k, a, to):
            return pltpu.make_async_remote_copy(
                src_ref=wire[a].at[k], dst_ref=from_chips[a].at[k],
                send_sem=send_sems.at[2 * k + a], recv_sem=recv_sems.at[2 * k + a],
                device_id=to, device_id_type=MESH)

        for k, (px, py) in enumerate(chips):
            for a in range(2):
                swap_copy(k, a).wait_recv()
                chip_sum_to_wire(a, 2 * px + py, k)
                cp = shard_copy(k, a, (px, py, c))
                cp.start()
                sends.append(cp)
        for a in range(2):
            swap_copy(n_shards - 1, a).wait_recv()
            chip_sum(a, me)

        for rel in range(1, N_DEVICES):
            px, py, pc = related(rel)
            pack_copy(rel, 4 * px + 2 * py + pc, (px, py, pc)).wait_recv()
        total = rpack[0]
        for i in range(1, N_DEVICES):
            total = total + rpack[i]
        opack_ref[...] = total
        sq_err = jnp.sum(total[PACK_LOSS_ROW:PACK_LOSS_ROW + 1, :], axis=1, keepdims=True)
        opack_ref[PACK_LOSS_ROW:PACK_LOSS_ROW + 1, :] = jnp.broadcast_to(sq_err * (0.5 / D_MODEL), (1, n_cols))

        for k, (px, py) in enumerate(chips):
            for a in range(2):
                shard_copy(k, a, (px, py, c)).wait_recv()

        def finish(a):
            def add(r, carry):
                rs = pl.ds(pl.multiple_of(r * row_chunk, row_chunk), row_chunk)
                mine[a][me, rs, :] = ((mine[a][me, rs, :] + from_chips[a][0, rs, :].astype(F32))
                                      + from_chips[a][1, rs, :].astype(F32)) + from_chips[a][2, rs, :].astype(F32)
                return carry

            lax.fori_loop(0, half_rows[a] // row_chunk, add, 0)

        def final_copy(a, which):
            return pltpu.make_async_remote_copy(
                src_ref=mine[a].at[me], dst_ref=result[a].at[half(a, which), :],
                send_sem=fin_send.at[a], recv_sem=fin_recv.at[a], device_id=sibling, device_id_type=MESH)

        stores = []
        for a in range(2):
            finish(a)
            cp = final_copy(a, c)
            cp.start()
            sends.append(cp)
            st = pltpu.make_async_copy(mine[a].at[me], result[a].at[half(a, c), :], local_sems.at[2 + a])
            st.start()
            stores.append(st)
        for a in range(2):
            final_copy(a, 1 - c).wait_recv()
        for cp in sends:
            cp.wait_send()
        for st in stores:
            st.wait()

    vmem = pl.BlockSpec(memory_space=pltpu.VMEM)
    hbm = pl.BlockSpec(memory_space=pl.ANY)
    in_shape = (n_rows_in, n_cols)
    out_shape = (n_rows_out, n_cols)
    half_in = (half_rows[0], n_cols)
    half_out = (half_rows[1], n_cols)
    return pl.pallas_call(
        body, name="reduce_grads",
        out_shape=(jax.ShapeDtypeStruct(in_shape, F32), jax.ShapeDtypeStruct(out_shape, F32),
                   jax.ShapeDtypeStruct(pack.shape, F32)),
        in_specs=[hbm, hbm, vmem], out_specs=(hbm, hbm, vmem),
        scratch_shapes=[pltpu.VMEM((n_shards,) + half_in, F32), pltpu.VMEM((n_shards,) + half_out, F32),
                        pltpu.VMEM((n_shards,) + half_in, F32), pltpu.VMEM((n_shards,) + half_out, F32),
                        pltpu.VMEM((N_CHIPS - 1,) + half_in, BF16), pltpu.VMEM((N_CHIPS - 1,) + half_out, BF16),
                        pltpu.VMEM((N_CHIPS - 1,) + half_in, BF16), pltpu.VMEM((N_CHIPS - 1,) + half_out, BF16),
                        pltpu.VMEM((N_DEVICES,) + pack.shape, F32),
                        pltpu.SemaphoreType.DMA((4,)),
                        pltpu.SemaphoreType.DMA((2 * n_shards,)), pltpu.SemaphoreType.DMA((2 * n_shards,)),
                        pltpu.SemaphoreType.DMA((6,)), pltpu.SemaphoreType.DMA((6,)),
                        pltpu.SemaphoreType.DMA((N_DEVICES - 1,)), pltpu.SemaphoreType.DMA((N_DEVICES - 1,)),
                        pltpu.SemaphoreType.DMA((2,)), pltpu.SemaphoreType.DMA((2,))],
        compiler_params=_compiler_params(48),
    )(gwin, gwout, pack)


def _adamw(name, w, g, m, v):
    n_rows, n_cols = w.shape
    tr = 256 if n_rows % 256 == 0 else n_rows
    m_corr = 1.0 - ADAM_B1 ** ADAM_STEP
    v_corr = 1.0 - ADAM_B2 ** ADAM_STEP

    def body(w_ref, g_ref, m_ref, v_ref, d_ref, nm_ref, nv_ref):
        gv = g_ref[...]
        nm = ADAM_B1 * m_ref[...] + (1.0 - ADAM_B1) * gv
        nv = ADAM_B2 * v_ref[...] + (1.0 - ADAM_B2) * (gv * gv)
        nm_ref[...] = nm
        nv_ref[...] = nv
        d_ref[...] = -ADAM_LR * ((nm / m_corr) / (jnp.sqrt(nv / v_corr) + ADAM_EPS) + ADAM_WD * w_ref[...])

    blk = lambda: pl.BlockSpec((tr, n_cols), lambda i: (i, 0))
    shape = jax.ShapeDtypeStruct(w.shape, F32)
    return pl.pallas_call(
        body, name=name,
        grid=(n_rows // tr,),
        in_specs=[blk(), blk(), blk(), blk()], out_specs=(blk(), blk(), blk()),
        out_shape=(shape, shape, shape),
        compiler_params=_compiler_params(32, ("arbitrary",)),
    )(w, g, m, v)


def kernel(x, w_in, conv_w, w_out, ln_gain, ln_bias, loss_target, m_w_in, m_conv_w, m_w_out, m_ln_gain, m_ln_bias, v_w_in, v_conv_w, v_w_out, v_ln_gain, v_ln_bias):
    xs = x[0]
    target = loss_target[0]

    wing, woutg, cwg = _gather_weights(w_in[0], w_out[0], conv_w[0])
    conv_full = jnp.transpose(cwg, (1, 0, 2)).reshape(3, CONV_WIDTH)
    wout_full = woutg.reshape(D_MODEL, D_MODEL)

    p_conv, qkv, z_attn, xt = _proj(xs, wing)
    mix_c = _conv_fwd(p_conv, conv_full)
    o, mix_a, carries = _attn_fwd(qkv, z_attn)
    dxa, dmix, gwout, stats = _out_ln(mix_c, mix_a, wout_full, xs, target, ln_gain, ln_bias)
    dpc, dconv = _conv_bwd(p_conv, dmix, conv_full)
    dpa = _attn_bwd(qkv, z_attn, o, dmix, carries)
    gwin = _grad_w_in(xt, dpc, dpa)
    grad_x = _grad_x(dxa, dpc, dpa, wing)

    pack = jnp.concatenate(
        [stats[0:2], jnp.pad(dconv[0:3], ((0, 0), (0, D_MODEL - CONV_WIDTH))), stats[2:3],
         jnp.zeros((2, D_MODEL), F32)], axis=0)
    g_w_in, g_w_out, tot = _reduce_grads(gwin, gwout.reshape(N_CHIPS, D_MODEL // N_CHIPS, D_MODEL), pack)

    chip = 2 * lax.axis_index("x") + lax.axis_index("y")
    g_gain = tot[0:1]
    g_bias = tot[1:2]
    g_conv = lax.dynamic_slice(tot, (2, chip * LANES), (3, LANES))
    loss = tot[PACK_LOSS_ROW, 0]

    d_w_in, nm_w_in, nv_w_in = _adamw("adamw_w_in", w_in[0], g_w_in, m_w_in[0], v_w_in[0])
    d_w_out, nm_w_out, nv_w_out = _adamw("adamw_w_out", w_out[0], g_w_out, m_w_out[0], v_w_out[0])
    d_conv, nm_conv, nv_conv = _adamw("adamw_conv_w", conv_w[0], g_conv, m_conv_w[0], v_conv_w[0])
    d_gain, nm_gain, nv_gain = _adamw("adamw_ln_gain", ln_gain, g_gain, m_ln_gain, v_ln_gain)
    d_bias, nm_bias, nv_bias = _adamw("adamw_ln_bias", ln_bias, g_bias, m_ln_bias, v_ln_bias)

    lead = lambda a: a[None]
    return (loss, lead(grad_x),
            lead(g_w_in), lead(g_conv), lead(g_w_out), g_gain, g_bias,
            lead(d_w_in), lead(d_conv), lead(d_w_out), d_gain, d_bias,
            lead(nm_w_in), lead(nm_conv), lead(nm_w_out), nm_gain, nm_bias,
            lead(nv_w_in), lead(nv_conv), lead(nv_w_out), nv_gain, nv_bias)
```

```python
import jax
import jax.numpy as jnp
from jax import lax
from jax.experimental import pallas as pl
from jax.experimental.pallas import tpu as pltpu

F32 = jnp.float32
BF16 = jnp.bfloat16
MESH = pl.DeviceIdType.MESH

D_MODEL = 1024
CONV_WIDTH = 512
ATTN_WIDTH = 512
HEAD_DIM = 64
N_GROUPS = 8
N_CHIPS = 4
N_DEVICES = 8
LN_EPS = 1e-5
DEEPNORM_ALPHA = 2.0 ** 0.25
Q_SCALE = HEAD_DIM ** -0.5
ADAM_LR = 0.001
ADAM_B1 = 0.9
ADAM_B2 = 0.999
ADAM_EPS = 1e-08
ADAM_WD = 0.01
ADAM_STEP = 10

LANES = 128
SUBLANES = 8
V7X_VMEM_BYTES = 64 * 1024 * 1024
MIB = 1024 * 1024

TQ = 256
TK = 256
HEADS_PER_STEP = LANES // HEAD_DIM
CONV_ROWS = 512


def _compiler_params(vmem_mib, semantics=None):
    assert vmem_mib * MIB < V7X_VMEM_BYTES
    return pltpu.CompilerParams(dimension_semantics=semantics, vmem_limit_bytes=vmem_mib * MIB)


def _sigmoid(z):
    return 1.0 / (1.0 + jnp.exp(-z))


def _dot(a, b):
    return jnp.dot(a, b, preferred_element_type=F32)


def _dot_nt(a, b):
    return lax.dot_general(a, b, (((1,), (1,)), ((), ())), preferred_element_type=F32)


def _dot_tn(a, b):
    return lax.dot_general(a, b, (((0,), (0,)), ((), ())), preferred_element_type=F32)


def _truncate_to_bf16(a):
    bits = lax.bitcast_convert_type(a, jnp.uint32) & jnp.uint32(0xFFFF0000)
    return lax.bitcast_convert_type(bits, F32)


def _split(a):
    hi = _truncate_to_bf16(a)
    return [hi.astype(BF16), (a - hi).astype(BF16)]


def _block_sums(parts, tri):
    n = len(parts) // 2
    res = _dot(jnp.concatenate(parts, axis=0), tri).reshape(n, 2, TQ, TK)
    return [res[i, 0] + res[i, 1] for i in range(n)]


MASKED_LOGIT = -1e30
LOG2_E = 1.4426950408889634


def _softplus2(z2):
    return jnp.maximum(z2, 0.0) + jnp.log2(1.0 + jnp.exp2(-jnp.abs(z2)))


def _gather_weights(w_in, w_out, conv_w):
    d_rows, d_cols = w_in.shape
    o_rows, o_cols = w_out.shape
    half_rows = (d_rows // 2, o_rows // 2)
    row_chunk = 128

    def body(win_ref, wout_ref, cw_ref, wing_ref, woutg_ref, cwg_ref, cast_in, cast_out,
             send_sems, recv_sems, pass_send, pass_recv, store_sems):
        x, y, c = lax.axis_index("x"), lax.axis_index("y"), lax.axis_index("c")
        me = 2 * x + y
        srcs = (win_ref, wout_ref)
        bufs = (wing_ref, woutg_ref)
        casts = (cast_in, cast_out)

        def rows_of(a, which):
            rows = half_rows[a]
            return pl.ds(pl.multiple_of(which * rows, rows), rows)

        def half(a, shard, which):
            return bufs[a].at[shard, rows_of(a, which), :]

        def cast_half(which):
            for a in range(2):
                def cast(r, carry):
                    rows = pl.ds(pl.multiple_of(which * half_rows[a] + r * row_chunk, row_chunk), row_chunk)
                    casts[a][rows, :] = srcs[a][rows, :].astype(BF16)
                    return carry

                lax.fori_loop(0, half_rows[a] // row_chunk, cast, 0)

        chips = [(1 - x, y), (x, 1 - y), (1 - x, 1 - y)]

        def ici_copy(k, a, shard, to):
            if a == 2:
                src = dst = cwg_ref.at[shard]
            else:
                src, dst = casts[a].at[rows_of(a, c), :], half(a, shard, c)
            return pltpu.make_async_remote_copy(
                src_ref=src, dst_ref=dst, send_sem=send_sems.at[3 * k + a], recv_sem=recv_sems.at[3 * k + a],
                device_id=to, device_id_type=MESH)

        def pass_copy(k, a, shard, which):
            ref = half(a, shard, which)
            return pltpu.make_async_remote_copy(
                src_ref=ref, dst_ref=ref, send_sem=pass_send.at[2 * k + a], recv_sem=pass_recv.at[2 * k + a],
                device_id=(x, y, 1 - c), device_id_type=MESH)

        cast_half(c)
        cwg_ref[me] = cw_ref[...]
        sends = []
        for k, (px, py) in enumerate(chips):
            for a in range(3):
                cp = ici_copy(k, a, me, (px, py, c))
                cp.start()
                sends.append(cp)
        cast_half(1 - c)
        stores = []
        for a in range(2):
            st = pltpu.make_async_copy(casts[a], bufs[a].at[me], store_sems.at[a])
            st.start()
            stores.append(st)
        for k, (px, py) in enumerate(chips):
            for a in range(2):
                ici_copy(k, a, 2 * px + py, (px, py, c)).wait_recv()
                cp = pass_copy(k, a, 2 * px + py, c)
                cp.start()
                sends.append(cp)
        for k, (px, py) in enumerate(chips):
            ici_copy(k, 2, 2 * px + py, (px, py, c)).wait_recv()
            for a in range(2):
                pass_copy(k, a, 2 * px + py, 1 - c).wait_recv()
        for cp in sends:
            cp.wait_send()
        for st in stores:
            st.wait()

    vmem = pl.BlockSpec(memory_space=pltpu.VMEM)
    hbm = pl.BlockSpec(memory_space=pl.ANY)
    return pl.pallas_call(
        body, name="gather_weights",
        out_shape=(jax.ShapeDtypeStruct((N_CHIPS, d_rows, d_cols), BF16),
                   jax.ShapeDtypeStruct((N_CHIPS, o_rows, o_cols), BF16),
                   jax.ShapeDtypeStruct((N_CHIPS,) + conv_w.shape, F32)),
        in_specs=[vmem, vmem, vmem], out_specs=(hbm, hbm, vmem),
        scratch_shapes=[pltpu.VMEM((d_rows, d_cols), BF16), pltpu.VMEM((o_rows, o_cols), BF16),
                        pltpu.SemaphoreType.DMA((9,)), pltpu.SemaphoreType.DMA((9,)),
                        pltpu.SemaphoreType.DMA((6,)), pltpu.SemaphoreType.DMA((6,)),
                        pltpu.SemaphoreType.DMA((2,))],
        compiler_params=_compiler_params(32),
    )(w_in, w_out, conv_w)


def _proj(x, wing, conv_w):
    seq = x.shape[0]
    tm = 512
    w = CONV_WIDTH

    def body(x_ref, w_ref, cw_ref, conv_ref, qkv_ref, za_ref, xt_ref, mix_ref, tail_s):
        xv = x_ref[...]
        xb = xv.astype(BF16)
        bc = _dot(xb, w_ref[0])
        hz = _dot(xb, w_ref[1])
        conv_ref[:, 0:D_MODEL] = bc
        conv_ref[:, D_MODEL:] = hz

        @pl.when(pl.program_id(0) == 0)
        def _():
            tail_s[...] = jnp.zeros(tail_s.shape, F32)

        u = bc[:, w:] * hz[:, 0:w]
        ext = jnp.concatenate([tail_s[...], u], axis=0)
        tail_s[...] = u[tm - SUBLANES:, :]
        taps = cw_ref[...]
        y = taps[2:3, :] * u
        y = y + taps[0:1, :] * _shift_down(ext, 2, tm)
        y = y + taps[1:2, :] * _shift_down(ext, 1, tm)
        z = hz[:, w:]
        mix_ref[...] = ((z * _sigmoid(z)) * (bc[:, 0:w] * y)).astype(BF16)

        qk = _dot(xb, w_ref[2])
        qkv_ref[:, 0:w] = (qk[:, 0:w] * Q_SCALE).astype(BF16)
        qkv_ref[:, w:2 * w] = qk[:, w:].astype(BF16)
        vz = _dot(xb, w_ref[3])
        qkv_ref[:, 2 * w:] = vz[:, 0:w].astype(BF16)
        za_ref[...] = vz[:, w:]
        xt_ref[...] = xv.T.astype(BF16)

    rows = lambda width: pl.BlockSpec((tm, width), lambda i: (i, 0))
    return pl.pallas_call(
        body, name="proj",
        grid=(seq // tm,),
        in_specs=[rows(D_MODEL),
                  pl.BlockSpec((N_CHIPS, D_MODEL, D_MODEL), lambda i: (0, 0, 0), pipeline_mode=pl.Buffered(1)),
                  pl.BlockSpec((3, w), lambda i: (0, 0))],
        out_specs=(rows(4 * w), rows(3 * w), rows(w), pl.BlockSpec((D_MODEL, tm), lambda i: (0, i)), rows(w)),
        out_shape=(jax.ShapeDtypeStruct((seq, 4 * w), F32), jax.ShapeDtypeStruct((seq, 3 * w), BF16),
                   jax.ShapeDtypeStruct((seq, w), F32), jax.ShapeDtypeStruct((D_MODEL, seq), BF16),
                   jax.ShapeDtypeStruct((seq, w), BF16)),
        scratch_shapes=[pltpu.VMEM((SUBLANES, w), F32)],
        compiler_params=_compiler_params(48, ("arbitrary",)),
    )(x, wing, conv_w)


def _col_block(group, n_sub=CONV_WIDTH // LANES):
    return lambda j: (0, group * n_sub + j)


def _shift_down(ext, k, rows):
    return pltpu.roll(ext, k, 0)[SUBLANES:, :]


def _shift_up(ext, k, rows):
    return pltpu.roll(ext, rows + SUBLANES - k, 0)[:rows, :]


def _conv_bwd(proj, dmix, conv_w):
    seq = proj.shape[0]
    rows = CONV_ROWS
    n_chunks = seq // rows

    def body(b_ref, c_ref, h_ref, z_ref, d_ref, w_ref, dp_ref, dw_ref, u_s, dy_s):
        u_s[0:SUBLANES, :] = jnp.zeros((SUBLANES, LANES), F32)
        dy_s[seq:seq + SUBLANES, :] = jnp.zeros((SUBLANES, LANES), F32)

        def fill(r, carry):
            r0 = pl.multiple_of(r * rows, rows)
            rs = pl.ds(r0, rows)
            u_s[pl.ds(pl.multiple_of(r0 + SUBLANES, SUBLANES), rows), :] = c_ref[rs, :] * h_ref[rs, :]
            z = z_ref[rs, :]
            dy_s[rs, :] = d_ref[rs, :] * (z * _sigmoid(z)) * b_ref[rs, :]
            return carry

        lax.fori_loop(0, n_chunks, fill, 0)
        w = w_ref[...]

        def chunk(r, acc):
            r0 = pl.multiple_of(r * rows, rows)
            rs = pl.ds(r0, rows)
            ext = u_s[pl.ds(r0, rows + SUBLANES), :]
            u = ext[SUBLANES:, :]
            um1 = _shift_down(ext, 1, rows)
            um2 = _shift_down(ext, 2, rows)
            y = w[2:3, :] * u
            y = y + w[0:1, :] * um2
            y = y + w[1:2, :] * um1
            z = z_ref[rs, :]
            b = b_ref[rs, :]
            dco = d_ref[rs, :]
            sg = _sigmoid(z)
            g = z * sg
            dp_ref[0, rs, :] = (dco * g * y).astype(BF16)
            dp_ref[3, rs, :] = (dco * b * y * (sg * (1.0 + z * (1.0 - sg)))).astype(BF16)
            ext_dy = dy_s[pl.ds(r0, rows + SUBLANES), :]
            dy = ext_dy[:rows, :]
            du = w[2:3, :] * dy + w[1:2, :] * _shift_up(ext_dy, 1, rows) + w[0:1, :] * _shift_up(ext_dy, 2, rows)
            dp_ref[1, rs, :] = (du * h_ref[rs, :]).astype(BF16)
            dp_ref[2, rs, :] = (du * c_ref[rs, :]).astype(BF16)
            a0, a1, a2 = acc
            return (a0 + jnp.sum(dy * um2, axis=0, keepdims=True),
                    a1 + jnp.sum(dy * um1, axis=0, keepdims=True),
                    a2 + jnp.sum(dy * u, axis=0, keepdims=True))

        zero = jnp.zeros((1, LANES), F32)
        a0, a1, a2 = lax.fori_loop(0, n_chunks, chunk, (zero, zero, zero))
        dw_ref[...] = jnp.concatenate([a0, a1, a2, jnp.zeros((SUBLANES - 3, LANES), F32)], axis=0)

    col = lambda g: pl.BlockSpec((seq, LANES), _col_block(g))
    return pl.pallas_call(
        body, name="conv_bwd",
        grid=(CONV_WIDTH // LANES,),
        in_specs=[col(0), col(1), col(2), col(3), col(0), pl.BlockSpec((3, LANES), lambda j: (0, j))],
        out_specs=(pl.BlockSpec((4, seq, LANES), lambda j: (0, 0, j)),
                   pl.BlockSpec((SUBLANES, LANES), lambda j: (0, j))),
        out_shape=(jax.ShapeDtypeStruct((4, seq, CONV_WIDTH), BF16),
                   jax.ShapeDtypeStruct((SUBLANES, CONV_WIDTH), F32)),
        scratch_shapes=[pltpu.VMEM((seq + SUBLANES, LANES), F32), pltpu.VMEM((seq + SUBLANES, LANES), F32)],
        compiler_params=_compiler_params(48, ("arbitrary",)),
    )(proj, proj, proj, proj, dmix, conv_w)


def _tri_masks():
    r = lax.broadcasted_iota(jnp.int32, (TK, TK), 0)
    s = lax.broadcasted_iota(jnp.int32, (TK, TK), 1)
    return r, s


def _head_mask():
    lane = lax.broadcasted_iota(jnp.int32, (1, LANES), 1)
    return lane < HEAD_DIM


def _fill_tri(ref, ones):
    ref[...] = jnp.where(ones, 1.0, 0.0).astype(BF16)


def _tile_lanes(a):
    return jnp.tile(a, (1, TK // LANES))


DEAD_CARRY_BITS = 160.0


def _attn_fwd(qkv, z_attn):
    seq = qkv.shape[0]
    n_q = seq // TQ
    n_kb = seq // TK
    assert TQ == TK and n_kb <= LANES
    heads = range(HEADS_PER_STEP)

    def body(q_ref, k_ref, v_ref, za_ref, o_ref, mix_ref, car_ref, qq_s, vt_s, suffix_s, carry_s, diff_s, ot_s):
        head_a = _head_mask()
        r_i, s_i = _tri_masks()
        diff_s[...] = r_i - s_i
        _fill_tri(suffix_s, r_i > s_i)
        lane = lax.broadcasted_iota(jnp.int32, (1, LANES), 1)

        def prep(r, carry):
            rs = pl.ds(pl.multiple_of(r * TQ, TQ), TQ)
            q = q_ref[rs, :]
            qq_s[r, 0] = jnp.where(head_a, q, jnp.zeros_like(q))
            qq_s[r, 1] = jnp.where(head_a, jnp.zeros_like(q), q)
            vt_s[r] = v_ref[rs, :].astype(F32).T.astype(BF16)
            return carry

        lax.fori_loop(0, n_q, prep, 0)

        def block_pair(j, carry):
            qbs = [2 * j, 2 * j + 1]
            rows = [pl.ds(pl.multiple_of(qb * TQ, TQ), TQ) for qb in qbs]
            both = [qq_s[qb].reshape(HEADS_PER_STEP * TQ, LANES) for qb in qbs]
            carry_s[...] = jnp.zeros(carry_s.shape, F32)
            ot_s[...] = jnp.zeros(ot_s.shape, F32)
            for rs in rows:
                for h in heads:
                    car_ref[h, rs, :] = jnp.zeros((TQ, LANES), F32)

            def tiles(items):
                kbs = [qbs[s] - n for s, n in items]
                cols = [pl.ds(pl.multiple_of(jnp.maximum(kb, 0) * TK, TK), TK) for kb in kbs]
                zs = [_dot_nt(both[s], k_ref[cs, :]).reshape(HEADS_PER_STEP, TQ, TK) for (s, _), cs in zip(items, cols)]
                zms, cars, sufs = [], [], []
                for (s, n), kb, z in zip(items, kbs, zs):
                    mask = diff_s[...] > jnp.where(kb < 0, TK, jnp.where(n == 0, 0, -TK))
                    parts = []
                    for h in heads:
                        z2 = jnp.where(mask, z[h] * LOG2_E, MASKED_LOGIT)
                        sp = _softplus2(z2)
                        zms.append(z2 - sp)
                        parts += _split(sp)
                        car = carry_s[s, h]
                        cars.append(car)
                        car_ref[h, rows[s], :] = jnp.where(lane == kb, car, car_ref[h, rows[s], :])
                        carry_s[s, h] = car + jnp.sum(sp, axis=1, keepdims=True)
                    sufs += _block_sums(parts, suffix_s[...])
                for i, ((s, _), kb) in enumerate(zip(items, kbs)):
                    out = []
                    for h in heads:
                        j2 = i * HEADS_PER_STEP + h
                        w = jnp.exp2(zms[j2] - (sufs[j2] + _tile_lanes(cars[j2]))).astype(BF16)
                        out.append(_dot_nt(vt_s[jnp.maximum(kb, 0), h * HEAD_DIM:(h + 1) * HEAD_DIM, :], w))
                    ot_s[s] += jnp.concatenate(out, axis=0)

            tiles([(0, 0), (0, 1), (1, 0), (1, 1)])

            for s, (qb, rs) in enumerate(zip(qbs, rows)):
                def lowest_carry():
                    return jnp.min(jnp.minimum(carry_s[s, 0], carry_s[s, 1]))

                def live(state):
                    n, lowest = state
                    return jnp.logical_and(n <= qb, lowest < DEAD_CARRY_BITS)

                def step(state):
                    n, _ = state
                    tiles([(s, n)])
                    return n + 1, lowest_carry()

                n_done, _ = lax.while_loop(live, step, (jnp.int32(2), lowest_carry()))
                for h in heads:
                    car_ref[h, rs, :] = jnp.where(lane < qb + 1 - n_done, carry_s[s, h], car_ref[h, rs, :])
                o = ot_s[s].T
                o_ref[rs, :] = o
                za = za_ref[rs, :]
                mix_ref[rs, :] = ((za * _sigmoid(za)) * o).astype(BF16)
            return carry

        assert n_q % 2 == 0
        lax.fori_loop(0, n_q // 2, block_pair, 0)

    col = lambda g: pl.BlockSpec((seq, LANES), _col_block(g))
    n_pairs = ATTN_WIDTH // LANES
    return pl.pallas_call(
        body, name="attn_fwd",
        grid=(n_pairs,),
        in_specs=[col(0), col(1), col(2), col(0)],
        out_specs=(pl.BlockSpec((seq, LANES), lambda p: (0, p)),
                   pl.BlockSpec((seq, LANES), lambda p: (0, p)),
                   pl.BlockSpec((HEADS_PER_STEP, seq, LANES), lambda p: (p, 0, 0))),
        out_shape=(jax.ShapeDtypeStruct((seq, ATTN_WIDTH), F32),
                   jax.ShapeDtypeStruct((seq, ATTN_WIDTH), BF16),
                   jax.ShapeDtypeStruct((n_pairs * HEADS_PER_STEP, seq, LANES), F32)),
        scratch_shapes=[pltpu.VMEM((n_q, HEADS_PER_STEP, TQ, LANES), BF16),
                        pltpu.VMEM((n_kb, LANES, TK), BF16),
                        pltpu.VMEM((TK, TK), BF16),
                        pltpu.VMEM((2, HEADS_PER_STEP, TQ, LANES), F32),
                        pltpu.VMEM((TQ, TK), jnp.int32),
                        pltpu.VMEM((2, LANES, TQ), F32)],
        compiler_params=_compiler_params(48, ("arbitrary",)),
    )(qkv, qkv, qkv, z_attn)


def _attn_bwd(qkv, z_attn, o, dmix, carries):
    seq = qkv.shape[0]
    n_q = seq // TQ
    n_kb = seq // TK
    assert TQ == TK and n_kb <= LANES
    heads = range(HEADS_PER_STEP)

    def body(q_ref, k_ref, v_ref, za_ref, o_ref, dm_ref, car_ref, dp_ref,
             qq_s, dd_s, qt_s, dot_s, kt_s, suffix_s, prefix_s, ccar_s, dq_s, dk_s, dv_s, diff_s):
        head_a = _head_mask()
        r_i, s_i = _tri_masks()
        diff_s[...] = r_i - s_i
        _fill_tri(suffix_s, r_i > s_i)
        _fill_tri(prefix_s, r_i < s_i)
        lane = lax.broadcasted_iota(jnp.int32, (1, LANES), 1)

        def prep(r, carry):
            rs = pl.ds(pl.multiple_of(r * TQ, TQ), TQ)
            q = q_ref[rs, :]
            qq_s[r, 0] = jnp.where(head_a, q, jnp.zeros_like(q))
            qq_s[r, 1] = jnp.where(head_a, jnp.zeros_like(q), q)
            qt_s[r] = q.astype(F32).T.astype(BF16)
            kt_s[r] = k_ref[rs, :].astype(F32).T.astype(BF16)
            za = za_ref[rs, :]
            sg = _sigmoid(za)
            dm = dm_ref[rs, :]
            do = dm * (za * sg)
            dd_s[r, 0] = jnp.where(head_a, do, 0.0).astype(BF16)
            dd_s[r, 1] = jnp.where(head_a, 0.0, do).astype(BF16)
            dot_s[r] = do.T.astype(BF16)
            dp_ref[3, rs, :] = (dm * o_ref[rs, :] * (sg * (1.0 + za * (1.0 - sg)))).astype(BF16)
            dq_s[:, rs] = jnp.zeros((LANES, TQ), F32)
            dk_s[:, rs] = jnp.zeros((LANES, TQ), F32)
            dv_s[:, rs] = jnp.zeros((LANES, TQ), F32)
            return carry

        lax.fori_loop(0, n_q, prep, 0)

        def block_pair(j, carry):
            qbs = [2 * j, 2 * j + 1]
            rows = [pl.ds(pl.multiple_of(qb * TQ, TQ), TQ) for qb in qbs]
            both_q = [qq_s[qb].reshape(HEADS_PER_STEP * TQ, LANES) for qb in qbs]
            both_do = [dd_s[qb].reshape(HEADS_PER_STEP * TQ, LANES) for qb in qbs]
            ccar_s[...] = jnp.zeros(ccar_s.shape, F32)

            def tiles(items):
                cols = [pl.ds(pl.multiple_of(jnp.maximum(kb, 0) * TK, TK), TK) for _, kb in items]
                zs = [_dot_nt(both_q[s], k_ref[cs, :]).reshape(HEADS_PER_STEP, TQ, TK) for (s, _), cs in zip(items, cols)]
                dws = [_dot_nt(both_do[s], v_ref[cs, :]).reshape(HEADS_PER_STEP, TQ, TK)
                       for (s, _), cs in zip(items, cols)]
                zms, sufs = [], []
                for (s, kb), z in zip(items, zs):
                    mask = diff_s[...] > jnp.where(kb < 0, TK, jnp.where(kb == qbs[s], 0, -TK))
                    parts = []
                    for h in heads:
                        z2 = jnp.where(mask, z[h] * LOG2_E, MASKED_LOGIT)
                        sp = _softplus2(z2)
                        zms.append(z2 - sp)
                        parts += _split(sp)
                    sufs += _block_sums(parts, suffix_s[...])
                ws, dlws, pres = [], [], []
                for i, (s, kb) in enumerate(items):
                    parts = []
                    for h in heads:
                        j2 = i * HEADS_PER_STEP + h
                        cin = jnp.sum(jnp.where(lane == kb, car_ref[h, rows[s], :], 0.0), axis=1, keepdims=True)
                        w = jnp.exp2(zms[j2] - (sufs[j2] + cin))
                        ws.append(w.astype(BF16))
                        dlw = dws[i][h] * w
                        dlws.append(dlw)
                        parts += _split(dlw)
                    pres += _block_sums(parts, prefix_s[...])
                for i, ((s, kb), cs) in enumerate(zip(items, cols)):
                    dq_t, dk_t, dv_t = [], [], []
                    for h in heads:
                        j2 = i * HEADS_PER_STEP + h
                        ccar = ccar_s[s, h]
                        dz = (dlws[j2] - jnp.exp2(zms[j2]) * (dlws[j2] + (pres[j2] + _tile_lanes(ccar)))).astype(BF16)
                        ccar_s[s, h] = ccar + jnp.sum(dlws[j2], axis=1, keepdims=True)
                        dims = slice(h * HEAD_DIM, (h + 1) * HEAD_DIM)
                        dq_t.append(_dot_nt(kt_s[jnp.maximum(kb, 0), dims, :], dz))
                        dk_t.append(_dot(qt_s[qbs[s], dims, :], dz))
                        dv_t.append(_dot(dot_s[qbs[s], dims, :], ws[j2]))
                    dq_s[:, rows[s]] += jnp.concatenate(dq_t, axis=0)
                    dk_s[:, cs] += jnp.concatenate(dk_t, axis=0)
                    dv_s[:, cs] += jnp.concatenate(dv_t, axis=0)

            for s, (qb, rs) in enumerate(zip(qbs, rows)):
                lowest = jnp.min(jnp.minimum(car_ref[0, rs, :], car_ref[1, rs, :]), axis=0, keepdims=True)
                dead = jnp.logical_and(lowest >= DEAD_CARRY_BITS, lane < qb)
                first_live = jnp.sum(jnp.where(dead, 1, 0))

                def one(kb, c2):
                    tiles([(s, kb)])
                    return c2

                lax.fori_loop(first_live, qb - 1, one, 0)

            for s, qb in enumerate(qbs):
                tiles([(s, qb - 1), (s, qb)])
            return carry

        assert n_q % 2 == 0
        lax.fori_loop(0, n_q // 2, block_pair, 0)

        def finish(r, carry):
            rs = pl.ds(pl.multiple_of(r * TQ, TQ), TQ)
            dp_ref[0, rs, :] = (dq_s[:, rs].T * Q_SCALE).astype(BF16)
            dp_ref[1, rs, :] = dk_s[:, rs].T.astype(BF16)
            dp_ref[2, rs, :] = dv_s[:, rs].T.astype(BF16)
            return carry

        lax.fori_loop(0, n_q, finish, 0)

    def col(g, n_sub=CONV_WIDTH // LANES):
        return pl.BlockSpec((seq, LANES), _col_block(g, n_sub))

    n_pairs = ATTN_WIDTH // LANES
    by_head = lambda: pltpu.VMEM((n_q, HEADS_PER_STEP, TQ, LANES), BF16)
    transposed = lambda: pltpu.VMEM((n_q, LANES, TQ), BF16)
    acc_t = lambda: pltpu.VMEM((LANES, seq), F32)
    return pl.pallas_call(
        body, name="attn_bwd",
        grid=(n_pairs,),
        in_specs=[col(0), col(1), col(2), col(0), col(0), col(1),
                  pl.BlockSpec((HEADS_PER_STEP, seq, LANES), lambda p: (p, 0, 0))],
        out_specs=pl.BlockSpec((4, seq, LANES), lambda p: (0, 0, p)),
        out_shape=jax.ShapeDtypeStruct((4, seq, ATTN_WIDTH), BF16),
        scratch_shapes=[by_head(), by_head(), transposed(), transposed(), transposed(),
                        pltpu.VMEM((TK, TK), BF16), pltpu.VMEM((TK, TK), BF16),
                        pltpu.VMEM((2, HEADS_PER_STEP, TQ, LANES), F32),
                        acc_t(), acc_t(), acc_t(),
                        pltpu.VMEM((TQ, TK), jnp.int32)],
        compiler_params=_compiler_params(56, ("arbitrary",)),
    )(qkv, qkv, qkv, z_attn, o, dmix, carries)


def _out_ln(mix_c, mix_a, woutg, x, target, gain, bias):
    seq = x.shape[0]
    tm = 512
    inv_d = 1.0 / D_MODEL

    def body(mc_ref, ma_ref, wo_ref, x_ref, t_ref, g_ref, b_ref, dxa_ref, dmix_ref, gw_ref, st_ref):
        @pl.when(pl.program_id(0) == 0)
        def _():
            gw_ref[...] = jnp.zeros(gw_ref.shape, F32)
            st_ref[...] = jnp.zeros(st_ref.shape, F32)

        mc = mc_ref[...]
        ma = ma_ref[...]
        sub = _dot(mc, wo_ref[0:CONV_WIDTH, :]) + _dot(ma, wo_ref[CONV_WIDTH:, :])
        r = DEEPNORM_ALPHA * x_ref[...] + sub
        mu = jnp.mean(r, axis=-1, keepdims=True)
        rc = r - mu
        var = jnp.mean(rc * rc, axis=-1, keepdims=True)
        rstd = lax.rsqrt(var + LN_EPS)
        xhat = rc * rstd
        gain_v = g_ref[...]
        diff = (xhat * gain_v + b_ref[...]) - t_ref[...]
        dy = diff * inv_d
        st_ref[0:1, :] += jnp.sum(dy * xhat, axis=0, keepdims=True)
        st_ref[1:2, :] += jnp.sum(dy, axis=0, keepdims=True)
        st_ref[2:3, :] += jnp.sum(diff * diff, axis=0, keepdims=True)
        dxh = dy * gain_v
        m1 = jnp.mean(dxh, axis=-1, keepdims=True)
        m2 = jnp.mean(dxh * xhat, axis=-1, keepdims=True)
        dr = rstd * (dxh - m1 - xhat * m2)
        dxa_ref[...] = DEEPNORM_ALPHA * dr
        drb = dr.astype(BF16)
        dmix_ref[...] = _dot_nt(drb, wo_ref[...])
        gw_ref[0:CONV_WIDTH, :] += _dot_tn(mc, drb)
        gw_ref[CONV_WIDTH:, :] += _dot_tn(ma, drb)

    rows_d = lambda: pl.BlockSpec((tm, D_MODEL), lambda i: (i, 0))
    rows_h = lambda: pl.BlockSpec((tm, CONV_WIDTH), lambda i: (i, 0))
    whole = lambda shape: pl.BlockSpec(shape, lambda i: (0, 0))
    return pl.pallas_call(
        body, name="out_ln",
        grid=(seq // tm,),
        in_specs=[rows_h(), rows_h(), whole((D_MODEL, D_MODEL)), rows_d(), rows_d(),
                  whole((1, D_MODEL)), whole((1, D_MODEL))],
        out_specs=(rows_d(), rows_d(), whole((D_MODEL, D_MODEL)), whole((SUBLANES, D_MODEL))),
        out_shape=(jax.ShapeDtypeStruct((seq, D_MODEL), F32), jax.ShapeDtypeStruct((seq, D_MODEL), F32),
                   jax.ShapeDtypeStruct((D_MODEL, D_MODEL), F32), jax.ShapeDtypeStruct((SUBLANES, D_MODEL), F32)),
        compiler_params=_compiler_params(48, ("arbitrary",)),
    )(mix_c, mix_a, woutg, x, target, gain, bias)


def _group_maps():
    half = N_GROUPS // 2
    return (lambda g: jnp.minimum(g, half - 1)), (lambda g: jnp.maximum(g - half, 0))


def _grad_w_in(xt, dpc, dpa):
    seq = xt.shape[1]
    half = N_GROUPS // 2
    in_conv, in_attn = _group_maps()

    def body(xt_ref, dc_ref, da_ref, out_ref):
        g = pl.program_id(0)

        @pl.when(g < half)
        def _():
            out_ref[...] = _dot(xt_ref[...], dc_ref[...])

        @pl.when(g >= half)
        def _():
            out_ref[...] = _dot(xt_ref[...], da_ref[...])

    return pl.pallas_call(
        body, name="grad_w_in",
        grid=(N_GROUPS,),
        in_specs=[pl.BlockSpec((D_MODEL, seq), lambda g: (0, 0), pipeline_mode=pl.Buffered(1)),
                  pl.BlockSpec((None, seq, CONV_WIDTH), lambda g: (in_conv(g), 0, 0)),
                  pl.BlockSpec((None, seq, CONV_WIDTH), lambda g: (in_attn(g), 0, 0))],
        out_specs=pl.BlockSpec((None, D_MODEL, CONV_WIDTH), lambda g: (g // 2, 0, g % 2)),
        out_shape=jax.ShapeDtypeStruct((N_CHIPS, D_MODEL, D_MODEL), F32),
        compiler_params=_compiler_params(48, ("arbitrary",)),
    )(xt, dpc, dpa)


def _grad_x(dxa, dpc, dpa, wing):
    seq = dxa.shape[0]
    tm = 512
    half = N_GROUPS // 2

    def body(dxa_ref, dc_ref, da_ref, w_ref, out_ref):
        acc = dxa_ref[...]
        for g in range(N_GROUPS):
            dp = dc_ref[g] if g < half else da_ref[g - half]
            cols = slice((g % 2) * CONV_WIDTH, (g % 2 + 1) * CONV_WIDTH)
            acc = acc + _dot_nt(dp, w_ref[g // 2, :, cols])
        out_ref[...] = acc

    return pl.pallas_call(
        body, name="grad_x",
        grid=(seq // tm,),
        in_specs=[pl.BlockSpec((tm, D_MODEL), lambda i: (i, 0)),
                  pl.BlockSpec((half, tm, CONV_WIDTH), lambda i: (0, i, 0)),
                  pl.BlockSpec((half, tm, CONV_WIDTH), lambda i: (0, i, 0)),
                  pl.BlockSpec((N_CHIPS, D_MODEL, D_MODEL), lambda i: (0, 0, 0), pipeline_mode=pl.Buffered(1))],
        out_specs=pl.BlockSpec((tm, D_MODEL), lambda i: (i, 0)),
        out_shape=jax.ShapeDtypeStruct((seq, D_MODEL), F32),
        compiler_params=_compiler_params(40, ("arbitrary",)),
    )(dxa, dpc, dpa, wing)


PACK_LOSS_ROW = 5


def _reduce_grads(gwin, gwout, pack):
    n_shards, n_rows_in, n_cols = gwin.shape
    n_rows_out = gwout.shape[1]
    half_rows = (n_rows_in // 2, n_rows_out // 2)
    row_chunk = 128

    def body(gwin_ref, gwout_ref, pack_ref, oin_ref, oout_ref, opack_ref,
             mine_in, mine_out, sib_in, sib_out, wire_in, wire_out, rin, rout, rpack,
             local_sems, sib_send, sib_recv, send_sems, recv_sems, pack_send, pack_recv, fin_send, fin_recv):
        x, y, c = lax.axis_index("x"), lax.axis_index("y"), lax.axis_index("c")
        me = 2 * x + y
        my_id = 4 * x + 2 * y + c
        chips = [(1 - x, y), (x, 1 - y), (1 - x, 1 - y)]
        sibling = (x, y, 1 - c)
        partial = (gwin_ref, gwout_ref)
        mine = (mine_in, mine_out)
        from_sib = (sib_in, sib_out)
        wire = (wire_in, wire_out)
        from_chips = (rin, rout)
        result = (oin_ref, oout_ref)

        def half(a, which):
            rows = half_rows[a]
            return pl.ds(pl.multiple_of(which * rows, rows), rows)

        order = [2 * px + py for px, py in chips] + [me]

        def swap_copy(i, a):
            shard = order[i]
            return pltpu.make_async_remote_copy(
                src_ref=partial[a].at[shard, half(a, 1 - c), :], dst_ref=from_sib[a].at[shard],
                send_sem=sib_send.at[2 * i + a], recv_sem=sib_recv.at[2 * i + a],
                device_id=sibling, device_id_type=MESH)

        sends = []
        loads = []
        for a in range(2):
            ld = pltpu.make_async_copy(partial[a].at[:, half(a, c), :], mine[a], local_sems.at[a])
            ld.start()
            loads.append(ld)
        for i in range(n_shards):
            for a in range(2):
                cp = swap_copy(i, a)
                cp.start()
                sends.append(cp)

        rpack[my_id] = pack_ref[...]

        def pack_copy(rel, slot, to):
            return pltpu.make_async_remote_copy(
                src_ref=rpack.at[slot], dst_ref=rpack.at[slot],
                send_sem=pack_send.at[rel - 1], recv_sem=pack_recv.at[rel - 1],
                device_id=to, device_id_type=MESH)

        def related(rel):
            px = (1 - x) if rel & 4 else x
            py = (1 - y) if rel & 2 else y
            pc = (1 - c) if rel & 1 else c
            return px, py, pc

        for rel in range(1, N_DEVICES):
            cp = pack_copy(rel, my_id, related(rel))
            cp.start()
            sends.append(cp)

        for ld in loads:
            ld.wait()

        def chip_sum(a, shard):
            def add(r, carry):
                rs = pl.ds(pl.multiple_of(r * row_chunk, row_chunk), row_chunk)
                mine[a][shard, rs, :] = mine[a][shard, rs, :] + from_sib[a][shard, rs, :]
                return carry

            lax.fori_loop(0, half_rows[a] // row_chunk, add, 0)

        def chip_sum_to_wire(a, shard, k):
            def add(r, carry):
                rs = pl.ds(pl.multiple_of(r * row_chunk, row_chunk), row_chunk)
                wire[a][k, rs, :] = (mine[a][shard, rs, :] + from_sib[a][shard, rs, :]).astype(BF16)
                return carry

            lax.fori_loop(0, half_rows[a] // row_chunk, add, 0)

        def shard_copy(k, a, to):
            return pltpu.make_async_remote_copy(
                src_ref=wire[a].at[k], dst_ref=from_chips[a].at[k],
                send_sem=send_sems.at[2 * k + a], recv_sem=recv_sems.at[2 * k + a],
                device_id=to, device_id_type=MESH)

        for k, (px, py) in enumerate(chips):
            for a in range(2):
                swap_copy(k, a).wait_recv()
                chip_sum_to_wire(a, 2 * px + py, k)
                cp = shard_copy(k, a, (px, py, c))
                cp.start()
                sends.append(cp)
        for a in range(2):
            swap_copy(n_shards - 1, a).wait_recv()
            chip_sum(a, me)

        for rel in range(1, N_DEVICES):
            px, py, pc = related(rel)
            pack_copy(rel, 4 * px + 2 * py + pc, (px, py, pc)).wait_recv()
        total = rpack[0]
        for i in range(1, N_DEVICES):
            total = total + rpack[i]
        opack_ref[...] = total
        sq_err = jnp.sum(total[PACK_LOSS_ROW:PACK_LOSS_ROW + 1, :], axis=1, keepdims=True)
        opack_ref[PACK_LOSS_ROW:PACK_LOSS_ROW + 1, :] = jnp.broadcast_to(sq_err * (0.5 / D_MODEL), (1, n_cols))

        for k, (px, py) in enumerate(chips):
            for a in range(2):
                shard_copy(k, a, (px, py, c)).wait_recv()

        def finish(a):
            def add(r, carry):
                rs = pl.ds(pl.multiple_of(r * row_chunk, row_chunk), row_chunk)
                mine[a][me, rs, :] = ((mine[a][me, rs, :] + from_chips[a][0, rs, :].astype(F32))
                                      + from_chips[a][1, rs, :].astype(F32)) + from_chips[a][2, rs, :].astype(F32)
                return carry

            lax.fori_loop(0, half_rows[a] // row_chunk, add, 0)

        def final_copy(a, which):
            return pltpu.make_async_remote_copy(
                src_ref=mine[a].at[me], dst_ref=result[a].at[half(a, which), :],
                send_sem=fin_send.at[a], recv_sem=fin_recv.at[a], device_id=sibling, device_id_type=MESH)

        stores = []
        for a in range(2):
            finish(a)
            cp = final_copy(a, c)
            cp.start()
            sends.append(cp)
            st = pltpu.make_async_copy(mine[a].at[me], result[a].at[half(a, c), :], local_sems.at[2 + a])
            st.start()
            stores.append(st)
        for a in range(2):
            final_copy(a, 1 - c).wait_recv()
        for cp in sends:
            cp.wait_send()
        for st in stores:
            st.wait()

    vmem = pl.BlockSpec(memory_space=pltpu.VMEM)
    hbm = pl.BlockSpec(memory_space=pl.ANY)
    in_shape = (n_rows_in, n_cols)
    out_shape = (n_rows_out, n_cols)
    half_in = (half_rows[0], n_cols)
    half_out = (half_rows[1], n_cols)
    return pl.pallas_call(
        body, name="reduce_grads",
        out_shape=(jax.ShapeDtypeStruct(in_shape, F32), jax.ShapeDtypeStruct(out_shape, F32),
                   jax.ShapeDtypeStruct(pack.shape, F32)),
        in_specs=[hbm, hbm, vmem], out_specs=(hbm, hbm, vmem),
        scratch_shapes=[pltpu.VMEM((n_shards,) + half_in, F32), pltpu.VMEM((n_shards,) + half_out, F32),
                        pltpu.VMEM((n_shards,) + half_in, F32), pltpu.VMEM((n_shards,) + half_out, F32),
                        pltpu.VMEM((N_CHIPS - 1,) + half_in, BF16), pltpu.VMEM((N_CHIPS - 1,) + half_out, BF16),
                        pltpu.VMEM((N_CHIPS - 1,) + half_in, BF16), pltpu.VMEM((N_CHIPS - 1,) + half_out, BF16),
                        pltpu.VMEM((N_DEVICES,) + pack.shape, F32),
                        pltpu.SemaphoreType.DMA((4,)),
                        pltpu.SemaphoreType.DMA((2 * n_shards,)), pltpu.SemaphoreType.DMA((2 * n_shards,)),
                        pltpu.SemaphoreType.DMA((6,)), pltpu.SemaphoreType.DMA((6,)),
                        pltpu.SemaphoreType.DMA((N_DEVICES - 1,)), pltpu.SemaphoreType.DMA((N_DEVICES - 1,)),
                        pltpu.SemaphoreType.DMA((2,)), pltpu.SemaphoreType.DMA((2,))],
        compiler_params=_compiler_params(48),
    )(gwin, gwout, pack)


def _adamw(name, w, g, m, v):
    n_rows, n_cols = w.shape
    tr = 256 if n_rows % 256 == 0 else n_rows
    m_corr = 1.0 - ADAM_B1 ** ADAM_STEP
    v_corr = 1.0 - ADAM_B2 ** ADAM_STEP

    def body(w_ref, g_ref, m_ref, v_ref, d_ref, nm_ref, nv_ref):
        gv = g_ref[...]
        nm = ADAM_B1 * m_ref[...] + (1.0 - ADAM_B1) * gv
        nv = ADAM_B2 * v_ref[...] + (1.0 - ADAM_B2) * (gv * gv)
        nm_ref[...] = nm
        nv_ref[...] = nv
        d_ref[...] = -ADAM_LR * ((nm / m_corr) / (jnp.sqrt(nv / v_corr) + ADAM_EPS) + ADAM_WD * w_ref[...])

    blk = lambda: pl.BlockSpec((tr, n_cols), lambda i: (i, 0))
    shape = jax.ShapeDtypeStruct(w.shape, F32)
    return pl.pallas_call(
        body, name=name,
        grid=(n_rows // tr,),
        in_specs=[blk(), blk(), blk(), blk()], out_specs=(blk(), blk(), blk()),
        out_shape=(shape, shape, shape),
        compiler_params=_compiler_params(32, ("arbitrary",)),
    )(w, g, m, v)


def kernel(x, w_in, conv_w, w_out, ln_gain, ln_bias, loss_target, m_w_in, m_conv_w, m_w_out, m_ln_gain, m_ln_bias, v_w_in, v_conv_w, v_w_out, v_ln_gain, v_ln_bias):
    xs = x[0]
    target = loss_target[0]

    wing, woutg, cwg = _gather_weights(w_in[0], w_out[0], conv_w[0])
    conv_full = jnp.transpose(cwg, (1, 0, 2)).reshape(3, CONV_WIDTH)
    wout_full = woutg.reshape(D_MODEL, D_MODEL)

    p_conv, qkv, z_attn, xt, mix_c = _proj(xs, wing, conv_full)
    o, mix_a, carries = _attn_fwd(qkv, z_attn)
    dxa, dmix, gwout, stats = _out_ln(mix_c, mix_a, wout_full, xs, target, ln_gain, ln_bias)
    dpc, dconv = _conv_bwd(p_conv, dmix, conv_full)
    dpa = _attn_bwd(qkv, z_attn, o, dmix, carries)
    gwin = _grad_w_in(xt, dpc, dpa)
    grad_x = _grad_x(dxa, dpc, dpa, wing)

    pack = jnp.concatenate(
        [stats[0:2], jnp.pad(dconv[0:3], ((0, 0), (0, D_MODEL - CONV_WIDTH))), stats[2:3],
         jnp.zeros((2, D_MODEL), F32)], axis=0)
    g_w_in, g_w_out, tot = _reduce_grads(gwin, gwout.reshape(N_CHIPS, D_MODEL // N_CHIPS, D_MODEL), pack)

    chip = 2 * lax.axis_index("x") + lax.axis_index("y")
    g_gain = tot[0:1]
    g_bias = tot[1:2]
    g_conv = lax.dynamic_slice(tot, (2, chip * LANES), (3, LANES))
    loss = tot[PACK_LOSS_ROW, 0]

    d_w_in, nm_w_in, nv_w_in = _adamw("adamw_w_in", w_in[0], g_w_in, m_w_in[0], v_w_in[0])
    d_w_out, nm_w_out, nv_w_out = _adamw("adamw_w_out", w_out[0], g_w_out, m_w_out[0], v_w_out[0])
    d_conv, nm_conv, nv_conv = _adamw("adamw_conv_w", conv_w[0], g_conv, m_conv_w[0], v_conv_w[0])
    d_gain, nm_gain, nv_gain = _adamw("adamw_ln_gain", ln_gain, g_gain, m_ln_gain, v_ln_gain)
    d_bias, nm_bias, nv_bias = _adamw("adamw_ln_bias", ln_bias, g_bias, m_ln_bias, v_ln_bias)

    lead = lambda a: a[None]
    return (loss, lead(grad_x),
            lead(g_w_in), lead(g_conv), lead(g_w_out), g_gain, g_bias,
            lead(d_w_in), lead(d_conv), lead(d_w_out), d_gain, d_bias,
            lead(nm_w_in), lead(nm_conv), lead(nm_w_out), nm_gain, nm_bias,
            lead(nv_w_in), lead(nv_conv), lead(nv_w_out), nv_gain, nv_bias)
```

```python
import jax
import jax.numpy as jnp
from jax import lax
from jax.experimental import pallas as pl
from jax.experimental.pallas import tpu as pltpu

F32 = jnp.float32
BF16 = jnp.bfloat16
MESH = pl.DeviceIdType.MESH

D_MODEL = 1024
CONV_WIDTH = 512
ATTN_WIDTH = 512
HEAD_DIM = 64
N_GROUPS = 8
N_CHIPS = 4
N_DEVICES = 8
LN_EPS = 1e-5
DEEPNORM_ALPHA = 2.0 ** 0.25
Q_SCALE = HEAD_DIM ** -0.5
ADAM_LR = 0.001
ADAM_B1 = 0.9
ADAM_B2 = 0.999
ADAM_EPS = 1e-08
ADAM_WD = 0.01
ADAM_STEP = 10

LANES = 128
SUBLANES = 8
V7X_VMEM_BYTES = 64 * 1024 * 1024
MIB = 1024 * 1024

TQ = 256
TK = 256
HEADS_PER_STEP = LANES // HEAD_DIM


def _compiler_params(vmem_mib, semantics=None):
    assert vmem_mib * MIB < V7X_VMEM_BYTES
    return pltpu.CompilerParams(dimension_semantics=semantics, vmem_limit_bytes=vmem_mib * MIB)


def _sigmoid(z):
    return 1.0 / (1.0 + jnp.exp(-z))


def _dot(a, b):
    return jnp.dot(a, b, preferred_element_type=F32)


def _dot_nt(a, b):
    return lax.dot_general(a, b, (((1,), (1,)), ((), ())), preferred_element_type=F32)


def _dot_tn(a, b):
    return lax.dot_general(a, b, (((0,), (0,)), ((), ())), preferred_element_type=F32)


def _truncate_to_bf16(a):
    bits = lax.bitcast_convert_type(a, jnp.uint32) & jnp.uint32(0xFFFF0000)
    return lax.bitcast_convert_type(bits, F32)


def _split(a):
    hi = _truncate_to_bf16(a)
    return [hi.astype(BF16), (a - hi).astype(BF16)]


def _block_sums(parts, tri):
    n = len(parts) // 2
    res = _dot(jnp.concatenate(parts, axis=0), tri).reshape(n, 2, TQ, TK)
    return [res[i, 0] + res[i, 1] for i in range(n)]


MASKED_LOGIT = -1e30
LOG2_E = 1.4426950408889634


def _softplus2(z2):
    return jnp.maximum(z2, 0.0) + jnp.log2(1.0 + jnp.exp2(-jnp.abs(z2)))


def _gather_weights(w_in, w_out, conv_w):
    d_rows, d_cols = w_in.shape
    o_rows, o_cols = w_out.shape
    half_rows = (d_rows // 2, o_rows // 2)
    row_chunk = 128

    def body(win_ref, wout_ref, cw_ref, wing_ref, woutg_ref, cwg_ref, cast_in, cast_out,
             send_sems, recv_sems, pass_send, pass_recv, store_sems):
        x, y, c = lax.axis_index("x"), lax.axis_index("y"), lax.axis_index("c")
        me = 2 * x + y
        srcs = (win_ref, wout_ref)
        bufs = (wing_ref, woutg_ref)
        casts = (cast_in, cast_out)

        def rows_of(a, which):
            rows = half_rows[a]
            return pl.ds(pl.multiple_of(which * rows, rows), rows)

        def half(a, shard, which):
            return bufs[a].at[shard, rows_of(a, which), :]

        def cast_half(which):
            for a in range(2):
                def cast(r, carry):
                    rows = pl.ds(pl.multiple_of(which * half_rows[a] + r * row_chunk, row_chunk), row_chunk)
                    casts[a][rows, :] = srcs[a][rows, :].astype(BF16)
                    return carry

                lax.fori_loop(0, half_rows[a] // row_chunk, cast, 0)

        chips = [(1 - x, y), (x, 1 - y), (1 - x, 1 - y)]

        def ici_copy(k, a, shard, to):
            if a == 2:
                src = dst = cwg_ref.at[shard]
            else:
                src, dst = casts[a].at[rows_of(a, c), :], half(a, shard, c)
            return pltpu.make_async_remote_copy(
                src_ref=src, dst_ref=dst, send_sem=send_sems.at[3 * k + a], recv_sem=recv_sems.at[3 * k + a],
                device_id=to, device_id_type=MESH)

        def pass_copy(k, a, shard, which):
            ref = half(a, shard, which)
            return pltpu.make_async_remote_copy(
                src_ref=ref, dst_ref=ref, send_sem=pass_send.at[2 * k + a], recv_sem=pass_recv.at[2 * k + a],
                device_id=(x, y, 1 - c), device_id_type=MESH)

        cast_half(c)
        cwg_ref[me] = cw_ref[...]
        sends = []
        for k, (px, py) in enumerate(chips):
            for a in range(3):
                cp = ici_copy(k, a, me, (px, py, c))
                cp.start()
                sends.append(cp)
        cast_half(1 - c)
        stores = []
        for a in range(2):
            st = pltpu.make_async_copy(casts[a], bufs[a].at[me], store_sems.at[a])
            st.start()
            stores.append(st)
        for k, (px, py) in enumerate(chips):
            for a in range(2):
                ici_copy(k, a, 2 * px + py, (px, py, c)).wait_recv()
                cp = pass_copy(k, a, 2 * px + py, c)
                cp.start()
                sends.append(cp)
        for k, (px, py) in enumerate(chips):
            ici_copy(k, 2, 2 * px + py, (px, py, c)).wait_recv()
            for a in range(2):
                pass_copy(k, a, 2 * px + py, 1 - c).wait_recv()
        for cp in sends:
            cp.wait_send()
        for st in stores:
            st.wait()

    vmem = pl.BlockSpec(memory_space=pltpu.VMEM)
    hbm = pl.BlockSpec(memory_space=pl.ANY)
    return pl.pallas_call(
        body, name="gather_weights",
        out_shape=(jax.ShapeDtypeStruct((N_CHIPS, d_rows, d_cols), BF16),
                   jax.ShapeDtypeStruct((N_CHIPS, o_rows, o_cols), BF16),
                   jax.ShapeDtypeStruct((N_CHIPS,) + conv_w.shape, F32)),
        in_specs=[vmem, vmem, vmem], out_specs=(hbm, hbm, vmem),
        scratch_shapes=[pltpu.VMEM((d_rows, d_cols), BF16), pltpu.VMEM((o_rows, o_cols), BF16),
                        pltpu.SemaphoreType.DMA((9,)), pltpu.SemaphoreType.DMA((9,)),
                        pltpu.SemaphoreType.DMA((6,)), pltpu.SemaphoreType.DMA((6,)),
                        pltpu.SemaphoreType.DMA((2,))],
        compiler_params=_compiler_params(32),
    )(w_in, w_out, conv_w)


def _proj(x, wing, conv_w):
    seq = x.shape[0]
    tm = 512
    w = CONV_WIDTH

    def body(x_ref, w_ref, cw_ref, conv_ref, qkv_ref, za_ref, xt_ref, mix_ref, tail_s):
        xv = x_ref[...]
        xb = xv.astype(BF16)
        bc = _dot(xb, w_ref[0])
        hz = _dot(xb, w_ref[1])
        conv_ref[:, 0:D_MODEL] = bc
        conv_ref[:, D_MODEL:] = hz

        @pl.when(pl.program_id(0) == 0)
        def _():
            tail_s[...] = jnp.zeros(tail_s.shape, F32)

        u = bc[:, w:] * hz[:, 0:w]
        ext = jnp.concatenate([tail_s[...], u], axis=0)
        tail_s[...] = u[tm - SUBLANES:, :]
        taps = cw_ref[...]
        y = taps[2:3, :] * u
        y = y + taps[0:1, :] * _shift_down(ext, 2, tm)
        y = y + taps[1:2, :] * _shift_down(ext, 1, tm)
        z = hz[:, w:]
        mix_ref[...] = ((z * _sigmoid(z)) * (bc[:, 0:w] * y)).astype(BF16)

        qk = _dot(xb, w_ref[2])
        qkv_ref[:, 0:w] = (qk[:, 0:w] * Q_SCALE).astype(BF16)
        qkv_ref[:, w:2 * w] = qk[:, w:].astype(BF16)
        vz = _dot(xb, w_ref[3])
        qkv_ref[:, 2 * w:] = vz[:, 0:w].astype(BF16)
        za_ref[...] = vz[:, w:]
        xt_ref[...] = xv.T.astype(BF16)

    rows = lambda width: pl.BlockSpec((tm, width), lambda i: (i, 0))
    return pl.pallas_call(
        body, name="proj",
        grid=(seq // tm,),
        in_specs=[rows(D_MODEL),
                  pl.BlockSpec((N_CHIPS, D_MODEL, D_MODEL), lambda i: (0, 0, 0), pipeline_mode=pl.Buffered(1)),
                  pl.BlockSpec((3, w), lambda i: (0, 0))],
        out_specs=(rows(4 * w), rows(3 * w), rows(w), pl.BlockSpec((D_MODEL, tm), lambda i: (0, i)), rows(w)),
        out_shape=(jax.ShapeDtypeStruct((seq, 4 * w), F32), jax.ShapeDtypeStruct((seq, 3 * w), BF16),
                   jax.ShapeDtypeStruct((seq, w), F32), jax.ShapeDtypeStruct((D_MODEL, seq), BF16),
                   jax.ShapeDtypeStruct((seq, w), BF16)),
        scratch_shapes=[pltpu.VMEM((SUBLANES, w), F32)],
        compiler_params=_compiler_params(48, ("arbitrary",)),
    )(x, wing, conv_w)


def _col_block(group, n_sub=CONV_WIDTH // LANES):
    return lambda j: (0, group * n_sub + j)


def _shift_down(ext, k, rows):
    return pltpu.roll(ext, k, 0)[SUBLANES:, :]


def _shift_up(ext, k, rows):
    return pltpu.roll(ext, rows + SUBLANES - k, 0)[:rows, :]


def _tri_masks():
    r = lax.broadcasted_iota(jnp.int32, (TK, TK), 0)
    s = lax.broadcasted_iota(jnp.int32, (TK, TK), 1)
    return r, s


def _head_mask():
    lane = lax.broadcasted_iota(jnp.int32, (1, LANES), 1)
    return lane < HEAD_DIM


def _fill_tri(ref, ones):
    ref[...] = jnp.where(ones, 1.0, 0.0).astype(BF16)


def _tile_lanes(a):
    return jnp.tile(a, (1, TK // LANES))


DEAD_CARRY_BITS = 160.0


def _attn_fwd(qkv, z_attn):
    seq = qkv.shape[0]
    n_q = seq // TQ
    n_kb = seq // TK
    assert TQ == TK and n_kb <= LANES
    heads = range(HEADS_PER_STEP)

    def body(q_ref, k_ref, v_ref, za_ref, o_ref, mix_ref, car_ref, qq_s, vt_s, suffix_s, carry_s, diff_s, ot_s):
        head_a = _head_mask()
        r_i, s_i = _tri_masks()
        diff_s[...] = r_i - s_i
        _fill_tri(suffix_s, r_i > s_i)
        lane = lax.broadcasted_iota(jnp.int32, (1, LANES), 1)

        def prep(r, carry):
            rs = pl.ds(pl.multiple_of(r * TQ, TQ), TQ)
            q = q_ref[rs, :]
            qq_s[r, 0] = jnp.where(head_a, q, jnp.zeros_like(q))
            qq_s[r, 1] = jnp.where(head_a, jnp.zeros_like(q), q)
            vt_s[r] = v_ref[rs, :].astype(F32).T.astype(BF16)
            return carry

        lax.fori_loop(0, n_q, prep, 0)

        def block_pair(j, carry):
            qbs = [2 * j, 2 * j + 1]
            rows = [pl.ds(pl.multiple_of(qb * TQ, TQ), TQ) for qb in qbs]
            both = [qq_s[qb].reshape(HEADS_PER_STEP * TQ, LANES) for qb in qbs]
            carry_s[...] = jnp.zeros(carry_s.shape, F32)
            ot_s[...] = jnp.zeros(ot_s.shape, F32)
            for rs in rows:
                for h in heads:
                    car_ref[h, rs, :] = jnp.zeros((TQ, LANES), F32)

            def tiles(items):
                kbs = [qbs[s] - n for s, n in items]
                cols = [pl.ds(pl.multiple_of(jnp.maximum(kb, 0) * TK, TK), TK) for kb in kbs]
                zs = [_dot_nt(both[s], k_ref[cs, :]).reshape(HEADS_PER_STEP, TQ, TK) for (s, _), cs in zip(items, cols)]
                zms, cars, sufs = [], [], []
                for (s, n), kb, z in zip(items, kbs, zs):
                    mask = diff_s[...] > jnp.where(kb < 0, TK, jnp.where(n == 0, 0, -TK))
                    parts = []
                    for h in heads:
                        z2 = jnp.where(mask, z[h] * LOG2_E, MASKED_LOGIT)
                        sp = _softplus2(z2)
                        zms.append(z2 - sp)
                        parts += _split(sp)
                        car = carry_s[s, h]
                        cars.append(car)
                        car_ref[h, rows[s], :] = jnp.where(lane == kb, car, car_ref[h, rows[s], :])
                        carry_s[s, h] = car + jnp.sum(sp, axis=1, keepdims=True)
                    sufs += _block_sums(parts, suffix_s[...])
                for i, ((s, _), kb) in enumerate(zip(items, kbs)):
                    out = []
                    for h in heads:
                        j2 = i * HEADS_PER_STEP + h
                        w = jnp.exp2(zms[j2] - (sufs[j2] + _tile_lanes(cars[j2]))).astype(BF16)
                        out.append(_dot_nt(vt_s[jnp.maximum(kb, 0), h * HEAD_DIM:(h + 1) * HEAD_DIM, :], w))
                    ot_s[s] += jnp.concatenate(out, axis=0)

            tiles([(0, 0), (0, 1), (1, 0), (1, 1)])

            for s, (qb, rs) in enumerate(zip(qbs, rows)):
                def lowest_carry():
                    return jnp.min(jnp.minimum(carry_s[s, 0], carry_s[s, 1]))

                def live(state):
                    n, lowest = state
                    return jnp.logical_and(n <= qb, lowest < DEAD_CARRY_BITS)

                def step(state):
                    n, _ = state
                    tiles([(s, n)])
                    return n + 1, lowest_carry()

                n_done, _ = lax.while_loop(live, step, (jnp.int32(2), lowest_carry()))
                for h in heads:
                    car_ref[h, rs, :] = jnp.where(lane < qb + 1 - n_done, carry_s[s, h], car_ref[h, rs, :])
                o = ot_s[s].T
                o_ref[rs, :] = o
                za = za_ref[rs, :]
                mix_ref[rs, :] = ((za * _sigmoid(za)) * o).astype(BF16)
            return carry

        assert n_q % 2 == 0
        lax.fori_loop(0, n_q // 2, block_pair, 0)

    col = lambda g: pl.BlockSpec((seq, LANES), _col_block(g))
    n_pairs = ATTN_WIDTH // LANES
    return pl.pallas_call(
        body, name="attn_fwd",
        grid=(n_pairs,),
        in_specs=[col(0), col(1), col(2), col(0)],
        out_specs=(pl.BlockSpec((seq, LANES), lambda p: (0, p)),
                   pl.BlockSpec((seq, LANES), lambda p: (0, p)),
                   pl.BlockSpec((HEADS_PER_STEP, seq, LANES), lambda p: (p, 0, 0))),
        out_shape=(jax.ShapeDtypeStruct((seq, ATTN_WIDTH), F32),
                   jax.ShapeDtypeStruct((seq, ATTN_WIDTH), BF16),
                   jax.ShapeDtypeStruct((n_pairs * HEADS_PER_STEP, seq, LANES), F32)),
        scratch_shapes=[pltpu.VMEM((n_q, HEADS_PER_STEP, TQ, LANES), BF16),
                        pltpu.VMEM((n_kb, LANES, TK), BF16),
                        pltpu.VMEM((TK, TK), BF16),
                        pltpu.VMEM((2, HEADS_PER_STEP, TQ, LANES), F32),
                        pltpu.VMEM((TQ, TK), jnp.int32),
                        pltpu.VMEM((2, LANES, TQ), F32)],
        compiler_params=_compiler_params(48, ("arbitrary",)),
    )(qkv, qkv, qkv, z_attn)


def _attn_bwd(qkv, z_attn, o, dmix, carries):
    seq = qkv.shape[0]
    n_q = seq // TQ
    n_kb = seq // TK
    assert TQ == TK and n_kb <= LANES
    heads = range(HEADS_PER_STEP)

    def body(q_ref, k_ref, v_ref, za_ref, o_ref, dm_ref, car_ref, dp_ref,
             qq_s, dd_s, qt_s, dot_s, kt_s, suffix_s, prefix_s, ccar_s, dq_s, dk_s, dv_s, diff_s):
        head_a = _head_mask()
        r_i, s_i = _tri_masks()
        diff_s[...] = r_i - s_i
        _fill_tri(suffix_s, r_i > s_i)
        _fill_tri(prefix_s, r_i < s_i)
        lane = lax.broadcasted_iota(jnp.int32, (1, LANES), 1)

        def prep(r, carry):
            rs = pl.ds(pl.multiple_of(r * TQ, TQ), TQ)
            q = q_ref[rs, :]
            qq_s[r, 0] = jnp.where(head_a, q, jnp.zeros_like(q))
            qq_s[r, 1] = jnp.where(head_a, jnp.zeros_like(q), q)
            qt_s[r] = q.astype(F32).T.astype(BF16)
            kt_s[r] = k_ref[rs, :].astype(F32).T.astype(BF16)
            za = za_ref[rs, :]
            sg = _sigmoid(za)
            dm = dm_ref[rs, :]
            do = dm * (za * sg)
            dd_s[r, 0] = jnp.where(head_a, do, 0.0).astype(BF16)
            dd_s[r, 1] = jnp.where(head_a, 0.0, do).astype(BF16)
            dot_s[r] = do.T.astype(BF16)
            dp_ref[3, rs, :] = (dm * o_ref[rs, :] * (sg * (1.0 + za * (1.0 - sg)))).astype(BF16)
            dq_s[:, rs] = jnp.zeros((LANES, TQ), F32)
            dk_s[:, rs] = jnp.zeros((LANES, TQ), F32)
            dv_s[:, rs] = jnp.zeros((LANES, TQ), F32)
            return carry

        lax.fori_loop(0, n_q, prep, 0)

        def block_pair(j, carry):
            qbs = [2 * j, 2 * j + 1]
            rows = [pl.ds(pl.multiple_of(qb * TQ, TQ), TQ) for qb in qbs]
            both_q = [qq_s[qb].reshape(HEADS_PER_STEP * TQ, LANES) for qb in qbs]
            both_do = [dd_s[qb].reshape(HEADS_PER_STEP * TQ, LANES) for qb in qbs]
            ccar_s[...] = jnp.zeros(ccar_s.shape, F32)

            def tiles(items):
                cols = [pl.ds(pl.multiple_of(jnp.maximum(kb, 0) * TK, TK), TK) for _, kb in items]
                zs = [_dot_nt(both_q[s], k_ref[cs, :]).reshape(HEADS_PER_STEP, TQ, TK) for (s, _), cs in zip(items, cols)]
                dws = [_dot_nt(both_do[s], v_ref[cs, :]).reshape(HEADS_PER_STEP, TQ, TK)
                       for (s, _), cs in zip(items, cols)]
                zms, sufs = [], []
                for (s, kb), z in zip(items, zs):
                    mask = diff_s[...] > jnp.where(kb < 0, TK, jnp.where(kb == qbs[s], 0, -TK))
                    parts = []
                    for h in heads:
                        z2 = jnp.where(mask, z[h] * LOG2_E, MASKED_LOGIT)
                        sp = _softplus2(z2)
                        zms.append(z2 - sp)
                        parts += _split(sp)
                    sufs += _block_sums(parts, suffix_s[...])
                ws, dlws, pres = [], [], []
                for i, (s, kb) in enumerate(items):
                    parts = []
                    for h in heads:
                        j2 = i * HEADS_PER_STEP + h
                        cin = jnp.sum(jnp.where(lane == kb, car_ref[h, rows[s], :], 0.0), axis=1, keepdims=True)
                        w = jnp.exp2(zms[j2] - (sufs[j2] + cin))
                        ws.append(w.astype(BF16))
                        dlw = dws[i][h] * w
                        dlws.append(dlw)
                        parts += _split(dlw)
                    pres += _block_sums(parts, prefix_s[...])
                for i, ((s, kb), cs) in enumerate(zip(items, cols)):
                    dq_t, dk_t, dv_t = [], [], []
                    for h in heads:
                        j2 = i * HEADS_PER_STEP + h
                        ccar = ccar_s[s, h]
                        dz = (dlws[j2] - jnp.exp2(zms[j2]) * (dlws[j2] + (pres[j2] + _tile_lanes(ccar)))).astype(BF16)
                        ccar_s[s, h] = ccar + jnp.sum(dlws[j2], axis=1, keepdims=True)
                        dims = slice(h * HEAD_DIM, (h + 1) * HEAD_DIM)
                        dq_t.append(_dot_nt(kt_s[jnp.maximum(kb, 0), dims, :], dz))
                        dk_t.append(_dot(qt_s[qbs[s], dims, :], dz))
                        dv_t.append(_dot(dot_s[qbs[s], dims, :], ws[j2]))
                    dq_s[:, rows[s]] += jnp.concatenate(dq_t, axis=0)
                    dk_s[:, cs] += jnp.concatenate(dk_t, axis=0)
                    dv_s[:, cs] += jnp.concatenate(dv_t, axis=0)

            for s, (qb, rs) in enumerate(zip(qbs, rows)):
                lowest = jnp.min(jnp.minimum(car_ref[0, rs, :], car_ref[1, rs, :]), axis=0, keepdims=True)
                dead = jnp.logical_and(lowest >= DEAD_CARRY_BITS, lane < qb)
                first_live = jnp.sum(jnp.where(dead, 1, 0))

                def one(kb, c2):
                    tiles([(s, kb)])
                    return c2

                lax.fori_loop(first_live, qb - 1, one, 0)

            for s, qb in enumerate(qbs):
                tiles([(s, qb - 1), (s, qb)])
            return carry

        assert n_q % 2 == 0
        lax.fori_loop(0, n_q // 2, block_pair, 0)

        def finish(r, carry):
            rs = pl.ds(pl.multiple_of(r * TQ, TQ), TQ)
            dp_ref[0, rs, :] = (dq_s[:, rs].T * Q_SCALE).astype(BF16)
            dp_ref[1, rs, :] = dk_s[:, rs].T.astype(BF16)
            dp_ref[2, rs, :] = dv_s[:, rs].T.astype(BF16)
            return carry

        lax.fori_loop(0, n_q, finish, 0)

    def col(g, n_sub=CONV_WIDTH // LANES):
        return pl.BlockSpec((seq, LANES), _col_block(g, n_sub))

    n_pairs = ATTN_WIDTH // LANES
    by_head = lambda: pltpu.VMEM((n_q, HEADS_PER_STEP, TQ, LANES), BF16)
    transposed = lambda: pltpu.VMEM((n_q, LANES, TQ), BF16)
    acc_t = lambda: pltpu.VMEM((LANES, seq), F32)
    return pl.pallas_call(
        body, name="attn_bwd",
        grid=(n_pairs,),
        in_specs=[col(0), col(1), col(2), col(0), col(0), col(0),
                  pl.BlockSpec((HEADS_PER_STEP, seq, LANES), lambda p: (p, 0, 0))],
        out_specs=pl.BlockSpec((4, seq, LANES), lambda p: (0, 0, p)),
        out_shape=jax.ShapeDtypeStruct((4, seq, ATTN_WIDTH), BF16),
        scratch_shapes=[by_head(), by_head(), transposed(), transposed(), transposed(),
                        pltpu.VMEM((TK, TK), BF16), pltpu.VMEM((TK, TK), BF16),
                        pltpu.VMEM((2, HEADS_PER_STEP, TQ, LANES), F32),
                        acc_t(), acc_t(), acc_t(),
                        pltpu.VMEM((TQ, TK), jnp.int32)],
        compiler_params=_compiler_params(56, ("arbitrary",)),
    )(qkv, qkv, qkv, z_attn, o, dmix, carries)


def _out_ln(mix_c, mix_a, woutg, x, target, gain, bias, p_conv, conv_w):
    seq = x.shape[0]
    tm = 512
    n_blocks = seq // tm
    halo_per_block = tm // SUBLANES
    inv_d = 1.0 / D_MODEL
    w = CONV_WIDTH

    def body(mc_ref, ma_ref, wo_ref, x_ref, t_ref, g_ref, b_ref, pc_ref, halo_ref, cw_ref,
             dxa_ref, dmix_ref, gw_ref, st_ref, dp_ref, dcw_ref, head_s):
        @pl.when(pl.program_id(0) == 0)
        def _():
            gw_ref[...] = jnp.zeros(gw_ref.shape, F32)
            st_ref[...] = jnp.zeros(st_ref.shape, F32)
            dcw_ref[...] = jnp.zeros(dcw_ref.shape, F32)
            head_s[...] = jnp.zeros(head_s.shape, F32)

        mc = mc_ref[...]
        ma = ma_ref[...]
        sub = _dot(mc, wo_ref[0:CONV_WIDTH, :]) + _dot(ma, wo_ref[CONV_WIDTH:, :])
        r = DEEPNORM_ALPHA * x_ref[...] + sub
        mu = jnp.mean(r, axis=-1, keepdims=True)
        rc = r - mu
        var = jnp.mean(rc * rc, axis=-1, keepdims=True)
        rstd = lax.rsqrt(var + LN_EPS)
        xhat = rc * rstd
        gain_v = g_ref[...]
        diff = (xhat * gain_v + b_ref[...]) - t_ref[...]
        dy = diff * inv_d
        st_ref[0:1, :] += jnp.sum(dy * xhat, axis=0, keepdims=True)
        st_ref[1:2, :] += jnp.sum(dy, axis=0, keepdims=True)
        st_ref[2:3, :] += jnp.sum(diff * diff, axis=0, keepdims=True)
        dxh = dy * gain_v
        m1 = jnp.mean(dxh, axis=-1, keepdims=True)
        m2 = jnp.mean(dxh * xhat, axis=-1, keepdims=True)
        dr = rstd * (dxh - m1 - xhat * m2)
        dxa_ref[...] = DEEPNORM_ALPHA * dr
        drb = dr.astype(BF16)
        dmix = _dot_nt(drb, wo_ref[...])
        dmix_ref[...] = dmix[:, w:]
        gw_ref[0:CONV_WIDTH, :] += _dot_tn(mc, drb)
        gw_ref[CONV_WIDTH:, :] += _dot_tn(ma, drb)

        dco = dmix[:, 0:w]
        b, c = pc_ref[:, 0:w], pc_ref[:, w:2 * w]
        h, z = pc_ref[:, 2 * w:3 * w], pc_ref[:, 3 * w:]
        taps = cw_ref[...]
        first_rows = pl.program_id(0) == n_blocks - 1
        u_halo = jnp.where(first_rows, 0.0, halo_ref[:, w:2 * w] * halo_ref[:, 2 * w:3 * w])
        u = c * h
        ext_u = jnp.concatenate([u_halo, u], axis=0)
        um1 = _shift_down(ext_u, 1, tm)
        um2 = _shift_down(ext_u, 2, tm)
        y = taps[2:3, :] * u
        y = y + taps[0:1, :] * um2
        y = y + taps[1:2, :] * um1
        sg = _sigmoid(z)
        g = z * sg
        dp_ref[0] = (dco * g * y).astype(BF16)
        dp_ref[3] = (dco * b * y * (sg * (1.0 + z * (1.0 - sg)))).astype(BF16)
        dy_conv = dco * g * b
        ext_dy = jnp.concatenate([dy_conv, head_s[...]], axis=0)
        head_s[...] = dy_conv[0:SUBLANES, :]
        du = (taps[2:3, :] * dy_conv + taps[1:2, :] * _shift_up(ext_dy, 1, tm)
              + taps[0:1, :] * _shift_up(ext_dy, 2, tm))
        dp_ref[1] = (du * h).astype(BF16)
        dp_ref[2] = (du * c).astype(BF16)
        dcw_ref[0:1, :] += jnp.sum(dy_conv * um2, axis=0, keepdims=True)
        dcw_ref[1:2, :] += jnp.sum(dy_conv * um1, axis=0, keepdims=True)
        dcw_ref[2:3, :] += jnp.sum(dy_conv * u, axis=0, keepdims=True)

    rev = lambda i: n_blocks - 1 - i
    rows = lambda width: pl.BlockSpec((tm, width), lambda i: (rev(i), 0))
    whole = lambda shape: pl.BlockSpec(shape, lambda i: (0, 0))
    halo = pl.BlockSpec((SUBLANES, 4 * w), lambda i: (jnp.maximum(rev(i) * halo_per_block - 1, 0), 0))
    return pl.pallas_call(
        body, name="out_ln",
        grid=(n_blocks,),
        in_specs=[rows(w), rows(w), whole((D_MODEL, D_MODEL)), rows(D_MODEL), rows(D_MODEL),
                  whole((1, D_MODEL)), whole((1, D_MODEL)), rows(4 * w), halo, whole((3, w))],
        out_specs=(rows(D_MODEL), rows(w), whole((D_MODEL, D_MODEL)), whole((SUBLANES, D_MODEL)),
                   pl.BlockSpec((4, tm, w), lambda i: (0, rev(i), 0)), whole((SUBLANES, w))),
        out_shape=(jax.ShapeDtypeStruct((seq, D_MODEL), F32), jax.ShapeDtypeStruct((seq, w), F32),
                   jax.ShapeDtypeStruct((D_MODEL, D_MODEL), F32), jax.ShapeDtypeStruct((SUBLANES, D_MODEL), F32),
                   jax.ShapeDtypeStruct((4, seq, w), BF16), jax.ShapeDtypeStruct((SUBLANES, w), F32)),
        scratch_shapes=[pltpu.VMEM((SUBLANES, w), F32)],
        compiler_params=_compiler_params(56, ("arbitrary",)),
    )(mix_c, mix_a, woutg, x, target, gain, bias, p_conv, p_conv, conv_w)


def _group_maps():
    half = N_GROUPS // 2
    return (lambda g: jnp.minimum(g, half - 1)), (lambda g: jnp.maximum(g - half, 0))


def _grad_w_in(xt, dpc, dpa):
    seq = xt.shape[1]
    half = N_GROUPS // 2
    in_conv, in_attn = _group_maps()

    def body(xt_ref, dc_ref, da_ref, out_ref):
        g = pl.program_id(0)

        @pl.when(g < half)
        def _():
            out_ref[...] = _dot(xt_ref[...], dc_ref[...])

        @pl.when(g >= half)
        def _():
            out_ref[...] = _dot(xt_ref[...], da_ref[...])

    return pl.pallas_call(
        body, name="grad_w_in",
        grid=(N_GROUPS,),
        in_specs=[pl.BlockSpec((D_MODEL, seq), lambda g: (0, 0), pipeline_mode=pl.Buffered(1)),
                  pl.BlockSpec((None, seq, CONV_WIDTH), lambda g: (in_conv(g), 0, 0)),
                  pl.BlockSpec((None, seq, CONV_WIDTH), lambda g: (in_attn(g), 0, 0))],
        out_specs=pl.BlockSpec((None, D_MODEL, CONV_WIDTH), lambda g: (g // 2, 0, g % 2)),
        out_shape=jax.ShapeDtypeStruct((N_CHIPS, D_MODEL, D_MODEL), F32),
        compiler_params=_compiler_params(48, ("arbitrary",)),
    )(xt, dpc, dpa)


def _grad_x(dxa, dpc, dpa, wing):
    seq = dxa.shape[0]
    tm = 512
    half = N_GROUPS // 2

    def body(dxa_ref, dc_ref, da_ref, w_ref, out_ref):
        acc = dxa_ref[...]
        for g in range(N_GROUPS):
            dp = dc_ref[g] if g < half else da_ref[g - half]
            cols = slice((g % 2) * CONV_WIDTH, (g % 2 + 1) * CONV_WIDTH)
            acc = acc + _dot_nt(dp, w_ref[g // 2, :, cols])
        out_ref[...] = acc

    return pl.pallas_call(
        body, name="grad_x",
        grid=(seq // tm,),
        in_specs=[pl.BlockSpec((tm, D_MODEL), lambda i: (i, 0)),
                  pl.BlockSpec((half, tm, CONV_WIDTH), lambda i: (0, i, 0)),
                  pl.BlockSpec((half, tm, CONV_WIDTH), lambda i: (0, i, 0)),
                  pl.BlockSpec((N_CHIPS, D_MODEL, D_MODEL), lambda i: (0, 0, 0), pipeline_mode=pl.Buffered(1))],
        out_specs=pl.BlockSpec((tm, D_MODEL), lambda i: (i, 0)),
        out_shape=jax.ShapeDtypeStruct((seq, D_MODEL), F32),
        compiler_params=_compiler_params(40, ("arbitrary",)),
    )(dxa, dpc, dpa, wing)


PACK_LOSS_ROW = 5


def _reduce_grads(gwin, gwout, pack):
    n_shards, n_rows_in, n_cols = gwin.shape
    n_rows_out = gwout.shape[1]
    half_rows = (n_rows_in // 2, n_rows_out // 2)
    row_chunk = 128

    def body(gwin_ref, gwout_ref, pack_ref, oin_ref, oout_ref, opack_ref,
             mine_in, mine_out, sib_in, sib_out, wire_in, wire_out, rin, rout, rpack,
             local_sems, sib_send, sib_recv, send_sems, recv_sems, pack_send, pack_recv, fin_send, fin_recv):
        x, y, c = lax.axis_index("x"), lax.axis_index("y"), lax.axis_index("c")
        me = 2 * x + y
        my_id = 4 * x + 2 * y + c
        chips = [(1 - x, y), (x, 1 - y), (1 - x, 1 - y)]
        sibling = (x, y, 1 - c)
        partial = (gwin_ref, gwout_ref)
        mine = (mine_in, mine_out)
        from_sib = (sib_in, sib_out)
        wire = (wire_in, wire_out)
        from_chips = (rin, rout)
        result = (oin_ref, oout_ref)

        def half(a, which):
            rows = half_rows[a]
            return pl.ds(pl.multiple_of(which * rows, rows), rows)

        order = [2 * px + py for px, py in chips] + [me]

        def swap_copy(i, a):
            shard = order[i]
            return pltpu.make_async_remote_copy(
                src_ref=partial[a].at[shard, half(a, 1 - c), :], dst_ref=from_sib[a].at[shard],
                send_sem=sib_send.at[2 * i + a], recv_sem=sib_recv.at[2 * i + a],
                device_id=sibling, device_id_type=MESH)

        sends = []
        loads = []
        for a in range(2):
            ld = pltpu.make_async_copy(partial[a].at[:, half(a, c), :], mine[a], local_sems.at[a])
            ld.start()
            loads.append(ld)
        for i in range(n_shards):
            for a in range(2):
                cp = swap_copy(i, a)
                cp.start()
                sends.append(cp)

        rpack[my_id] = pack_ref[...]

        def pack_copy(rel, slot, to):
            return pltpu.make_async_remote_copy(
                src_ref=rpack.at[slot], dst_ref=rpack.at[slot],
                send_sem=pack_send.at[rel - 1], recv_sem=pack_recv.at[rel - 1],
                device_id=to, device_id_type=MESH)

        def related(rel):
            px = (1 - x) if rel & 4 else x
            py = (1 - y) if rel & 2 else y
            pc = (1 - c) if rel & 1 else c
            return px, py, pc

        for rel in range(1, N_DEVICES):
            cp = pack_copy(rel, my_id, related(rel))
            cp.start()
            sends.append(cp)

        for ld in loads:
            ld.wait()

        def chip_sum(a, shard):
            def add(r, carry):
                rs = pl.ds(pl.multiple_of(r * row_chunk, row_chunk), row_chunk)
                mine[a][shard, rs, :] = mine[a][shard, rs, :] + from_sib[a][shard, rs, :]
                return carry

            lax.fori_loop(0, half_rows[a] // row_chunk, add, 0)

        def chip_sum_to_wire(a, shard, k):
            def add(r, carry):
                rs = pl.ds(pl.multiple_of(r * row_chunk, row_chunk), row_chunk)
                wire[a][k, rs, :] = (mine[a][shard, rs, :] + from_sib[a][shard, rs, :]).astype(BF16)
                return carry

            lax.fori_loop(0, half_rows[a] // row_chunk, add, 0)

        def shard_copy(k, a, to):
            return pltpu.make_async_remote_copy(
                src_ref=wire[a].at[k], dst_ref=from_chips[a].at[k],
                send_sem=send_sems.at[2 * k + a], recv_sem=recv_sems.at[2 * k + a],
                device_id=to, device_id_type=MESH)

        for k, (px, py) in enumerate(chips):
            for a in range(2):
                swap_copy(k, a).wait_recv()
                chip_sum_to_wire(a, 2 * px + py, k)
                cp = shard_copy(k, a, (px, py, c))
                cp.start()
                sends.append(cp)
        for a in range(2):
            swap_copy(n_shards - 1, a).wait_recv()
            chip_sum(a, me)

        for rel in range(1, N_DEVICES):
            px, py, pc = related(rel)
            pack_copy(rel, 4 * px + 2 * py + pc, (px, py, pc)).wait_recv()
        total = rpack[0]
        for i in range(1, N_DEVICES):
            total = total + rpack[i]
        opack_ref[...] = total
        sq_err = jnp.sum(total[PACK_LOSS_ROW:PACK_LOSS_ROW + 1, :], axis=1, keepdims=True)
        opack_ref[PACK_LOSS_ROW:PACK_LOSS_ROW + 1, :] = jnp.broadcast_to(sq_err * (0.5 / D_MODEL), (1, n_cols))

        for k, (px, py) in enumerate(chips):
            for a in range(2):
                shard_copy(k, a, (px, py, c)).wait_recv()

        def finish(a):
            def add(r, carry):
                rs = pl.ds(pl.multiple_of(r * row_chunk, row_chunk), row_chunk)
                mine[a][me, rs, :] = ((mine[a][me, rs, :] + from_chips[a][0, rs, :].astype(F32))
                                      + from_chips[a][1, rs, :].astype(F32)) + from_chips[a][2, rs, :].astype(F32)
                return carry

            lax.fori_loop(0, half_rows[a] // row_chunk, add, 0)

        def final_copy(a, which):
            return pltpu.make_async_remote_copy(
                src_ref=mine[a].at[me], dst_ref=result[a].at[half(a, which), :],
                send_sem=fin_send.at[a], recv_sem=fin_recv.at[a], device_id=sibling, device_id_type=MESH)

        stores = []
        for a in range(2):
            finish(a)
            cp = final_copy(a, c)
            cp.start()
            sends.append(cp)
            st = pltpu.make_async_copy(mine[a].at[me], result[a].at[half(a, c), :], local_sems.at[2 + a])
            st.start()
            stores.append(st)
        for a in range(2):
            final_copy(a, 1 - c).wait_recv()
        for cp in sends:
            cp.wait_send()
        for st in stores:
            st.wait()

    vmem = pl.BlockSpec(memory_space=pltpu.VMEM)
    hbm = pl.BlockSpec(memory_space=pl.ANY)
    in_shape = (n_rows_in, n_cols)
    out_shape = (n_rows_out, n_cols)
    half_in = (half_rows[0], n_cols)
    half_out = (half_rows[1], n_cols)
    return pl.pallas_call(
        body, name="reduce_grads",
        out_shape=(jax.ShapeDtypeStruct(in_shape, F32), jax.ShapeDtypeStruct(out_shape, F32),
                   jax.ShapeDtypeStruct(pack.shape, F32)),
        in_specs=[hbm, hbm, vmem], out_specs=(hbm, hbm, vmem),
        scratch_shapes=[pltpu.VMEM((n_shards,) + half_in, F32), pltpu.VMEM((n_shards,) + half_out, F32),
                        pltpu.VMEM((n_shards,) + half_in, F32), pltpu.VMEM((n_shards,) + half_out, F32),
                        pltpu.VMEM((N_CHIPS - 1,) + half_in, BF16), pltpu.VMEM((N_CHIPS - 1,) + half_out, BF16),
                        pltpu.VMEM((N_CHIPS - 1,) + half_in, BF16), pltpu.VMEM((N_CHIPS - 1,) + half_out, BF16),
                        pltpu.VMEM((N_DEVICES,) + pack.shape, F32),
                        pltpu.SemaphoreType.DMA((4,)),
                        pltpu.SemaphoreType.DMA((2 * n_shards,)), pltpu.SemaphoreType.DMA((2 * n_shards,)),
                        pltpu.SemaphoreType.DMA((6,)), pltpu.SemaphoreType.DMA((6,)),
                        pltpu.SemaphoreType.DMA((N_DEVICES - 1,)), pltpu.SemaphoreType.DMA((N_DEVICES - 1,)),
                        pltpu.SemaphoreType.DMA((2,)), pltpu.SemaphoreType.DMA((2,))],
        compiler_params=_compiler_params(48),
    )(gwin, gwout, pack)


def _adamw(name, w, g, m, v):
    n_rows, n_cols = w.shape
    tr = 256 if n_rows % 256 == 0 else n_rows
    m_corr = 1.0 - ADAM_B1 ** ADAM_STEP
    v_corr = 1.0 - ADAM_B2 ** ADAM_STEP

    def body(w_ref, g_ref, m_ref, v_ref, d_ref, nm_ref, nv_ref):
        gv = g_ref[...]
        nm = ADAM_B1 * m_ref[...] + (1.0 - ADAM_B1) * gv
        nv = ADAM_B2 * v_ref[...] + (1.0 - ADAM_B2) * (gv * gv)
        nm_ref[...] = nm
        nv_ref[...] = nv
        d_ref[...] = -ADAM_LR * ((nm / m_corr) / (jnp.sqrt(nv / v_corr) + ADAM_EPS) + ADAM_WD * w_ref[...])

    blk = lambda: pl.BlockSpec((tr, n_cols), lambda i: (i, 0))
    shape = jax.ShapeDtypeStruct(w.shape, F32)
    return pl.pallas_call(
        body, name=name,
        grid=(n_rows // tr,),
        in_specs=[blk(), blk(), blk(), blk()], out_specs=(blk(), blk(), blk()),
        out_shape=(shape, shape, shape),
        compiler_params=_compiler_params(32, ("arbitrary",)),
    )(w, g, m, v)


def kernel(x, w_in, conv_w, w_out, ln_gain, ln_bias, loss_target, m_w_in, m_conv_w, m_w_out, m_ln_gain, m_ln_bias, v_w_in, v_conv_w, v_w_out, v_ln_gain, v_ln_bias):
    xs = x[0]
    target = loss_target[0]

    wing, woutg, cwg = _gather_weights(w_in[0], w_out[0], conv_w[0])
    conv_full = jnp.transpose(cwg, (1, 0, 2)).reshape(3, CONV_WIDTH)
    wout_full = woutg.reshape(D_MODEL, D_MODEL)

    p_conv, qkv, z_attn, xt, mix_c = _proj(xs, wing, conv_full)
    o, mix_a, carries = _attn_fwd(qkv, z_attn)
    dxa, dmix_a, gwout, stats, dpc, dconv = _out_ln(
        mix_c, mix_a, wout_full, xs, target, ln_gain, ln_bias, p_conv, conv_full)
    dpa = _attn_bwd(qkv, z_attn, o, dmix_a, carries)
    gwin = _grad_w_in(xt, dpc, dpa)
    grad_x = _grad_x(dxa, dpc, dpa, wing)

    pack = jnp.concatenate(
        [stats[0:2], jnp.pad(dconv[0:3], ((0, 0), (0, D_MODEL - CONV_WIDTH))), stats[2:3],
         jnp.zeros((2, D_MODEL), F32)], axis=0)
    g_w_in, g_w_out, tot = _reduce_grads(gwin, gwout.reshape(N_CHIPS, D_MODEL // N_CHIPS, D_MODEL), pack)

    chip = 2 * lax.axis_index("x") + lax.axis_index("y")
    g_gain = tot[0:1]
    g_bias = tot[1:2]
    g_conv = lax.dynamic_slice(tot, (2, chip * LANES), (3, LANES))
    loss = tot[PACK_LOSS_ROW, 0]

    d_w_in, nm_w_in, nv_w_in = _adamw("adamw_w_in", w_in[0], g_w_in, m_w_in[0], v_w_in[0])
    d_w_out, nm_w_out, nv_w_out = _adamw("adamw_w_out", w_out[0], g_w_out, m_w_out[0], v_w_out[0])
    d_conv, nm_conv, nv_conv = _adamw("adamw_conv_w", conv_w[0], g_conv, m_conv_w[0], v_conv_w[0])
    d_gain, nm_gain, nv_gain = _adamw("adamw_ln_gain", ln_gain, g_gain, m_ln_gain, v_ln_gain)
    d_bias, nm_bias, nv_bias = _adamw("adamw_ln_bias", ln_bias, g_bias, m_ln_bias, v_ln_bias)

    lead = lambda a: a[None]
    return (loss, lead(grad_x),
            lead(g_w_in), lead(g_conv), lead(g_w_out), g_gain, g_bias,
            lead(d_w_in), lead(d_conv), lead(d_w_out), d_gain, d_bias,
            lead(nm_w_in), lead(nm_conv), lead(nm_w_out), nm_gain, nm_bias,
            lead(nv_w_in), lead(nv_conv), lead(nv_w_out), nv_gain, nv_bias)
```

```python
import jax
import jax.numpy as jnp
from jax import lax
from jax.experimental import pallas as pl
from jax.experimental.pallas import tpu as pltpu

F32 = jnp.float32
BF16 = jnp.bfloat16
MESH = pl.DeviceIdType.MESH

D_MODEL = 1024
CONV_WIDTH = 512
ATTN_WIDTH = 512
HEAD_DIM = 64
N_GROUPS = 8
N_CHIPS = 4
N_DEVICES = 8
LN_EPS = 1e-5
DEEPNORM_ALPHA = 2.0 ** 0.25
Q_SCALE = HEAD_DIM ** -0.5
ADAM_LR = 0.001
ADAM_B1 = 0.9
ADAM_B2 = 0.999
ADAM_EPS = 1e-08
ADAM_WD = 0.01
ADAM_STEP = 10

LANES = 128
SUBLANES = 8
V7X_VMEM_BYTES = 64 * 1024 * 1024
MIB = 1024 * 1024

TQ = 256
TK = 256
HEADS_PER_STEP = LANES // HEAD_DIM
CONV_ROWS = 512


def _compiler_params(vmem_mib, semantics=None):
    assert vmem_mib * MIB < V7X_VMEM_BYTES
    return pltpu.CompilerParams(dimension_semantics=semantics, vmem_limit_bytes=vmem_mib * MIB)


def _sigmoid(z):
    return 1.0 / (1.0 + jnp.exp(-z))


def _dot(a, b):
    return jnp.dot(a, b, preferred_element_type=F32)


def _dot_nt(a, b):
    return lax.dot_general(a, b, (((1,), (1,)), ((), ())), preferred_element_type=F32)


def _dot_tn(a, b):
    return lax.dot_general(a, b, (((0,), (0,)), ((), ())), preferred_element_type=F32)


def _truncate_to_bf16(a):
    bits = lax.bitcast_convert_type(a, jnp.uint32) & jnp.uint32(0xFFFF0000)
    return lax.bitcast_convert_type(bits, F32)


def _split(a):
    hi = _truncate_to_bf16(a)
    return [hi.astype(BF16), (a - hi).astype(BF16)]


def _block_sums(parts, tri):
    n = len(parts) // 2
    res = _dot(jnp.concatenate(parts, axis=0), tri).reshape(n, 2, TQ, TK)
    return [res[i, 0] + res[i, 1] for i in range(n)]


MASKED_LOGIT = -1e30
LOG2_E = 1.4426950408889634


def _softplus2(z2):
    return jnp.maximum(z2, 0.0) + jnp.log2(1.0 + jnp.exp2(-jnp.abs(z2)))


def _gather_weights(w_in, w_out, conv_w):
    d_rows, d_cols = w_in.shape
    o_rows, o_cols = w_out.shape
    half_rows = (d_rows // 2, o_rows // 2)
    row_chunk = 128

    def body(win_ref, wout_ref, cw_ref, wing_ref, woutg_ref, cwg_ref, cast_in, cast_out,
             send_sems, recv_sems, pass_send, pass_recv, store_sems):
        x, y, c = lax.axis_index("x"), lax.axis_index("y"), lax.axis_index("c")
        me = 2 * x + y
        srcs = (win_ref, wout_ref)
        bufs = (wing_ref, woutg_ref)
        casts = (cast_in, cast_out)

        def rows_of(a, which):
            rows = half_rows[a]
            return pl.ds(pl.multiple_of(which * rows, rows), rows)

        def half(a, shard, which):
            return bufs[a].at[shard, rows_of(a, which), :]

        def cast_half(which):
            for a in range(2):
                def cast(r, carry):
                    rows = pl.ds(pl.multiple_of(which * half_rows[a] + r * row_chunk, row_chunk), row_chunk)
                    casts[a][rows, :] = srcs[a][rows, :].astype(BF16)
                    return carry

                lax.fori_loop(0, half_rows[a] // row_chunk, cast, 0)

        chips = [(1 - x, y), (x, 1 - y), (1 - x, 1 - y)]

        def ici_copy(k, a, shard, to):
            if a == 2:
                src = dst = cwg_ref.at[shard]
            else:
                src, dst = casts[a].at[rows_of(a, c), :], half(a, shard, c)
            return pltpu.make_async_remote_copy(
                src_ref=src, dst_ref=dst, send_sem=send_sems.at[3 * k + a], recv_sem=recv_sems.at[3 * k + a],
                device_id=to, device_id_type=MESH)

        def pass_copy(k, a, shard, which):
            ref = half(a, shard, which)
            return pltpu.make_async_remote_copy(
                src_ref=ref, dst_ref=ref, send_sem=pass_send.at[2 * k + a], recv_sem=pass_recv.at[2 * k + a],
                device_id=(x, y, 1 - c), device_id_type=MESH)

        cast_half(c)
        cwg_ref[me] = cw_ref[...]
        sends = []
        for k, (px, py) in enumerate(chips):
            for a in range(3):
                cp = ici_copy(k, a, me, (px, py, c))
                cp.start()
                sends.append(cp)
        cast_half(1 - c)
        stores = []
        for a in range(2):
            st = pltpu.make_async_copy(casts[a], bufs[a].at[me], store_sems.at[a])
            st.start()
            stores.append(st)
        for k, (px, py) in enumerate(chips):
            for a in range(2):
                ici_copy(k, a, 2 * px + py, (px, py, c)).wait_recv()
                cp = pass_copy(k, a, 2 * px + py, c)
                cp.start()
                sends.append(cp)
        for k, (px, py) in enumerate(chips):
            ici_copy(k, 2, 2 * px + py, (px, py, c)).wait_recv()
            for a in range(2):
                pass_copy(k, a, 2 * px + py, 1 - c).wait_recv()
        for cp in sends:
            cp.wait_send()
        for st in stores:
            st.wait()

    vmem = pl.BlockSpec(memory_space=pltpu.VMEM)
    hbm = pl.BlockSpec(memory_space=pl.ANY)
    return pl.pallas_call(
        body, name="gather_weights",
        out_shape=(jax.ShapeDtypeStruct((N_CHIPS, d_rows, d_cols), BF16),
                   jax.ShapeDtypeStruct((N_CHIPS, o_rows, o_cols), BF16),
                   jax.ShapeDtypeStruct((N_CHIPS,) + conv_w.shape, F32)),
        in_specs=[vmem, vmem, vmem], out_specs=(hbm, hbm, vmem),
        scratch_shapes=[pltpu.VMEM((d_rows, d_cols), BF16), pltpu.VMEM((o_rows, o_cols), BF16),
                        pltpu.SemaphoreType.DMA((9,)), pltpu.SemaphoreType.DMA((9,)),
                        pltpu.SemaphoreType.DMA((6,)), pltpu.SemaphoreType.DMA((6,)),
                        pltpu.SemaphoreType.DMA((2,))],
        compiler_params=_compiler_params(32),
    )(w_in, w_out, conv_w)


def _proj(x, wing, conv_w):
    seq = x.shape[0]
    tm = 512
    w = CONV_WIDTH

    def body(x_ref, w_ref, cw_ref, conv_ref, qkv_ref, za_ref, xt_ref, mix_ref, tail_s):
        xv = x_ref[...]
        xb = xv.astype(BF16)
        bc = _dot(xb, w_ref[0])
        hz = _dot(xb, w_ref[1])
        conv_ref[:, 0:D_MODEL] = bc
        conv_ref[:, D_MODEL:] = hz

        @pl.when(pl.program_id(0) == 0)
        def _():
            tail_s[...] = jnp.zeros(tail_s.shape, F32)

        u = bc[:, w:] * hz[:, 0:w]
        ext = jnp.concatenate([tail_s[...], u], axis=0)
        tail_s[...] = u[tm - SUBLANES:, :]
        taps = cw_ref[...]
        y = taps[2:3, :] * u
        y = y + taps[0:1, :] * _shift_down(ext, 2, tm)
        y = y + taps[1:2, :] * _shift_down(ext, 1, tm)
        z = hz[:, w:]
        mix_ref[...] = ((z * _sigmoid(z)) * (bc[:, 0:w] * y)).astype(BF16)

        qk = _dot(xb, w_ref[2])
        qkv_ref[:, 0:w] = (qk[:, 0:w] * Q_SCALE).astype(BF16)
        qkv_ref[:, w:2 * w] = qk[:, w:].astype(BF16)
        vz = _dot(xb, w_ref[3])
        qkv_ref[:, 2 * w:] = vz[:, 0:w].astype(BF16)
        za_ref[...] = vz[:, w:]
        xt_ref[...] = xv.T.astype(BF16)

    rows = lambda width: pl.BlockSpec((tm, width), lambda i: (i, 0))
    return pl.pallas_call(
        body, name="proj",
        grid=(seq // tm,),
        in_specs=[rows(D_MODEL),
                  pl.BlockSpec((N_CHIPS, D_MODEL, D_MODEL), lambda i: (0, 0, 0), pipeline_mode=pl.Buffered(1)),
                  pl.BlockSpec((3, w), lambda i: (0, 0))],
        out_specs=(rows(4 * w), rows(3 * w), rows(w), pl.BlockSpec((D_MODEL, tm), lambda i: (0, i)), rows(w)),
        out_shape=(jax.ShapeDtypeStruct((seq, 4 * w), F32), jax.ShapeDtypeStruct((seq, 3 * w), BF16),
                   jax.ShapeDtypeStruct((seq, w), F32), jax.ShapeDtypeStruct((D_MODEL, seq), BF16),
                   jax.ShapeDtypeStruct((seq, w), BF16)),
        scratch_shapes=[pltpu.VMEM((SUBLANES, w), F32)],
        compiler_params=_compiler_params(48, ("arbitrary",)),
    )(x, wing, conv_w)


def _col_block(group, n_sub=CONV_WIDTH // LANES):
    return lambda j: (0, group * n_sub + j)


def _shift_down(ext, k, rows):
    return pltpu.roll(ext, k, 0)[SUBLANES:, :]


def _shift_up(ext, k, rows):
    return pltpu.roll(ext, rows + SUBLANES - k, 0)[:rows, :]


def _conv_bwd(proj, dmix, conv_w):
    seq = proj.shape[0]
    rows = CONV_ROWS
    n_chunks = seq // rows

    def body(b_ref, c_ref, h_ref, z_ref, d_ref, w_ref, dp_ref, dw_ref, u_s, dy_s):
        u_s[0:SUBLANES, :] = jnp.zeros((SUBLANES, LANES), F32)
        dy_s[seq:seq + SUBLANES, :] = jnp.zeros((SUBLANES, LANES), F32)

        def fill(r, carry):
            r0 = pl.multiple_of(r * rows, rows)
            rs = pl.ds(r0, rows)
            u_s[pl.ds(pl.multiple_of(r0 + SUBLANES, SUBLANES), rows), :] = c_ref[rs, :] * h_ref[rs, :]
            z = z_ref[rs, :]
            dy_s[rs, :] = d_ref[rs, :] * (z * _sigmoid(z)) * b_ref[rs, :]
            return carry

        lax.fori_loop(0, n_chunks, fill, 0)
        w = w_ref[...]

        def chunk(r, acc):
            r0 = pl.multiple_of(r * rows, rows)
            rs = pl.ds(r0, rows)
            ext = u_s[pl.ds(r0, rows + SUBLANES), :]
            u = ext[SUBLANES:, :]
            um1 = _shift_down(ext, 1, rows)
            um2 = _shift_down(ext, 2, rows)
            y = w[2:3, :] * u
            y = y + w[0:1, :] * um2
            y = y + w[1:2, :] * um1
            z = z_ref[rs, :]
            b = b_ref[rs, :]
            dco = d_ref[rs, :]
            sg = _sigmoid(z)
            g = z * sg
            dp_ref[0, rs, :] = (dco * g * y).astype(BF16)
            dp_ref[3, rs, :] = (dco * b * y * (sg * (1.0 + z * (1.0 - sg)))).astype(BF16)
            ext_dy = dy_s[pl.ds(r0, rows + SUBLANES), :]
            dy = ext_dy[:rows, :]
            du = w[2:3, :] * dy + w[1:2, :] * _shift_up(ext_dy, 1, rows) + w[0:1, :] * _shift_up(ext_dy, 2, rows)
            dp_ref[1, rs, :] = (du * h_ref[rs, :]).astype(BF16)
            dp_ref[2, rs, :] = (du * c_ref[rs, :]).astype(BF16)
            a0, a1, a2 = acc
            return (a0 + jnp.sum(dy * um2, axis=0, keepdims=True),
                    a1 + jnp.sum(dy * um1, axis=0, keepdims=True),
                    a2 + jnp.sum(dy * u, axis=0, keepdims=True))

        zero = jnp.zeros((1, LANES), F32)
        a0, a1, a2 = lax.fori_loop(0, n_chunks, chunk, (zero, zero, zero))
        dw_ref[...] = jnp.concatenate([a0, a1, a2, jnp.zeros((SUBLANES - 3, LANES), F32)], axis=0)

    col = lambda g: pl.BlockSpec((seq, LANES), _col_block(g))
    return pl.pallas_call(
        body, name="conv_bwd",
        grid=(CONV_WIDTH // LANES,),
        in_specs=[col(0), col(1), col(2), col(3), col(0), pl.BlockSpec((3, LANES), lambda j: (0, j))],
        out_specs=(pl.BlockSpec((4, seq, LANES), lambda j: (0, 0, j)),
                   pl.BlockSpec((SUBLANES, LANES), lambda j: (0, j))),
        out_shape=(jax.ShapeDtypeStruct((4, seq, CONV_WIDTH), BF16),
                   jax.ShapeDtypeStruct((SUBLANES, CONV_WIDTH), F32)),
        scratch_shapes=[pltpu.VMEM((seq + SUBLANES, LANES), F32), pltpu.VMEM((seq + SUBLANES, LANES), F32)],
        compiler_params=_compiler_params(48, ("arbitrary",)),
    )(proj, proj, proj, proj, dmix, conv_w)


def _tri_masks():
    r = lax.broadcasted_iota(jnp.int32, (TK, TK), 0)
    s = lax.broadcasted_iota(jnp.int32, (TK, TK), 1)
    return r, s


def _head_mask():
    lane = lax.broadcasted_iota(jnp.int32, (1, LANES), 1)
    return lane < HEAD_DIM


def _fill_tri(ref, ones):
    ref[...] = jnp.where(ones, 1.0, 0.0).astype(BF16)


def _tile_lanes(a):
    return jnp.tile(a, (1, TK // LANES))


DEAD_CARRY_BITS = 160.0


def _attn_fwd(qkv, z_attn):
    seq = qkv.shape[0]
    n_q = seq // TQ
    n_kb = seq // TK
    assert TQ == TK and n_kb <= LANES
    heads = range(HEADS_PER_STEP)

    def body(q_ref, k_ref, v_ref, za_ref, o_ref, mix_ref, car_ref, qq_s, vt_s, suffix_s, carry_s, diff_s, ot_s):
        head_a = _head_mask()
        r_i, s_i = _tri_masks()
        diff_s[...] = r_i - s_i
        _fill_tri(suffix_s, r_i > s_i)
        lane = lax.broadcasted_iota(jnp.int32, (1, LANES), 1)

        def prep(r, carry):
            rs = pl.ds(pl.multiple_of(r * TQ, TQ), TQ)
            q = q_ref[rs, :]
            qq_s[r, 0] = jnp.where(head_a, q, jnp.zeros_like(q))
            qq_s[r, 1] = jnp.where(head_a, jnp.zeros_like(q), q)
            vt_s[r] = v_ref[rs, :].astype(F32).T.astype(BF16)
            return carry

        lax.fori_loop(0, n_q, prep, 0)

        def block_pair(j, carry):
            qbs = [2 * j, 2 * j + 1]
            rows = [pl.ds(pl.multiple_of(qb * TQ, TQ), TQ) for qb in qbs]
            both = [qq_s[qb].reshape(HEADS_PER_STEP * TQ, LANES) for qb in qbs]
            carry_s[...] = jnp.zeros(carry_s.shape, F32)
            ot_s[...] = jnp.zeros(ot_s.shape, F32)
            for rs in rows:
                for h in heads:
                    car_ref[h, rs, :] = jnp.zeros((TQ, LANES), F32)

            def tiles(items):
                kbs = [qbs[s] - n for s, n in items]
                cols = [pl.ds(pl.multiple_of(jnp.maximum(kb, 0) * TK, TK), TK) for kb in kbs]
                zs = [_dot_nt(both[s], k_ref[cs, :]).reshape(HEADS_PER_STEP, TQ, TK) for (s, _), cs in zip(items, cols)]
                zms, cars, sufs = [], [], []
                for (s, n), kb, z in zip(items, kbs, zs):
                    mask = diff_s[...] > jnp.where(kb < 0, TK, jnp.where(n == 0, 0, -TK))
                    parts = []
                    for h in heads:
                        z2 = jnp.where(mask, z[h] * LOG2_E, MASKED_LOGIT)
                        sp = _softplus2(z2)
                        zms.append(z2 - sp)
                        parts += _split(sp)
                        car = carry_s[s, h]
                        cars.append(car)
                        car_ref[h, rows[s], :] = jnp.where(lane == kb, car, car_ref[h, rows[s], :])
                        carry_s[s, h] = car + jnp.sum(sp, axis=1, keepdims=True)
                    sufs += _block_sums(parts, suffix_s[...])
                for i, ((s, _), kb) in enumerate(zip(items, kbs)):
                    out = []
                    for h in heads:
                        j2 = i * HEADS_PER_STEP + h
                        w = jnp.exp2(zms[j2] - (sufs[j2] + _tile_lanes(cars[j2]))).astype(BF16)
                        out.append(_dot_nt(vt_s[jnp.maximum(kb, 0), h * HEAD_DIM:(h + 1) * HEAD_DIM, :], w))
                    ot_s[s] += jnp.concatenate(out, axis=0)

            tiles([(0, 0), (0, 1), (1, 0), (1, 1)])

            for s, (qb, rs) in enumerate(zip(qbs, rows)):
                def lowest_carry():
                    return jnp.min(jnp.minimum(carry_s[s, 0], carry_s[s, 1]))

                def live(state):
                    n, lowest = state
                    return jnp.logical_and(n <= qb, lowest < DEAD_CARRY_BITS)

                def step(state):
                    n, _ = state
                    tiles([(s, n)])
                    return n + 1, lowest_carry()

                n_done, _ = lax.while_loop(live, step, (jnp.int32(2), lowest_carry()))
                for h in heads:
                    car_ref[h, rs, :] = jnp.where(lane < qb + 1 - n_done, carry_s[s, h], car_ref[h, rs, :])
                o = ot_s[s].T
                o_ref[rs, :] = o
                za = za_ref[rs, :]
                mix_ref[rs, :] = ((za * _sigmoid(za)) * o).astype(BF16)
            return carry

        assert n_q % 2 == 0
        lax.fori_loop(0, n_q // 2, block_pair, 0)

    col = lambda g: pl.BlockSpec((seq, LANES), _col_block(g))
    n_pairs = ATTN_WIDTH // LANES
    return pl.pallas_call(
        body, name="attn_fwd",
        grid=(n_pairs,),
        in_specs=[col(0), col(1), col(2), col(0)],
        out_specs=(pl.BlockSpec((seq, LANES), lambda p: (0, p)),
                   pl.BlockSpec((seq, LANES), lambda p: (0, p)),
                   pl.BlockSpec((HEADS_PER_STEP, seq, LANES), lambda p: (p, 0, 0))),
        out_shape=(jax.ShapeDtypeStruct((seq, ATTN_WIDTH), F32),
                   jax.ShapeDtypeStruct((seq, ATTN_WIDTH), BF16),
                   jax.ShapeDtypeStruct((n_pairs * HEADS_PER_STEP, seq, LANES), F32)),
        scratch_shapes=[pltpu.VMEM((n_q, HEADS_PER_STEP, TQ, LANES), BF16),
                        pltpu.VMEM((n_kb, LANES, TK), BF16),
                        pltpu.VMEM((TK, TK), BF16),
                        pltpu.VMEM((2, HEADS_PER_STEP, TQ, LANES), F32),
                        pltpu.VMEM((TQ, TK), jnp.int32),
                        pltpu.VMEM((2, LANES, TQ), F32)],
        compiler_params=_compiler_params(48, ("arbitrary",)),
    )(qkv, qkv, qkv, z_attn)


def _attn_bwd(qkv, z_attn, o, dmix, carries):
    seq = qkv.shape[0]
    n_q = seq // TQ
    n_kb = seq // TK
    assert TQ == TK and n_kb <= LANES
    heads = range(HEADS_PER_STEP)

    def body(q_ref, k_ref, v_ref, za_ref, o_ref, dm_ref, car_ref, dp_ref,
             qq_s, dd_s, qt_s, dot_s, kt_s, suffix_s, prefix_s, ccar_s, dq_s, dk_s, dv_s, diff_s):
        head_a = _head_mask()
        r_i, s_i = _tri_masks()
        diff_s[...] = r_i - s_i
        _fill_tri(suffix_s, r_i > s_i)
        _fill_tri(prefix_s, r_i < s_i)
        lane = lax.broadcasted_iota(jnp.int32, (1, LANES), 1)

        def prep(r, carry):
            rs = pl.ds(pl.multiple_of(r * TQ, TQ), TQ)
            q = q_ref[rs, :]
            qq_s[r, 0] = jnp.where(head_a, q, jnp.zeros_like(q))
            qq_s[r, 1] = jnp.where(head_a, jnp.zeros_like(q), q)
            qt_s[r] = q.astype(F32).T.astype(BF16)
            kt_s[r] = k_ref[rs, :].astype(F32).T.astype(BF16)
            za = za_ref[rs, :]
            sg = _sigmoid(za)
            dm = dm_ref[rs, :]
            do = dm * (za * sg)
            dd_s[r, 0] = jnp.where(head_a, do, 0.0).astype(BF16)
            dd_s[r, 1] = jnp.where(head_a, 0.0, do).astype(BF16)
            dot_s[r] = do.T.astype(BF16)
            dp_ref[3, rs, :] = (dm * o_ref[rs, :] * (sg * (1.0 + za * (1.0 - sg)))).astype(BF16)
            dq_s[:, rs] = jnp.zeros((LANES, TQ), F32)
            dk_s[:, rs] = jnp.zeros((LANES, TQ), F32)
            dv_s[:, rs] = jnp.zeros((LANES, TQ), F32)
            return carry

        lax.fori_loop(0, n_q, prep, 0)

        def block_pair(j, carry):
            qbs = [2 * j, 2 * j + 1]
            rows = [pl.ds(pl.multiple_of(qb * TQ, TQ), TQ) for qb in qbs]
            both_q = [qq_s[qb].reshape(HEADS_PER_STEP * TQ, LANES) for qb in qbs]
            both_do = [dd_s[qb].reshape(HEADS_PER_STEP * TQ, LANES) for qb in qbs]
            ccar_s[...] = jnp.zeros(ccar_s.shape, F32)

            def tiles(items):
                cols = [pl.ds(pl.multiple_of(jnp.maximum(kb, 0) * TK, TK), TK) for _, kb in items]
                zs = [_dot_nt(both_q[s], k_ref[cs, :]).reshape(HEADS_PER_STEP, TQ, TK) for (s, _), cs in zip(items, cols)]
                dws = [_dot_nt(both_do[s], v_ref[cs, :]).reshape(HEADS_PER_STEP, TQ, TK)
                       for (s, _), cs in zip(items, cols)]
                zms, sufs = [], []
                for (s, kb), z in zip(items, zs):
                    mask = diff_s[...] > jnp.where(kb < 0, TK, jnp.where(kb == qbs[s], 0, -TK))
                    parts = []
                    for h in heads:
                        z2 = jnp.where(mask, z[h] * LOG2_E, MASKED_LOGIT)
                        sp = _softplus2(z2)
                        zms.append(z2 - sp)
                        parts += _split(sp)
                    sufs += _block_sums(parts, suffix_s[...])
                ws, dlws, pres = [], [], []
                for i, (s, kb) in enumerate(items):
                    parts = []
                    for h in heads:
                        j2 = i * HEADS_PER_STEP + h
                        cin = jnp.sum(jnp.where(lane == kb, car_ref[h, rows[s], :], 0.0), axis=1, keepdims=True)
                        w = jnp.exp2(zms[j2] - (sufs[j2] + cin))
                        ws.append(w.astype(BF16))
                        dlw = dws[i][h] * w
                        dlws.append(dlw)
                        parts += _split(dlw)
                    pres += _block_sums(parts, prefix_s[...])
                for i, ((s, kb), cs) in enumerate(zip(items, cols)):
                    dq_t, dk_t, dv_t = [], [], []
                    for h in heads:
                        j2 = i * HEADS_PER_STEP + h
                        ccar = ccar_s[s, h]
                        dz = (dlws[j2] - jnp.exp2(zms[j2]) * (dlws[j2] + (pres[j2] + _tile_lanes(ccar)))).astype(BF16)
                        ccar_s[s, h] = ccar + jnp.sum(dlws[j2], axis=1, keepdims=True)
                        dims = slice(h * HEAD_DIM, (h + 1) * HEAD_DIM)
                        dq_t.append(_dot_nt(kt_s[jnp.maximum(kb, 0), dims, :], dz))
                        dk_t.append(_dot(qt_s[qbs[s], dims, :], dz))
                        dv_t.append(_dot(dot_s[qbs[s], dims, :], ws[j2]))
                    dq_s[:, rows[s]] += jnp.concatenate(dq_t, axis=0)
                    dk_s[:, cs] += jnp.concatenate(dk_t, axis=0)
                    dv_s[:, cs] += jnp.concatenate(dv_t, axis=0)

            for s, (qb, rs) in enumerate(zip(qbs, rows)):
                lowest = jnp.min(jnp.minimum(car_ref[0, rs, :], car_ref[1, rs, :]), axis=0, keepdims=True)
                dead = jnp.logical_and(lowest >= DEAD_CARRY_BITS, lane < qb)
                first_live = jnp.sum(jnp.where(dead, 1, 0))

                def one(kb, c2):
                    tiles([(s, kb)])
                    return c2

                lax.fori_loop(first_live, qb - 1, one, 0)

            for s, qb in enumerate(qbs):
                tiles([(s, qb - 1), (s, qb)])
            return carry

        assert n_q % 2 == 0
        lax.fori_loop(0, n_q // 2, block_pair, 0)

        def finish(r, carry):
            rs = pl.ds(pl.multiple_of(r * TQ, TQ), TQ)
            dp_ref[0, rs, :] = (dq_s[:, rs].T * Q_SCALE).astype(BF16)
            dp_ref[1, rs, :] = dk_s[:, rs].T.astype(BF16)
            dp_ref[2, rs, :] = dv_s[:, rs].T.astype(BF16)
            return carry

        lax.fori_loop(0, n_q, finish, 0)

    def col(g, n_sub=CONV_WIDTH // LANES):
        return pl.BlockSpec((seq, LANES), _col_block(g, n_sub))

    n_pairs = ATTN_WIDTH // LANES
    by_head = lambda: pltpu.VMEM((n_q, HEADS_PER_STEP, TQ, LANES), BF16)
    transposed = lambda: pltpu.VMEM((n_q, LANES, TQ), BF16)
    acc_t = lambda: pltpu.VMEM((LANES, seq), F32)
    return pl.pallas_call(
        body, name="attn_bwd",
        grid=(n_pairs,),
        in_specs=[col(0), col(1), col(2), col(0), col(0), col(1),
                  pl.BlockSpec((HEADS_PER_STEP, seq, LANES), lambda p: (p, 0, 0))],
        out_specs=pl.BlockSpec((4, seq, LANES), lambda p: (0, 0, p)),
        out_shape=jax.ShapeDtypeStruct((4, seq, ATTN_WIDTH), BF16),
        scratch_shapes=[by_head(), by_head(), transposed(), transposed(), transposed(),
                        pltpu.VMEM((TK, TK), BF16), pltpu.VMEM((TK, TK), BF16),
                        pltpu.VMEM((2, HEADS_PER_STEP, TQ, LANES), F32),
                        acc_t(), acc_t(), acc_t(),
                        pltpu.VMEM((TQ, TK), jnp.int32)],
        compiler_params=_compiler_params(56, ("arbitrary",)),
    )(qkv, qkv, qkv, z_attn, o, dmix, carries)


def _out_ln(mix_c, mix_a, woutg, x, target, gain, bias):
    seq = x.shape[0]
    tm = 512
    inv_d = 1.0 / D_MODEL

    def body(mc_ref, ma_ref, wo_ref, x_ref, t_ref, g_ref, b_ref, dxa_ref, dmix_ref, gw_ref, st_ref):
        @pl.when(pl.program_id(0) == 0)
        def _():
            gw_ref[...] = jnp.zeros(gw_ref.shape, F32)
            st_ref[...] = jnp.zeros(st_ref.shape, F32)

        mc = mc_ref[...]
        ma = ma_ref[...]
        sub = _dot(mc, wo_ref[0:CONV_WIDTH, :]) + _dot(ma, wo_ref[CONV_WIDTH:, :])
        r = DEEPNORM_ALPHA * x_ref[...] + sub
        mu = jnp.mean(r, axis=-1, keepdims=True)
        rc = r - mu
        var = jnp.mean(rc * rc, axis=-1, keepdims=True)
        rstd = lax.rsqrt(var + LN_EPS)
        xhat = rc * rstd
        gain_v = g_ref[...]
        diff = (xhat * gain_v + b_ref[...]) - t_ref[...]
        dy = diff * inv_d
        st_ref[0:1, :] += jnp.sum(dy * xhat, axis=0, keepdims=True)
        st_ref[1:2, :] += jnp.sum(dy, axis=0, keepdims=True)
        st_ref[2:3, :] += jnp.sum(diff * diff, axis=0, keepdims=True)
        dxh = dy * gain_v
        m1 = jnp.mean(dxh, axis=-1, keepdims=True)
        m2 = jnp.mean(dxh * xhat, axis=-1, keepdims=True)
        dr = rstd * (dxh - m1 - xhat * m2)
        dxa_ref[...] = DEEPNORM_ALPHA * dr
        drb = dr.astype(BF16)
        dmix_ref[...] = _dot_nt(drb, wo_ref[...])
        gw_ref[0:CONV_WIDTH, :] += _dot_tn(mc, drb)
        gw_ref[CONV_WIDTH:, :] += _dot_tn(ma, drb)

    rows_d = lambda: pl.BlockSpec((tm, D_MODEL), lambda i: (i, 0))
    rows_h = lambda: pl.BlockSpec((tm, CONV_WIDTH), lambda i: (i, 0))
    whole = lambda shape: pl.BlockSpec(shape, lambda i: (0, 0))
    return pl.pallas_call(
        body, name="out_ln",
        grid=(seq // tm,),
        in_specs=[rows_h(), rows_h(), whole((D_MODEL, D_MODEL)), rows_d(), rows_d(),
                  whole((1, D_MODEL)), whole((1, D_MODEL))],
        out_specs=(rows_d(), rows_d(), whole((D_MODEL, D_MODEL)), whole((SUBLANES, D_MODEL))),
        out_shape=(jax.ShapeDtypeStruct((seq, D_MODEL), F32), jax.ShapeDtypeStruct((seq, D_MODEL), F32),
                   jax.ShapeDtypeStruct((D_MODEL, D_MODEL), F32), jax.ShapeDtypeStruct((SUBLANES, D_MODEL), F32)),
        compiler_params=_compiler_params(48, ("arbitrary",)),
    )(mix_c, mix_a, woutg, x, target, gain, bias)


def _group_maps():
    half = N_GROUPS // 2
    return (lambda g: jnp.minimum(g, half - 1)), (lambda g: jnp.maximum(g - half, 0))


def _grad_w_in(xt, dpc, dpa):
    seq = xt.shape[1]
    half = N_GROUPS // 2
    in_conv, in_attn = _group_maps()

    def body(xt_ref, dc_ref, da_ref, out_ref):
        g = pl.program_id(0)

        @pl.when(g < half)
        def _():
            out_ref[...] = _dot(xt_ref[...], dc_ref[...])

        @pl.when(g >= half)
        def _():
            out_ref[...] = _dot(xt_ref[...], da_ref[...])

    return pl.pallas_call(
        body, name="grad_w_in",
        grid=(N_GROUPS,),
        in_specs=[pl.BlockSpec((D_MODEL, seq), lambda g: (0, 0), pipeline_mode=pl.Buffered(1)),
                  pl.BlockSpec((None, seq, CONV_WIDTH), lambda g: (in_conv(g), 0, 0)),
                  pl.BlockSpec((None, seq, CONV_WIDTH), lambda g: (in_attn(g), 0, 0))],
        out_specs=pl.BlockSpec((None, D_MODEL, CONV_WIDTH), lambda g: (g // 2, 0, g % 2)),
        out_shape=jax.ShapeDtypeStruct((N_CHIPS, D_MODEL, D_MODEL), F32),
        compiler_params=_compiler_params(48, ("arbitrary",)),
    )(xt, dpc, dpa)


def _grad_x(dxa, dpc, dpa, wing):
    seq = dxa.shape[0]
    tm = 512
    half = N_GROUPS // 2

    def body(dxa_ref, dc_ref, da_ref, w_ref, out_ref):
        acc = dxa_ref[...]
        for g in range(N_GROUPS):
            dp = dc_ref[g] if g < half else da_ref[g - half]
            cols = slice((g % 2) * CONV_WIDTH, (g % 2 + 1) * CONV_WIDTH)
            acc = acc + _dot_nt(dp, w_ref[g // 2, :, cols])
        out_ref[...] = acc

    return pl.pallas_call(
        body, name="grad_x",
        grid=(seq // tm,),
        in_specs=[pl.BlockSpec((tm, D_MODEL), lambda i: (i, 0)),
                  pl.BlockSpec((half, tm, CONV_WIDTH), lambda i: (0, i, 0)),
                  pl.BlockSpec((half, tm, CONV_WIDTH), lambda i: (0, i, 0)),
                  pl.BlockSpec((N_CHIPS, D_MODEL, D_MODEL), lambda i: (0, 0, 0), pipeline_mode=pl.Buffered(1))],
        out_specs=pl.BlockSpec((tm, D_MODEL), lambda i: (i, 0)),
        out_shape=jax.ShapeDtypeStruct((seq, D_MODEL), F32),
        compiler_params=_compiler_params(40, ("arbitrary",)),
    )(dxa, dpc, dpa, wing)


PACK_LOSS_ROW = 5


def _reduce_grads(gwin, gwout, pack):
    n_shards, n_rows_in, n_cols = gwin.shape
    n_rows_out = gwout.shape[1]
    half_rows = (n_rows_in // 2, n_rows_out // 2)
    row_chunk = 128

    def body(gwin_ref, gwout_ref, pack_ref, oin_ref, oout_ref, opack_ref,
             mine_in, mine_out, sib_in, sib_out, wire_in, wire_out, rin, rout, rpack,
             local_sems, sib_send, sib_recv, send_sems, recv_sems, pack_send, pack_recv, fin_send, fin_recv):
        x, y, c = lax.axis_index("x"), lax.axis_index("y"), lax.axis_index("c")
        me = 2 * x + y
        my_id = 4 * x + 2 * y + c
        chips = [(1 - x, 1 - y), (1 - x, y), (x, 1 - y)]
        sibling = (x, y, 1 - c)
        partial = (gwin_ref, gwout_ref)
        mine = (mine_in, mine_out)
        from_sib = (sib_in, sib_out)
        wire = (wire_in, wire_out)
        from_chips = (rin, rout)
        result = (oin_ref, oout_ref)

        def half(a, which):
            rows = half_rows[a]
            return pl.ds(pl.multiple_of(which * rows, rows), rows)

        order = [2 * px + py for px, py in chips] + [me]

        def swap_copy(i, a):
            shard = order[i]
            return pltpu.make_async_remote_copy(
                src_ref=partial[a].at[shard, half(a, 1 - c), :], dst_ref=from_sib[a].at[shard],
                send_sem=sib_send.at[2 * i + a], recv_sem=sib_recv.at[2 * i + a],
                device_id=sibling, device_id_type=MESH)

        sends = []
        loads = []
        for a in range(2):
            ld = pltpu.make_async_copy(partial[a].at[:, half(a, c), :], mine[a], local_sems.at[a])
            ld.start()
            loads.append(ld)
        for i in range(n_shards):
            for a in range(2):
                cp = swap_copy(i, a)
                cp.start()
                sends.append(cp)

        rpack[my_id] = pack_ref[...]

        def pack_copy(rel, slot, to):
            return pltpu.make_async_remote_copy(
                src_ref=rpack.at[slot], dst_ref=rpack.at[slot],
                send_sem=pack_send.at[rel - 1], recv_sem=pack_recv.at[rel - 1],
                device_id=to, device_id_type=MESH)

        def related(rel):
            px = (1 - x) if rel & 4 else x
            py = (1 - y) if rel & 2 else y
            pc = (1 - c) if rel & 1 else c
            return px, py, pc

        for rel in range(1, N_DEVICES):
            cp = pack_copy(rel, my_id, related(rel))
            cp.start()
            sends.append(cp)

        for ld in loads:
            ld.wait()

        def chip_sum(a, shard):
            def add(r, carry):
                rs = pl.ds(pl.multiple_of(r * row_chunk, row_chunk), row_chunk)
                mine[a][shard, rs, :] = mine[a][shard, rs, :] + from_sib[a][shard, rs, :]
                return carry

            lax.fori_loop(0, half_rows[a] // row_chunk, add, 0)

        def chip_sum_to_wire(a, shard, k):
            def add(r, carry):
                rs = pl.ds(pl.multiple_of(r * row_chunk, row_chunk), row_chunk)
                wire[a][k, rs, :] = (mine[a][shard, rs, :] + from_sib[a][shard, rs, :]).astype(BF16)
                return carry

            lax.fori_loop(0, half_rows[a] // row_chunk, add, 0)

        def shard_copy(k, a, to):
            return pltpu.make_async_remote_copy(
                src_ref=wire[a].at[k], dst_ref=from_chips[a].at[k],
                send_sem=send_sems.at[2 * k + a], recv_sem=recv_sems.at[2 * k + a],
                device_id=to, device_id_type=MESH)

        for k, (px, py) in enumerate(chips):
            for a in range(2):
                swap_copy(k, a).wait_recv()
                chip_sum_to_wire(a, 2 * px + py, k)
                cp = shard_copy(k, a, (px, py, c))
                cp.start()
                sends.append(cp)
        for a in range(2):
            swap_copy(n_shards - 1, a).wait_recv()
            chip_sum(a, me)

        for rel in range(1, N_DEVICES):
            px, py, pc = related(rel)
            pack_copy(rel, 4 * px + 2 * py + pc, (px, py, pc)).wait_recv()
        total = rpack[0]
        for i in range(1, N_DEVICES):
            total = total + rpack[i]
        opack_ref[...] = total
        sq_err = jnp.sum(total[PACK_LOSS_ROW:PACK_LOSS_ROW + 1, :], axis=1, keepdims=True)
        opack_ref[PACK_LOSS_ROW:PACK_LOSS_ROW + 1, :] = jnp.broadcast_to(sq_err * (0.5 / D_MODEL), (1, n_cols))

        for k, (px, py) in enumerate(chips):
            for a in range(2):
                shard_copy(k, a, (px, py, c)).wait_recv()

        def finish(a):
            def add(r, carry):
                rs = pl.ds(pl.multiple_of(r * row_chunk, row_chunk), row_chunk)
                mine[a][me, rs, :] = ((mine[a][me, rs, :] + from_chips[a][0, rs, :].astype(F32))
                                      + from_chips[a][1, rs, :].astype(F32)) + from_chips[a][2, rs, :].astype(F32)
                return carry

            lax.fori_loop(0, half_rows[a] // row_chunk, add, 0)

        def final_copy(a, which):
            return pltpu.make_async_remote_copy(
                src_ref=mine[a].at[me], dst_ref=result[a].at[half(a, which), :],
                send_sem=fin_send.at[a], recv_sem=fin_recv.at[a], device_id=sibling, device_id_type=MESH)

        stores = []
        for a in range(2):
            finish(a)
            cp = final_copy(a, c)
            cp.start()
            sends.append(cp)
            st = pltpu.make_async_copy(mine[a].at[me], result[a].at[half(a, c), :], local_sems.at[2 + a])
            st.start()
            stores.append(st)
        for a in range(2):
            final_copy(a, 1 - c).wait_recv()
        for cp in sends:
            cp.wait_send()
        for st in stores:
            st.wait()

    vmem = pl.BlockSpec(memory_space=pltpu.VMEM)
    hbm = pl.BlockSpec(memory_space=pl.ANY)
    in_shape = (n_rows_in, n_cols)
    out_shape = (n_rows_out, n_cols)
    half_in = (half_rows[0], n_cols)
    half_out = (half_rows[1], n_cols)
    return pl.pallas_call(
        body, name="reduce_grads",
        out_shape=(jax.ShapeDtypeStruct(in_shape, F32), jax.ShapeDtypeStruct(out_shape, F32),
                   jax.ShapeDtypeStruct(pack.shape, F32)),
        in_specs=[hbm, hbm, vmem], out_specs=(hbm, hbm, vmem),
        scratch_shapes=[pltpu.VMEM((n_shards,) + half_in, F32), pltpu.VMEM((n_shards,) + half_out, F32),
                        pltpu.VMEM((n_shards,) + half_in, F32), pltpu.VMEM((n_shards,) + half_out, F32),
                        pltpu.VMEM((N_CHIPS - 1,) + half_in, BF16), pltpu.VMEM((N_CHIPS - 1,) + half_out, BF16),
                        pltpu.VMEM((N_CHIPS - 1,) + half_in, BF16), pltpu.VMEM((N_CHIPS - 1,) + half_out, BF16),
                        pltpu.VMEM((N_DEVICES,) + pack.shape, F32),
                        pltpu.SemaphoreType.DMA((4,)),
                        pltpu.SemaphoreType.DMA((2 * n_shards,)), pltpu.SemaphoreType.DMA((2 * n_shards,)),
                        pltpu.SemaphoreType.DMA((6,)), pltpu.SemaphoreType.DMA((6,)),
                        pltpu.SemaphoreType.DMA((N_DEVICES - 1,)), pltpu.SemaphoreType.DMA((N_DEVICES - 1,)),
                        pltpu.SemaphoreType.DMA((2,)), pltpu.SemaphoreType.DMA((2,))],
        compiler_params=_compiler_params(48),
    )(gwin, gwout, pack)


def _adamw(name, w, g, m, v):
    n_rows, n_cols = w.shape
    tr = 256 if n_rows % 256 == 0 else n_rows
    m_corr = 1.0 - ADAM_B1 ** ADAM_STEP
    v_corr = 1.0 - ADAM_B2 ** ADAM_STEP

    def body(w_ref, g_ref, m_ref, v_ref, d_ref, nm_ref, nv_ref):
        gv = g_ref[...]
        nm = ADAM_B1 * m_ref[...] + (1.0 - ADAM_B1) * gv
        nv = ADAM_B2 * v_ref[...] + (1.0 - ADAM_B2) * (gv * gv)
        nm_ref[...] = nm
        nv_ref[...] = nv
        d_ref[...] = -ADAM_LR * ((nm / m_corr) / (jnp.sqrt(nv / v_corr) + ADAM_EPS) + ADAM_WD * w_ref[...])

    blk = lambda: pl.BlockSpec((tr, n_cols), lambda i: (i, 0))
    shape = jax.ShapeDtypeStruct(w.shape, F32)
    return pl.pallas_call(
        body, name=name,
        grid=(n_rows // tr,),
        in_specs=[blk(), blk(), blk(), blk()], out_specs=(blk(), blk(), blk()),
        out_shape=(shape, shape, shape),
        compiler_params=_compiler_params(32, ("arbitrary",)),
    )(w, g, m, v)


def kernel(x, w_in, conv_w, w_out, ln_gain, ln_bias, loss_target, m_w_in, m_conv_w, m_w_out, m_ln_gain, m_ln_bias, v_w_in, v_conv_w, v_w_out, v_ln_gain, v_ln_bias):
    xs = x[0]
    target = loss_target[0]

    wing, woutg, cwg = _gather_weights(w_in[0], w_out[0], conv_w[0])
    conv_full = jnp.transpose(cwg, (1, 0, 2)).reshape(3, CONV_WIDTH)
    wout_full = woutg.reshape(D_MODEL, D_MODEL)

    p_conv, qkv, z_attn, xt, mix_c = _proj(xs, wing, conv_full)
    o, mix_a, carries = _attn_fwd(qkv, z_attn)
    dxa, dmix, gwout, stats = _out_ln(mix_c, mix_a, wout_full, xs, target, ln_gain, ln_bias)
    dpc, dconv = _conv_bwd(p_conv, dmix, conv_full)
    dpa = _attn_bwd(qkv, z_attn, o, dmix, carries)
    gwin = _grad_w_in(xt, dpc, dpa)
    grad_x = _grad_x(dxa, dpc, dpa, wing)

    pack = jnp.concatenate(
        [stats[0:2], jnp.pad(dconv[0:3], ((0, 0), (0, D_MODEL - CONV_WIDTH))), stats[2:3],
         jnp.zeros((2, D_MODEL), F32)], axis=0)
    g_w_in, g_w_out, tot = _reduce_grads(gwin, gwout.reshape(N_CHIPS, D_MODEL // N_CHIPS, D_MODEL), pack)

    chip = 2 * lax.axis_index("x") + lax.axis_index("y")
    g_gain = tot[0:1]
    g_bias = tot[1:2]
    g_conv = lax.dynamic_slice(tot, (2, chip * LANES), (3, LANES))
    loss = tot[PACK_LOSS_ROW, 0]

    d_w_in, nm_w_in, nv_w_in = _adamw("adamw_w_in", w_in[0], g_w_in, m_w_in[0], v_w_in[0])
    d_w_out, nm_w_out, nv_w_out = _adamw("adamw_w_out", w_out[0], g_w_out, m_w_out[0], v_w_out[0])
    d_conv, nm_conv, nv_conv = _adamw("adamw_conv_w", conv_w[0], g_conv, m_conv_w[0], v_conv_w[0])
    d_gain, nm_gain, nv_gain = _adamw("adamw_ln_gain", ln_gain, g_gain, m_ln_gain, v_ln_gain)
    d_bias, nm_bias, nv_bias = _adamw("adamw_ln_bias", ln_bias, g_bias, m_ln_bias, v_ln_bias)

    lead = lambda a: a[None]
    return (loss, lead(grad_x),
            lead(g_w_in), lead(g_conv), lead(g_w_out), g_gain, g_bias,
            lead(d_w_in), lead(d_conv), lead(d_w_out), d_gain, d_bias,
            lead(nm_w_in), lead(nm_conv), lead(nm_w_out), nm_gain, nm_bias,
            lead(nv_w_in), lead(nv_conv), lead(nv_w_out), nv_gain, nv_bias)
```

```python
import jax
import jax.numpy as jnp
from jax import lax
from jax.experimental import pallas as pl
from jax.experimental.pallas import tpu as pltpu

F32 = jnp.float32
BF16 = jnp.bfloat16
MESH = pl.DeviceIdType.MESH

D_MODEL = 1024
CONV_WIDTH = 512
ATTN_WIDTH = 512
HEAD_DIM = 64
N_GROUPS = 8
N_CHIPS = 4
N_DEVICES = 8
LN_EPS = 1e-5
DEEPNORM_ALPHA = 2.0 ** 0.25
Q_SCALE = HEAD_DIM ** -0.5
ADAM_LR = 0.001
ADAM_B1 = 0.9
ADAM_B2 = 0.999
ADAM_EPS = 1e-08
ADAM_WD = 0.01
ADAM_STEP = 10

LANES = 128
SUBLANES = 8
V7X_VMEM_BYTES = 64 * 1024 * 1024
MIB = 1024 * 1024

TQ = 256
TK = 256
HEADS_PER_STEP = LANES // HEAD_DIM
CONV_ROWS = 512


def _compiler_params(vmem_mib, semantics=None):
    assert vmem_mib * MIB < V7X_VMEM_BYTES
    return pltpu.CompilerParams(dimension_semantics=semantics, vmem_limit_bytes=vmem_mib * MIB)


def _sigmoid(z):
    return 1.0 / (1.0 + jnp.exp(-z))


def _dot(a, b):
    return jnp.dot(a, b, preferred_element_type=F32)


def _dot_nt(a, b):
    return lax.dot_general(a, b, (((1,), (1,)), ((), ())), preferred_element_type=F32)


def _dot_tn(a, b):
    return lax.dot_general(a, b, (((0,), (0,)), ((), ())), preferred_element_type=F32)


def _truncate_to_bf16(a):
    bits = lax.bitcast_convert_type(a, jnp.uint32) & jnp.uint32(0xFFFF0000)
    return lax.bitcast_convert_type(bits, F32)


def _split(a):
    hi = _truncate_to_bf16(a)
    return [hi.astype(BF16), (a - hi).astype(BF16)]


def _block_sums(parts, tri):
    n = len(parts) // 2
    res = _dot(jnp.concatenate(parts, axis=0), tri).reshape(n, 2, TQ, TK)
    return [res[i, 0] + res[i, 1] for i in range(n)]


MASKED_LOGIT = -1e30
LOG2_E = 1.4426950408889634


def _softplus2(z2):
    return jnp.maximum(z2, 0.0) + jnp.log2(1.0 + jnp.exp2(-jnp.abs(z2)))


def _gather_weights(w_in, w_out, conv_w):
    d_rows, d_cols = w_in.shape
    o_rows, o_cols = w_out.shape
    half_rows = (d_rows // 2, o_rows // 2)
    row_chunk = 128

    def body(win_ref, wout_ref, cw_ref, wing_ref, woutg_ref, cwg_ref, cast_in, cast_out,
             send_sems, recv_sems, pass_send, pass_recv, store_sems):
        x, y, c = lax.axis_index("x"), lax.axis_index("y"), lax.axis_index("c")
        me = 2 * x + y
        srcs = (win_ref, wout_ref)
        bufs = (wing_ref, woutg_ref)
        casts = (cast_in, cast_out)

        def rows_of(a, which):
            rows = half_rows[a]
            return pl.ds(pl.multiple_of(which * rows, rows), rows)

        def half(a, shard, which):
            return bufs[a].at[shard, rows_of(a, which), :]

        def cast_half(which):
            for a in range(2):
                def cast(r, carry):
                    rows = pl.ds(pl.multiple_of(which * half_rows[a] + r * row_chunk, row_chunk), row_chunk)
                    casts[a][rows, :] = srcs[a][rows, :].astype(BF16)
                    return carry

                lax.fori_loop(0, half_rows[a] // row_chunk, cast, 0)

        chips = [(1 - x, y), (x, 1 - y), (1 - x, 1 - y)]

        def ici_copy(k, a, shard, to):
            if a == 2:
                src = dst = cwg_ref.at[shard]
            else:
                src, dst = casts[a].at[rows_of(a, c), :], half(a, shard, c)
            return pltpu.make_async_remote_copy(
                src_ref=src, dst_ref=dst, send_sem=send_sems.at[3 * k + a], recv_sem=recv_sems.at[3 * k + a],
                device_id=to, device_id_type=MESH)

        def pass_copy(k, a, shard, which):
            ref = half(a, shard, which)
            return pltpu.make_async_remote_copy(
                src_ref=ref, dst_ref=ref, send_sem=pass_send.at[2 * k + a], recv_sem=pass_recv.at[2 * k + a],
                device_id=(x, y, 1 - c), device_id_type=MESH)

        cast_half(c)
        cwg_ref[me] = cw_ref[...]
        sends = []
        for k, (px, py) in enumerate(chips):
            for a in range(3):
                cp = ici_copy(k, a, me, (px, py, c))
                cp.start()
                sends.append(cp)
        cast_half(1 - c)
        stores = []
        for a in range(2):
            st = pltpu.make_async_copy(casts[a], bufs[a].at[me], store_sems.at[a])
            st.start()
            stores.append(st)
        for k, (px, py) in enumerate(chips):
            for a in range(2):
                ici_copy(k, a, 2 * px + py, (px, py, c)).wait_recv()
                cp = pass_copy(k, a, 2 * px + py, c)
                cp.start()
                sends.append(cp)
        for k, (px, py) in enumerate(chips):
            ici_copy(k, 2, 2 * px + py, (px, py, c)).wait_recv()
            for a in range(2):
                pass_copy(k, a, 2 * px + py, 1 - c).wait_recv()
        for cp in sends:
            cp.wait_send()
        for st in stores:
            st.wait()

    vmem = pl.BlockSpec(memory_space=pltpu.VMEM)
    hbm = pl.BlockSpec(memory_space=pl.ANY)
    return pl.pallas_call(
        body, name="gather_weights",
        out_shape=(jax.ShapeDtypeStruct((N_CHIPS, d_rows, d_cols), BF16),
                   jax.ShapeDtypeStruct((N_CHIPS, o_rows, o_cols), BF16),
                   jax.ShapeDtypeStruct((N_CHIPS,) + conv_w.shape, F32)),
        in_specs=[vmem, vmem, vmem], out_specs=(hbm, hbm, vmem),
        scratch_shapes=[pltpu.VMEM((d_rows, d_cols), BF16), pltpu.VMEM((o_rows, o_cols), BF16),
                        pltpu.SemaphoreType.DMA((9,)), pltpu.SemaphoreType.DMA((9,)),
                        pltpu.SemaphoreType.DMA((6,)), pltpu.SemaphoreType.DMA((6,)),
                        pltpu.SemaphoreType.DMA((2,))],
        compiler_params=_compiler_params(32),
    )(w_in, w_out, conv_w)


def _proj(x, wing, conv_w):
    seq = x.shape[0]
    tm = 512
    w = CONV_WIDTH

    def body(x_ref, w_ref, cw_ref, conv_ref, qkv_ref, za_ref, xt_ref, mix_ref, tail_s):
        xv = x_ref[...]
        xb = xv.astype(BF16)
        bc = _dot(xb, w_ref[0])
        hz = _dot(xb, w_ref[1])
        conv_ref[:, 0:D_MODEL] = bc
        conv_ref[:, D_MODEL:] = hz

        @pl.when(pl.program_id(0) == 0)
        def _():
            tail_s[...] = jnp.zeros(tail_s.shape, F32)

        u = bc[:, w:] * hz[:, 0:w]
        ext = jnp.concatenate([tail_s[...], u], axis=0)
        tail_s[...] = u[tm - SUBLANES:, :]
        taps = cw_ref[...]
        y = taps[2:3, :] * u
        y = y + taps[0:1, :] * _shift_down(ext, 2, tm)
        y = y + taps[1:2, :] * _shift_down(ext, 1, tm)
        z = hz[:, w:]
        mix_ref[...] = ((z * _sigmoid(z)) * (bc[:, 0:w] * y)).astype(BF16)

        qk = _dot(xb, w_ref[2])
        qkv_ref[:, 0:w] = (qk[:, 0:w] * Q_SCALE).astype(BF16)
        qkv_ref[:, w:2 * w] = qk[:, w:].astype(BF16)
        vz = _dot(xb, w_ref[3])
        qkv_ref[:, 2 * w:] = vz[:, 0:w].astype(BF16)
        za_ref[...] = vz[:, w:]
        xt_ref[...] = xv.T.astype(BF16)

    rows = lambda width: pl.BlockSpec((tm, width), lambda i: (i, 0))
    return pl.pallas_call(
        body, name="proj",
        grid=(seq // tm,),
        in_specs=[rows(D_MODEL),
                  pl.BlockSpec((N_CHIPS, D_MODEL, D_MODEL), lambda i: (0, 0, 0), pipeline_mode=pl.Buffered(1)),
                  pl.BlockSpec((3, w), lambda i: (0, 0))],
        out_specs=(rows(4 * w), rows(3 * w), rows(w), pl.BlockSpec((D_MODEL, tm), lambda i: (0, i)), rows(w)),
        out_shape=(jax.ShapeDtypeStruct((seq, 4 * w), F32), jax.ShapeDtypeStruct((seq, 3 * w), BF16),
                   jax.ShapeDtypeStruct((seq, w), F32), jax.ShapeDtypeStruct((D_MODEL, seq), BF16),
                   jax.ShapeDtypeStruct((seq, w), BF16)),
        scratch_shapes=[pltpu.VMEM((SUBLANES, w), F32)],
        compiler_params=_compiler_params(48, ("arbitrary",)),
    )(x, wing, conv_w)


def _col_block(group, n_sub=CONV_WIDTH // LANES):
    return lambda j: (0, group * n_sub + j)


def _shift_down(ext, k, rows):
    return pltpu.roll(ext, k, 0)[SUBLANES:, :]


def _shift_up(ext, k, rows):
    return pltpu.roll(ext, rows + SUBLANES - k, 0)[:rows, :]


def _conv_bwd(proj, dmix, conv_w):
    seq = proj.shape[0]
    rows = CONV_ROWS
    n_chunks = seq // rows

    def body(b_ref, c_ref, h_ref, z_ref, d_ref, w_ref, dp_ref, dw_ref, u_s, dy_s):
        u_s[0:SUBLANES, :] = jnp.zeros((SUBLANES, LANES), F32)
        dy_s[seq:seq + SUBLANES, :] = jnp.zeros((SUBLANES, LANES), F32)

        def fill(r, carry):
            r0 = pl.multiple_of(r * rows, rows)
            rs = pl.ds(r0, rows)
            u_s[pl.ds(pl.multiple_of(r0 + SUBLANES, SUBLANES), rows), :] = c_ref[rs, :] * h_ref[rs, :]
            z = z_ref[rs, :]
            dy_s[rs, :] = d_ref[rs, :] * (z * _sigmoid(z)) * b_ref[rs, :]
            return carry

        lax.fori_loop(0, n_chunks, fill, 0)
        w = w_ref[...]

        def chunk(r, acc):
            r0 = pl.multiple_of(r * rows, rows)
            rs = pl.ds(r0, rows)
            ext = u_s[pl.ds(r0, rows + SUBLANES), :]
            u = ext[SUBLANES:, :]
            um1 = _shift_down(ext, 1, rows)
            um2 = _shift_down(ext, 2, rows)
            y = w[2:3, :] * u
            y = y + w[0:1, :] * um2
            y = y + w[1:2, :] * um1
            z = z_ref[rs, :]
            b = b_ref[rs, :]
            dco = d_ref[rs, :]
            sg = _sigmoid(z)
            g = z * sg
            dp_ref[0, rs, :] = (dco * g * y).astype(BF16)
            dp_ref[3, rs, :] = (dco * b * y * (sg * (1.0 + z * (1.0 - sg)))).astype(BF16)
            ext_dy = dy_s[pl.ds(r0, rows + SUBLANES), :]
            dy = ext_dy[:rows, :]
            du = w[2:3, :] * dy + w[1:2, :] * _shift_up(ext_dy, 1, rows) + w[0:1, :] * _shift_up(ext_dy, 2, rows)
            dp_ref[1, rs, :] = (du * h_ref[rs, :]).astype(BF16)
            dp_ref[2, rs, :] = (du * c_ref[rs, :]).astype(BF16)
            a0, a1, a2 = acc
            return (a0 + jnp.sum(dy * um2, axis=0, keepdims=True),
                    a1 + jnp.sum(dy * um1, axis=0, keepdims=True),
                    a2 + jnp.sum(dy * u, axis=0, keepdims=True))

        zero = jnp.zeros((1, LANES), F32)
        a0, a1, a2 = lax.fori_loop(0, n_chunks, chunk, (zero, zero, zero))
        dw_ref[...] = jnp.concatenate([a0, a1, a2, jnp.zeros((SUBLANES - 3, LANES), F32)], axis=0)

    col = lambda g: pl.BlockSpec((seq, LANES), _col_block(g))
    return pl.pallas_call(
        body, name="conv_bwd",
        grid=(CONV_WIDTH // LANES,),
        in_specs=[col(0), col(1), col(2), col(3), col(0), pl.BlockSpec((3, LANES), lambda j: (0, j))],
        out_specs=(pl.BlockSpec((4, seq, LANES), lambda j: (0, 0, j)),
                   pl.BlockSpec((SUBLANES, LANES), lambda j: (0, j))),
        out_shape=(jax.ShapeDtypeStruct((4, seq, CONV_WIDTH), BF16),
                   jax.ShapeDtypeStruct((SUBLANES, CONV_WIDTH), F32)),
        scratch_shapes=[pltpu.VMEM((seq + SUBLANES, LANES), F32), pltpu.VMEM((seq + SUBLANES, LANES), F32)],
        compiler_params=_compiler_params(48, ("arbitrary",)),
    )(proj, proj, proj, proj, dmix, conv_w)


def _tri_masks():
    r = lax.broadcasted_iota(jnp.int32, (TK, TK), 0)
    s = lax.broadcasted_iota(jnp.int32, (TK, TK), 1)
    return r, s


def _head_mask():
    lane = lax.broadcasted_iota(jnp.int32, (1, LANES), 1)
    return lane < HEAD_DIM


def _fill_tri(ref, ones):
    ref[...] = jnp.where(ones, 1.0, 0.0).astype(BF16)


def _tile_lanes(a):
    return jnp.tile(a, (1, TK // LANES))


DEAD_CARRY_BITS = 160.0


def _attn_fwd(qkv, z_attn):
    seq = qkv.shape[0]
    n_q = seq // TQ
    n_kb = seq // TK
    assert TQ == TK and n_kb <= LANES
    heads = range(HEADS_PER_STEP)

    def body(q_ref, k_ref, v_ref, za_ref, o_ref, mix_ref, car_ref, qq_s, vt_s, suffix_s, carry_s, diff_s, ot_s):
        head_a = _head_mask()
        r_i, s_i = _tri_masks()
        diff_s[...] = r_i - s_i
        _fill_tri(suffix_s, r_i > s_i)
        lane = lax.broadcasted_iota(jnp.int32, (1, LANES), 1)

        def prep(r, carry):
            rs = pl.ds(pl.multiple_of(r * TQ, TQ), TQ)
            q = q_ref[rs, :]
            qq_s[r, 0] = jnp.where(head_a, q, jnp.zeros_like(q))
            qq_s[r, 1] = jnp.where(head_a, jnp.zeros_like(q), q)
            vt_s[r] = v_ref[rs, :].astype(F32).T.astype(BF16)
            return carry

        lax.fori_loop(0, n_q, prep, 0)

        def block_pair(j, carry):
            qbs = [2 * j, 2 * j + 1]
            rows = [pl.ds(pl.multiple_of(qb * TQ, TQ), TQ) for qb in qbs]
            both = [qq_s[qb].reshape(HEADS_PER_STEP * TQ, LANES) for qb in qbs]
            carry_s[...] = jnp.zeros(carry_s.shape, F32)
            ot_s[...] = jnp.zeros(ot_s.shape, F32)
            for rs in rows:
                for h in heads:
                    car_ref[h, rs, :] = jnp.zeros((TQ, LANES), F32)

            def tiles(items):
                kbs = [qbs[s] - n for s, n in items]
                cols = [pl.ds(pl.multiple_of(jnp.maximum(kb, 0) * TK, TK), TK) for kb in kbs]
                zs = [_dot_nt(both[s], k_ref[cs, :]).reshape(HEADS_PER_STEP, TQ, TK) for (s, _), cs in zip(items, cols)]
                zms, cars, sufs = [], [], []
                for (s, n), kb, z in zip(items, kbs, zs):
                    mask = diff_s[...] > jnp.where(kb < 0, TK, jnp.where(n == 0, 0, -TK))
                    parts = []
                    for h in heads:
                        z2 = jnp.where(mask, z[h] * LOG2_E, MASKED_LOGIT)
                        sp = _softplus2(z2)
                        zms.append(z2 - sp)
                        parts += _split(sp)
                        car = carry_s[s, h]
                        cars.append(car)
                        car_ref[h, rows[s], :] = jnp.where(lane == kb, car, car_ref[h, rows[s], :])
                        carry_s[s, h] = car + jnp.sum(sp, axis=1, keepdims=True)
                    sufs += _block_sums(parts, suffix_s[...])
                for i, ((s, _), kb) in enumerate(zip(items, kbs)):
                    out = []
                    for h in heads:
                        j2 = i * HEADS_PER_STEP + h
                        w = jnp.exp2(zms[j2] - (sufs[j2] + _tile_lanes(cars[j2]))).astype(BF16)
                        out.append(_dot_nt(vt_s[jnp.maximum(kb, 0), h * HEAD_DIM:(h + 1) * HEAD_DIM, :], w))
                    ot_s[s] += jnp.concatenate(out, axis=0)

            tiles([(0, 0), (0, 1), (1, 0), (1, 1)])

            for s, (qb, rs) in enumerate(zip(qbs, rows)):
                def lowest_carry():
                    return jnp.min(jnp.minimum(carry_s[s, 0], carry_s[s, 1]))

                def live(state):
                    n, lowest = state
                    return jnp.logical_and(n <= qb, lowest < DEAD_CARRY_BITS)

                def step(state):
                    n, _ = state
                    tiles([(s, n)])
                    return n + 1, lowest_carry()

                n_done, _ = lax.while_loop(live, step, (jnp.int32(2), lowest_carry()))
                for h in heads:
                    car_ref[h, rs, :] = jnp.where(lane < qb + 1 - n_done, carry_s[s, h], car_ref[h, rs, :])
                o = ot_s[s].T
                o_ref[rs, :] = o
                za = za_ref[rs, :]
                mix_ref[rs, :] = ((za * _sigmoid(za)) * o).astype(BF16)
            return carry

        assert n_q % 2 == 0
        lax.fori_loop(0, n_q // 2, block_pair, 0)

    col = lambda g: pl.BlockSpec((seq, LANES), _col_block(g))
    n_pairs = ATTN_WIDTH // LANES
    return pl.pallas_call(
        body, name="attn_fwd",
        grid=(n_pairs,),
        in_specs=[col(0), col(1), col(2), col(0)],
        out_specs=(pl.BlockSpec((seq, LANES), lambda p: (0, p)),
                   pl.BlockSpec((seq, LANES), lambda p: (0, p)),
                   pl.BlockSpec((HEADS_PER_STEP, seq, LANES), lambda p: (p, 0, 0))),
        out_shape=(jax.ShapeDtypeStruct((seq, ATTN_WIDTH), F32),
                   jax.ShapeDtypeStruct((seq, ATTN_WIDTH), BF16),
                   jax.ShapeDtypeStruct((n_pairs * HEADS_PER_STEP, seq, LANES), F32)),
        scratch_shapes=[pltpu.VMEM((n_q, HEADS_PER_STEP, TQ, LANES), BF16),
                        pltpu.VMEM((n_kb, LANES, TK), BF16),
                        pltpu.VMEM((TK, TK), BF16),
                        pltpu.VMEM((2, HEADS_PER_STEP, TQ, LANES), F32),
                        pltpu.VMEM((TQ, TK), jnp.int32),
                        pltpu.VMEM((2, LANES, TQ), F32)],
        compiler_params=_compiler_params(48, ("arbitrary",)),
    )(qkv, qkv, qkv, z_attn)


def _attn_bwd(qkv, z_attn, o, dmix, carries):
    seq = qkv.shape[0]
    n_q = seq // TQ
    n_kb = seq // TK
    assert TQ == TK and n_kb <= LANES
    heads = range(HEADS_PER_STEP)

    def body(q_ref, k_ref, v_ref, za_ref, o_ref, dm_ref, car_ref, dp_ref,
             qq_s, dd_s, qt_s, dot_s, kt_s, suffix_s, prefix_s, ccar_s, dq_s, dk_s, dv_s, diff_s):
        head_a = _head_mask()
        r_i, s_i = _tri_masks()
        diff_s[...] = r_i - s_i
        _fill_tri(suffix_s, r_i > s_i)
        _fill_tri(prefix_s, r_i < s_i)
        lane = lax.broadcasted_iota(jnp.int32, (1, LANES), 1)

        def prep(r, carry):
            rs = pl.ds(pl.multiple_of(r * TQ, TQ), TQ)
            q = q_ref[rs, :]
            qq_s[r, 0] = jnp.where(head_a, q, jnp.zeros_like(q))
            qq_s[r, 1] = jnp.where(head_a, jnp.zeros_like(q), q)
            qt_s[r] = q.astype(F32).T.astype(BF16)
            kt_s[r] = k_ref[rs, :].astype(F32).T.astype(BF16)
            za = za_ref[rs, :]
            sg = _sigmoid(za)
            dm = dm_ref[rs, :]
            do = dm * (za * sg)
            dd_s[r, 0] = jnp.where(head_a, do, 0.0).astype(BF16)
            dd_s[r, 1] = jnp.where(head_a, 0.0, do).astype(BF16)
            dot_s[r] = do.T.astype(BF16)
            dp_ref[3, rs, :] = (dm * o_ref[rs, :] * (sg * (1.0 + za * (1.0 - sg)))).astype(BF16)
            dq_s[:, rs] = jnp.zeros((LANES, TQ), F32)
            dk_s[:, rs] = jnp.zeros((LANES, TQ), F32)
            dv_s[:, rs] = jnp.zeros((LANES, TQ), F32)
            return carry

        lax.fori_loop(0, n_q, prep, 0)

        def block_pair(j, carry):
            qbs = [2 * j, 2 * j + 1]
            rows = [pl.ds(pl.multiple_of(qb * TQ, TQ), TQ) for qb in qbs]
            both_q = [qq_s[qb].reshape(HEADS_PER_STEP * TQ, LANES) for qb in qbs]
            both_do = [dd_s[qb].reshape(HEADS_PER_STEP * TQ, LANES) for qb in qbs]
            ccar_s[...] = jnp.zeros(ccar_s.shape, F32)

            def tiles(items):
                cols = [pl.ds(pl.multiple_of(jnp.maximum(kb, 0) * TK, TK), TK) for _, kb in items]
                zs = [_dot_nt(both_q[s], k_ref[cs, :]).reshape(HEADS_PER_STEP, TQ, TK) for (s, _), cs in zip(items, cols)]
                dws = [_dot_nt(both_do[s], v_ref[cs, :]).reshape(HEADS_PER_STEP, TQ, TK)
                       for (s, _), cs in zip(items, cols)]
                zms, sufs = [], []
                for (s, kb), z in zip(items, zs):
                    mask = diff_s[...] > jnp.where(kb < 0, TK, jnp.where(kb == qbs[s], 0, -TK))
                    parts = []
                    for h in heads:
                        z2 = jnp.where(mask, z[h] * LOG2_E, MASKED_LOGIT)
                        sp = _softplus2(z2)
                        zms.append(z2 - sp)
                        parts += _split(sp)
                    sufs += _block_sums(parts, suffix_s[...])
                ws, dlws, pres = [], [], []
                for i, (s, kb) in enumerate(items):
                    parts = []
                    for h in heads:
                        j2 = i * HEADS_PER_STEP + h
                        cin = jnp.sum(jnp.where(lane == kb, car_ref[h, rows[s], :], 0.0), axis=1, keepdims=True)
                        w = jnp.exp2(zms[j2] - (sufs[j2] + cin))
                        ws.append(w.astype(BF16))
                        dlw = dws[i][h] * w
                        dlws.append(dlw)
                        parts += _split(dlw)
                    pres += _block_sums(parts, prefix_s[...])
                for i, ((s, kb), cs) in enumerate(zip(items, cols)):
                    dq_t, dk_t, dv_t = [], [], []
                    for h in heads:
                        j2 = i * HEADS_PER_STEP + h
                        ccar = ccar_s[s, h]
                        dz = (dlws[j2] - jnp.exp2(zms[j2]) * (dlws[j2] + (pres[j2] + _tile_lanes(ccar)))).astype(BF16)
                        ccar_s[s, h] = ccar + jnp.sum(dlws[j2], axis=1, keepdims=True)
                        dims = slice(h * HEAD_DIM, (h + 1) * HEAD_DIM)
                        dq_t.append(_dot_nt(kt_s[jnp.maximum(kb, 0), dims, :], dz))
                        dk_t.append(_dot(qt_s[qbs[s], dims, :], dz))
                        dv_t.append(_dot(dot_s[qbs[s], dims, :], ws[j2]))
                    dq_s[:, rows[s]] += jnp.concatenate(dq_t, axis=0)
                    dk_s[:, cs] += jnp.concatenate(dk_t, axis=0)
                    dv_s[:, cs] += jnp.concatenate(dv_t, axis=0)

            for s, (qb, rs) in enumerate(zip(qbs, rows)):
                lowest = jnp.min(jnp.minimum(car_ref[0, rs, :], car_ref[1, rs, :]), axis=0, keepdims=True)
                dead = jnp.logical_and(lowest >= DEAD_CARRY_BITS, lane < qb)
                first_live = jnp.sum(jnp.where(dead, 1, 0))

                def one(kb, c2):
                    tiles([(s, kb)])
                    return c2

                lax.fori_loop(first_live, qb - 1, one, 0)

            for s, qb in enumerate(qbs):
                tiles([(s, qb - 1), (s, qb)])
            return carry

        assert n_q % 2 == 0
        lax.fori_loop(0, n_q // 2, block_pair, 0)

        def finish(r, carry):
            rs = pl.ds(pl.multiple_of(r * TQ, TQ), TQ)
            dp_ref[0, rs, :] = (dq_s[:, rs].T * Q_SCALE).astype(BF16)
            dp_ref[1, rs, :] = dk_s[:, rs].T.astype(BF16)
            dp_ref[2, rs, :] = dv_s[:, rs].T.astype(BF16)
            return carry

        lax.fori_loop(0, n_q, finish, 0)

    def col(g, n_sub=CONV_WIDTH // LANES):
        return pl.BlockSpec((seq, LANES), _col_block(g, n_sub))

    n_pairs = ATTN_WIDTH // LANES
    by_head = lambda: pltpu.VMEM((n_q, HEADS_PER_STEP, TQ, LANES), BF16)
    transposed = lambda: pltpu.VMEM((n_q, LANES, TQ), BF16)
    acc_t = lambda: pltpu.VMEM((LANES, seq), F32)
    return pl.pallas_call(
        body, name="attn_bwd",
        grid=(n_pairs,),
        in_specs=[col(0), col(1), col(2), col(0), col(0), col(1),
                  pl.BlockSpec((HEADS_PER_STEP, seq, LANES), lambda p: (p, 0, 0))],
        out_specs=pl.BlockSpec((4, seq, LANES), lambda p: (0, 0, p)),
        out_shape=jax.ShapeDtypeStruct((4, seq, ATTN_WIDTH), BF16),
        scratch_shapes=[by_head(), by_head(), transposed(), transposed(), transposed(),
                        pltpu.VMEM((TK, TK), BF16), pltpu.VMEM((TK, TK), BF16),
                        pltpu.VMEM((2, HEADS_PER_STEP, TQ, LANES), F32),
                        acc_t(), acc_t(), acc_t(),
                        pltpu.VMEM((TQ, TK), jnp.int32)],
        compiler_params=_compiler_params(56, ("arbitrary",)),
    )(qkv, qkv, qkv, z_attn, o, dmix, carries)


def _out_ln(mix_c, mix_a, woutg, x, target, gain, bias):
    seq = x.shape[0]
    tm = 512
    inv_d = 1.0 / D_MODEL

    def body(mc_ref, ma_ref, wo_ref, x_ref, t_ref, g_ref, b_ref, dxa_ref, dmix_ref, gw_ref, st_ref):
        @pl.when(pl.program_id(0) == 0)
        def _():
            gw_ref[...] = jnp.zeros(gw_ref.shape, F32)
            st_ref[...] = jnp.zeros(st_ref.shape, F32)

        mc = mc_ref[...]
        ma = ma_ref[...]
        sub = _dot(mc, wo_ref[0:CONV_WIDTH, :]) + _dot(ma, wo_ref[CONV_WIDTH:, :])
        r = DEEPNORM_ALPHA * x_ref[...] + sub
        mu = jnp.mean(r, axis=-1, keepdims=True)
        rc = r - mu
        var = jnp.mean(rc * rc, axis=-1, keepdims=True)
        rstd = lax.rsqrt(var + LN_EPS)
        xhat = rc * rstd
        gain_v = g_ref[...]
        diff = (xhat * gain_v + b_ref[...]) - t_ref[...]
        dy = diff * inv_d
        st_ref[0:1, :] += jnp.sum(dy * xhat, axis=0, keepdims=True)
        st_ref[1:2, :] += jnp.sum(dy, axis=0, keepdims=True)
        st_ref[2:3, :] += jnp.sum(diff * diff, axis=0, keepdims=True)
        dxh = dy * gain_v
        m1 = jnp.mean(dxh, axis=-1, keepdims=True)
        m2 = jnp.mean(dxh * xhat, axis=-1, keepdims=True)
        dr = rstd * (dxh - m1 - xhat * m2)
        dxa_ref[...] = DEEPNORM_ALPHA * dr
        drb = dr.astype(BF16)
        dmix_ref[...] = _dot_nt(drb, wo_ref[...])
        gw_ref[0:CONV_WIDTH, :] += _dot_tn(mc, drb)
        gw_ref[CONV_WIDTH:, :] += _dot_tn(ma, drb)

    rows_d = lambda: pl.BlockSpec((tm, D_MODEL), lambda i: (i, 0))
    rows_h = lambda: pl.BlockSpec((tm, CONV_WIDTH), lambda i: (i, 0))
    whole = lambda shape: pl.BlockSpec(shape, lambda i: (0, 0))
    return pl.pallas_call(
        body, name="out_ln",
        grid=(seq // tm,),
        in_specs=[rows_h(), rows_h(), whole((D_MODEL, D_MODEL)), rows_d(), rows_d(),
                  whole((1, D_MODEL)), whole((1, D_MODEL))],
        out_specs=(rows_d(), rows_d(), whole((D_MODEL, D_MODEL)), whole((SUBLANES, D_MODEL))),
        out_shape=(jax.ShapeDtypeStruct((seq, D_MODEL), F32), jax.ShapeDtypeStruct((seq, D_MODEL), F32),
                   jax.ShapeDtypeStruct((D_MODEL, D_MODEL), F32), jax.ShapeDtypeStruct((SUBLANES, D_MODEL), F32)),
        compiler_params=_compiler_params(48, ("arbitrary",)),
    )(mix_c, mix_a, woutg, x, target, gain, bias)


def _group_maps():
    half = N_GROUPS // 2
    return (lambda g: jnp.minimum(g, half - 1)), (lambda g: jnp.maximum(g - half, 0))


def _grad_w_in(xt, dpc, dpa):
    seq = xt.shape[1]
    half = N_GROUPS // 2
    in_conv, in_attn = _group_maps()

    def body(xt_ref, dc_ref, da_ref, out_ref):
        g = pl.program_id(0)

        @pl.when(g < half)
        def _():
            out_ref[...] = _dot(xt_ref[...], dc_ref[...])

        @pl.when(g >= half)
        def _():
            out_ref[...] = _dot(xt_ref[...], da_ref[...])

    return pl.pallas_call(
        body, name="grad_w_in",
        grid=(N_GROUPS,),
        in_specs=[pl.BlockSpec((D_MODEL, seq), lambda g: (0, 0), pipeline_mode=pl.Buffered(1)),
                  pl.BlockSpec((None, seq, CONV_WIDTH), lambda g: (in_conv(g), 0, 0)),
                  pl.BlockSpec((None, seq, CONV_WIDTH), lambda g: (in_attn(g), 0, 0))],
        out_specs=pl.BlockSpec((None, D_MODEL, CONV_WIDTH), lambda g: (g // 2, 0, g % 2)),
        out_shape=jax.ShapeDtypeStruct((N_CHIPS, D_MODEL, D_MODEL), F32),
        compiler_params=_compiler_params(48, ("arbitrary",)),
    )(xt, dpc, dpa)


def _grad_x(dxa, dpc, dpa, wing):
    seq = dxa.shape[0]
    tm = 512
    half = N_GROUPS // 2

    def body(dxa_ref, dc_ref, da_ref, w_ref, out_ref):
        acc = dxa_ref[...]
        for g in range(N_GROUPS):
            dp = dc_ref[g] if g < half else da_ref[g - half]
            cols = slice((g % 2) * CONV_WIDTH, (g % 2 + 1) * CONV_WIDTH)
            acc = acc + _dot_nt(dp, w_ref[g // 2, :, cols])
        out_ref[...] = acc

    return pl.pallas_call(
        body, name="grad_x",
        grid=(seq // tm,),
        in_specs=[pl.BlockSpec((tm, D_MODEL), lambda i: (i, 0)),
                  pl.BlockSpec((half, tm, CONV_WIDTH), lambda i: (0, i, 0)),
                  pl.BlockSpec((half, tm, CONV_WIDTH), lambda i: (0, i, 0)),
                  pl.BlockSpec((N_CHIPS, D_MODEL, D_MODEL), lambda i: (0, 0, 0), pipeline_mode=pl.Buffered(1))],
        out_specs=pl.BlockSpec((tm, D_MODEL), lambda i: (i, 0)),
        out_shape=jax.ShapeDtypeStruct((seq, D_MODEL), F32),
        compiler_params=_compiler_params(40, ("arbitrary",)),
    )(dxa, dpc, dpa, wing)


PACK_LOSS_ROW = 5


def _reduce_grads(gwin, gwout, pack):
    n_shards, n_rows_in, n_cols = gwin.shape
    n_rows_out = gwout.shape[1]
    half_rows = (n_rows_in // 2, n_rows_out // 2)
    row_chunk = 128

    def body(gwin_ref, gwout_ref, pack_ref, oin_ref, oout_ref, opack_ref,
             mine_in, mine_out, sib_in, sib_out, wire_in, wire_out, rin, rout, rpack,
             local_sems, sib_send, sib_recv, send_sems, recv_sems, pack_send, pack_recv, fin_send, fin_recv):
        x, y, c = lax.axis_index("x"), lax.axis_index("y"), lax.axis_index("c")
        me = 2 * x + y
        my_id = 4 * x + 2 * y + c
        chips = [(1 - x, 1 - y), (1 - x, y), (x, 1 - y)]
        sibling = (x, y, 1 - c)
        partial = (gwin_ref, gwout_ref)
        mine = (mine_in, mine_out)
        from_sib = (sib_in, sib_out)
        wire = (wire_in, wire_out)
        from_chips = (rin, rout)
        result = (oin_ref, oout_ref)

        def half(a, which):
            rows = half_rows[a]
            return pl.ds(pl.multiple_of(which * rows, rows), rows)

        order = [2 * px + py for px, py in chips] + [me]

        def swap_copy(i, a):
            shard = order[i]
            return pltpu.make_async_remote_copy(
                src_ref=partial[a].at[shard, half(a, 1 - c), :], dst_ref=from_sib[a].at[shard],
                send_sem=sib_send.at[2 * i + a], recv_sem=sib_recv.at[2 * i + a],
                device_id=sibling, device_id_type=MESH)

        sends = []
        loads = []
        for a in range(2):
            ld = pltpu.make_async_copy(partial[a].at[:, half(a, c), :], mine[a], local_sems.at[a])
            ld.start()
            loads.append(ld)
        for i in range(n_shards):
            for a in range(2):
                cp = swap_copy(i, a)
                cp.start()
                sends.append(cp)

        rpack[my_id] = pack_ref[...]

        def pack_copy(rel, slot, to):
            return pltpu.make_async_remote_copy(
                src_ref=rpack.at[slot], dst_ref=rpack.at[slot],
                send_sem=pack_send.at[rel - 1], recv_sem=pack_recv.at[rel - 1],
                device_id=to, device_id_type=MESH)

        def related(rel):
            px = (1 - x) if rel & 4 else x
            py = (1 - y) if rel & 2 else y
            pc = (1 - c) if rel & 1 else c
            return px, py, pc

        for rel in range(1, N_DEVICES):
            cp = pack_copy(rel, my_id, related(rel))
            cp.start()
            sends.append(cp)

        for ld in loads:
            ld.wait()

        def chip_sum(a, shard):
            def add(r, carry):
                rs = pl.ds(pl.multiple_of(r * row_chunk, row_chunk), row_chunk)
                mine[a][shard, rs, :] = mine[a][shard, rs, :] + from_sib[a][shard, rs, :]
                return carry

            lax.fori_loop(0, half_rows[a] // row_chunk, add, 0)

        def chip_sum_to_wire(a, shard, k):
            def add(r, carry):
                rs = pl.ds(pl.multiple_of(r * row_chunk, row_chunk), row_chunk)
                wire[a][k, rs, :] = (mine[a][shard, rs, :] + from_sib[a][shard, rs, :]).astype(BF16)
                return carry

            lax.fori_loop(0, half_rows[a] // row_chunk, add, 0)

        def shard_copy(k, a, to):
            return pltpu.make_async_remote_copy(
                src_ref=wire[a].at[k], dst_ref=from_chips[a].at[k],
                send_sem=send_sems.at[2 * k + a], recv_sem=recv_sems.at[2 * k + a],
                device_id=to, device_id_type=MESH)

        for k, (px, py) in enumerate(chips):
            for a in range(2):
                swap_copy(k, a).wait_recv()
                chip_sum_to_wire(a, 2 * px + py, k)
                cp = shard_copy(k, a, (px, py, c))
                cp.start()
                sends.append(cp)
        for a in range(2):
            swap_copy(n_shards - 1, a).wait_recv()
            chip_sum(a, me)

        for rel in range(1, N_DEVICES):
            px, py, pc = related(rel)
            pack_copy(rel, 4 * px + 2 * py + pc, (px, py, pc)).wait_recv()
        total = rpack[0]
        for i in range(1, N_DEVICES):
            total = total + rpack[i]
        opack_ref[...] = total
        sq_err = jnp.sum(total[PACK_LOSS_ROW:PACK_LOSS_ROW + 1, :], axis=1, keepdims=True)
        opack_ref[PACK_LOSS_ROW:PACK_LOSS_ROW + 1, :] = jnp.broadcast_to(sq_err * (0.5 / D_MODEL), (1, n_cols))

        for k, (px, py) in enumerate(chips):
            for a in range(2):
                shard_copy(k, a, (px, py, c)).wait_recv()

        def finish(a):
            def add(r, carry):
                rs = pl.ds(pl.multiple_of(r * row_chunk, row_chunk), row_chunk)
                mine[a][me, rs, :] = ((mine[a][me, rs, :] + from_chips[a][0, rs, :].astype(F32))
                                      + from_chips[a][1, rs, :].astype(F32)) + from_chips[a][2, rs, :].astype(F32)
                return carry

            lax.fori_loop(0, half_rows[a] // row_chunk, add, 0)

        def final_copy(a, which):
            return pltpu.make_async_remote_copy(
                src_ref=mine[a].at[me], dst_ref=result[a].at[half(a, which), :],
                send_sem=fin_send.at[a], recv_sem=fin_recv.at[a], device_id=sibling, device_id_type=MESH)

        stores = []
        for a in range(2):
            finish(a)
            cp = final_copy(a, c)
            cp.start()
            sends.append(cp)
            st = pltpu.make_async_copy(mine[a].at[me], result[a].at[half(a, c), :], local_sems.at[2 + a])
            st.start()
            stores.append(st)
        for a in range(2):
            final_copy(a, 1 - c).wait_recv()
        for cp in sends:
            cp.wait_send()
        for st in stores:
            st.wait()

    vmem = pl.BlockSpec(memory_space=pltpu.VMEM)
    hbm = pl.BlockSpec(memory_space=pl.ANY)
    in_shape = (n_rows_in, n_cols)
    out_shape = (n_rows_out, n_cols)
    half_in = (half_rows[0], n_cols)
    half_out = (half_rows[1], n_cols)
    return pl.pallas_call(
        body, name="reduce_grads",
        out_shape=(jax.ShapeDtypeStruct(in_shape, F32), jax.ShapeDtypeStruct(out_shape, F32),
                   jax.ShapeDtypeStruct(pack.shape, F32)),
        in_specs=[hbm, hbm, vmem], out_specs=(hbm, hbm, vmem),
        scratch_shapes=[pltpu.VMEM((n_shards,) + half_in, F32), pltpu.VMEM((n_shards,) + half_out, F32),
                        pltpu.VMEM((n_shards,) + half_in, F32), pltpu.VMEM((n_shards,) + half_out, F32),
                        pltpu.VMEM((N_CHIPS - 1,) + half_in, BF16), pltpu.VMEM((N_CHIPS - 1,) + half_out, BF16),
                        pltpu.VMEM((N_CHIPS - 1,) + half_in, BF16), pltpu.VMEM((N_CHIPS - 1,) + half_out, BF16),
                        pltpu.VMEM((N_DEVICES,) + pack.shape, F32),
                        pltpu.SemaphoreType.DMA((4,)),
                        pltpu.SemaphoreType.DMA((2 * n_shards,)), pltpu.SemaphoreType.DMA((2 * n_shards,)),
                        pltpu.SemaphoreType.DMA((6,)), pltpu.SemaphoreType.DMA((6,)),
                        pltpu.SemaphoreType.DMA((N_DEVICES - 1,)), pltpu.SemaphoreType.DMA((N_DEVICES - 1,)),
                        pltpu.SemaphoreType.DMA((2,)), pltpu.SemaphoreType.DMA((2,))],
        compiler_params=_compiler_params(48),
    )(gwin, gwout, pack)


def _adamw(sets):
    n_rows, n_cols = sets[0][0].shape
    tr = 256
    assert n_rows % tr == 0
    m_corr = 1.0 - ADAM_B1 ** ADAM_STEP
    v_corr = 1.0 - ADAM_B2 ** ADAM_STEP
    n_sets = len(sets)

    def update(w_ref, g_ref, m_ref, v_ref, d_ref, nm_ref, nv_ref):
        gv = g_ref[...]
        nm = ADAM_B1 * m_ref[...] + (1.0 - ADAM_B1) * gv
        nv = ADAM_B2 * v_ref[...] + (1.0 - ADAM_B2) * (gv * gv)
        nm_ref[...] = nm
        nv_ref[...] = nv
        d_ref[...] = -ADAM_LR * ((nm / m_corr) / (jnp.sqrt(nv / v_corr) + ADAM_EPS) + ADAM_WD * w_ref[...])

    def body(*refs):
        ins, outs = refs[:4 * n_sets], refs[4 * n_sets:]
        update(*ins[0:4], *outs[0:3])

        @pl.when(pl.program_id(0) == 0)
        def _():
            for k in range(1, n_sets):
                update(*ins[4 * k:4 * k + 4], *outs[3 * k:3 * k + 3])

    def spec(k):
        if k == 0:
            return pl.BlockSpec((tr, n_cols), lambda i: (i, 0))
        return pl.BlockSpec(sets[k][0].shape, lambda i: (0, 0))

    in_specs = [spec(k) for k in range(n_sets) for _ in range(4)]
    out_specs = tuple(spec(k) for k in range(n_sets) for _ in range(3))
    out_shape = tuple(jax.ShapeDtypeStruct(sets[k][0].shape, F32) for k in range(n_sets) for _ in range(3))
    flat = pl.pallas_call(
        body, name="adamw",
        grid=(n_rows // tr,),
        in_specs=in_specs, out_specs=out_specs, out_shape=out_shape,
        compiler_params=_compiler_params(48, ("arbitrary",)),
    )(*[a for group in sets for a in group])
    return [tuple(flat[3 * k:3 * k + 3]) for k in range(n_sets)]


def kernel(x, w_in, conv_w, w_out, ln_gain, ln_bias, loss_target, m_w_in, m_conv_w, m_w_out, m_ln_gain, m_ln_bias, v_w_in, v_conv_w, v_w_out, v_ln_gain, v_ln_bias):
    xs = x[0]
    target = loss_target[0]

    wing, woutg, cwg = _gather_weights(w_in[0], w_out[0], conv_w[0])
    conv_full = jnp.transpose(cwg, (1, 0, 2)).reshape(3, CONV_WIDTH)
    wout_full = woutg.reshape(D_MODEL, D_MODEL)

    p_conv, qkv, z_attn, xt, mix_c = _proj(xs, wing, conv_full)
    o, mix_a, carries = _attn_fwd(qkv, z_attn)
    dxa, dmix, gwout, stats = _out_ln(mix_c, mix_a, wout_full, xs, target, ln_gain, ln_bias)
    dpc, dconv = _conv_bwd(p_conv, dmix, conv_full)
    dpa = _attn_bwd(qkv, z_attn, o, dmix, carries)
    gwin = _grad_w_in(xt, dpc, dpa)
    grad_x = _grad_x(dxa, dpc, dpa, wing)

    pack = jnp.concatenate(
        [stats[0:2], jnp.pad(dconv[0:3], ((0, 0), (0, D_MODEL - CONV_WIDTH))), stats[2:3],
         jnp.zeros((2, D_MODEL), F32)], axis=0)
    g_w_in, g_w_out, tot = _reduce_grads(gwin, gwout.reshape(N_CHIPS, D_MODEL // N_CHIPS, D_MODEL), pack)

    chip = 2 * lax.axis_index("x") + lax.axis_index("y")
    g_gain = tot[0:1]
    g_bias = tot[1:2]
    g_conv = lax.dynamic_slice(tot, (2, chip * LANES), (3, LANES))
    loss = tot[PACK_LOSS_ROW, 0]

    ((d_w_in, nm_w_in, nv_w_in), (d_w_out, nm_w_out, nv_w_out), (d_conv, nm_conv, nv_conv),
     (d_gain, nm_gain, nv_gain), (d_bias, nm_bias, nv_bias)) = _adamw([
        (w_in[0], g_w_in, m_w_in[0], v_w_in[0]),
        (w_out[0], g_w_out, m_w_out[0], v_w_out[0]),
        (conv_w[0], g_conv, m_conv_w[0], v_conv_w[0]),
        (ln_gain, g_gain, m_ln_gain, v_ln_gain),
        (ln_bias, g_bias, m_ln_bias, v_ln_bias)])

    lead = lambda a: a[None]
    return (loss, lead(grad_x),
            lead(g_w_in), lead(g_conv), lead(g_w_out), g_gain, g_bias,
            lead(d_w_in), lead(d_conv), lead(d_w_out), d_gain, d_bias,
            lead(nm_w_in), lead(nm_conv), lead(nm_w_out), nm_gain, nm_bias,
            lead(nv_w_in), lead(nv_conv), lead(nv_w_out), nv_gain, nv_bias)
```

```python
import jax
import jax.numpy as jnp
from jax import lax
from jax.experimental import pallas as pl
from jax.experimental.pallas import tpu as pltpu

F32 = jnp.float32
BF16 = jnp.bfloat16
MESH = pl.DeviceIdType.MESH

D_MODEL = 1024
CONV_WIDTH = 512
ATTN_WIDTH = 512
HEAD_DIM = 64
N_GROUPS = 8
N_CHIPS = 4
N_DEVICES = 8
LN_EPS = 1e-5
DEEPNORM_ALPHA = 2.0 ** 0.25
Q_SCALE = HEAD_DIM ** -0.5
ADAM_LR = 0.001
ADAM_B1 = 0.9
ADAM_B2 = 0.999
ADAM_EPS = 1e-08
ADAM_WD = 0.01
ADAM_STEP = 10

LANES = 128
SUBLANES = 8
V7X_VMEM_BYTES = 64 * 1024 * 1024
MIB = 1024 * 1024

TQ = 256
TK = 256
HEADS_PER_STEP = LANES // HEAD_DIM
CONV_ROWS = 512


def _compiler_params(vmem_mib, semantics=None):
    assert vmem_mib * MIB < V7X_VMEM_BYTES
    return pltpu.CompilerParams(dimension_semantics=semantics, vmem_limit_bytes=vmem_mib * MIB)


def _sigmoid(z):
    return 1.0 / (1.0 + jnp.exp(-z))


def _dot(a, b):
    return jnp.dot(a, b, preferred_element_type=F32)


def _dot_nt(a, b):
    return lax.dot_general(a, b, (((1,), (1,)), ((), ())), preferred_element_type=F32)


def _dot_tn(a, b):
    return lax.dot_general(a, b, (((0,), (0,)), ((), ())), preferred_element_type=F32)


def _truncate_to_bf16(a):
    bits = lax.bitcast_convert_type(a, jnp.uint32) & jnp.uint32(0xFFFF0000)
    return lax.bitcast_convert_type(bits, F32)


def _split(a):
    hi = _truncate_to_bf16(a)
    return [hi.astype(BF16), (a - hi).astype(BF16)]


def _block_sums(parts, tri):
    n = len(parts) // 2
    res = _dot(jnp.concatenate(parts, axis=0), tri).reshape(n, 2, TQ, TK)
    return [res[i, 0] + res[i, 1] for i in range(n)]


MASKED_LOGIT = -1e30
LOG2_E = 1.4426950408889634


def _softplus2(z2):
    return jnp.maximum(z2, 0.0) + jnp.log2(1.0 + jnp.exp2(-jnp.abs(z2)))


def _gather_weights(w_in, w_out, conv_w):
    d_rows, d_cols = w_in.shape
    o_rows, o_cols = w_out.shape
    half_rows = (d_rows // 2, o_rows // 2)
    row_chunk = 128

    def body(win_ref, wout_ref, cw_ref, wing_ref, woutg_ref, cwg_ref, cast_in, cast_out,
             send_sems, recv_sems, pass_send, pass_recv, store_sems):
        x, y, c = lax.axis_index("x"), lax.axis_index("y"), lax.axis_index("c")
        me = 2 * x + y
        srcs = (win_ref, wout_ref)
        bufs = (wing_ref, woutg_ref)
        casts = (cast_in, cast_out)

        def rows_of(a, which):
            rows = half_rows[a]
            return pl.ds(pl.multiple_of(which * rows, rows), rows)

        def half(a, shard, which):
            return bufs[a].at[shard, rows_of(a, which), :]

        def cast_half(which):
            for a in range(2):
                def cast(r, carry):
                    rows = pl.ds(pl.multiple_of(which * half_rows[a] + r * row_chunk, row_chunk), row_chunk)
                    casts[a][rows, :] = srcs[a][rows, :].astype(BF16)
                    return carry

                lax.fori_loop(0, half_rows[a] // row_chunk, cast, 0)

        chips = [(1 - x, y), (x, 1 - y), (1 - x, 1 - y)]

        def ici_copy(k, a, shard, to):
            if a == 2:
                src = dst = cwg_ref.at[shard]
            else:
                src, dst = casts[a].at[rows_of(a, c), :], half(a, shard, c)
            return pltpu.make_async_remote_copy(
                src_ref=src, dst_ref=dst, send_sem=send_sems.at[3 * k + a], recv_sem=recv_sems.at[3 * k + a],
                device_id=to, device_id_type=MESH)

        def pass_copy(k, a, shard, which):
            ref = half(a, shard, which)
            return pltpu.make_async_remote_copy(
                src_ref=ref, dst_ref=ref, send_sem=pass_send.at[2 * k + a], recv_sem=pass_recv.at[2 * k + a],
                device_id=(x, y, 1 - c), device_id_type=MESH)

        cast_half(c)
        cwg_ref[me] = cw_ref[...]
        sends = []
        for k, (px, py) in enumerate(chips):
            for a in range(3):
                cp = ici_copy(k, a, me, (px, py, c))
                cp.start()
                sends.append(cp)
        cast_half(1 - c)
        stores = []
        for a in range(2):
            st = pltpu.make_async_copy(casts[a], bufs[a].at[me], store_sems.at[a])
            st.start()
            stores.append(st)
        for k, (px, py) in enumerate(chips):
            for a in range(2):
                ici_copy(k, a, 2 * px + py, (px, py, c)).wait_recv()
                cp = pass_copy(k, a, 2 * px + py, c)
                cp.start()
                sends.append(cp)
        for k, (px, py) in enumerate(chips):
            ici_copy(k, 2, 2 * px + py, (px, py, c)).wait_recv()
            for a in range(2):
                pass_copy(k, a, 2 * px + py, 1 - c).wait_recv()
        for cp in sends:
            cp.wait_send()
        for st in stores:
            st.wait()

    vmem = pl.BlockSpec(memory_space=pltpu.VMEM)
    hbm = pl.BlockSpec(memory_space=pl.ANY)
    return pl.pallas_call(
        body, name="gather_weights",
        out_shape=(jax.ShapeDtypeStruct((N_CHIPS, d_rows, d_cols), BF16),
                   jax.ShapeDtypeStruct((N_CHIPS, o_rows, o_cols), BF16),
                   jax.ShapeDtypeStruct((N_CHIPS,) + conv_w.shape, F32)),
        in_specs=[vmem, vmem, vmem], out_specs=(hbm, hbm, vmem),
        scratch_shapes=[pltpu.VMEM((d_rows, d_cols), BF16), pltpu.VMEM((o_rows, o_cols), BF16),
                        pltpu.SemaphoreType.DMA((9,)), pltpu.SemaphoreType.DMA((9,)),
                        pltpu.SemaphoreType.DMA((6,)), pltpu.SemaphoreType.DMA((6,)),
                        pltpu.SemaphoreType.DMA((2,))],
        compiler_params=_compiler_params(32),
    )(w_in, w_out, conv_w)


def _proj(x, wing, conv_w):
    seq = x.shape[0]
    tm = 512
    w = CONV_WIDTH

    def body(x_ref, w_ref, cw_ref, conv_ref, qkv_ref, za_ref, xt_ref, mix_ref, tail_s):
        xv = x_ref[...]
        xb = xv.astype(BF16)
        bc = _dot(xb, w_ref[0])
        hz = _dot(xb, w_ref[1])
        conv_ref[:, 0:D_MODEL] = bc
        conv_ref[:, D_MODEL:] = hz

        @pl.when(pl.program_id(0) == 0)
        def _():
            tail_s[...] = jnp.zeros(tail_s.shape, F32)

        u = bc[:, w:] * hz[:, 0:w]
        ext = jnp.concatenate([tail_s[...], u], axis=0)
        tail_s[...] = u[tm - SUBLANES:, :]
        taps = cw_ref[...]
        y = taps[2:3, :] * u
        y = y + taps[0:1, :] * _shift_down(ext, 2, tm)
        y = y + taps[1:2, :] * _shift_down(ext, 1, tm)
        z = hz[:, w:]
        mix_ref[...] = ((z * _sigmoid(z)) * (bc[:, 0:w] * y)).astype(BF16)

        qk = _dot(xb, w_ref[2])
        qkv_ref[:, 0:w] = (qk[:, 0:w] * Q_SCALE).astype(BF16)
        qkv_ref[:, w:2 * w] = qk[:, w:].astype(BF16)
        vz = _dot(xb, w_ref[3])
        qkv_ref[:, 2 * w:] = vz[:, 0:w].astype(BF16)
        za_ref[...] = vz[:, w:]
        xt_ref[...] = xv.T.astype(BF16)

    rows = lambda width: pl.BlockSpec((tm, width), lambda i: (i, 0))
    return pl.pallas_call(
        body, name="proj",
        grid=(seq // tm,),
        in_specs=[rows(D_MODEL),
                  pl.BlockSpec((N_CHIPS, D_MODEL, D_MODEL), lambda i: (0, 0, 0), pipeline_mode=pl.Buffered(1)),
                  pl.BlockSpec((3, w), lambda i: (0, 0))],
        out_specs=(rows(4 * w), rows(3 * w), rows(w), pl.BlockSpec((D_MODEL, tm), lambda i: (0, i)), rows(w)),
        out_shape=(jax.ShapeDtypeStruct((seq, 4 * w), F32), jax.ShapeDtypeStruct((seq, 3 * w), BF16),
                   jax.ShapeDtypeStruct((seq, w), F32), jax.ShapeDtypeStruct((D_MODEL, seq), BF16),
                   jax.ShapeDtypeStruct((seq, w), BF16)),
        scratch_shapes=[pltpu.VMEM((SUBLANES, w), F32)],
        compiler_params=_compiler_params(48, ("arbitrary",)),
    )(x, wing, conv_w)


def _col_block(group, n_sub=CONV_WIDTH // LANES):
    return lambda j: (0, group * n_sub + j)


def _shift_down(ext, k, rows):
    return pltpu.roll(ext, k, 0)[SUBLANES:, :]


def _shift_up(ext, k, rows):
    return pltpu.roll(ext, rows + SUBLANES - k, 0)[:rows, :]


def _conv_bwd(proj, dmix, conv_w):
    seq = proj.shape[0]
    rows = CONV_ROWS
    n_chunks = seq // rows

    def body(b_ref, c_ref, h_ref, z_ref, d_ref, w_ref, dp_ref, dw_ref, u_s, dy_s):
        u_s[0:SUBLANES, :] = jnp.zeros((SUBLANES, LANES), F32)
        dy_s[seq:seq + SUBLANES, :] = jnp.zeros((SUBLANES, LANES), F32)

        def fill(r, carry):
            r0 = pl.multiple_of(r * rows, rows)
            rs = pl.ds(r0, rows)
            u_s[pl.ds(pl.multiple_of(r0 + SUBLANES, SUBLANES), rows), :] = c_ref[rs, :] * h_ref[rs, :]
            z = z_ref[rs, :]
            dy_s[rs, :] = d_ref[rs, :] * (z * _sigmoid(z)) * b_ref[rs, :]
            return carry

        lax.fori_loop(0, n_chunks, fill, 0)
        w = w_ref[...]

        def chunk(r, acc):
            r0 = pl.multiple_of(r * rows, rows)
            rs = pl.ds(r0, rows)
            ext = u_s[pl.ds(r0, rows + SUBLANES), :]
            u = ext[SUBLANES:, :]
            um1 = _shift_down(ext, 1, rows)
            um2 = _shift_down(ext, 2, rows)
            y = w[2:3, :] * u
            y = y + w[0:1, :] * um2
            y = y + w[1:2, :] * um1
            z = z_ref[rs, :]
            b = b_ref[rs, :]
            dco = d_ref[rs, :]
            sg = _sigmoid(z)
            g = z * sg
            dp_ref[0, rs, :] = (dco * g * y).astype(BF16)
            dp_ref[3, rs, :] = (dco * b * y * (sg * (1.0 + z * (1.0 - sg)))).astype(BF16)
            ext_dy = dy_s[pl.ds(r0, rows + SUBLANES), :]
            dy = ext_dy[:rows, :]
            du = w[2:3, :] * dy + w[1:2, :] * _shift_up(ext_dy, 1, rows) + w[0:1, :] * _shift_up(ext_dy, 2, rows)
            dp_ref[1, rs, :] = (du * h_ref[rs, :]).astype(BF16)
            dp_ref[2, rs, :] = (du * c_ref[rs, :]).astype(BF16)
            a0, a1, a2 = acc
            return (a0 + jnp.sum(dy * um2, axis=0, keepdims=True),
                    a1 + jnp.sum(dy * um1, axis=0, keepdims=True),
                    a2 + jnp.sum(dy * u, axis=0, keepdims=True))

        zero = jnp.zeros((1, LANES), F32)
        a0, a1, a2 = lax.fori_loop(0, n_chunks, chunk, (zero, zero, zero))
        dw_ref[...] = jnp.concatenate([a0, a1, a2, jnp.zeros((SUBLANES - 3, LANES), F32)], axis=0)

    col = lambda g: pl.BlockSpec((seq, LANES), _col_block(g))
    return pl.pallas_call(
        body, name="conv_bwd",
        grid=(CONV_WIDTH // LANES,),
        in_specs=[col(0), col(1), col(2), col(3), col(0), pl.BlockSpec((3, LANES), lambda j: (0, j))],
        out_specs=(pl.BlockSpec((4, seq, LANES), lambda j: (0, 0, j)),
                   pl.BlockSpec((SUBLANES, LANES), lambda j: (0, j))),
        out_shape=(jax.ShapeDtypeStruct((4, seq, CONV_WIDTH), BF16),
                   jax.ShapeDtypeStruct((SUBLANES, CONV_WIDTH), F32)),
        scratch_shapes=[pltpu.VMEM((seq + SUBLANES, LANES), F32), pltpu.VMEM((seq + SUBLANES, LANES), F32)],
        compiler_params=_compiler_params(48, ("arbitrary",)),
    )(proj, proj, proj, proj, dmix, conv_w)


def _tri_masks():
    r = lax.broadcasted_iota(jnp.int32, (TK, TK), 0)
    s = lax.broadcasted_iota(jnp.int32, (TK, TK), 1)
    return r, s


def _head_mask():
    lane = lax.broadcasted_iota(jnp.int32, (1, LANES), 1)
    return lane < HEAD_DIM


def _fill_tri(ref, ones):
    ref[...] = jnp.where(ones, 1.0, 0.0).astype(BF16)


def _tile_lanes(a):
    return jnp.tile(a, (1, TK // LANES))


DEAD_CARRY_BITS = 160.0


def _attn_fwd(qkv, z_attn):
    seq = qkv.shape[0]
    n_q = seq // TQ
    n_kb = seq // TK
    assert TQ == TK and n_kb <= LANES
    heads = range(HEADS_PER_STEP)

    def body(q_ref, k_ref, v_ref, za_ref, o_ref, mix_ref, car_ref, qq_s, vt_s, suffix_s, carry_s, diff_s, ot_s):
        head_a = _head_mask()
        r_i, s_i = _tri_masks()
        diff_s[...] = r_i - s_i
        _fill_tri(suffix_s, r_i > s_i)
        lane = lax.broadcasted_iota(jnp.int32, (1, LANES), 1)

        def prep(r, carry):
            rs = pl.ds(pl.multiple_of(r * TQ, TQ), TQ)
            q = q_ref[rs, :]
            qq_s[r, 0] = jnp.where(head_a, q, jnp.zeros_like(q))
            qq_s[r, 1] = jnp.where(head_a, jnp.zeros_like(q), q)
            vt_s[r] = v_ref[rs, :].astype(F32).T.astype(BF16)
            return carry

        lax.fori_loop(0, n_q, prep, 0)

        def block_pair(j, carry):
            qbs = [2 * j, 2 * j + 1]
            rows = [pl.ds(pl.multiple_of(qb * TQ, TQ), TQ) for qb in qbs]
            both = [qq_s[qb].reshape(HEADS_PER_STEP * TQ, LANES) for qb in qbs]
            carry_s[...] = jnp.zeros(carry_s.shape, F32)
            ot_s[...] = jnp.zeros(ot_s.shape, F32)
            for rs in rows:
                for h in heads:
                    car_ref[h, rs, :] = jnp.zeros((TQ, LANES), F32)

            def tiles(items):
                kbs = [qbs[s] - n for s, n in items]
                cols = [pl.ds(pl.multiple_of(jnp.maximum(kb, 0) * TK, TK), TK) for kb in kbs]
                zs = [_dot_nt(both[s], k_ref[cs, :]).reshape(HEADS_PER_STEP, TQ, TK) for (s, _), cs in zip(items, cols)]
                zms, cars, sufs = [], [], []
                for (s, n), kb, z in zip(items, kbs, zs):
                    mask = diff_s[...] > jnp.where(kb < 0, TK, jnp.where(n == 0, 0, -TK))
                    parts = []
                    for h in heads:
                        z2 = jnp.where(mask, z[h] * LOG2_E, MASKED_LOGIT)
                        sp = _softplus2(z2)
                        zms.append(z2 - sp)
                        parts += _split(sp)
                        car = carry_s[s, h]
                        cars.append(car)
                        car_ref[h, rows[s], :] = jnp.where(lane == kb, car, car_ref[h, rows[s], :])
                        carry_s[s, h] = car + jnp.sum(sp, axis=1, keepdims=True)
                    sufs += _block_sums(parts, suffix_s[...])
                for i, ((s, _), kb) in enumerate(zip(items, kbs)):
                    out = []
                    for h in heads:
                        j2 = i * HEADS_PER_STEP + h
                        w = jnp.exp2(zms[j2] - (sufs[j2] + _tile_lanes(cars[j2]))).astype(BF16)
                        out.append(_dot_nt(vt_s[jnp.maximum(kb, 0), h * HEAD_DIM:(h + 1) * HEAD_DIM, :], w))
                    ot_s[s] += jnp.concatenate(out, axis=0)

            tiles([(0, 0), (0, 1), (1, 0), (1, 1)])

            for s, (qb, rs) in enumerate(zip(qbs, rows)):
                def lowest_carry():
                    return jnp.min(jnp.minimum(carry_s[s, 0], carry_s[s, 1]))

                def live(state):
                    n, lowest = state
                    return jnp.logical_and(n <= qb, lowest < DEAD_CARRY_BITS)

                def step(state):
                    n, _ = state
                    tiles([(s, n)])
                    return n + 1, lowest_carry()

                n_done, _ = lax.while_loop(live, step, (jnp.int32(2), lowest_carry()))
                for h in heads:
                    car_ref[h, rs, :] = jnp.where(lane < qb + 1 - n_done, carry_s[s, h], car_ref[h, rs, :])
                o = ot_s[s].T
                o_ref[rs, :] = o
                za = za_ref[rs, :]
                mix_ref[rs, :] = ((za * _sigmoid(za)) * o).astype(BF16)
            return carry

        assert n_q % 2 == 0
        lax.fori_loop(0, n_q // 2, block_pair, 0)

    col = lambda g: pl.BlockSpec((seq, LANES), _col_block(g))
    n_pairs = ATTN_WIDTH // LANES
    return pl.pallas_call(
        body, name="attn_fwd",
        grid=(n_pairs,),
        in_specs=[col(0), col(1), col(2), col(0)],
        out_specs=(pl.BlockSpec((seq, LANES), lambda p: (0, p)),
                   pl.BlockSpec((seq, LANES), lambda p: (0, p)),
                   pl.BlockSpec((HEADS_PER_STEP, seq, LANES), lambda p: (p, 0, 0))),
        out_shape=(jax.ShapeDtypeStruct((seq, ATTN_WIDTH), F32),
                   jax.ShapeDtypeStruct((seq, ATTN_WIDTH), BF16),
                   jax.ShapeDtypeStruct((n_pairs * HEADS_PER_STEP, seq, LANES), F32)),
        scratch_shapes=[pltpu.VMEM((n_q, HEADS_PER_STEP, TQ, LANES), BF16),
                        pltpu.VMEM((n_kb, LANES, TK), BF16),
                        pltpu.VMEM((TK, TK), BF16),
                        pltpu.VMEM((2, HEADS_PER_STEP, TQ, LANES), F32),
                        pltpu.VMEM((TQ, TK), jnp.int32),
                        pltpu.VMEM((2, LANES, TQ), F32)],
        compiler_params=_compiler_params(48, ("arbitrary",)),
    )(qkv, qkv, qkv, z_attn)


def _attn_bwd(qkv, z_attn, o, dmix, carries):
    seq = qkv.shape[0]
    n_q = seq // TQ
    n_kb = seq // TK
    assert TQ == TK and n_kb <= LANES
    heads = range(HEADS_PER_STEP)

    def body(q_ref, k_ref, v_ref, za_ref, o_ref, dm_ref, car_ref, dp_ref,
             qq_s, dd_s, qt_s, dot_s, kt_s, suffix_s, prefix_s, ccar_s, dq_s, dk_s, dv_s, diff_s):
        head_a = _head_mask()
        r_i, s_i = _tri_masks()
        diff_s[...] = r_i - s_i
        _fill_tri(suffix_s, r_i > s_i)
        _fill_tri(prefix_s, r_i < s_i)
        lane = lax.broadcasted_iota(jnp.int32, (1, LANES), 1)

        def prep(r, carry):
            rs = pl.ds(pl.multiple_of(r * TQ, TQ), TQ)
            q = q_ref[rs, :]
            qq_s[r, 0] = jnp.where(head_a, q, jnp.zeros_like(q))
            qq_s[r, 1] = jnp.where(head_a, jnp.zeros_like(q), q)
            qt_s[r] = q.astype(F32).T.astype(BF16)
            kt_s[r] = k_ref[rs, :].astype(F32).T.astype(BF16)
            za = za_ref[rs, :]
            sg = _sigmoid(za)
            dm = dm_ref[rs, :]
            do = dm * (za * sg)
            dd_s[r, 0] = jnp.where(head_a, do, 0.0).astype(BF16)
            dd_s[r, 1] = jnp.where(head_a, 0.0, do).astype(BF16)
            dot_s[r] = do.T.astype(BF16)
            dp_ref[3, rs, :] = (dm * o_ref[rs, :] * (sg * (1.0 + za * (1.0 - sg)))).astype(BF16)
            dq_s[:, rs] = jnp.zeros((LANES, TQ), F32)
            dk_s[:, rs] = jnp.zeros((LANES, TQ), F32)
            dv_s[:, rs] = jnp.zeros((LANES, TQ), F32)
            return carry

        lax.fori_loop(0, n_q, prep, 0)

        def block_pair(j, carry):
            qbs = [2 * j, 2 * j + 1]
            rows = [pl.ds(pl.multiple_of(qb * TQ, TQ), TQ) for qb in qbs]
            both_q = [qq_s[qb].reshape(HEADS_PER_STEP * TQ, LANES) for qb in qbs]
            both_do = [dd_s[qb].reshape(HEADS_PER_STEP * TQ, LANES) for qb in qbs]
            ccar_s[...] = jnp.zeros(ccar_s.shape, F32)

            def tiles(items):
                cols = [pl.ds(pl.multiple_of(jnp.maximum(kb, 0) * TK, TK), TK) for _, kb in items]
                zs = [_dot_nt(both_q[s], k_ref[cs, :]).reshape(HEADS_PER_STEP, TQ, TK) for (s, _), cs in zip(items, cols)]
                dws = [_dot_nt(both_do[s], v_ref[cs, :]).reshape(HEADS_PER_STEP, TQ, TK)
                       for (s, _), cs in zip(items, cols)]
                zms, sufs = [], []
                for (s, kb), z in zip(items, zs):
                    mask = diff_s[...] > jnp.where(kb < 0, TK, jnp.where(kb == qbs[s], 0, -TK))
                    parts = []
                    for h in heads:
                        z2 = jnp.where(mask, z[h] * LOG2_E, MASKED_LOGIT)
                        sp = _softplus2(z2)
                        zms.append(z2 - sp)
                        parts += _split(sp)
                    sufs += _block_sums(parts, suffix_s[...])
                ws, dlws, pres = [], [], []
                for i, (s, kb) in enumerate(items):
                    parts = []
                    for h in heads:
                        j2 = i * HEADS_PER_STEP + h
                        cin = jnp.sum(jnp.where(lane == kb, car_ref[h, rows[s], :], 0.0), axis=1, keepdims=True)
                        w = jnp.exp2(zms[j2] - (sufs[j2] + cin))
                        ws.append(w.astype(BF16))
                        dlw = dws[i][h] * w
                        dlws.append(dlw)
                        parts += _split(dlw)
                    pres += _block_sums(parts, prefix_s[...])
                for i, ((s, kb), cs) in enumerate(zip(items, cols)):
                    dq_t, dk_t, dv_t = [], [], []
                    for h in heads:
                        j2 = i * HEADS_PER_STEP + h
                        ccar = ccar_s[s, h]
                        dz = (dlws[j2] - jnp.exp2(zms[j2]) * (dlws[j2] + (pres[j2] + _tile_lanes(ccar)))).astype(BF16)
                        ccar_s[s, h] = ccar + jnp.sum(dlws[j2], axis=1, keepdims=True)
                        dims = slice(h * HEAD_DIM, (h + 1) * HEAD_DIM)
                        dq_t.append(_dot_nt(kt_s[jnp.maximum(kb, 0), dims, :], dz))
                        dk_t.append(_dot(qt_s[qbs[s], dims, :], dz))
                        dv_t.append(_dot(dot_s[qbs[s], dims, :], ws[j2]))
                    dq_s[:, rows[s]] += jnp.concatenate(dq_t, axis=0)
                    dk_s[:, cs] += jnp.concatenate(dk_t, axis=0)
                    dv_s[:, cs] += jnp.concatenate(dv_t, axis=0)

            for s, (qb, rs) in enumerate(zip(qbs, rows)):
                lowest = jnp.min(jnp.minimum(car_ref[0, rs, :], car_ref[1, rs, :]), axis=0, keepdims=True)
                dead = jnp.logical_and(lowest >= DEAD_CARRY_BITS, lane < qb)
                first_live = jnp.sum(jnp.where(dead, 1, 0))

                def one(kb, c2):
                    tiles([(s, kb)])
                    return c2

                lax.fori_loop(first_live, qb - 1, one, 0)

            for s, qb in enumerate(qbs):
                tiles([(s, qb - 1), (s, qb)])
            return carry

        assert n_q % 2 == 0
        lax.fori_loop(0, n_q // 2, block_pair, 0)

        def finish(r, carry):
            rs = pl.ds(pl.multiple_of(r * TQ, TQ), TQ)
            dp_ref[0, rs, :] = (dq_s[:, rs].T * Q_SCALE).astype(BF16)
            dp_ref[1, rs, :] = dk_s[:, rs].T.astype(BF16)
            dp_ref[2, rs, :] = dv_s[:, rs].T.astype(BF16)
            return carry

        lax.fori_loop(0, n_q, finish, 0)

    def col(g, n_sub=CONV_WIDTH // LANES):
        return pl.BlockSpec((seq, LANES), _col_block(g, n_sub))

    n_pairs = ATTN_WIDTH // LANES
    by_head = lambda: pltpu.VMEM((n_q, HEADS_PER_STEP, TQ, LANES), BF16)
    transposed = lambda: pltpu.VMEM((n_q, LANES, TQ), BF16)
    acc_t = lambda: pltpu.VMEM((LANES, seq), F32)
    return pl.pallas_call(
        body, name="attn_bwd",
        grid=(n_pairs,),
        in_specs=[col(0), col(1), col(2), col(0), col(0), col(1),
                  pl.BlockSpec((HEADS_PER_STEP, seq, LANES), lambda p: (p, 0, 0))],
        out_specs=pl.BlockSpec((4, seq, LANES), lambda p: (0, 0, p)),
        out_shape=jax.ShapeDtypeStruct((4, seq, ATTN_WIDTH), BF16),
        scratch_shapes=[by_head(), by_head(), transposed(), transposed(), transposed(),
                        pltpu.VMEM((TK, TK), BF16), pltpu.VMEM((TK, TK), BF16),
                        pltpu.VMEM((2, HEADS_PER_STEP, TQ, LANES), F32),
                        acc_t(), acc_t(), acc_t(),
                        pltpu.VMEM((TQ, TK), jnp.int32)],
        compiler_params=_compiler_params(56, ("arbitrary",)),
    )(qkv, qkv, qkv, z_attn, o, dmix, carries)


def _out_ln(mix_c, mix_a, woutg, x, target, gain, bias):
    seq = x.shape[0]
    tm = 512
    inv_d = 1.0 / D_MODEL

    def body(mc_ref, ma_ref, wo_ref, x_ref, t_ref, g_ref, b_ref, dxa_ref, dmix_ref, gw_ref, st_ref):
        @pl.when(pl.program_id(0) == 0)
        def _():
            gw_ref[...] = jnp.zeros(gw_ref.shape, F32)
            st_ref[...] = jnp.zeros(st_ref.shape, F32)

        mc = mc_ref[...]
        ma = ma_ref[...]
        sub = _dot(mc, wo_ref[0:CONV_WIDTH, :]) + _dot(ma, wo_ref[CONV_WIDTH:, :])
        r = DEEPNORM_ALPHA * x_ref[...] + sub
        mu = jnp.mean(r, axis=-1, keepdims=True)
        rc = r - mu
        var = jnp.mean(rc * rc, axis=-1, keepdims=True)
        rstd = lax.rsqrt(var + LN_EPS)
        xhat = rc * rstd
        gain_v = g_ref[...]
        diff = (xhat * gain_v + b_ref[...]) - t_ref[...]
        dy = diff * inv_d
        st_ref[0:1, :] += jnp.sum(dy * xhat, axis=0, keepdims=True)
        st_ref[1:2, :] += jnp.sum(dy, axis=0, keepdims=True)
        st_ref[2:3, :] += jnp.sum(diff * diff, axis=0, keepdims=True)
        dxh = dy * gain_v
        m1 = jnp.mean(dxh, axis=-1, keepdims=True)
        m2 = jnp.mean(dxh * xhat, axis=-1, keepdims=True)
        dr = rstd * (dxh - m1 - xhat * m2)
        dxa_ref[...] = DEEPNORM_ALPHA * dr
        drb = dr.astype(BF16)
        dmix_ref[...] = _dot_nt(drb, wo_ref[...])
        gw_ref[0:CONV_WIDTH, :] += _dot_tn(mc, drb)
        gw_ref[CONV_WIDTH:, :] += _dot_tn(ma, drb)

    rows_d = lambda: pl.BlockSpec((tm, D_MODEL), lambda i: (i, 0))
    rows_h = lambda: pl.BlockSpec((tm, CONV_WIDTH), lambda i: (i, 0))
    whole = lambda shape: pl.BlockSpec(shape, lambda i: (0, 0))
    return pl.pallas_call(
        body, name="out_ln",
        grid=(seq // tm,),
        in_specs=[rows_h(), rows_h(), whole((D_MODEL, D_MODEL)), rows_d(), rows_d(),
                  whole((1, D_MODEL)), whole((1, D_MODEL))],
        out_specs=(rows_d(), rows_d(), whole((D_MODEL, D_MODEL)), whole((SUBLANES, D_MODEL))),
        out_shape=(jax.ShapeDtypeStruct((seq, D_MODEL), F32), jax.ShapeDtypeStruct((seq, D_MODEL), F32),
                   jax.ShapeDtypeStruct((D_MODEL, D_MODEL), F32), jax.ShapeDtypeStruct((SUBLANES, D_MODEL), F32)),
        compiler_params=_compiler_params(48, ("arbitrary",)),
    )(mix_c, mix_a, woutg, x, target, gain, bias)


def _group_maps():
    half = N_GROUPS // 2
    return (lambda g: jnp.minimum(g, half - 1)), (lambda g: jnp.maximum(g - half, 0))


def _grad_w_in(xt, dpc, dpa):
    seq = xt.shape[1]
    half = N_GROUPS // 2
    in_conv, in_attn = _group_maps()

    def body(xt_ref, dc_ref, da_ref, out_ref):
        g = pl.program_id(0)

        @pl.when(g < half)
        def _():
            out_ref[...] = _dot(xt_ref[...], dc_ref[...])

        @pl.when(g >= half)
        def _():
            out_ref[...] = _dot(xt_ref[...], da_ref[...])

    return pl.pallas_call(
        body, name="grad_w_in",
        grid=(N_GROUPS,),
        in_specs=[pl.BlockSpec((D_MODEL, seq), lambda g: (0, 0), pipeline_mode=pl.Buffered(1)),
                  pl.BlockSpec((None, seq, CONV_WIDTH), lambda g: (in_conv(g), 0, 0)),
                  pl.BlockSpec((None, seq, CONV_WIDTH), lambda g: (in_attn(g), 0, 0))],
        out_specs=pl.BlockSpec((None, D_MODEL, CONV_WIDTH), lambda g: (g // 2, 0, g % 2)),
        out_shape=jax.ShapeDtypeStruct((N_CHIPS, D_MODEL, D_MODEL), F32),
        compiler_params=_compiler_params(48, ("arbitrary",)),
    )(xt, dpc, dpa)


def _grad_x(dxa, dpc, dpa, wing):
    seq = dxa.shape[0]
    tm = 512
    half = N_GROUPS // 2

    def body(dxa_ref, dc_ref, da_ref, w_ref, out_ref):
        acc = dxa_ref[...]
        for g in range(N_GROUPS):
            dp = dc_ref[g] if g < half else da_ref[g - half]
            cols = slice((g % 2) * CONV_WIDTH, (g % 2 + 1) * CONV_WIDTH)
            acc = acc + _dot_nt(dp, w_ref[g // 2, :, cols])
        out_ref[...] = acc

    return pl.pallas_call(
        body, name="grad_x",
        grid=(seq // tm,),
        in_specs=[pl.BlockSpec((tm, D_MODEL), lambda i: (i, 0)),
                  pl.BlockSpec((half, tm, CONV_WIDTH), lambda i: (0, i, 0)),
                  pl.BlockSpec((half, tm, CONV_WIDTH), lambda i: (0, i, 0)),
                  pl.BlockSpec((N_CHIPS, D_MODEL, D_MODEL), lambda i: (0, 0, 0), pipeline_mode=pl.Buffered(1))],
        out_specs=pl.BlockSpec((tm, D_MODEL), lambda i: (i, 0)),
        out_shape=jax.ShapeDtypeStruct((seq, D_MODEL), F32),
        compiler_params=_compiler_params(40, ("arbitrary",)),
    )(dxa, dpc, dpa, wing)


PACK_LOSS_ROW = 5


def _reduce_grads(gwin, gwout, pack):
    n_shards, n_rows_in, n_cols = gwin.shape
    n_rows_out = gwout.shape[1]
    half_rows = (n_rows_in // 2, n_rows_out // 2)
    row_chunk = 128

    def body(gwin_ref, gwout_ref, pack_ref, oin_ref, oout_ref, opack_ref,
             mine_in, mine_out, sib_in, sib_out, wire_in, wire_out, rin, rout, rpack,
             local_sems, sib_send, sib_recv, send_sems, recv_sems, pack_send, pack_recv, fin_send, fin_recv):
        x, y, c = lax.axis_index("x"), lax.axis_index("y"), lax.axis_index("c")
        me = 2 * x + y
        my_id = 4 * x + 2 * y + c
        chips = [(1 - x, 1 - y), (1 - x, y), (x, 1 - y)]
        sibling = (x, y, 1 - c)
        partial = (gwin_ref, gwout_ref)
        mine = (mine_in, mine_out)
        from_sib = (sib_in, sib_out)
        wire = (wire_in, wire_out)
        from_chips = (rin, rout)
        result = (oin_ref, oout_ref)

        def half(a, which):
            rows = half_rows[a]
            return pl.ds(pl.multiple_of(which * rows, rows), rows)

        order = [2 * px + py for px, py in chips] + [me]

        def swap_copy(i, a):
            shard = order[i]
            return pltpu.make_async_remote_copy(
                src_ref=partial[a].at[shard, half(a, 1 - c), :], dst_ref=from_sib[a].at[shard],
                send_sem=sib_send.at[2 * i + a], recv_sem=sib_recv.at[2 * i + a],
                device_id=sibling, device_id_type=MESH)

        sends = []
        loads = []
        for a in range(2):
            ld = pltpu.make_async_copy(partial[a].at[:, half(a, c), :], mine[a], local_sems.at[a])
            ld.start()
            loads.append(ld)
        for i in range(n_shards):
            for a in range(2):
                cp = swap_copy(i, a)
                cp.start()
                sends.append(cp)

        rpack[my_id] = pack_ref[...]

        def pack_copy(rel, slot, to):
            return pltpu.make_async_remote_copy(
                src_ref=rpack.at[slot], dst_ref=rpack.at[slot],
                send_sem=pack_send.at[rel - 1], recv_sem=pack_recv.at[rel - 1],
                device_id=to, device_id_type=MESH)

        def related(rel):
            px = (1 - x) if rel & 4 else x
            py = (1 - y) if rel & 2 else y
            pc = (1 - c) if rel & 1 else c
            return px, py, pc

        for rel in range(1, N_DEVICES):
            cp = pack_copy(rel, my_id, related(rel))
            cp.start()
            sends.append(cp)

        for ld in loads:
            ld.wait()

        def chip_sum(a, shard):
            def add(r, carry):
                rs = pl.ds(pl.multiple_of(r * row_chunk, row_chunk), row_chunk)
                mine[a][shard, rs, :] = mine[a][shard, rs, :] + from_sib[a][shard, rs, :]
                return carry

            lax.fori_loop(0, half_rows[a] // row_chunk, add, 0)

        def chip_sum_to_wire(a, shard, k):
            def add(r, carry):
                rs = pl.ds(pl.multiple_of(r * row_chunk, row_chunk), row_chunk)
                wire[a][k, rs, :] = (mine[a][shard, rs, :] + from_sib[a][shard, rs, :]).astype(BF16)
                return carry

            lax.fori_loop(0, half_rows[a] // row_chunk, add, 0)

        def shard_copy(k, a, to):
            return pltpu.make_async_remote_copy(
                src_ref=wire[a].at[k], dst_ref=from_chips[a].at[k],
                send_sem=send_sems.at[2 * k + a], recv_sem=recv_sems.at[2 * k + a],
                device_id=to, device_id_type=MESH)

        for k, (px, py) in enumerate(chips):
            for a in range(2):
                swap_copy(k, a).wait_recv()
                chip_sum_to_wire(a, 2 * px + py, k)
                cp = shard_copy(k, a, (px, py, c))
                cp.start()
                sends.append(cp)
        for a in range(2):
            swap_copy(n_shards - 1, a).wait_recv()
            chip_sum(a, me)

        for rel in range(1, N_DEVICES):
            px, py, pc = related(rel)
            pack_copy(rel, 4 * px + 2 * py + pc, (px, py, pc)).wait_recv()
        total = rpack[0]
        for i in range(1, N_DEVICES):
            total = total + rpack[i]
        opack_ref[...] = total
        sq_err = jnp.sum(total[PACK_LOSS_ROW:PACK_LOSS_ROW + 1, :], axis=1, keepdims=True)
        opack_ref[PACK_LOSS_ROW:PACK_LOSS_ROW + 1, :] = jnp.broadcast_to(sq_err * (0.5 / D_MODEL), (1, n_cols))

        for k, (px, py) in enumerate(chips):
            for a in range(2):
                shard_copy(k, a, (px, py, c)).wait_recv()

        def finish(a):
            def add(r, carry):
                rs = pl.ds(pl.multiple_of(r * row_chunk, row_chunk), row_chunk)
                mine[a][me, rs, :] = ((mine[a][me, rs, :] + from_chips[a][0, rs, :].astype(F32))
                                      + from_chips[a][1, rs, :].astype(F32)) + from_chips[a][2, rs, :].astype(F32)
                return carry

            lax.fori_loop(0, half_rows[a] // row_chunk, add, 0)

        def final_copy(a, which):
            return pltpu.make_async_remote_copy(
                src_ref=mine[a].at[me], dst_ref=result[a].at[half(a, which), :],
                send_sem=fin_send.at[a], recv_sem=fin_recv.at[a], device_id=sibling, device_id_type=MESH)

        stores = []
        for a in range(2):
            finish(a)
            cp = final_copy(a, c)
            cp.start()
            sends.append(cp)
            st = pltpu.make_async_copy(mine[a].at[me], result[a].at[half(a, c), :], local_sems.at[2 + a])
            st.start()
            stores.append(st)
        for a in range(2):
            final_copy(a, 1 - c).wait_recv()
        for cp in sends:
            cp.wait_send()
        for st in stores:
            st.wait()

    vmem = pl.BlockSpec(memory_space=pltpu.VMEM)
    hbm = pl.BlockSpec(memory_space=pl.ANY)
    in_shape = (n_rows_in, n_cols)
    out_shape = (n_rows_out, n_cols)
    half_in = (half_rows[0], n_cols)
    half_out = (half_rows[1], n_cols)
    return pl.pallas_call(
        body, name="reduce_grads",
        out_shape=(jax.ShapeDtypeStruct(in_shape, F32), jax.ShapeDtypeStruct(out_shape, F32),
                   jax.ShapeDtypeStruct(pack.shape, F32)),
        in_specs=[hbm, hbm, vmem], out_specs=(hbm, hbm, vmem),
        scratch_shapes=[pltpu.VMEM((n_shards,) + half_in, F32), pltpu.VMEM((n_shards,) + half_out, F32),
                        pltpu.VMEM((n_shards,) + half_in, F32), pltpu.VMEM((n_shards,) + half_out, F32),
                        pltpu.VMEM((N_CHIPS - 1,) + half_in, BF16), pltpu.VMEM((N_CHIPS - 1,) + half_out, BF16),
                        pltpu.VMEM((N_CHIPS - 1,) + half_in, BF16), pltpu.VMEM((N_CHIPS - 1,) + half_out, BF16),
                        pltpu.VMEM((N_DEVICES,) + pack.shape, F32),
                        pltpu.SemaphoreType.DMA((4,)),
                        pltpu.SemaphoreType.DMA((2 * n_shards,)), pltpu.SemaphoreType.DMA((2 * n_shards,)),
                        pltpu.SemaphoreType.DMA((6,)), pltpu.SemaphoreType.DMA((6,)),
                        pltpu.SemaphoreType.DMA((N_DEVICES - 1,)), pltpu.SemaphoreType.DMA((N_DEVICES - 1,)),
                        pltpu.SemaphoreType.DMA((2,)), pltpu.SemaphoreType.DMA((2,))],
        compiler_params=_compiler_params(48),
    )(gwin, gwout, pack)


def _adamw(name, sets):
    n_rows, n_cols = sets[0][0].shape
    tr = 256 if n_rows % 256 == 0 else n_rows
    m_corr = 1.0 - ADAM_B1 ** ADAM_STEP
    v_corr = 1.0 - ADAM_B2 ** ADAM_STEP
    n_sets = len(sets)

    def update(w_ref, g_ref, m_ref, v_ref, d_ref, nm_ref, nv_ref):
        gv = g_ref[...]
        nm = ADAM_B1 * m_ref[...] + (1.0 - ADAM_B1) * gv
        nv = ADAM_B2 * v_ref[...] + (1.0 - ADAM_B2) * (gv * gv)
        nm_ref[...] = nm
        nv_ref[...] = nv
        d_ref[...] = -ADAM_LR * ((nm / m_corr) / (jnp.sqrt(nv / v_corr) + ADAM_EPS) + ADAM_WD * w_ref[...])

    def body(*refs):
        ins, outs = refs[:4 * n_sets], refs[4 * n_sets:]
        update(*ins[0:4], *outs[0:3])

        @pl.when(pl.program_id(0) == 0)
        def _():
            for k in range(1, n_sets):
                update(*ins[4 * k:4 * k + 4], *outs[3 * k:3 * k + 3])

    def spec(k):
        if k == 0:
            return pl.BlockSpec((tr, n_cols), lambda i: (i, 0))
        return pl.BlockSpec(sets[k][0].shape, lambda i: (0, 0))

    in_specs = [spec(k) for k in range(n_sets) for _ in range(4)]
    out_specs = tuple(spec(k) for k in range(n_sets) for _ in range(3))
    out_shape = tuple(jax.ShapeDtypeStruct(sets[k][0].shape, F32) for k in range(n_sets) for _ in range(3))
    flat = pl.pallas_call(
        body, name=name,
        grid=(n_rows // tr,),
        in_specs=in_specs, out_specs=out_specs, out_shape=out_shape,
        compiler_params=_compiler_params(32, ("arbitrary",)),
    )(*[a for group in sets for a in group])
    return [tuple(flat[3 * k:3 * k + 3]) for k in range(n_sets)]


def kernel(x, w_in, conv_w, w_out, ln_gain, ln_bias, loss_target, m_w_in, m_conv_w, m_w_out, m_ln_gain, m_ln_bias, v_w_in, v_conv_w, v_w_out, v_ln_gain, v_ln_bias):
    xs = x[0]
    target = loss_target[0]

    wing, woutg, cwg = _gather_weights(w_in[0], w_out[0], conv_w[0])
    conv_full = jnp.transpose(cwg, (1, 0, 2)).reshape(3, CONV_WIDTH)
    wout_full = woutg.reshape(D_MODEL, D_MODEL)

    p_conv, qkv, z_attn, xt, mix_c = _proj(xs, wing, conv_full)
    o, mix_a, carries = _attn_fwd(qkv, z_attn)
    dxa, dmix, gwout, stats = _out_ln(mix_c, mix_a, wout_full, xs, target, ln_gain, ln_bias)
    dpc, dconv = _conv_bwd(p_conv, dmix, conv_full)
    dpa = _attn_bwd(qkv, z_attn, o, dmix, carries)
    gwin = _grad_w_in(xt, dpc, dpa)
    grad_x = _grad_x(dxa, dpc, dpa, wing)

    pack = jnp.concatenate(
        [stats[0:2], jnp.pad(dconv[0:3], ((0, 0), (0, D_MODEL - CONV_WIDTH))), stats[2:3],
         jnp.zeros((2, D_MODEL), F32)], axis=0)
    g_w_in, g_w_out, tot = _reduce_grads(gwin, gwout.reshape(N_CHIPS, D_MODEL // N_CHIPS, D_MODEL), pack)

    chip = 2 * lax.axis_index("x") + lax.axis_index("y")
    g_gain = tot[0:1]
    g_bias = tot[1:2]
    g_conv = lax.dynamic_slice(tot, (2, chip * LANES), (3, LANES))
    loss = tot[PACK_LOSS_ROW, 0]

    ((d_w_in, nm_w_in, nv_w_in), (d_conv, nm_conv, nv_conv),
     (d_gain, nm_gain, nv_gain), (d_bias, nm_bias, nv_bias)) = _adamw("adamw_w_in_and_small", [
        (w_in[0], g_w_in, m_w_in[0], v_w_in[0]),
        (conv_w[0], g_conv, m_conv_w[0], v_conv_w[0]),
        (ln_gain, g_gain, m_ln_gain, v_ln_gain),
        (ln_bias, g_bias, m_ln_bias, v_ln_bias)])
    ((d_w_out, nm_w_out, nv_w_out),) = _adamw("adamw_w_out", [(w_out[0], g_w_out, m_w_out[0], v_w_out[0])])

    lead = lambda a: a[None]
    return (loss, lead(grad_x),
            lead(g_w_in), lead(g_conv), lead(g_w_out), g_gain, g_bias,
            lead(d_w_in), lead(d_conv), lead(d_w_out), d_gain, d_bias,
            lead(nm_w_in), lead(nm_conv), lead(nm_w_out), nm_gain, nm_bias,
            lead(nv_w_in), lead(nv_conv), lead(nv_w_out), nv_gain, nv_bias)
```
